```python
import jax, jax.numpy as jnp
from jax import lax
import numpy as np

D_MODEL = 1024
BATCH = 2
SEQ = 16384
DEPTH = 2
DEC_BATCH = 8
DEC_SEQ = 32
PAST_LEN = 2048

CHUNK = 64
HEAD_DIM = 64
N_BRANCH = 4
BRANCH_W = D_MODEL // 4
SB_HEADS = BRANCH_W // HEAD_DIM
HG_HEADS = 4
HG_DK = BRANCH_W // HG_HEADS
HG_DV = BRANCH_W // HG_HEADS
CONV_CH = BRANCH_W
CONV_WIDTH = 31
RW_HEADS = 4
RW_HD = BRANCH_W // RW_HEADS
RW_DECAY_LORA = 64
RW_AAA_LORA = 64
RW_GATE_LORA = 128
SB_COLS = 3 * BRANCH_W
HG_COLS = 4 * BRANCH_W
CV_COLS = 2 * CONV_CH
RW_COLS = 3 * BRANCH_W + RW_DECAY_LORA + RW_AAA_LORA + RW_GATE_LORA
GATE_COLS = N_BRANCH * D_MODEL
IN_COLS = SB_COLS + HG_COLS + CV_COLS + RW_COLS + GATE_COLS
IN_SPLITS = (SB_COLS, SB_COLS + HG_COLS, SB_COLS + HG_COLS + CV_COLS,
             SB_COLS + HG_COLS + CV_COLS + RW_COLS)
RW_SPLITS = (BRANCH_W, 2 * BRANCH_W, 3 * BRANCH_W, 3 * BRANCH_W + RW_DECAY_LORA,
             3 * BRANCH_W + RW_DECAY_LORA + RW_AAA_LORA)
Q_BLOCK = 128
PK_HEADS = 8
PK_NKEYS = 128
PK_EXPERTS = PK_NKEYS * PK_NKEYS
PK_DQ = 256
PK_TOPK = 16
PEER_BLOCK = 128
EPS = 1e-6
RW_GN_EPS = 64e-5
F32 = jnp.float32

kernel_name = 'hybrid_stickbreak_hgrn2_conformer_rwkv7_peer_step'


def rmsnorm(x, g):
    xf = x.astype(F32)
    y = xf * lax.rsqrt(jnp.mean(xf * xf, axis=-1, keepdims=True) + EPS)
    return (y * g.astype(F32)).astype(x.dtype)


def sb_block(q, k, v, q_pos):
    logits = jnp.einsum('bqhd,bkhd->bhqk', q.astype(F32), k.astype(F32)) * (HEAD_DIM ** -0.5)
    mask = jnp.arange(k.shape[1], dtype=jnp.int32)[None, :] < q_pos[:, None]
    log_1m = jnp.where(mask, jax.nn.log_sigmoid(-logits), 0.0)
    later = lax.cumsum(log_1m, axis=3, reverse=True) - log_1m
    w = jnp.where(mask, jnp.exp(jax.nn.log_sigmoid(logits) + later), 0.0)
    return jnp.einsum('bhqk,bkhd->bqhd', w, v.astype(F32)).astype(v.dtype)


def sb_attn_prompt(q, k, v):
    B_, S_, H_, dh = q.shape
    nb = S_ // Q_BLOCK
    qb = q.reshape(B_, nb, Q_BLOCK, H_, dh).transpose(1, 0, 2, 3, 4)
    starts = jnp.arange(nb, dtype=jnp.int32) * Q_BLOCK

    def one(args):
        qi, s0 = args
        return sb_block(qi, k, v, s0 + jnp.arange(Q_BLOCK, dtype=jnp.int32))

    o = lax.map(one, (qb, starts))
    return o.transpose(1, 0, 2, 3, 4).reshape(B_, S_, H_, dh)


def gla_chunked(q, k, v, log_f, S0, blk):
    B_, L, H_, _ = q.shape
    n = L // blk

    def to_blocks(t):
        return t.reshape(B_, n, blk, H_, t.shape[-1]).transpose(1, 0, 3, 2, 4)

    tri = jnp.tril(jnp.ones((blk, blk), dtype=bool))

    def step(S, xs):
        qb, kb, vb, gb = xs
        b = jnp.cumsum(gb, axis=2)
        dec = jnp.exp(jnp.where(tri[None, None, :, :, None],
                                b[:, :, :, None, :] - b[:, :, None, :, :], -jnp.inf))
        A = jnp.einsum('bhtc,bhtsc,bhsc->bhts', qb, dec, kb)
        o = jnp.einsum('bhts,bhsv->bhtv', A, vb) + jnp.einsum('bhtc,bhcv->bhtv', qb * jnp.exp(b), S)
        bL = b[:, :, -1:, :]
        S_new = jnp.exp(bL[:, :, 0, :])[..., None] * S + jnp.einsum('bhsc,bhsv->bhcv', kb * jnp.exp(bL - b), vb)
        return S_new, o

    S_fin, o = lax.scan(step, S0, (to_blocks(q), to_blocks(k), to_blocks(v), to_blocks(log_f)))
    return o.transpose(1, 0, 3, 2, 4).reshape(B_, L, H_, v.shape[-1]), S_fin


def hgrn2(z, S0, lb, norm_g):
    B_, L, _ = z.shape
    zq, zf, zi, zg = jnp.split(z.astype(F32), 4, axis=-1)
    lbf = lb.astype(F32)
    log_f = jnp.logaddexp(jnp.log(lbf), jnp.log1p(-lbf) + jax.nn.log_sigmoid(zf))
    k = -jnp.expm1(log_f)

    def hd(t):
        return t.reshape(B_, L, HG_HEADS, HG_DK)

    blk = CHUNK if L % CHUNK == 0 else L
    o, S = gla_chunked(hd(jax.nn.silu(zq)), hd(k), hd(zi), hd(log_f), S0.astype(F32), blk)
    o = rmsnorm(o, norm_g) * jax.nn.silu(zg).reshape(B_, L, HG_HEADS, HG_DV)
    return o.reshape(B_, L, BRANCH_W).astype(z.dtype), S


def conformer_conv(z, buf, w, b, ln_g, ln_b):
    zf = z.astype(F32)
    val, gate = jnp.split(zf, 2, axis=-1)
    h = val * jax.nn.sigmoid(gate)
    hp = jnp.concatenate([buf.astype(F32), h], axis=1)
    y = lax.conv_general_dilated(hp, w.astype(F32)[:, None, :], window_strides=(1,), padding='VALID',
                                 dimension_numbers=('NWC', 'WIO', 'NWC'),
                                 feature_group_count=CONV_CH) + b.astype(F32)
    mu = jnp.mean(y, axis=-1, keepdims=True)
    var = jnp.mean(jnp.square(y - mu), axis=-1, keepdims=True)
    y = (y - mu) * lax.rsqrt(var + EPS) * ln_g.astype(F32) + ln_b.astype(F32)
    y = jax.nn.silu(y)
    return y.astype(z.dtype), hp[:, -(CONV_WIDTH - 1):].astype(z.dtype)


def rwkv_scan(S0, r, w, k, v, kk, a):
    def step(S, xs):
        r_t, w_t, k_t, v_t, kk_t, a_t = xs
        Skk = jnp.einsum('bhvc,bhc->bhv', S, kk_t)
        S = S * w_t[:, :, None, :] - Skk[..., None] * (kk_t * a_t)[:, :, None, :] + v_t[..., None] * k_t[:, :, None, :]
        return S, jnp.einsum('bhvc,bhc->bhv', S, r_t)

    xs = tuple(t.transpose(1, 0, 2, 3) for t in (r, w, k, v, kk, a))
    S, o = lax.scan(step, S0, xs)
    return o.transpose(1, 0, 2, 3), S


def rwkv7(z, S0, shift_prev, mu, w0, w2, a0, a2, g2, k_k, k_a, r_k, ln_g, ln_b):
    B_, L, _ = z.shape
    zf = z.astype(F32)
    z_prev = jnp.concatenate([shift_prev.astype(F32)[:, None, :], zf[:, :-1]], axis=1)
    zs = zf + (z_prev - zf) * mu.astype(F32)
    r, k, v, xw, xa, xg = jnp.split(zs, RW_SPLITS, axis=-1)
    w = -jax.nn.softplus(-(w0.astype(F32) + jnp.tanh(xw) @ w2.astype(F32))) - 0.5
    decay = jnp.exp(-jnp.exp(w))
    a = jax.nn.sigmoid(a0.astype(F32) + xa @ a2.astype(F32))
    g = jax.nn.sigmoid(xg) @ g2.astype(F32)

    def hd(t):
        return t.reshape(B_, L, RW_HEADS, RW_HD)

    kk = hd(k * k_k.astype(F32))
    kk = kk * lax.rsqrt(jnp.sum(kk * kk, axis=-1, keepdims=True) + 1e-12)
    k = k * (1.0 + (a - 1.0) * k_a.astype(F32))
    r_h, k_h, v_h, a_h, w_h = hd(r), hd(k), hd(v), hd(a), hd(decay)
    o, S = rwkv_scan(S0.astype(F32), r_h, w_h, k_h, v_h, kk, a_h)
    mean = jnp.mean(o, axis=-1, keepdims=True)
    var = jnp.mean(jnp.square(o - mean), axis=-1, keepdims=True)
    o = ((o - mean) * lax.rsqrt(var + RW_GN_EPS)).reshape(B_, L, BRANCH_W) * ln_g.astype(F32) + ln_b.astype(F32)
    bonus = jnp.sum(r_h * k_h * r_k.astype(F32), axis=-1, keepdims=True) * v_h
    o = (o + bonus.reshape(B_, L, BRANCH_W)) * g
    return o.astype(z.dtype), S, z[:, -1]


def peer_ffn(x, wq, qn_g, sub_keys, u_tab, v_tab):
    B_, L, D_ = x.shape
    n_tok = B_ * L
    pad = (-n_tok) % PEER_BLOCK
    xb = jnp.pad(x.reshape(n_tok, D_), ((0, pad), (0, 0))).reshape(-1, PEER_BLOCK, D_)
    half = PK_DQ // 2
    k1 = sub_keys[:, 0].astype(F32)
    k2 = sub_keys[:, 1].astype(F32)

    def one(xi):
        qh = rmsnorm((xi @ wq).reshape(PEER_BLOCK, PK_HEADS, PK_DQ), qn_g).astype(F32)
        s1 = jnp.einsum('thc,hkc->thk', qh[..., :half], k1)
        s2 = jnp.einsum('thc,hkc->thk', qh[..., half:], k2)
        v1, i1 = lax.top_k(s1, PK_TOPK)
        v2, i2 = lax.top_k(s2, PK_TOPK)
        cand_s = (v1[..., :, None] + v2[..., None, :]).reshape(PEER_BLOCK, PK_HEADS, PK_TOPK * PK_TOPK)
        cand_i = (i1[..., :, None] * PK_NKEYS + i2[..., None, :]).reshape(PEER_BLOCK, PK_HEADS, PK_TOPK * PK_TOPK)
        top_s, top_p = lax.top_k(cand_s, PK_TOPK)
        expert = jnp.take_along_axis(cand_i, top_p, axis=-1)
        gate = jax.nn.softmax(top_s, axis=-1)
        act = jax.nn.gelu(jnp.einsum('thkd,td->thk', u_tab[expert].astype(F32), xi.astype(F32)))
        return jnp.einsum('thk,thkd->td', gate * act, v_tab[expert].astype(F32)).astype(xi.dtype)

    y = lax.map(one, xb).reshape(-1, D_)[:n_tok]
    return y.reshape(B_, L, D_)


def trunk_layer(x, k_past, v_past, s_hg, buf_conv, s_rw, buf_shift, p):
    B_, L, _ = x.shape
    h = rmsnorm(x, p['norm_mix_g'])
    z = h @ p['w_in']
    z_sb, z_hg, z_cv, z_rw, z_gate = jnp.split(z, IN_SPLITS, axis=-1)
    q, k, v = [t.reshape(B_, L, SB_HEADS, HEAD_DIM) for t in jnp.split(z_sb, 3, axis=-1)]
    if k_past is None:
        o_sb = sb_attn_prompt(q, k, v)
    else:
        past = k_past.shape[1]
        o_sb = sb_block(q, jnp.concatenate([k_past.astype(k.dtype), k], axis=1),
                        jnp.concatenate([v_past.astype(v.dtype), v], axis=1),
                        past + jnp.arange(L, dtype=jnp.int32))
    o_sb = o_sb.reshape(B_, L, BRANCH_W)
    o_hg, s_hg_new = hgrn2(z_hg, s_hg, p['lb'], p['hg_norm_g'])
    o_cv, buf_conv_new = conformer_conv(z_cv, buf_conv, p['conv_w'], p['conv_b'], p['conv_ln_g'], p['conv_ln_b'])
    o_rw, s_rw_new, buf_shift_new = rwkv7(z_rw, s_rw, buf_shift, p['rw_mu'], p['rw_w0'], p['rw_w2'], p['rw_a0'],
                                          p['rw_a2'], p['rw_g2'], p['rw_k_k'], p['rw_k_a'], p['rw_r_k'],
                                          p['rw_ln_g'], p['rw_ln_b'])
    branches = jnp.stack([o_sb, o_hg, o_cv, o_rw], axis=2)
    gates = jax.nn.sigmoid(z_gate.reshape(B_, L, N_BRANCH, D_MODEL))
    merged = jnp.sum(jnp.einsum('blnc,ncd->blnd', branches, p['w_branch']) * gates, axis=2)
    x = x + merged @ p['w_out']
    x = x + peer_ffn(rmsnorm(x, p['norm_ffn_g']), p['peer_wq'], p['peer_qn_g'], p['peer_keys'],
                     p['peer_u'], p['peer_v'])
    return x, (k, v, s_hg_new, buf_conv_new, s_rw_new, buf_shift_new)


def setup_inputs(seed: int = 0) -> dict:
    key = jax.random.key(seed)
    keys = jax.random.split(key, 48)
    counter = [0]

    def nxt():
        kk = keys[counter[0]]
        counter[0] += 1
        return kk

    def nrm(shape, scale):
        return jax.random.normal(nxt(), shape, F32) * scale

    def gain(shape):
        return 1.0 + nrm(shape, 0.02)

    D = D_MODEL
    return {
        'x_prompt': nrm((BATCH, SEQ, D), 1.0),
        'x_sample': nrm((DEC_BATCH, DEC_SEQ, D), 1.0),
        'cache_sb_k': nrm((DEPTH, DEC_BATCH, PAST_LEN, SB_HEADS, HEAD_DIM), 1.0),
        'cache_sb_v': nrm((DEPTH, DEC_BATCH, PAST_LEN, SB_HEADS, HEAD_DIM), 1.0),
        'state_hgrn': nrm((DEPTH, DEC_BATCH, HG_HEADS, HG_DK, HG_DV), 0.5),
        'state_conv': nrm((DEPTH, DEC_BATCH, CONV_WIDTH - 1, CONV_CH), 0.5),
        'state_rwkv': nrm((DEPTH, DEC_BATCH, RW_HEADS, RW_HD, RW_HD), 1.0),
        'state_shift': nrm((DEPTH, DEC_BATCH, RW_COLS), 1.0),
        'norm_mix_g': gain((DEPTH, D)),
        'w_in': nrm((DEPTH, D, IN_COLS), D ** -0.5),
        'hg_lb_logits': nrm((DEPTH, BRANCH_W), 1.0),
        'hg_norm_g': gain((DEPTH, HG_DV)),
        'conv_w': nrm((DEPTH, CONV_WIDTH, CONV_CH), CONV_WIDTH ** -0.5),
        'conv_b': nrm((DEPTH, CONV_CH), 0.02),
        'conv_ln_g': gain((DEPTH, CONV_CH)),
        'conv_ln_b': nrm((DEPTH, CONV_CH), 0.02),
        'rw_mu': jax.random.uniform(nxt(), (DEPTH, RW_COLS), F32),
        'rw_w0': jnp.linspace(-6.0, 1.0, BRANCH_W, dtype=F32)[None, :] + nrm((DEPTH, BRANCH_W), 0.1),
        'rw_w2': nrm((DEPTH, RW_DECAY_LORA, BRANCH_W), 0.1),
        'rw_a0': nrm((DEPTH, BRANCH_W), 0.1),
        'rw_a2': nrm((DEPTH, RW_AAA_LORA, BRANCH_W), 0.1),
        'rw_g2': nrm((DEPTH, RW_GATE_LORA, BRANCH_W), RW_GATE_LORA ** -0.5),
        'rw_k_k': 0.85 + nrm((DEPTH, BRANCH_W), 0.02),
        'rw_k_a': 1.0 + nrm((DEPTH, BRANCH_W), 0.02),
        'rw_r_k': nrm((DEPTH, RW_HEADS, RW_HD), 0.1),
        'rw_ln_g': gain((DEPTH, BRANCH_W)),
        'rw_ln_b': nrm((DEPTH, BRANCH_W), 0.02),
        'w_branch': nrm((DEPTH, N_BRANCH, BRANCH_W, D), BRANCH_W ** -0.5),
        'w_out': nrm((DEPTH, D, D), D ** -0.5),
        'norm_ffn_g': gain((DEPTH, D)),
        'peer_wq': nrm((DEPTH, D, PK_HEADS * PK_DQ), D ** -0.5),
        'peer_qn_g': gain((DEPTH, PK_DQ)),
        'peer_keys': nrm((DEPTH, PK_HEADS, 2, PK_NKEYS, PK_DQ // 2), (PK_DQ // 2) ** -0.5),
        'peer_u': nrm((DEPTH, PK_EXPERTS, D), D ** -0.5),
        'peer_v': nrm((DEPTH, PK_EXPERTS, D), PK_HEADS ** -0.5),
        'final_norm_g': gain((D,)),
    }


def reference(x_prompt, x_sample, cache_sb_k, cache_sb_v, state_hgrn, state_conv, state_rwkv, state_shift,
              norm_mix_g, w_in, hg_lb_logits, hg_norm_g, conv_w, conv_b, conv_ln_g, conv_ln_b,
              rw_mu, rw_w0, rw_w2, rw_a0, rw_a2, rw_g2, rw_k_k, rw_k_a, rw_r_k, rw_ln_g, rw_ln_b,
              w_branch, w_out, norm_ffn_g, peer_wq, peer_qn_g, peer_keys, peer_u, peer_v, final_norm_g):
    lb_all = jnp.cumsum(jax.nn.softmax(hg_lb_logits.astype(F32), axis=0), axis=0)
    lb_all = lb_all - lb_all[0:1]
    Bp = x_prompt.shape[0]
    hg0 = jnp.zeros((Bp, HG_HEADS, HG_DK, HG_DV), F32)
    cv0 = jnp.zeros((Bp, CONV_WIDTH - 1, CONV_CH), x_prompt.dtype)
    rw0 = jnp.zeros((Bp, RW_HEADS, RW_HD, RW_HD), F32)
    sh0 = jnp.zeros((Bp, RW_COLS), x_prompt.dtype)
    xp, xs = x_prompt, x_sample
    outs_p, outs_s = [], []
    for l in range(DEPTH):
        lp = dict(norm_mix_g=norm_mix_g[l], w_in=w_in[l], lb=lb_all[l], hg_norm_g=hg_norm_g[l],
                  conv_w=conv_w[l], conv_b=conv_b[l], conv_ln_g=conv_ln_g[l], conv_ln_b=conv_ln_b[l],
                  rw_mu=rw_mu[l], rw_w0=rw_w0[l], rw_w2=rw_w2[l], rw_a0=rw_a0[l], rw_a2=rw_a2[l],
                  rw_g2=rw_g2[l], rw_k_k=rw_k_k[l], rw_k_a=rw_k_a[l], rw_r_k=rw_r_k[l],
                  rw_ln_g=rw_ln_g[l], rw_ln_b=rw_ln_b[l], w_branch=w_branch[l], w_out=w_out[l],
                  norm_ffn_g=norm_ffn_g[l], peer_wq=peer_wq[l], peer_qn_g=peer_qn_g[l],
                  peer_keys=peer_keys[l], peer_u=peer_u[l], peer_v=peer_v[l])
        xp, st_p = trunk_layer(xp, None, None, hg0, cv0, rw0, sh0, lp)
        xs, st_s = trunk_layer(xs, cache_sb_k[l], cache_sb_v[l], state_hgrn[l], state_conv[l],
                               state_rwkv[l], state_shift[l], lp)
        outs_p.append(st_p)
        outs_s.append(st_s)

    def stk(outs, i):
        return jnp.stack([o[i] for o in outs], axis=0)

    y_prompt = rmsnorm(xp, final_norm_g)
    y_sample = rmsnorm(xs, final_norm_g)
    return (y_prompt, y_sample,
            stk(outs_p, 0), stk(outs_p, 1), stk(outs_p, 2), stk(outs_p, 3), stk(outs_p, 4), stk(outs_p, 5),
            stk(outs_s, 0), stk(outs_s, 1), stk(outs_s, 2), stk(outs_s, 3), stk(outs_s, 4), stk(outs_s, 5))
```

```python
from functools import partial

import jax
import jax.numpy as jnp
from jax import lax
from jax.experimental import pallas as pl
from jax.experimental.pallas import tpu as pltpu

D_MODEL = 1024
DEPTH = 2
CHUNK = 64
HEAD_DIM = 64
N_BRANCH = 4
BRANCH_W = D_MODEL // 4
SB_HEADS = BRANCH_W // HEAD_DIM
HG_HEADS = 4
HG_DK = BRANCH_W // HG_HEADS
HG_DV = BRANCH_W // HG_HEADS
CONV_CH = BRANCH_W
CONV_WIDTH = 31
RW_HEADS = 4
RW_HD = BRANCH_W // RW_HEADS
RW_DECAY_LORA = 64
RW_AAA_LORA = 64
RW_GATE_LORA = 128
SB_COLS = 3 * BRANCH_W
HG_COLS = 4 * BRANCH_W
CV_COLS = 2 * CONV_CH
RW_COLS = 3 * BRANCH_W + RW_DECAY_LORA + RW_AAA_LORA + RW_GATE_LORA
GATE_COLS = N_BRANCH * D_MODEL
IN_COLS = SB_COLS + HG_COLS + CV_COLS + RW_COLS + GATE_COLS
IN_SPLITS = (SB_COLS, SB_COLS + HG_COLS, SB_COLS + HG_COLS + CV_COLS,
             SB_COLS + HG_COLS + CV_COLS + RW_COLS)
RW_SPLITS = (BRANCH_W, 2 * BRANCH_W, 3 * BRANCH_W, 3 * BRANCH_W + RW_DECAY_LORA,
             3 * BRANCH_W + RW_DECAY_LORA + RW_AAA_LORA)
Q_BLOCK = 128
PK_HEADS = 8
PK_NKEYS = 128
PK_DQ = 256
PK_TOPK = 16
PEER_BLOCK = 128
EPS = 1e-6
RW_GN_EPS = 64e-5
F32 = jnp.float32


def _rmsnorm(x, g):
    xf = x.astype(F32)
    y = xf * lax.rsqrt(jnp.mean(xf * xf, axis=-1, keepdims=True) + EPS)
    return (y * g.astype(F32)).astype(x.dtype)


def _rmsnorm_body(x_ref, g_ref, o_ref):
    x = x_ref[...]
    y = x * lax.rsqrt(jnp.mean(x * x, axis=-1, keepdims=True) + EPS)
    o_ref[...] = y * g_ref[...]


def _rmsnorm_pallas(x, g, rows=512):
    shape = x.shape
    x2 = x.reshape(-1, shape[-1])
    n, d = x2.shape
    rows = min(rows, n)
    out = pl.pallas_call(
        _rmsnorm_body,
        grid=(n // rows,),
        in_specs=[pl.BlockSpec((rows, d), lambda i: (i, 0)),
                  pl.BlockSpec((1, d), lambda i: (0, 0))],
        out_specs=pl.BlockSpec((rows, d), lambda i: (i, 0)),
        out_shape=jax.ShapeDtypeStruct((n, d), x.dtype),
        name="final_rmsnorm",
    )(x2, g.reshape(1, d))
    return out.reshape(shape)


def _sb_block(q, k, v, q_pos):
    logits = jnp.einsum('bqhd,bkhd->bhqk', q.astype(F32), k.astype(F32)) * (HEAD_DIM ** -0.5)
    mask = jnp.arange(k.shape[1], dtype=jnp.int32)[None, :] < q_pos[:, None]
    log_1m = jnp.where(mask, jax.nn.log_sigmoid(-logits), 0.0)
    later = lax.cumsum(log_1m, axis=3, reverse=True) - log_1m
    w = jnp.where(mask, jnp.exp(jax.nn.log_sigmoid(logits) + later), 0.0)
    return jnp.einsum('bhqk,bkhd->bqhd', w, v.astype(F32)).astype(v.dtype)


def _sb_attn_prompt(q, k, v):
    B_, S_, H_, dh = q.shape
    nb = S_ // Q_BLOCK
    qb = q.reshape(B_, nb, Q_BLOCK, H_, dh).transpose(1, 0, 2, 3, 4)
    starts = jnp.arange(nb, dtype=jnp.int32) * Q_BLOCK

    def one(args):
        qi, s0 = args
        return _sb_block(qi, k, v, s0 + jnp.arange(Q_BLOCK, dtype=jnp.int32))

    o = lax.map(one, (qb, starts))
    return o.transpose(1, 0, 2, 3, 4).reshape(B_, S_, H_, dh)


def _gla_chunked(q, k, v, log_f, S0, blk):
    B_, L, H_, _ = q.shape
    n = L // blk

    def to_blocks(t):
        return t.reshape(B_, n, blk, H_, t.shape[-1]).transpose(1, 0, 3, 2, 4)

    tri = jnp.tril(jnp.ones((blk, blk), dtype=bool))

    def step(S, xs):
        qb, kb, vb, gb = xs
        b = jnp.cumsum(gb, axis=2)
        dec = jnp.exp(jnp.where(tri[None, None, :, :, None],
                                b[:, :, :, None, :] - b[:, :, None, :, :], -jnp.inf))
        A = jnp.einsum('bhtc,bhtsc,bhsc->bhts', qb, dec, kb)
        o = jnp.einsum('bhts,bhsv->bhtv', A, vb) + jnp.einsum('bhtc,bhcv->bhtv', qb * jnp.exp(b), S)
        bL = b[:, :, -1:, :]
        S_new = jnp.exp(bL[:, :, 0, :])[..., None] * S + jnp.einsum('bhsc,bhsv->bhcv', kb * jnp.exp(bL - b), vb)
        return S_new, o

    S_fin, o = lax.scan(step, S0, (to_blocks(q), to_blocks(k), to_blocks(v), to_blocks(log_f)))
    return o.transpose(1, 0, 3, 2, 4).reshape(B_, L, H_, v.shape[-1]), S_fin


def _hgrn2(z, S0, lb, norm_g):
    B_, L, _ = z.shape
    zq, zf, zi, zg = jnp.split(z.astype(F32), 4, axis=-1)
    lbf = lb.astype(F32)
    log_f = jnp.logaddexp(jnp.log(lbf), jnp.log1p(-lbf) + jax.nn.log_sigmoid(zf))
    k = -jnp.expm1(log_f)

    def hd(t):
        return t.reshape(B_, L, HG_HEADS, HG_DK)

    blk = CHUNK if L % CHUNK == 0 else L
    o, S = _gla_chunked(hd(jax.nn.silu(zq)), hd(k), hd(zi), hd(log_f), S0.astype(F32), blk)
    o = _rmsnorm(o, norm_g) * jax.nn.silu(zg).reshape(B_, L, HG_HEADS, HG_DV)
    return o.reshape(B_, L, BRANCH_W).astype(z.dtype), S


def _conformer_conv(z, buf, w, b, ln_g, ln_b):
    zf = z.astype(F32)
    val, gate = jnp.split(zf, 2, axis=-1)
    h = val * jax.nn.sigmoid(gate)
    hp = jnp.concatenate([buf.astype(F32), h], axis=1)
    y = lax.conv_general_dilated(hp, w.astype(F32)[:, None, :], window_strides=(1,), padding='VALID',
                                 dimension_numbers=('NWC', 'WIO', 'NWC'),
                                 feature_group_count=CONV_CH) + b.astype(F32)
    mu = jnp.mean(y, axis=-1, keepdims=True)
    var = jnp.mean(jnp.square(y - mu), axis=-1, keepdims=True)
    y = (y - mu) * lax.rsqrt(var + EPS) * ln_g.astype(F32) + ln_b.astype(F32)
    y = jax.nn.silu(y)
    return y.astype(z.dtype), hp[:, -(CONV_WIDTH - 1):].astype(z.dtype)


def _rwkv_scan(S0, r, w, k, v, kk, a):
    def step(S, xs):
        r_t, w_t, k_t, v_t, kk_t, a_t = xs
        Skk = jnp.einsum('bhvc,bhc->bhv', S, kk_t)
        S = S * w_t[:, :, None, :] - Skk[..., None] * (kk_t * a_t)[:, :, None, :] + v_t[..., None] * k_t[:, :, None, :]
        return S, jnp.einsum('bhvc,bhc->bhv', S, r_t)

    xs = tuple(t.transpose(1, 0, 2, 3) for t in (r, w, k, v, kk, a))
    S, o = lax.scan(step, S0, xs)
    return o.transpose(1, 0, 2, 3), S


def _rwkv7(z, S0, shift_prev, mu, w0, w2, a0, a2, g2, k_k, k_a, r_k, ln_g, ln_b):
    B_, L, _ = z.shape
    zf = z.astype(F32)
    z_prev = jnp.concatenate([shift_prev.astype(F32)[:, None, :], zf[:, :-1]], axis=1)
    zs = zf + (z_prev - zf) * mu.astype(F32)
    r, k, v, xw, xa, xg = jnp.split(zs, RW_SPLITS, axis=-1)
    w = -jax.nn.softplus(-(w0.astype(F32) + jnp.tanh(xw) @ w2.astype(F32))) - 0.5
    decay = jnp.exp(-jnp.exp(w))
    a = jax.nn.sigmoid(a0.astype(F32) + xa @ a2.astype(F32))
    g = jax.nn.sigmoid(xg) @ g2.astype(F32)

    def hd(t):
        return t.reshape(B_, L, RW_HEADS, RW_HD)

    kk = hd(k * k_k.astype(F32))
    kk = kk * lax.rsqrt(jnp.sum(kk * kk, axis=-1, keepdims=True) + 1e-12)
    k = k * (1.0 + (a - 1.0) * k_a.astype(F32))
    r_h, k_h, v_h, a_h, w_h = hd(r), hd(k), hd(v), hd(a), hd(decay)
    o, S = _rwkv_scan(S0.astype(F32), r_h, w_h, k_h, v_h, kk, a_h)
    mean = jnp.mean(o, axis=-1, keepdims=True)
    var = jnp.mean(jnp.square(o - mean), axis=-1, keepdims=True)
    o = ((o - mean) * lax.rsqrt(var + RW_GN_EPS)).reshape(B_, L, BRANCH_W) * ln_g.astype(F32) + ln_b.astype(F32)
    bonus = jnp.sum(r_h * k_h * r_k.astype(F32), axis=-1, keepdims=True) * v_h
    o = (o + bonus.reshape(B_, L, BRANCH_W)) * g
    return o.astype(z.dtype), S, z[:, -1]


def _peer_ffn(x, wq, qn_g, sub_keys, u_tab, v_tab):
    B_, L, D_ = x.shape
    n_tok = B_ * L
    pad = (-n_tok) % PEER_BLOCK
    xb = jnp.pad(x.reshape(n_tok, D_), ((0, pad), (0, 0))).reshape(-1, PEER_BLOCK, D_)
    half = PK_DQ // 2
    k1 = sub_keys[:, 0].astype(F32)
    k2 = sub_keys[:, 1].astype(F32)

    def one(xi):
        qh = _rmsnorm((xi @ wq).reshape(PEER_BLOCK, PK_HEADS, PK_DQ), qn_g).astype(F32)
        s1 = jnp.einsum('thc,hkc->thk', qh[..., :half], k1)
        s2 = jnp.einsum('thc,hkc->thk', qh[..., half:], k2)
        v1, i1 = lax.top_k(s1, PK_TOPK)
        v2, i2 = lax.top_k(s2, PK_TOPK)
        cand_s = (v1[..., :, None] + v2[..., None, :]).reshape(PEER_BLOCK, PK_HEADS, PK_TOPK * PK_TOPK)
        cand_i = (i1[..., :, None] * PK_NKEYS + i2[..., None, :]).reshape(PEER_BLOCK, PK_HEADS, PK_TOPK * PK_TOPK)
        top_s, top_p = lax.top_k(cand_s, PK_TOPK)
        expert = jnp.take_along_axis(cand_i, top_p, axis=-1)
        gate = jax.nn.softmax(top_s, axis=-1)
        act = jax.nn.gelu(jnp.einsum('thkd,td->thk', u_tab[expert].astype(F32), xi.astype(F32)))
        return jnp.einsum('thk,thkd->td', gate * act, v_tab[expert].astype(F32)).astype(xi.dtype)

    y = lax.map(one, xb).reshape(-1, D_)[:n_tok]
    return y.reshape(B_, L, D_)


def _trunk_layer(x, k_past, v_past, s_hg, buf_conv, s_rw, buf_shift, p):
    B_, L, _ = x.shape
    h = _rmsnorm(x, p['norm_mix_g'])
    z = h @ p['w_in']
    z_sb, z_hg, z_cv, z_rw, z_gate = jnp.split(z, IN_SPLITS, axis=-1)
    q, k, v = [t.reshape(B_, L, SB_HEADS, HEAD_DIM) for t in jnp.split(z_sb, 3, axis=-1)]
    if k_past is None:
        o_sb = _sb_attn_prompt(q, k, v)
    else:
        past = k_past.shape[1]
        o_sb = _sb_block(q, jnp.concatenate([k_past.astype(k.dtype), k], axis=1),
                         jnp.concatenate([v_past.astype(v.dtype), v], axis=1),
                         past + jnp.arange(L, dtype=jnp.int32))
    o_sb = o_sb.reshape(B_, L, BRANCH_W)
    o_hg, s_hg_new = _hgrn2(z_hg, s_hg, p['lb'], p['hg_norm_g'])
    o_cv, buf_conv_new = _conformer_conv(z_cv, buf_conv, p['conv_w'], p['conv_b'], p['conv_ln_g'], p['conv_ln_b'])
    o_rw, s_rw_new, buf_shift_new = _rwkv7(z_rw, s_rw, buf_shift, p['rw_mu'], p['rw_w0'], p['rw_w2'], p['rw_a0'],
                                           p['rw_a2'], p['rw_g2'], p['rw_k_k'], p['rw_k_a'], p['rw_r_k'],
                                           p['rw_ln_g'], p['rw_ln_b'])
    branches = jnp.stack([o_sb, o_hg, o_cv, o_rw], axis=2)
    gates = jax.nn.sigmoid(z_gate.reshape(B_, L, N_BRANCH, D_MODEL))
    merged = jnp.sum(jnp.einsum('blnc,ncd->blnd', branches, p['w_branch']) * gates, axis=2)
    x = x + merged @ p['w_out']
    x = x + _peer_ffn(_rmsnorm(x, p['norm_ffn_g']), p['peer_wq'], p['peer_qn_g'], p['peer_keys'],
                      p['peer_u'], p['peer_v'])
    return x, (k, v, s_hg_new, buf_conv_new, s_rw_new, buf_shift_new)


def kernel(x_prompt, x_sample, cache_sb_k, cache_sb_v, state_hgrn, state_conv, state_rwkv, state_shift,
           norm_mix_g, w_in, hg_lb_logits, hg_norm_g, conv_w, conv_b, conv_ln_g, conv_ln_b,
           rw_mu, rw_w0, rw_w2, rw_a0, rw_a2, rw_g2, rw_k_k, rw_k_a, rw_r_k, rw_ln_g, rw_ln_b,
           w_branch, w_out, norm_ffn_g, peer_wq, peer_qn_g, peer_keys, peer_u, peer_v, final_norm_g):
    lb_all = jnp.cumsum(jax.nn.softmax(hg_lb_logits.astype(F32), axis=0), axis=0)
    lb_all = lb_all - lb_all[0:1]
    Bp = x_prompt.shape[0]
    hg0 = jnp.zeros((Bp, HG_HEADS, HG_DK, HG_DV), F32)
    cv0 = jnp.zeros((Bp, CONV_WIDTH - 1, CONV_CH), x_prompt.dtype)
    rw0 = jnp.zeros((Bp, RW_HEADS, RW_HD, RW_HD), F32)
    sh0 = jnp.zeros((Bp, RW_COLS), x_prompt.dtype)
    xp, xs = x_prompt, x_sample
    outs_p, outs_s = [], []
    for l in range(DEPTH):
        lp = dict(norm_mix_g=norm_mix_g[l], w_in=w_in[l], lb=lb_all[l], hg_norm_g=hg_norm_g[l],
                  conv_w=conv_w[l], conv_b=conv_b[l], conv_ln_g=conv_ln_g[l], conv_ln_b=conv_ln_b[l],
                  rw_mu=rw_mu[l], rw_w0=rw_w0[l], rw_w2=rw_w2[l], rw_a0=rw_a0[l], rw_a2=rw_a2[l],
                  rw_g2=rw_g2[l], rw_k_k=rw_k_k[l], rw_k_a=rw_k_a[l], rw_r_k=rw_r_k[l],
                  rw_ln_g=rw_ln_g[l], rw_ln_b=rw_ln_b[l], w_branch=w_branch[l], w_out=w_out[l],
                  norm_ffn_g=norm_ffn_g[l], peer_wq=peer_wq[l], peer_qn_g=peer_qn_g[l],
                  peer_keys=peer_keys[l], peer_u=peer_u[l], peer_v=peer_v[l])
        xp, st_p = _trunk_layer(xp, None, None, hg0, cv0, rw0, sh0, lp)
        xs, st_s = _trunk_layer(xs, cache_sb_k[l], cache_sb_v[l], state_hgrn[l], state_conv[l],
                                state_rwkv[l], state_shift[l], lp)
        outs_p.append(st_p)
        outs_s.append(st_s)

    def stk(outs, i):
        return jnp.stack([o[i] for o in outs], axis=0)

    y_prompt = _rmsnorm_pallas(xp, final_norm_g)
    y_sample = _rmsnorm_pallas(xs, final_norm_g)
    return (y_prompt, y_sample,
            stk(outs_p, 0), stk(outs_p, 1), stk(outs_p, 2), stk(outs_p, 3), stk(outs_p, 4), stk(outs_p, 5),
            stk(outs_s, 0), stk(outs_s, 1), stk(outs_s, 2), stk(outs_s, 3), stk(outs_s, 4), stk(outs_s, 5))
```

```python
from functools import partial

import jax
import jax.numpy as jnp
from jax import lax
from jax.experimental import pallas as pl
from jax.experimental.pallas import tpu as pltpu

D_MODEL = 1024
DEPTH = 2
CHUNK = 64
HEAD_DIM = 64
N_BRANCH = 4
BRANCH_W = D_MODEL // 4
SB_HEADS = BRANCH_W // HEAD_DIM
HG_HEADS = 4
HG_DK = BRANCH_W // HG_HEADS
HG_DV = BRANCH_W // HG_HEADS
CONV_CH = BRANCH_W
CONV_WIDTH = 31
RW_HEADS = 4
RW_HD = BRANCH_W // RW_HEADS
RW_DECAY_LORA = 64
RW_AAA_LORA = 64
RW_GATE_LORA = 128
SB_COLS = 3 * BRANCH_W
HG_COLS = 4 * BRANCH_W
CV_COLS = 2 * CONV_CH
RW_COLS = 3 * BRANCH_W + RW_DECAY_LORA + RW_AAA_LORA + RW_GATE_LORA
GATE_COLS = N_BRANCH * D_MODEL
IN_COLS = SB_COLS + HG_COLS + CV_COLS + RW_COLS + GATE_COLS
IN_SPLITS = (SB_COLS, SB_COLS + HG_COLS, SB_COLS + HG_COLS + CV_COLS,
             SB_COLS + HG_COLS + CV_COLS + RW_COLS)
RW_SPLITS = (BRANCH_W, 2 * BRANCH_W, 3 * BRANCH_W, 3 * BRANCH_W + RW_DECAY_LORA,
             3 * BRANCH_W + RW_DECAY_LORA + RW_AAA_LORA)
Q_BLOCK = 128
PK_HEADS = 8
PK_NKEYS = 128
PK_DQ = 256
PK_TOPK = 16
PEER_BLOCK = 128
EPS = 1e-6
RW_GN_EPS = 64e-5
SB_TQ = 256
SB_TK = 256
RW_CHUNK = 64
F32 = jnp.float32
BF16 = jnp.bfloat16


def _rmsnorm(x, g):
    xf = x.astype(F32)
    y = xf * lax.rsqrt(jnp.mean(xf * xf, axis=-1, keepdims=True) + EPS)
    return (y * g.astype(F32)).astype(x.dtype)


def _rmsnorm_body(x_ref, g_ref, o_ref):
    x = x_ref[...]
    y = x * lax.rsqrt(jnp.mean(x * x, axis=-1, keepdims=True) + EPS)
    o_ref[...] = y * g_ref[...]


def _rmsnorm_pallas(x, g, rows=512):
    shape = x.shape
    x2 = x.reshape(-1, shape[-1])
    n, d = x2.shape
    rows = min(rows, n)
    out = pl.pallas_call(
        _rmsnorm_body,
        grid=(n // rows,),
        in_specs=[pl.BlockSpec((rows, d), lambda i: (i, 0)),
                  pl.BlockSpec((1, d), lambda i: (0, 0))],
        out_specs=pl.BlockSpec((rows, d), lambda i: (i, 0)),
        out_shape=jax.ShapeDtypeStruct((n, d), x.dtype),
        name="final_rmsnorm",
    )(x2, g.reshape(1, d))
    return out.reshape(shape)


VMEM_LIMIT_V7X = 48 * 1024 * 1024


def _sb_attn_body(q_ref, k_ref, v_ref, o_ref, acc_ref, carry_ref, *, tq, tk, off, n_diag):
    qi = pl.program_id(1)
    q0 = off + qi * tq
    n_full = q0 // tk
    acc_ref[...] = jnp.zeros_like(acc_ref)
    carry_ref[...] = jnp.zeros_like(carry_ref)
    q_all = q_ref[0]
    q_heads = [q_all[:, h * HEAD_DIM:(h + 1) * HEAD_DIM] for h in range(SB_HEADS)]
    jj = lax.broadcasted_iota(jnp.int32, (tk, tk), 0)
    ss = lax.broadcasted_iota(jnp.int32, (tk, tk), 1)
    later_mat = (jj > ss).astype(BF16)

    def block(kb, masked):
        start = pl.multiple_of(kb * tk, tk)
        k_blk = k_ref[0, pl.ds(start, tk), :]
        v_blk = v_ref[0, pl.ds(start, tk), :]
        if masked:
            key_pos = start + lax.broadcasted_iota(jnp.int32, (tq, tk), 1)
            q_pos = q0 + lax.broadcasted_iota(jnp.int32, (tq, tk), 0)
            mask = key_pos < q_pos
        outs = []
        for h in range(SB_HEADS):
            kh = k_blk[:, h * HEAD_DIM:(h + 1) * HEAD_DIM]
            vh = v_blk[:, h * HEAD_DIM:(h + 1) * HEAD_DIM]
            l = lax.dot_general(q_heads[h], kh, (((1,), (1,)), ((), ())), preferred_element_type=F32)
            lsm = -(jnp.maximum(l, 0.0) + jnp.log(1.0 + jnp.exp(-jnp.abs(l))))
            if masked:
                lsm = jnp.where(mask, lsm, 0.0)
            later = jnp.dot(lsm.astype(BF16), later_mat, preferred_element_type=F32)
            carry = carry_ref[h]
            expo = l + lsm + later + jnp.concatenate([carry] * (tk // 128), axis=1)
            w = jnp.exp(expo)
            if masked:
                w = jnp.where(mask, w, 0.0)
            outs.append(jnp.dot(w.astype(BF16), vh, preferred_element_type=F32))
            row = later[:, 0:1] + lsm[:, 0:1]
            carry_ref[h] = carry + jnp.broadcast_to(row, carry.shape)
        acc_ref[...] += jnp.concatenate(outs, axis=1)

    for d in range(n_diag - 1, -1, -1):
        block(n_full + d, True)

    def full_step(i, c):
        block(n_full - 1 - i, False)
        return c

    lax.fori_loop(0, n_full, full_step, 0)
    o_ref[0] = acc_ref[...].astype(o_ref.dtype)


def _sb_attention(q, k, v, *, off, tq, tk, out_dtype=BF16):
    B, Lq, W = q.shape
    Lk = k.shape[1]
    assert W == BRANCH_W and Lq % tq == 0 and Lk % tk == 0 and tk % 128 == 0
    nq = Lq // tq
    assert nq == 1 or (tq % tk == 0 and off % tk == 0)
    n_diag = -(-((off % tk) + tq - 1) // tk)
    assert (off + Lq - 1 + tk - 1) // tk <= Lk // tk
    body = partial(_sb_attn_body, tq=tq, tk=tk, off=off, n_diag=n_diag)
    return pl.pallas_call(
        body,
        grid=(B, nq),
        in_specs=[pl.BlockSpec((1, tq, W), lambda b, i: (b, i, 0)),
                  pl.BlockSpec((1, Lk, W), lambda b, i: (b, 0, 0)),
                  pl.BlockSpec((1, Lk, W), lambda b, i: (b, 0, 0))],
        out_specs=pl.BlockSpec((1, tq, W), lambda b, i: (b, i, 0)),
        out_shape=jax.ShapeDtypeStruct((B, Lq, W), out_dtype),
        scratch_shapes=[pltpu.VMEM((tq, W), F32), pltpu.VMEM((SB_HEADS, tq, 128), F32)],
        compiler_params=pltpu.CompilerParams(dimension_semantics=("arbitrary", "arbitrary"),
                                             vmem_limit_bytes=VMEM_LIMIT_V7X),
        name="sb_attention",
    )(q, k, v)


RW_SUB = 16
HI = lax.Precision.HIGHEST


def _split(x):
    hi = x.astype(BF16)
    lo = (x - hi.astype(F32)).astype(BF16)
    return hi, lo


def _mm3(a, b):
    ah, al = _split(a)
    bh, bl = _split(b)
    d = partial(jnp.dot, preferred_element_type=F32)
    return d(ah, bh) + (d(ah, bl) + d(al, bh))


def _dot_nt(a, b):
    return lax.dot_general(a, b, (((1,), (1,)), ((), ())), preferred_element_type=F32)


def _dot_tn(a, b):
    return lax.dot_general(a, b, (((0,), (0,)), ((), ())), preferred_element_type=F32)


def _rwkv_body(z_ref, shift_ref, s0_ref, mu_ref, w0_ref, w2_ref, a0_ref, a2_ref, g2_ref, kk_ref, ka_ref,
               rk_ref, lng_ref, lnb_ref, o_ref, s_out_ref, shift_out_ref, state_ref, prev_ref, *, T):
    ci = pl.program_id(1)
    nc = pl.num_programs(1)
    W = BRANCH_W
    N = RW_HEADS * T

    @pl.when(ci == 0)
    def _():
        state_ref[...] = s0_ref[0]
        prev_ref[...] = shift_ref[0]

    z = z_ref[0]
    row = lax.broadcasted_iota(jnp.int32, (T, RW_COLS), 0)
    z_prev = jnp.where(row == 0, jnp.broadcast_to(prev_ref[...], (T, RW_COLS)), pltpu.roll(z, 1, axis=0))
    prev_ref[...] = z[T - 1:T, :]
    zs = z + (z_prev - z) * mu_ref[...]
    r = zs[:, 0:W]
    k = zs[:, W:2 * W]
    v = zs[:, 2 * W:3 * W]
    xwa = zs[:, 3 * W:3 * W + 128]
    xg = zs[:, 3 * W + 128:]
    dotf = partial(jnp.dot, preferred_element_type=F32, precision=HI)
    w_log = -jax.nn.softplus(-(w0_ref[...] + dotf(jnp.tanh(xwa), w2_ref[...]))) - 0.5
    logdec = -jnp.exp(w_log)
    a = jax.nn.sigmoid(a0_ref[...] + dotf(xwa, a2_ref[...]))
    g = dotf(jax.nn.sigmoid(xg), g2_ref[...])
    li = lax.broadcasted_iota(jnp.int32, (W, W), 0) // RW_HD
    lj = lax.broadcasted_iota(jnp.int32, (W, W), 1) // RW_HD
    head_sum = (li == lj).astype(F32)
    kk = k * kk_ref[...]
    kk = kk * lax.rsqrt(dotf(kk * kk, head_sum) + 1e-12)
    k2 = k * (1.0 + (a - 1.0) * ka_ref[...])
    beta = kk * a
    ti = lax.broadcasted_iota(jnp.int32, (T, T), 0)
    tj = lax.broadcasted_iota(jnp.int32, (T, T), 1)
    b = dotf((ti >= tj).astype(F32), logdec)
    b_last = b[T - 1:T, :]
    k_in = kk * jnp.exp(b - logdec)
    r_in = r * jnp.exp(b)
    inv_p = jnp.exp(-b)
    k_out = k2 * inv_p
    b_out = beta * inv_p
    to_end = jnp.exp(b_last - b)
    k_end = k2 * to_end
    b_end = beta * to_end

    sh = lax.broadcasted_iota(jnp.int32, (N, W), 0) // T
    sl = lax.broadcasted_iota(jnp.int32, (N, W), 1) // RW_HD
    own = sh == sl

    def stack(x, masked):
        xs = jnp.concatenate([x] * RW_HEADS, axis=0)
        return jnp.where(own, xs, 0.0) if masked else xs

    k_in_s = stack(k_in, True).astype(BF16)
    r_in_s = stack(r_in, True).astype(BF16)
    k_out_s = stack(k_out, False).astype(BF16)
    b_out_s = stack(b_out, False).astype(BF16)
    v_s = stack(v, True).astype(BF16)
    k_end_s = stack(k_end, True).astype(BF16)
    b_end_s = stack(b_end, True).astype(BF16)

    ri = lax.broadcasted_iota(jnp.int32, (N, N), 0)
    rj = lax.broadcasted_iota(jnp.int32, (N, N), 1)
    same_head = (ri // T) == (rj // T)
    strict = same_head & (ri > rj)
    incl = same_head & (ri >= rj)
    a_mat = jnp.where(strict, _dot_nt(k_in_s, b_out_s), 0.0)
    kk_mat = jnp.where(strict, _dot_nt(k_in_s, k_out_s), 0.0)
    rk_mat = jnp.where(incl, _dot_nt(r_in_s, k_out_s), 0.0)
    rb_mat = jnp.where(incl, _dot_nt(r_in_s, b_out_s), 0.0)

    eye = (ri == rj).astype(F32)
    a_bd = jnp.where((ri // RW_SUB) == (rj // RW_SUB), a_mat, 0.0)
    x = eye - a_bd
    p = _mm3(a_bd, a_bd)
    x = x + _mm3(x, p)
    p = _mm3(p, p)
    x = x + _mm3(x, p)
    p = _mm3(p, p)
    x = x + _mm3(x, p)
    size = RW_SUB
    while size < T:
        lower = ((ri // (2 * size)) == (rj // (2 * size))) & ((ri // size) > (rj // size))
        x = x - _mm3(_mm3(x, jnp.where(lower, a_mat, 0.0)), x)
        size *= 2

    state = state_ref[...]
    state_b = state.astype(BF16)
    d = partial(jnp.dot, preferred_element_type=F32)
    rhs = _dot_nt(k_in_s, state_b) + d(kk_mat.astype(BF16), v_s)
    u = _mm3(x, rhs)
    u_b = u.astype(BF16)
    o_s = _dot_nt(r_in_s, state_b) + d(rk_mat.astype(BF16), v_s) - d(rb_mat.astype(BF16), u_b)
    o = o_s[0:T]
    for h in range(1, RW_HEADS):
        o = o + o_s[h * T:(h + 1) * T]
    state_ref[...] = state * jnp.exp(b_last) + _dot_tn(v_s, k_end_s) - _dot_tn(u_b, b_end_s)

    head_mean = head_sum * (1.0 / RW_HD)
    mean = dotf(o, head_mean)
    var = dotf(jnp.square(o - mean), head_mean)
    o = (o - mean) * lax.rsqrt(var + RW_GN_EPS) * lng_ref[...] + lnb_ref[...]
    bonus = dotf(r * k2 * rk_ref[...], head_sum) * v
    o_ref[0] = ((o + bonus) * g).astype(o_ref.dtype)

    @pl.when(ci == nc - 1)
    def _():
        s_out_ref[0] = state_ref[...]
        shift_out_ref[0] = z[T - 1:T, :]


def _rwkv7_pallas(z, s0_bd, shift_prev, mu, w0, w2, a0, a2, g2, k_k, k_a, r_k, ln_g, ln_b, *, T, out_dtype=F32):
    B, L, C = z.shape
    assert C == RW_COLS and L % T == 0 and T % RW_SUB == 0
    W = BRANCH_W
    w2p = jnp.concatenate([w2, jnp.zeros_like(w2)], axis=0)
    a2p = jnp.concatenate([jnp.zeros_like(a2), a2], axis=0)
    vec = lambda t: t.reshape(1, -1).astype(F32)
    full = lambda shape: pl.BlockSpec(shape, lambda b, c: (0,) * len(shape))
    return pl.pallas_call(
        partial(_rwkv_body, T=T),
        grid=(B, L // T),
        in_specs=[pl.BlockSpec((1, T, C), lambda b, c: (b, c, 0)),
                  pl.BlockSpec((1, 1, C), lambda b, c: (b, 0, 0)),
                  pl.BlockSpec((1, W, W), lambda b, c: (b, 0, 0)),
                  full((1, C)), full((1, W)), full((128, W)), full((1, W)), full((128, W)), full((128, W)),
                  full((1, W)), full((1, W)), full((1, W)), full((1, W)), full((1, W))],
        out_specs=[pl.BlockSpec((1, T, W), lambda b, c: (b, c, 0)),
                   pl.BlockSpec((1, W, W), lambda b, c: (b, 0, 0)),
                   pl.BlockSpec((1, 1, C), lambda b, c: (b, 0, 0))],
        out_shape=[jax.ShapeDtypeStruct((B, L, W), out_dtype),
                   jax.ShapeDtypeStruct((B, W, W), F32),
                   jax.ShapeDtypeStruct((B, 1, C), F32)],
        scratch_shapes=[pltpu.VMEM((W, W), F32), pltpu.VMEM((1, C), F32)],
        compiler_params=pltpu.CompilerParams(dimension_semantics=("arbitrary", "arbitrary")),
        name="rwkv7_chunked",
    )(z, shift_prev.reshape(B, 1, C), s0_bd, vec(mu), vec(w0), w2p, vec(a0), a2p, g2, vec(k_k), vec(k_a),
      vec(r_k), vec(ln_g), vec(ln_b))


def _state_to_bd(s):
    B = s.shape[0]
    eye = jnp.eye(RW_HEADS, dtype=s.dtype)
    return (s[:, :, :, None, :] * eye[None, :, None, :, None]).reshape(B, BRANCH_W, BRANCH_W)


def _state_from_bd(s_bd):
    B = s_bd.shape[0]
    s5 = s_bd.reshape(B, RW_HEADS, RW_HD, RW_HEADS, RW_HD)
    return jnp.stack([s5[:, h, :, h, :] for h in range(RW_HEADS)], axis=1)


def _gla_chunked(q, k, v, log_f, S0, blk):
    B_, L, H_, _ = q.shape
    n = L // blk

    def to_blocks(t):
        return t.reshape(B_, n, blk, H_, t.shape[-1]).transpose(1, 0, 3, 2, 4)

    tri = jnp.tril(jnp.ones((blk, blk), dtype=bool))

    def step(S, xs):
        qb, kb, vb, gb = xs
        b = jnp.cumsum(gb, axis=2)
        dec = jnp.exp(jnp.where(tri[None, None, :, :, None],
                                b[:, :, :, None, :] - b[:, :, None, :, :], -jnp.inf))
        A = jnp.einsum('bhtc,bhtsc,bhsc->bhts', qb, dec, kb)
        o = jnp.einsum('bhts,bhsv->bhtv', A, vb) + jnp.einsum('bhtc,bhcv->bhtv', qb * jnp.exp(b), S)
        bL = b[:, :, -1:, :]
        S_new = jnp.exp(bL[:, :, 0, :])[..., None] * S + jnp.einsum('bhsc,bhsv->bhcv', kb * jnp.exp(bL - b), vb)
        return S_new, o

    S_fin, o = lax.scan(step, S0, (to_blocks(q), to_blocks(k), to_blocks(v), to_blocks(log_f)))
    return o.transpose(1, 0, 3, 2, 4).reshape(B_, L, H_, v.shape[-1]), S_fin


def _hgrn2(z, S0, lb, norm_g):
    B_, L, _ = z.shape
    zq, zf, zi, zg = jnp.split(z.astype(F32), 4, axis=-1)
    lbf = lb.astype(F32)
    log_f = jnp.logaddexp(jnp.log(lbf), jnp.log1p(-lbf) + jax.nn.log_sigmoid(zf))
    k = -jnp.expm1(log_f)

    def hd(t):
        return t.reshape(B_, L, HG_HEADS, HG_DK)

    blk = CHUNK if L % CHUNK == 0 else L
    o, S = _gla_chunked(hd(jax.nn.silu(zq)), hd(k), hd(zi), hd(log_f), S0.astype(F32), blk)
    o = _rmsnorm(o, norm_g) * jax.nn.silu(zg).reshape(B_, L, HG_HEADS, HG_DV)
    return o.reshape(B_, L, BRANCH_W).astype(z.dtype), S


def _conformer_conv(z, buf, w, b, ln_g, ln_b):
    zf = z.astype(F32)
    val, gate = jnp.split(zf, 2, axis=-1)
    h = val * jax.nn.sigmoid(gate)
    hp = jnp.concatenate([buf.astype(F32), h], axis=1)
    y = lax.conv_general_dilated(hp, w.astype(F32)[:, None, :], window_strides=(1,), padding='VALID',
                                 dimension_numbers=('NWC', 'WIO', 'NWC'),
                                 feature_group_count=CONV_CH) + b.astype(F32)
    mu = jnp.mean(y, axis=-1, keepdims=True)
    var = jnp.mean(jnp.square(y - mu), axis=-1, keepdims=True)
    y = (y - mu) * lax.rsqrt(var + EPS) * ln_g.astype(F32) + ln_b.astype(F32)
    y = jax.nn.silu(y)
    return y.astype(z.dtype), hp[:, -(CONV_WIDTH - 1):].astype(z.dtype)


def _peer_ffn(x, wq, qn_g, sub_keys, u_tab, v_tab):
    B_, L, D_ = x.shape
    n_tok = B_ * L
    pad = (-n_tok) % PEER_BLOCK
    xb = jnp.pad(x.reshape(n_tok, D_), ((0, pad), (0, 0))).reshape(-1, PEER_BLOCK, D_)
    half = PK_DQ // 2
    k1 = sub_keys[:, 0].astype(F32)
    k2 = sub_keys[:, 1].astype(F32)

    def one(xi):
        qh = _rmsnorm((xi @ wq).reshape(PEER_BLOCK, PK_HEADS, PK_DQ), qn_g).astype(F32)
        s1 = jnp.einsum('thc,hkc->thk', qh[..., :half], k1)
        s2 = jnp.einsum('thc,hkc->thk', qh[..., half:], k2)
        v1, i1 = lax.top_k(s1, PK_TOPK)
        v2, i2 = lax.top_k(s2, PK_TOPK)
        cand_s = (v1[..., :, None] + v2[..., None, :]).reshape(PEER_BLOCK, PK_HEADS, PK_TOPK * PK_TOPK)
        cand_i = (i1[..., :, None] * PK_NKEYS + i2[..., None, :]).reshape(PEER_BLOCK, PK_HEADS, PK_TOPK * PK_TOPK)
        top_s, top_p = lax.top_k(cand_s, PK_TOPK)
        expert = jnp.take_along_axis(cand_i, top_p, axis=-1)
        gate = jax.nn.softmax(top_s, axis=-1)
        act = jax.nn.gelu(jnp.einsum('thkd,td->thk', u_tab[expert].astype(F32), xi.astype(F32)))
        return jnp.einsum('thk,thkd->td', gate * act, v_tab[expert].astype(F32)).astype(xi.dtype)

    y = lax.map(one, xb).reshape(-1, D_)[:n_tok]
    return y.reshape(B_, L, D_)


def _trunk_layer(x, k_past, v_past, s_hg, buf_conv, s_rw, buf_shift, p):
    B_, L, _ = x.shape
    h = _rmsnorm(x, p['norm_mix_g'])
    z = h @ p['w_in']
    z_sb, z_hg, z_cv, z_rw, z_gate = jnp.split(z, IN_SPLITS, axis=-1)
    q, k, v = [t.reshape(B_, L, SB_HEADS, HEAD_DIM) for t in jnp.split(z_sb, 3, axis=-1)]
    qb = (q.reshape(B_, L, BRANCH_W) * (HEAD_DIM ** -0.5)).astype(BF16)
    kb = k.reshape(B_, L, BRANCH_W)
    vb = v.reshape(B_, L, BRANCH_W)
    if k_past is None:
        o_sb = _sb_attention(qb, kb.astype(BF16), vb.astype(BF16), off=0, tq=SB_TQ, tk=SB_TK, out_dtype=F32)
    else:
        past = k_past.shape[1]
        pad = (-(past + L)) % SB_TK
        kc = jnp.concatenate([k_past.reshape(B_, past, BRANCH_W), kb, jnp.zeros((B_, pad, BRANCH_W), F32)], axis=1)
        vc = jnp.concatenate([v_past.reshape(B_, past, BRANCH_W), vb, jnp.zeros((B_, pad, BRANCH_W), F32)], axis=1)
        o_sb = _sb_attention(qb, kc.astype(BF16), vc.astype(BF16), off=past, tq=L, tk=SB_TK, out_dtype=F32)
    o_hg, s_hg_new = _hgrn2(z_hg, s_hg, p['lb'], p['hg_norm_g'])
    o_cv, buf_conv_new = _conformer_conv(z_cv, buf_conv, p['conv_w'], p['conv_b'], p['conv_ln_g'], p['conv_ln_b'])
    o_rw, s_rw_bd, shift_new = _rwkv7_pallas(z_rw, _state_to_bd(s_rw.astype(F32)), buf_shift, p['rw_mu'], p['rw_w0'],
                                             p['rw_w2'], p['rw_a0'], p['rw_a2'], p['rw_g2'], p['rw_k_k'], p['rw_k_a'],
                                             p['rw_r_k'], p['rw_ln_g'], p['rw_ln_b'],
                                             T=RW_CHUNK if L % RW_CHUNK == 0 else L)
    s_rw_new = _state_from_bd(s_rw_bd)
    buf_shift_new = shift_new[:, 0]
    branches = jnp.stack([o_sb, o_hg, o_cv, o_rw], axis=2)
    gates = jax.nn.sigmoid(z_gate.reshape(B_, L, N_BRANCH, D_MODEL))
    merged = jnp.sum(jnp.einsum('blnc,ncd->blnd', branches, p['w_branch']) * gates, axis=2)
    x = x + merged @ p['w_out']
    x = x + _peer_ffn(_rmsnorm(x, p['norm_ffn_g']), p['peer_wq'], p['peer_qn_g'], p['peer_keys'],
                      p['peer_u'], p['peer_v'])
    return x, (k, v, s_hg_new, buf_conv_new, s_rw_new, buf_shift_new)


def kernel(x_prompt, x_sample, cache_sb_k, cache_sb_v, state_hgrn, state_conv, state_rwkv, state_shift,
           norm_mix_g, w_in, hg_lb_logits, hg_norm_g, conv_w, conv_b, conv_ln_g, conv_ln_b,
           rw_mu, rw_w0, rw_w2, rw_a0, rw_a2, rw_g2, rw_k_k, rw_k_a, rw_r_k, rw_ln_g, rw_ln_b,
           w_branch, w_out, norm_ffn_g, peer_wq, peer_qn_g, peer_keys, peer_u, peer_v, final_norm_g):
    lb_all = jnp.cumsum(jax.nn.softmax(hg_lb_logits.astype(F32), axis=0), axis=0)
    lb_all = lb_all - lb_all[0:1]
    Bp = x_prompt.shape[0]
    hg0 = jnp.zeros((Bp, HG_HEADS, HG_DK, HG_DV), F32)
    cv0 = jnp.zeros((Bp, CONV_WIDTH - 1, CONV_CH), x_prompt.dtype)
    rw0 = jnp.zeros((Bp, RW_HEADS, RW_HD, RW_HD), F32)
    sh0 = jnp.zeros((Bp, RW_COLS), x_prompt.dtype)
    xp, xs = x_prompt, x_sample
    outs_p, outs_s = [], []
    for l in range(DEPTH):
        lp = dict(norm_mix_g=norm_mix_g[l], w_in=w_in[l], lb=lb_all[l], hg_norm_g=hg_norm_g[l],
                  conv_w=conv_w[l], conv_b=conv_b[l], conv_ln_g=conv_ln_g[l], conv_ln_b=conv_ln_b[l],
                  rw_mu=rw_mu[l], rw_w0=rw_w0[l], rw_w2=rw_w2[l], rw_a0=rw_a0[l], rw_a2=rw_a2[l],
                  rw_g2=rw_g2[l], rw_k_k=rw_k_k[l], rw_k_a=rw_k_a[l], rw_r_k=rw_r_k[l],
                  rw_ln_g=rw_ln_g[l], rw_ln_b=rw_ln_b[l], w_branch=w_branch[l], w_out=w_out[l],
                  norm_ffn_g=norm_ffn_g[l], peer_wq=peer_wq[l], peer_qn_g=peer_qn_g[l],
                  peer_keys=peer_keys[l], peer_u=peer_u[l], peer_v=peer_v[l])
        xp, st_p = _trunk_layer(xp, None, None, hg0, cv0, rw0, sh0, lp)
        xs, st_s = _trunk_layer(xs, cache_sb_k[l], cache_sb_v[l], state_hgrn[l], state_conv[l],
                                state_rwkv[l], state_shift[l], lp)
        outs_p.append(st_p)
        outs_s.append(st_s)

    def stk(outs, i):
        return jnp.stack([o[i] for o in outs], axis=0)

    y_prompt = _rmsnorm_pallas(xp, final_norm_g)
    y_sample = _rmsnorm_pallas(xs, final_norm_g)
    return (y_prompt, y_sample,
            stk(outs_p, 0), stk(outs_p, 1), stk(outs_p, 2), stk(outs_p, 3), stk(outs_p, 4), stk(outs_p, 5),
            stk(outs_s, 0), stk(outs_s, 1), stk(outs_s, 2), stk(outs_s, 3), stk(outs_s, 4), stk(outs_s, 5))
```

```python
from functools import partial

import jax
import jax.numpy as jnp
from jax import lax
from jax.experimental import pallas as pl
from jax.experimental.pallas import tpu as pltpu

D_MODEL = 1024
DEPTH = 2
CHUNK = 64
HEAD_DIM = 64
N_BRANCH = 4
BRANCH_W = D_MODEL // 4
SB_HEADS = BRANCH_W // HEAD_DIM
HG_HEADS = 4
HG_DK = BRANCH_W // HG_HEADS
HG_DV = BRANCH_W // HG_HEADS
CONV_CH = BRANCH_W
CONV_WIDTH = 31
RW_HEADS = 4
RW_HD = BRANCH_W // RW_HEADS
RW_DECAY_LORA = 64
RW_AAA_LORA = 64
RW_GATE_LORA = 128
SB_COLS = 3 * BRANCH_W
HG_COLS = 4 * BRANCH_W
CV_COLS = 2 * CONV_CH
RW_COLS = 3 * BRANCH_W + RW_DECAY_LORA + RW_AAA_LORA + RW_GATE_LORA
GATE_COLS = N_BRANCH * D_MODEL
IN_COLS = SB_COLS + HG_COLS + CV_COLS + RW_COLS + GATE_COLS
IN_SPLITS = (SB_COLS, SB_COLS + HG_COLS, SB_COLS + HG_COLS + CV_COLS,
             SB_COLS + HG_COLS + CV_COLS + RW_COLS)
RW_SPLITS = (BRANCH_W, 2 * BRANCH_W, 3 * BRANCH_W, 3 * BRANCH_W + RW_DECAY_LORA,
             3 * BRANCH_W + RW_DECAY_LORA + RW_AAA_LORA)
Q_BLOCK = 128
PK_HEADS = 8
PK_NKEYS = 128
PK_EXPERTS = PK_NKEYS * PK_NKEYS
PK_DQ = 256
PK_TOPK = 16
PEER_BLOCK = 128
EPS = 1e-6
RW_GN_EPS = 64e-5
SB_TQ = 256
SB_TK = 256
RW_CHUNK = 64
F32 = jnp.float32
BF16 = jnp.bfloat16


def _rmsnorm(x, g):
    xf = x.astype(F32)
    y = xf * lax.rsqrt(jnp.mean(xf * xf, axis=-1, keepdims=True) + EPS)
    return (y * g.astype(F32)).astype(x.dtype)


def _rmsnorm_body(x_ref, g_ref, o_ref):
    x = x_ref[...]
    y = x * lax.rsqrt(jnp.mean(x * x, axis=-1, keepdims=True) + EPS)
    o_ref[...] = y * g_ref[...]


def _rmsnorm_pallas(x, g, rows=512):
    shape = x.shape
    x2 = x.reshape(-1, shape[-1])
    n, d = x2.shape
    rows = min(rows, n)
    out = pl.pallas_call(
        _rmsnorm_body,
        grid=(n // rows,),
        in_specs=[pl.BlockSpec((rows, d), lambda i: (i, 0)),
                  pl.BlockSpec((1, d), lambda i: (0, 0))],
        out_specs=pl.BlockSpec((rows, d), lambda i: (i, 0)),
        out_shape=jax.ShapeDtypeStruct((n, d), x.dtype),
        name="final_rmsnorm",
    )(x2, g.reshape(1, d))
    return out.reshape(shape)


VMEM_LIMIT_V7X = 48 * 1024 * 1024


def _sb_attn_body(q_ref, k_ref, v_ref, o_ref, acc_ref, carry_ref, *, tq, tk, off, n_diag):
    qi = pl.program_id(1)
    q0 = off + qi * tq
    n_full = q0 // tk
    acc_ref[...] = jnp.zeros_like(acc_ref)
    carry_ref[...] = jnp.zeros_like(carry_ref)
    q_all = q_ref[0]
    q_heads = [q_all[:, h * HEAD_DIM:(h + 1) * HEAD_DIM] for h in range(SB_HEADS)]
    jj = lax.broadcasted_iota(jnp.int32, (tk, tk), 0)
    ss = lax.broadcasted_iota(jnp.int32, (tk, tk), 1)
    later_mat = (jj > ss).astype(BF16)

    def block(kb, masked):
        start = pl.multiple_of(kb * tk, tk)
        k_blk = k_ref[0, pl.ds(start, tk), :]
        v_blk = v_ref[0, pl.ds(start, tk), :]
        if masked:
            key_pos = start + lax.broadcasted_iota(jnp.int32, (tq, tk), 1)
            q_pos = q0 + lax.broadcasted_iota(jnp.int32, (tq, tk), 0)
            mask = key_pos < q_pos
        outs = []
        for h in range(SB_HEADS):
            kh = k_blk[:, h * HEAD_DIM:(h + 1) * HEAD_DIM]
            vh = v_blk[:, h * HEAD_DIM:(h + 1) * HEAD_DIM]
            l = lax.dot_general(q_heads[h], kh, (((1,), (1,)), ((), ())), preferred_element_type=F32)
            lsm = -(jnp.maximum(l, 0.0) + jnp.log(1.0 + jnp.exp(-jnp.abs(l))))
            if masked:
                lsm = jnp.where(mask, lsm, 0.0)
            later = jnp.dot(lsm.astype(BF16), later_mat, preferred_element_type=F32)
            carry = carry_ref[h]
            expo = l + lsm + later + jnp.concatenate([carry] * (tk // 128), axis=1)
            w = jnp.exp(expo)
            if masked:
                w = jnp.where(mask, w, 0.0)
            outs.append(jnp.dot(w.astype(BF16), vh, preferred_element_type=F32))
            row = later[:, 0:1] + lsm[:, 0:1]
            carry_ref[h] = carry + jnp.broadcast_to(row, carry.shape)
        acc_ref[...] += jnp.concatenate(outs, axis=1)

    for d in range(n_diag - 1, -1, -1):
        block(n_full + d, True)

    def full_step(i, c):
        block(n_full - 1 - i, False)
        return c

    lax.fori_loop(0, n_full, full_step, 0)
    o_ref[0] = acc_ref[...].astype(o_ref.dtype)


def _sb_attention(q, k, v, *, off, tq, tk, out_dtype=BF16):
    B, Lq, W = q.shape
    Lk = k.shape[1]
    assert W == BRANCH_W and Lq % tq == 0 and Lk % tk == 0 and tk % 128 == 0
    nq = Lq // tq
    assert nq == 1 or (tq % tk == 0 and off % tk == 0)
    n_diag = -(-((off % tk) + tq - 1) // tk)
    assert (off + Lq - 1 + tk - 1) // tk <= Lk // tk
    body = partial(_sb_attn_body, tq=tq, tk=tk, off=off, n_diag=n_diag)
    return pl.pallas_call(
        body,
        grid=(B, nq),
        in_specs=[pl.BlockSpec((1, tq, W), lambda b, i: (b, i, 0)),
                  pl.BlockSpec((1, Lk, W), lambda b, i: (b, 0, 0)),
                  pl.BlockSpec((1, Lk, W), lambda b, i: (b, 0, 0))],
        out_specs=pl.BlockSpec((1, tq, W), lambda b, i: (b, i, 0)),
        out_shape=jax.ShapeDtypeStruct((B, Lq, W), out_dtype),
        scratch_shapes=[pltpu.VMEM((tq, W), F32), pltpu.VMEM((SB_HEADS, tq, 128), F32)],
        compiler_params=pltpu.CompilerParams(dimension_semantics=("arbitrary", "arbitrary"),
                                             vmem_limit_bytes=VMEM_LIMIT_V7X),
        name="sb_attention",
    )(q, k, v)


RW_SUB = 16
HI = lax.Precision.HIGHEST


def _split(x):
    hi = x.astype(BF16)
    lo = (x - hi.astype(F32)).astype(BF16)
    return hi, lo


def _mm3(a, b):
    ah, al = _split(a)
    bh, bl = _split(b)
    d = partial(jnp.dot, preferred_element_type=F32)
    return d(ah, bh) + (d(ah, bl) + d(al, bh))


def _dot_nt(a, b):
    return lax.dot_general(a, b, (((1,), (1,)), ((), ())), preferred_element_type=F32)


def _dot_tn(a, b):
    return lax.dot_general(a, b, (((0,), (0,)), ((), ())), preferred_element_type=F32)


def _rwkv_body(z_ref, shift_ref, s0_ref, mu_ref, w0_ref, w2_ref, a0_ref, a2_ref, g2_ref, kk_ref, ka_ref,
               rk_ref, lng_ref, lnb_ref, o_ref, s_out_ref, shift_out_ref, state_ref, prev_ref, *, T):
    ci = pl.program_id(1)
    nc = pl.num_programs(1)
    W = BRANCH_W
    N = RW_HEADS * T

    @pl.when(ci == 0)
    def _():
        state_ref[...] = s0_ref[0]
        prev_ref[...] = shift_ref[0]

    z = z_ref[0]
    row = lax.broadcasted_iota(jnp.int32, (T, RW_COLS), 0)
    z_prev = jnp.where(row == 0, jnp.broadcast_to(prev_ref[...], (T, RW_COLS)), pltpu.roll(z, 1, axis=0))
    prev_ref[...] = z[T - 1:T, :]
    zs = z + (z_prev - z) * mu_ref[...]
    r = zs[:, 0:W]
    k = zs[:, W:2 * W]
    v = zs[:, 2 * W:3 * W]
    xwa = zs[:, 3 * W:3 * W + 128]
    xg = zs[:, 3 * W + 128:]
    dotf = partial(jnp.dot, preferred_element_type=F32, precision=HI)
    w_log = -jax.nn.softplus(-(w0_ref[...] + dotf(jnp.tanh(xwa), w2_ref[...]))) - 0.5
    logdec = -jnp.exp(w_log)
    a = jax.nn.sigmoid(a0_ref[...] + dotf(xwa, a2_ref[...]))
    g = dotf(jax.nn.sigmoid(xg), g2_ref[...])
    li = lax.broadcasted_iota(jnp.int32, (W, W), 0) // RW_HD
    lj = lax.broadcasted_iota(jnp.int32, (W, W), 1) // RW_HD
    head_sum = (li == lj).astype(F32)
    kk = k * kk_ref[...]
    kk = kk * lax.rsqrt(dotf(kk * kk, head_sum) + 1e-12)
    k2 = k * (1.0 + (a - 1.0) * ka_ref[...])
    beta = kk * a
    ti = lax.broadcasted_iota(jnp.int32, (T, T), 0)
    tj = lax.broadcasted_iota(jnp.int32, (T, T), 1)
    b = dotf((ti >= tj).astype(F32), logdec)
    b_last = b[T - 1:T, :]
    k_in = kk * jnp.exp(b - logdec)
    r_in = r * jnp.exp(b)
    inv_p = jnp.exp(-b)
    k_out = k2 * inv_p
    b_out = beta * inv_p
    to_end = jnp.exp(b_last - b)
    k_end = k2 * to_end
    b_end = beta * to_end

    sh = lax.broadcasted_iota(jnp.int32, (N, W), 0) // T
    sl = lax.broadcasted_iota(jnp.int32, (N, W), 1) // RW_HD
    own = sh == sl

    def stack(x, masked):
        xs = jnp.concatenate([x] * RW_HEADS, axis=0)
        return jnp.where(own, xs, 0.0) if masked else xs

    k_in_s = stack(k_in, True).astype(BF16)
    r_in_s = stack(r_in, True).astype(BF16)
    k_out_s = stack(k_out, False).astype(BF16)
    b_out_s = stack(b_out, False).astype(BF16)
    v_s = stack(v, True).astype(BF16)
    k_end_s = stack(k_end, True).astype(BF16)
    b_end_s = stack(b_end, True).astype(BF16)

    ri = lax.broadcasted_iota(jnp.int32, (N, N), 0)
    rj = lax.broadcasted_iota(jnp.int32, (N, N), 1)
    same_head = (ri // T) == (rj // T)
    strict = same_head & (ri > rj)
    incl = same_head & (ri >= rj)
    a_mat = jnp.where(strict, _dot_nt(k_in_s, b_out_s), 0.0)
    kk_mat = jnp.where(strict, _dot_nt(k_in_s, k_out_s), 0.0)
    rk_mat = jnp.where(incl, _dot_nt(r_in_s, k_out_s), 0.0)
    rb_mat = jnp.where(incl, _dot_nt(r_in_s, b_out_s), 0.0)

    eye = (ri == rj).astype(F32)
    a_bd = jnp.where((ri // RW_SUB) == (rj // RW_SUB), a_mat, 0.0)
    x = eye - a_bd
    p = _mm3(a_bd, a_bd)
    x = x + _mm3(x, p)
    p = _mm3(p, p)
    x = x + _mm3(x, p)
    p = _mm3(p, p)
    x = x + _mm3(x, p)
    size = RW_SUB
    while size < T:
        lower = ((ri // (2 * size)) == (rj // (2 * size))) & ((ri // size) > (rj // size))
        x = x - _mm3(_mm3(x, jnp.where(lower, a_mat, 0.0)), x)
        size *= 2

    state = state_ref[...]
    state_b = state.astype(BF16)
    d = partial(jnp.dot, preferred_element_type=F32)
    rhs = _dot_nt(k_in_s, state_b) + d(kk_mat.astype(BF16), v_s)
    u = _mm3(x, rhs)
    u_b = u.astype(BF16)
    o_s = _dot_nt(r_in_s, state_b) + d(rk_mat.astype(BF16), v_s) - d(rb_mat.astype(BF16), u_b)
    o = o_s[0:T]
    for h in range(1, RW_HEADS):
        o = o + o_s[h * T:(h + 1) * T]
    state_ref[...] = state * jnp.exp(b_last) + _dot_tn(v_s, k_end_s) - _dot_tn(u_b, b_end_s)

    head_mean = head_sum * (1.0 / RW_HD)
    mean = dotf(o, head_mean)
    var = dotf(jnp.square(o - mean), head_mean)
    o = (o - mean) * lax.rsqrt(var + RW_GN_EPS) * lng_ref[...] + lnb_ref[...]
    bonus = dotf(r * k2 * rk_ref[...], head_sum) * v
    o_ref[0] = ((o + bonus) * g).astype(o_ref.dtype)

    @pl.when(ci == nc - 1)
    def _():
        s_out_ref[0] = state_ref[...]
        shift_out_ref[0] = z[T - 1:T, :]


def _rwkv7_pallas(z, s0_bd, shift_prev, mu, w0, w2, a0, a2, g2, k_k, k_a, r_k, ln_g, ln_b, *, T, out_dtype=F32):
    B, L, C = z.shape
    assert C == RW_COLS and L % T == 0 and T % RW_SUB == 0
    W = BRANCH_W
    w2p = jnp.concatenate([w2, jnp.zeros_like(w2)], axis=0)
    a2p = jnp.concatenate([jnp.zeros_like(a2), a2], axis=0)
    vec = lambda t: t.reshape(1, -1).astype(F32)
    full = lambda shape: pl.BlockSpec(shape, lambda b, c: (0,) * len(shape))
    return pl.pallas_call(
        partial(_rwkv_body, T=T),
        grid=(B, L // T),
        in_specs=[pl.BlockSpec((1, T, C), lambda b, c: (b, c, 0)),
                  pl.BlockSpec((1, 1, C), lambda b, c: (b, 0, 0)),
                  pl.BlockSpec((1, W, W), lambda b, c: (b, 0, 0)),
                  full((1, C)), full((1, W)), full((128, W)), full((1, W)), full((128, W)), full((128, W)),
                  full((1, W)), full((1, W)), full((1, W)), full((1, W)), full((1, W))],
        out_specs=[pl.BlockSpec((1, T, W), lambda b, c: (b, c, 0)),
                   pl.BlockSpec((1, W, W), lambda b, c: (b, 0, 0)),
                   pl.BlockSpec((1, 1, C), lambda b, c: (b, 0, 0))],
        out_shape=[jax.ShapeDtypeStruct((B, L, W), out_dtype),
                   jax.ShapeDtypeStruct((B, W, W), F32),
                   jax.ShapeDtypeStruct((B, 1, C), F32)],
        scratch_shapes=[pltpu.VMEM((W, W), F32), pltpu.VMEM((1, C), F32)],
        compiler_params=pltpu.CompilerParams(dimension_semantics=("arbitrary", "arbitrary")),
        name="rwkv7_chunked",
    )(z, shift_prev.reshape(B, 1, C), s0_bd, vec(mu), vec(w0), w2p, vec(a0), a2p, g2, vec(k_k), vec(k_a),
      vec(r_k), vec(ln_g), vec(ln_b))


def _state_to_bd(s):
    B = s.shape[0]
    eye = jnp.eye(RW_HEADS, dtype=s.dtype)
    return (s[:, :, :, None, :] * eye[None, :, None, :, None]).reshape(B, BRANCH_W, BRANCH_W)


def _state_from_bd(s_bd):
    B = s_bd.shape[0]
    s5 = s_bd.reshape(B, RW_HEADS, RW_HD, RW_HEADS, RW_HD)
    return jnp.stack([s5[:, h, :, h, :] for h in range(RW_HEADS)], axis=1)


def _gla_chunked(q, k, v, log_f, S0, blk):
    B_, L, H_, _ = q.shape
    n = L // blk

    def to_blocks(t):
        return t.reshape(B_, n, blk, H_, t.shape[-1]).transpose(1, 0, 3, 2, 4)

    tri = jnp.tril(jnp.ones((blk, blk), dtype=bool))

    def step(S, xs):
        qb, kb, vb, gb = xs
        b = jnp.cumsum(gb, axis=2)
        dec = jnp.exp(jnp.where(tri[None, None, :, :, None],
                                b[:, :, :, None, :] - b[:, :, None, :, :], -jnp.inf))
        A = jnp.einsum('bhtc,bhtsc,bhsc->bhts', qb, dec, kb)
        o = jnp.einsum('bhts,bhsv->bhtv', A, vb) + jnp.einsum('bhtc,bhcv->bhtv', qb * jnp.exp(b), S)
        bL = b[:, :, -1:, :]
        S_new = jnp.exp(bL[:, :, 0, :])[..., None] * S + jnp.einsum('bhsc,bhsv->bhcv', kb * jnp.exp(bL - b), vb)
        return S_new, o

    S_fin, o = lax.scan(step, S0, (to_blocks(q), to_blocks(k), to_blocks(v), to_blocks(log_f)))
    return o.transpose(1, 0, 3, 2, 4).reshape(B_, L, H_, v.shape[-1]), S_fin


def _hgrn2(z, S0, lb, norm_g):
    B_, L, _ = z.shape
    zq, zf, zi, zg = jnp.split(z.astype(F32), 4, axis=-1)
    lbf = lb.astype(F32)
    log_f = jnp.logaddexp(jnp.log(lbf), jnp.log1p(-lbf) + jax.nn.log_sigmoid(zf))
    k = -jnp.expm1(log_f)

    def hd(t):
        return t.reshape(B_, L, HG_HEADS, HG_DK)

    blk = CHUNK if L % CHUNK == 0 else L
    o, S = _gla_chunked(hd(jax.nn.silu(zq)), hd(k), hd(zi), hd(log_f), S0.astype(F32), blk)
    o = _rmsnorm(o, norm_g) * jax.nn.silu(zg).reshape(B_, L, HG_HEADS, HG_DV)
    return o.reshape(B_, L, BRANCH_W).astype(z.dtype), S


def _conformer_conv(z, buf, w, b, ln_g, ln_b):
    zf = z.astype(F32)
    val, gate = jnp.split(zf, 2, axis=-1)
    h = val * jax.nn.sigmoid(gate)
    hp = jnp.concatenate([buf.astype(F32), h], axis=1)
    y = lax.conv_general_dilated(hp, w.astype(F32)[:, None, :], window_strides=(1,), padding='VALID',
                                 dimension_numbers=('NWC', 'WIO', 'NWC'),
                                 feature_group_count=CONV_CH) + b.astype(F32)
    mu = jnp.mean(y, axis=-1, keepdims=True)
    var = jnp.mean(jnp.square(y - mu), axis=-1, keepdims=True)
    y = (y - mu) * lax.rsqrt(var + EPS) * ln_g.astype(F32) + ln_b.astype(F32)
    y = jax.nn.silu(y)
    return y.astype(z.dtype), hp[:, -(CONV_WIDTH - 1):].astype(z.dtype)


PK_PAIRS = PK_HEADS * PK_TOPK
PEER_TB = 128
SUBLANES = 8
LANES = 128
ROW_TILES = D_MODEL // LANES
PEER_VMEM_LIMIT_V7X = 44 * 1024 * 1024


def _topk_rows(s, payload, k):
    n = s.shape[0]
    rows = lax.broadcasted_iota(jnp.int32, s.shape, 0)
    vals, ids = [], []
    for _ in range(k):
        m = jnp.max(s, axis=0, keepdims=True)
        pos = jnp.min(jnp.where(s == m, rows, n), axis=0, keepdims=True)
        sel = rows == pos
        ids.append(pos if payload is None else jnp.max(jnp.where(sel, payload, -1), axis=0, keepdims=True))
        vals.append(m)
        s = jnp.where(sel, -jnp.inf, s)
    return jnp.concatenate(vals, axis=0), jnp.concatenate(ids, axis=0)


def _dot3(ah, al, bh, bl, dims):
    d = partial(lax.dot_general, dimension_numbers=dims, preferred_element_type=F32)
    return d(ah, bh) + (d(ah, bl) + d(al, bh))


def _peer_route_body(x_ref, g_ref, wqh_ref, wql_ref, qg_ref, k1_ref, k2_ref, xn_ref, idx_ref, shf_ref, gate_ref):
    x = x_ref[...]
    xn = x * lax.rsqrt(jnp.mean(x * x, axis=-1, keepdims=True) + EPS) * g_ref[...]
    xn_ref[...] = xn
    nn = (((1,), (0,)), ((), ()))
    nt = (((1,), (1,)), ((), ()))
    xh, xl = _split(xn)
    q = _dot3(xh, xl, wqh_ref[...], wql_ref[...], nn)
    half = PK_DQ // 2
    experts, gates = [], []
    for h in range(PK_HEADS):
        qh = q[:, h * PK_DQ:(h + 1) * PK_DQ]
        qh = qh * lax.rsqrt(jnp.mean(qh * qh, axis=-1, keepdims=True) + EPS) * qg_ref[...]
        qhh, qhl = _split(qh)
        k1h, k1l = _split(k1_ref[h])
        k2h, k2l = _split(k2_ref[h])
        s1 = _dot3(k1h, k1l, qhh[:, :half], qhl[:, :half], nt)
        s2 = _dot3(k2h, k2l, qhh[:, half:], qhl[:, half:], nt)
        v1, i1 = _topk_rows(s1, None, PK_TOPK)
        v2, i2 = _topk_rows(s2, None, PK_TOPK)
        cand_s = jnp.concatenate([v1[a:a + 1, :] + v2 for a in range(PK_TOPK)], axis=0)
        cand_i = jnp.concatenate([i1[a:a + 1, :] * PK_NKEYS + i2 for a in range(PK_TOPK)], axis=0)
        top_s, expert = _topk_rows(cand_s, cand_i, PK_TOPK)
        e = jnp.exp(top_s - jnp.max(top_s, axis=0, keepdims=True))
        gates.append(e / jnp.sum(e, axis=0, keepdims=True))
        experts.append(expert)
    expert = jnp.concatenate(experts, axis=0)
    idx_ref[0] = expert >> 1
    shf_ref[0] = (1 - (expert & 1)) * 16
    gate_ref[...] = jnp.concatenate(gates, axis=0)


def _peer_route(x2, norm_g, wq, qn_g, k1, k2):
    n_tok = x2.shape[0]
    TB = PEER_TB
    assert n_tok % TB == 0
    nb = n_tok // TB
    full = lambda shape: pl.BlockSpec(shape, lambda i: (0,) * len(shape))
    wq_hi = wq.astype(BF16)
    wq_lo = (wq - wq_hi.astype(F32)).astype(BF16)
    return pl.pallas_call(
        _peer_route_body,
        grid=(nb,),
        in_specs=[pl.BlockSpec((TB, D_MODEL), lambda i: (i, 0)), full((1, D_MODEL)),
                  full((D_MODEL, PK_HEADS * PK_DQ)), full((D_MODEL, PK_HEADS * PK_DQ)), full((1, PK_DQ)),
                  full((PK_HEADS, PK_NKEYS, PK_DQ // 2)), full((PK_HEADS, PK_NKEYS, PK_DQ // 2))],
        out_specs=[pl.BlockSpec((TB, D_MODEL), lambda i: (i, 0)),
                   pl.BlockSpec((1, PK_PAIRS, TB), lambda i: (i, 0, 0)),
                   pl.BlockSpec((1, PK_PAIRS, TB), lambda i: (i, 0, 0)),
                   pl.BlockSpec((PK_PAIRS, TB), lambda i: (0, i))],
        out_shape=[jax.ShapeDtypeStruct((n_tok, D_MODEL), F32),
                   jax.ShapeDtypeStruct((nb, PK_PAIRS, TB), jnp.int32),
                   jax.ShapeDtypeStruct((nb, PK_PAIRS, TB), jnp.int32),
                   jax.ShapeDtypeStruct((PK_PAIRS, n_tok), F32)],
        compiler_params=pltpu.CompilerParams(dimension_semantics=("arbitrary",)),
        name="peer_route",
    )(x2, norm_g.reshape(1, -1), wq_hi, wq_lo, qn_g.reshape(1, -1), k1, k2)


def _row_from_packed(tab_ref, idx, shf):
    word = tab_ref[idx]
    return pltpu.bitcast((word << shf) & jnp.int32(-65536), F32)


_BITREV3 = (0, 4, 2, 6, 1, 5, 3, 7)


def _fold_sublanes(tiles):
    sub = lax.broadcasted_iota(jnp.int32, (SUBLANES, LANES), 0)
    t = [tiles[_BITREV3[i]] for i in range(8)]
    m4 = sub < 4
    lvl1 = []
    for i in range(0, 8, 2):
        a, b = t[i], t[i + 1]
        lvl1.append(jnp.where(m4, a, b) + pltpu.roll(jnp.where(m4, b, a), 4, axis=0))
    m2 = (sub & 3) < 2
    lvl2 = []
    for i in range(0, 4, 2):
        a, b = lvl1[i], lvl1[i + 1]
        lvl2.append(jnp.where(m2, a, b) + jnp.where(m2, pltpu.roll(a, 6, axis=0), pltpu.roll(b, 2, axis=0)))
    m1 = (sub & 1) == 0
    a, b = lvl2
    return jnp.where(m1, a, b) + jnp.where(m1, pltpu.roll(a, 7, axis=0), pltpu.roll(b, 1, axis=0))


def _peer_act_body(idx_ref, shf_ref, x_ref, gate_ref, tab_ref, w_ref, s_ref):
    TB = x_ref.shape[0]
    lane = lax.broadcasted_iota(jnp.int32, (PK_PAIRS, TB), 1)
    s_ref[...] = jnp.zeros_like(s_ref)

    def token(t, c):
        x = x_ref[t]
        parts = []
        for g in range(PK_PAIRS // SUBLANES):
            tiles = []
            for j in range(SUBLANES):
                p = g * SUBLANES + j
                tiles.append(_row_from_packed(tab_ref, idx_ref[0, 0, p * TB + t], shf_ref[0, 0, p * TB + t]) * x)
            parts.append(_fold_sublanes(tiles))
        s_col = jnp.sum(jnp.concatenate(parts, axis=0), axis=1, keepdims=True)
        s_ref[...] = jnp.where(lane == t, s_col, s_ref[...])
        return c

    lax.fori_loop(0, TB, token, 0)
    w_ref[...] = gate_ref[...] * jax.nn.gelu(s_ref[...])


def _peer_mix_body(idx_ref, shf_ref, w_ref, tab_ref, y_ref, wrep_ref):
    TB = y_ref.shape[0]
    lane = lax.broadcasted_iota(jnp.int32, (PK_PAIRS, TB), 1)

    def token(t, c):
        w_col = jnp.sum(jnp.where(lane == t, w_ref[...], 0.0), axis=1, keepdims=True)
        wrep_ref[...] = jnp.broadcast_to(w_col, (PK_PAIRS, LANES))
        accs = [jnp.zeros((SUBLANES, LANES), F32) for _ in range(4)]
        for p in range(PK_PAIRS):
            row = _row_from_packed(tab_ref, idx_ref[0, 0, p * TB + t], shf_ref[0, 0, p * TB + t])
            accs[p % 4] = accs[p % 4] + jnp.broadcast_to(wrep_ref[p:p + 1, :], (SUBLANES, LANES)) * row
        y_ref[t] = (accs[0] + accs[1]) + (accs[2] + accs[3])
        return c

    lax.fori_loop(0, TB, token, 0)


def _pack_table(tab):
    bits = lax.bitcast_convert_type(tab.astype(BF16), jnp.uint16).astype(jnp.uint32)
    bits = bits.reshape(PK_EXPERTS // 2, 2, ROW_TILES, LANES)
    word = bits[:, 0] | (bits[:, 1] << 16)
    return lax.bitcast_convert_type(word, jnp.int32)


def _smem_spec(TB):
    return pl.BlockSpec((1, 1, PK_PAIRS * TB), lambda i: (i, 0, 0), memory_space=pltpu.SMEM)


def _table_spec():
    return pl.BlockSpec((PK_EXPERTS // 2, ROW_TILES, LANES), lambda i: (0, 0, 0), pipeline_mode=pl.Buffered(1))


def _peer_act(idx, shf, xn3, gate_t, tab_packed):
    n_tok = xn3.shape[0]
    TB = PEER_TB
    nb = n_tok // TB
    return pl.pallas_call(
        _peer_act_body,
        grid=(nb,),
        in_specs=[_smem_spec(TB), _smem_spec(TB),
                  pl.BlockSpec((TB, ROW_TILES, LANES), lambda i: (i, 0, 0)),
                  pl.BlockSpec((PK_PAIRS, TB), lambda i: (0, i)),
                  _table_spec()],
        out_specs=pl.BlockSpec((PK_PAIRS, TB), lambda i: (0, i)),
        out_shape=jax.ShapeDtypeStruct((PK_PAIRS, n_tok), F32),
        scratch_shapes=[pltpu.VMEM((PK_PAIRS, TB), F32)],
        compiler_params=pltpu.CompilerParams(dimension_semantics=("arbitrary",),
                                             vmem_limit_bytes=PEER_VMEM_LIMIT_V7X),
        name="peer_act",
    )(idx.reshape(nb, 1, PK_PAIRS * TB), shf.reshape(nb, 1, PK_PAIRS * TB), xn3, gate_t, tab_packed)


def _peer_mix(idx, shf, w_t, tab_packed):
    n_tok = w_t.shape[1]
    TB = PEER_TB
    nb = n_tok // TB
    return pl.pallas_call(
        _peer_mix_body,
        grid=(nb,),
        in_specs=[_smem_spec(TB), _smem_spec(TB),
                  pl.BlockSpec((PK_PAIRS, TB), lambda i: (0, i)),
                  _table_spec()],
        out_specs=pl.BlockSpec((TB, ROW_TILES, LANES), lambda i: (i, 0, 0)),
        out_shape=jax.ShapeDtypeStruct((n_tok, ROW_TILES, LANES), F32),
        scratch_shapes=[pltpu.VMEM((PK_PAIRS, LANES), F32)],
        compiler_params=pltpu.CompilerParams(dimension_semantics=("arbitrary",),
                                             vmem_limit_bytes=PEER_VMEM_LIMIT_V7X),
        name="peer_mix",
    )(idx.reshape(nb, 1, PK_PAIRS * TB), shf.reshape(nb, 1, PK_PAIRS * TB), w_t, tab_packed)


def _peer_ffn_pallas(x, norm_g, wq, qn_g, sub_keys, u_packed, v_packed):
    B, L, D = x.shape
    n_tok = B * L
    pad = (-n_tok) % PEER_TB
    x2 = jnp.pad(x.reshape(n_tok, D), ((0, pad), (0, 0)))
    xn, idx, shf, gate_t = _peer_route(x2, norm_g, wq, qn_g, sub_keys[:, 0], sub_keys[:, 1])
    w_t = _peer_act(idx, shf, xn.reshape(-1, ROW_TILES, LANES), gate_t, u_packed)
    y = _peer_mix(idx, shf, w_t, v_packed)
    return y.reshape(-1, D)[:n_tok].reshape(B, L, D)


def _trunk_layer(x, k_past, v_past, s_hg, buf_conv, s_rw, buf_shift, p):
    B_, L, _ = x.shape
    h = _rmsnorm(x, p['norm_mix_g'])
    z = h @ p['w_in']
    z_sb, z_hg, z_cv, z_rw, z_gate = jnp.split(z, IN_SPLITS, axis=-1)
    q, k, v = [t.reshape(B_, L, SB_HEADS, HEAD_DIM) for t in jnp.split(z_sb, 3, axis=-1)]
    qb = (q.reshape(B_, L, BRANCH_W) * (HEAD_DIM ** -0.5)).astype(BF16)
    kb = k.reshape(B_, L, BRANCH_W)
    vb = v.reshape(B_, L, BRANCH_W)
    if k_past is None:
        o_sb = _sb_attention(qb, kb.astype(BF16), vb.astype(BF16), off=0, tq=SB_TQ, tk=SB_TK, out_dtype=F32)
    else:
        past = k_past.shape[1]
        pad = (-(past + L)) % SB_TK
        kc = jnp.concatenate([k_past.reshape(B_, past, BRANCH_W), kb, jnp.zeros((B_, pad, BRANCH_W), F32)], axis=1)
        vc = jnp.concatenate([v_past.reshape(B_, past, BRANCH_W), vb, jnp.zeros((B_, pad, BRANCH_W), F32)], axis=1)
        o_sb = _sb_attention(qb, kc.astype(BF16), vc.astype(BF16), off=past, tq=L, tk=SB_TK, out_dtype=F32)
    o_hg, s_hg_new = _hgrn2(z_hg, s_hg, p['lb'], p['hg_norm_g'])
    o_cv, buf_conv_new = _conformer_conv(z_cv, buf_conv, p['conv_w'], p['conv_b'], p['conv_ln_g'], p['conv_ln_b'])
    o_rw, s_rw_bd, shift_new = _rwkv7_pallas(z_rw, _state_to_bd(s_rw.astype(F32)), buf_shift, p['rw_mu'], p['rw_w0'],
                                             p['rw_w2'], p['rw_a0'], p['rw_a2'], p['rw_g2'], p['rw_k_k'], p['rw_k_a'],
                                             p['rw_r_k'], p['rw_ln_g'], p['rw_ln_b'],
                                             T=RW_CHUNK if L % RW_CHUNK == 0 else L)
    s_rw_new = _state_from_bd(s_rw_bd)
    buf_shift_new = shift_new[:, 0]
    branches = jnp.stack([o_sb, o_hg, o_cv, o_rw], axis=2)
    gates = jax.nn.sigmoid(z_gate.reshape(B_, L, N_BRANCH, D_MODEL))
    merged = jnp.sum(jnp.einsum('blnc,ncd->blnd', branches, p['w_branch']) * gates, axis=2)
    x = x + merged @ p['w_out']
    x = x + _peer_ffn_pallas(x, p['norm_ffn_g'], p['peer_wq'], p['peer_qn_g'], p['peer_keys'],
                             p['peer_u'], p['peer_v'])
    return x, (k, v, s_hg_new, buf_conv_new, s_rw_new, buf_shift_new)


def kernel(x_prompt, x_sample, cache_sb_k, cache_sb_v, state_hgrn, state_conv, state_rwkv, state_shift,
           norm_mix_g, w_in, hg_lb_logits, hg_norm_g, conv_w, conv_b, conv_ln_g, conv_ln_b,
           rw_mu, rw_w0, rw_w2, rw_a0, rw_a2, rw_g2, rw_k_k, rw_k_a, rw_r_k, rw_ln_g, rw_ln_b,
           w_branch, w_out, norm_ffn_g, peer_wq, peer_qn_g, peer_keys, peer_u, peer_v, final_norm_g):
    lb_all = jnp.cumsum(jax.nn.softmax(hg_lb_logits.astype(F32), axis=0), axis=0)
    lb_all = lb_all - lb_all[0:1]
    Bp = x_prompt.shape[0]
    hg0 = jnp.zeros((Bp, HG_HEADS, HG_DK, HG_DV), F32)
    cv0 = jnp.zeros((Bp, CONV_WIDTH - 1, CONV_CH), x_prompt.dtype)
    rw0 = jnp.zeros((Bp, RW_HEADS, RW_HD, RW_HD), F32)
    sh0 = jnp.zeros((Bp, RW_COLS), x_prompt.dtype)
    xp, xs = x_prompt, x_sample
    outs_p, outs_s = [], []
    for l in range(DEPTH):
        lp = dict(norm_mix_g=norm_mix_g[l], w_in=w_in[l], lb=lb_all[l], hg_norm_g=hg_norm_g[l],
                  conv_w=conv_w[l], conv_b=conv_b[l], conv_ln_g=conv_ln_g[l], conv_ln_b=conv_ln_b[l],
                  rw_mu=rw_mu[l], rw_w0=rw_w0[l], rw_w2=rw_w2[l], rw_a0=rw_a0[l], rw_a2=rw_a2[l],
                  rw_g2=rw_g2[l], rw_k_k=rw_k_k[l], rw_k_a=rw_k_a[l], rw_r_k=rw_r_k[l],
                  rw_ln_g=rw_ln_g[l], rw_ln_b=rw_ln_b[l], w_branch=w_branch[l], w_out=w_out[l],
                  norm_ffn_g=norm_ffn_g[l], peer_wq=peer_wq[l], peer_qn_g=peer_qn_g[l],
                  peer_keys=peer_keys[l], peer_u=_pack_table(peer_u[l]), peer_v=_pack_table(peer_v[l]))
        xp, st_p = _trunk_layer(xp, None, None, hg0, cv0, rw0, sh0, lp)
        xs, st_s = _trunk_layer(xs, cache_sb_k[l], cache_sb_v[l], state_hgrn[l], state_conv[l],
                                state_rwkv[l], state_shift[l], lp)
        outs_p.append(st_p)
        outs_s.append(st_s)

    def stk(outs, i):
        return jnp.stack([o[i] for o in outs], axis=0)

    y_prompt = _rmsnorm_pallas(xp, final_norm_g)
    y_sample = _rmsnorm_pallas(xs, final_norm_g)
    return (y_prompt, y_sample,
            stk(outs_p, 0), stk(outs_p, 1), stk(outs_p, 2), stk(outs_p, 3), stk(outs_p, 4), stk(outs_p, 5),
            stk(outs_s, 0), stk(outs_s, 1), stk(outs_s, 2), stk(outs_s, 3), stk(outs_s, 4), stk(outs_s, 5))
```

```python
from functools import partial

import jax
import jax.numpy as jnp
from jax import lax
from jax.experimental import pallas as pl
from jax.experimental.pallas import tpu as pltpu

D_MODEL = 1024
DEPTH = 2
CHUNK = 64
HEAD_DIM = 64
N_BRANCH = 4
BRANCH_W = D_MODEL // 4
SB_HEADS = BRANCH_W // HEAD_DIM
HG_HEADS = 4
HG_DK = BRANCH_W // HG_HEADS
HG_DV = BRANCH_W // HG_HEADS
CONV_CH = BRANCH_W
CONV_WIDTH = 31
RW_HEADS = 4
RW_HD = BRANCH_W // RW_HEADS
RW_DECAY_LORA = 64
RW_AAA_LORA = 64
RW_GATE_LORA = 128
SB_COLS = 3 * BRANCH_W
HG_COLS = 4 * BRANCH_W
CV_COLS = 2 * CONV_CH
RW_COLS = 3 * BRANCH_W + RW_DECAY_LORA + RW_AAA_LORA + RW_GATE_LORA
GATE_COLS = N_BRANCH * D_MODEL
IN_COLS = SB_COLS + HG_COLS + CV_COLS + RW_COLS + GATE_COLS
IN_SPLITS = (SB_COLS, SB_COLS + HG_COLS, SB_COLS + HG_COLS + CV_COLS,
             SB_COLS + HG_COLS + CV_COLS + RW_COLS)
RW_SPLITS = (BRANCH_W, 2 * BRANCH_W, 3 * BRANCH_W, 3 * BRANCH_W + RW_DECAY_LORA,
             3 * BRANCH_W + RW_DECAY_LORA + RW_AAA_LORA)
Q_BLOCK = 128
PK_HEADS = 8
PK_NKEYS = 128
PK_EXPERTS = PK_NKEYS * PK_NKEYS
PK_DQ = 256
PK_TOPK = 16
PEER_BLOCK = 128
EPS = 1e-6
RW_GN_EPS = 64e-5
SB_TQ = 256
SB_TK = 256
RW_CHUNK = 64
F32 = jnp.float32
BF16 = jnp.bfloat16


def _rmsnorm(x, g):
    xf = x.astype(F32)
    y = xf * lax.rsqrt(jnp.mean(xf * xf, axis=-1, keepdims=True) + EPS)
    return (y * g.astype(F32)).astype(x.dtype)


def _rmsnorm_body(x_ref, g_ref, o_ref):
    x = x_ref[...]
    y = x * lax.rsqrt(jnp.mean(x * x, axis=-1, keepdims=True) + EPS)
    o_ref[...] = y * g_ref[...]


def _rmsnorm_pallas(x, g, rows=512):
    shape = x.shape
    x2 = x.reshape(-1, shape[-1])
    n, d = x2.shape
    rows = min(rows, n)
    out = pl.pallas_call(
        _rmsnorm_body,
        grid=(n // rows,),
        in_specs=[pl.BlockSpec((rows, d), lambda i: (i, 0)),
                  pl.BlockSpec((1, d), lambda i: (0, 0))],
        out_specs=pl.BlockSpec((rows, d), lambda i: (i, 0)),
        out_shape=jax.ShapeDtypeStruct((n, d), x.dtype),
        name="final_rmsnorm",
    )(x2, g.reshape(1, d))
    return out.reshape(shape)


VMEM_LIMIT_V7X = 48 * 1024 * 1024


def _sb_attn_body(q_ref, k_ref, v_ref, o_ref, acc_ref, carry_ref, *, tq, tk, off, n_diag):
    qi = pl.program_id(1)
    q0 = off + qi * tq
    n_full = q0 // tk
    acc_ref[...] = jnp.zeros_like(acc_ref)
    carry_ref[...] = jnp.zeros_like(carry_ref)
    q_all = q_ref[0]
    q_heads = [q_all[:, h * HEAD_DIM:(h + 1) * HEAD_DIM] for h in range(SB_HEADS)]
    jj = lax.broadcasted_iota(jnp.int32, (tk, tk), 0)
    ss = lax.broadcasted_iota(jnp.int32, (tk, tk), 1)
    later_mat = (jj > ss).astype(BF16)

    def block(kb, masked):
        start = pl.multiple_of(kb * tk, tk)
        k_blk = k_ref[0, pl.ds(start, tk), :]
        v_blk = v_ref[0, pl.ds(start, tk), :]
        if masked:
            key_pos = start + lax.broadcasted_iota(jnp.int32, (tq, tk), 1)
            q_pos = q0 + lax.broadcasted_iota(jnp.int32, (tq, tk), 0)
            mask = key_pos < q_pos
        outs = []
        for h in range(SB_HEADS):
            kh = k_blk[:, h * HEAD_DIM:(h + 1) * HEAD_DIM]
            vh = v_blk[:, h * HEAD_DIM:(h + 1) * HEAD_DIM]
            l = lax.dot_general(q_heads[h], kh, (((1,), (1,)), ((), ())), preferred_element_type=F32)
            lsm = -(jnp.maximum(l, 0.0) + jnp.log(1.0 + jnp.exp(-jnp.abs(l))))
            if masked:
                lsm = jnp.where(mask, lsm, 0.0)
            later = jnp.dot(lsm.astype(BF16), later_mat, preferred_element_type=F32)
            carry = carry_ref[h]
            expo = l + lsm + later + jnp.concatenate([carry] * (tk // 128), axis=1)
            w = jnp.exp(expo)
            if masked:
                w = jnp.where(mask, w, 0.0)
            outs.append(jnp.dot(w.astype(BF16), vh, preferred_element_type=F32))
            row = later[:, 0:1] + lsm[:, 0:1]
            carry_ref[h] = carry + jnp.broadcast_to(row, carry.shape)
        acc_ref[...] += jnp.concatenate(outs, axis=1)

    for d in range(n_diag - 1, -1, -1):
        block(n_full + d, True)

    def full_step(i, c):
        block(n_full - 1 - i, False)
        return c

    lax.fori_loop(0, n_full, full_step, 0)
    o_ref[0] = acc_ref[...].astype(o_ref.dtype)


def _sb_attention(q, k, v, *, off, tq, tk, out_dtype=BF16):
    B, Lq, W = q.shape
    Lk = k.shape[1]
    assert W == BRANCH_W and Lq % tq == 0 and Lk % tk == 0 and tk % 128 == 0
    nq = Lq // tq
    assert nq == 1 or (tq % tk == 0 and off % tk == 0)
    n_diag = -(-((off % tk) + tq - 1) // tk)
    assert (off + Lq - 1 + tk - 1) // tk <= Lk // tk
    body = partial(_sb_attn_body, tq=tq, tk=tk, off=off, n_diag=n_diag)
    return pl.pallas_call(
        body,
        grid=(B, nq),
        in_specs=[pl.BlockSpec((1, tq, W), lambda b, i: (b, i, 0)),
                  pl.BlockSpec((1, Lk, W), lambda b, i: (b, 0, 0)),
                  pl.BlockSpec((1, Lk, W), lambda b, i: (b, 0, 0))],
        out_specs=pl.BlockSpec((1, tq, W), lambda b, i: (b, i, 0)),
        out_shape=jax.ShapeDtypeStruct((B, Lq, W), out_dtype),
        scratch_shapes=[pltpu.VMEM((tq, W), F32), pltpu.VMEM((SB_HEADS, tq, 128), F32)],
        compiler_params=pltpu.CompilerParams(dimension_semantics=("arbitrary", "arbitrary"),
                                             vmem_limit_bytes=VMEM_LIMIT_V7X),
        name="sb_attention",
    )(q, k, v)


RW_SUB = 16
HI = lax.Precision.HIGHEST


def _split(x):
    hi = x.astype(BF16)
    lo = (x - hi.astype(F32)).astype(BF16)
    return hi, lo


def _mm3(a, b):
    ah, al = _split(a)
    bh, bl = _split(b)
    d = partial(jnp.dot, preferred_element_type=F32)
    return d(ah, bh) + (d(ah, bl) + d(al, bh))


def _dot_nt(a, b):
    return lax.dot_general(a, b, (((1,), (1,)), ((), ())), preferred_element_type=F32)


def _dot_tn(a, b):
    return lax.dot_general(a, b, (((0,), (0,)), ((), ())), preferred_element_type=F32)


def _rwkv_body(z_ref, shift_ref, s0_ref, mu_ref, w0_ref, w2_ref, a0_ref, a2_ref, g2_ref, kk_ref, ka_ref,
               rk_ref, lng_ref, lnb_ref, o_ref, s_out_ref, shift_out_ref, state_ref, prev_ref, *, T):
    ci = pl.program_id(1)
    nc = pl.num_programs(1)
    W = BRANCH_W
    N = RW_HEADS * T

    @pl.when(ci == 0)
    def _():
        state_ref[...] = s0_ref[0]
        prev_ref[...] = shift_ref[0]

    z = z_ref[0]
    row = lax.broadcasted_iota(jnp.int32, (T, RW_COLS), 0)
    z_prev = jnp.where(row == 0, jnp.broadcast_to(prev_ref[...], (T, RW_COLS)), pltpu.roll(z, 1, axis=0))
    prev_ref[...] = z[T - 1:T, :]
    zs = z + (z_prev - z) * mu_ref[...]
    r = zs[:, 0:W]
    k = zs[:, W:2 * W]
    v = zs[:, 2 * W:3 * W]
    xwa = zs[:, 3 * W:3 * W + 128]
    xg = zs[:, 3 * W + 128:]
    dotf = partial(jnp.dot, preferred_element_type=F32, precision=HI)
    w_log = -jax.nn.softplus(-(w0_ref[...] + dotf(jnp.tanh(xwa), w2_ref[...]))) - 0.5
    logdec = -jnp.exp(w_log)
    a = jax.nn.sigmoid(a0_ref[...] + dotf(xwa, a2_ref[...]))
    g = dotf(jax.nn.sigmoid(xg), g2_ref[...])
    li = lax.broadcasted_iota(jnp.int32, (W, W), 0) // RW_HD
    lj = lax.broadcasted_iota(jnp.int32, (W, W), 1) // RW_HD
    head_sum = (li == lj).astype(F32)
    kk = k * kk_ref[...]
    kk = kk * lax.rsqrt(dotf(kk * kk, head_sum) + 1e-12)
    k2 = k * (1.0 + (a - 1.0) * ka_ref[...])
    beta = kk * a
    ti = lax.broadcasted_iota(jnp.int32, (T, T), 0)
    tj = lax.broadcasted_iota(jnp.int32, (T, T), 1)
    b = dotf((ti >= tj).astype(F32), logdec)
    b_last = b[T - 1:T, :]
    k_in = kk * jnp.exp(b - logdec)
    r_in = r * jnp.exp(b)
    inv_p = jnp.exp(-b)
    k_out = k2 * inv_p
    b_out = beta * inv_p
    to_end = jnp.exp(b_last - b)
    k_end = k2 * to_end
    b_end = beta * to_end

    sh = lax.broadcasted_iota(jnp.int32, (N, W), 0) // T
    sl = lax.broadcasted_iota(jnp.int32, (N, W), 1) // RW_HD
    own = sh == sl

    def stack(x, masked):
        xs = jnp.concatenate([x] * RW_HEADS, axis=0)
        return jnp.where(own, xs, 0.0) if masked else xs

    k_in_s = stack(k_in, True).astype(BF16)
    r_in_s = stack(r_in, True).astype(BF16)
    k_out_s = stack(k_out, False).astype(BF16)
    b_out_s = stack(b_out, False).astype(BF16)
    v_s = stack(v, True).astype(BF16)
    k_end_s = stack(k_end, True).astype(BF16)
    b_end_s = stack(b_end, True).astype(BF16)

    ri = lax.broadcasted_iota(jnp.int32, (N, N), 0)
    rj = lax.broadcasted_iota(jnp.int32, (N, N), 1)
    same_head = (ri // T) == (rj // T)
    strict = same_head & (ri > rj)
    incl = same_head & (ri >= rj)
    a_mat = jnp.where(strict, _dot_nt(k_in_s, b_out_s), 0.0)
    kk_mat = jnp.where(strict, _dot_nt(k_in_s, k_out_s), 0.0)
    rk_mat = jnp.where(incl, _dot_nt(r_in_s, k_out_s), 0.0)
    rb_mat = jnp.where(incl, _dot_nt(r_in_s, b_out_s), 0.0)

    eye = (ri == rj).astype(F32)
    a_bd = jnp.where((ri // RW_SUB) == (rj // RW_SUB), a_mat, 0.0)
    x = eye - a_bd
    p = _mm3(a_bd, a_bd)
    x = x + _mm3(x, p)
    p = _mm3(p, p)
    x = x + _mm3(x, p)
    p = _mm3(p, p)
    x = x + _mm3(x, p)
    size = RW_SUB
    while size < T:
        lower = ((ri // (2 * size)) == (rj // (2 * size))) & ((ri // size) > (rj // size))
        x = x - _mm3(_mm3(x, jnp.where(lower, a_mat, 0.0)), x)
        size *= 2

    state = state_ref[...]
    state_b = state.astype(BF16)
    d = partial(jnp.dot, preferred_element_type=F32)
    rhs = _dot_nt(k_in_s, state_b) + d(kk_mat.astype(BF16), v_s)
    u = _mm3(x, rhs)
    u_b = u.astype(BF16)
    o_s = _dot_nt(r_in_s, state_b) + d(rk_mat.astype(BF16), v_s) - d(rb_mat.astype(BF16), u_b)
    o = o_s[0:T]
    for h in range(1, RW_HEADS):
        o = o + o_s[h * T:(h + 1) * T]
    state_ref[...] = state * jnp.exp(b_last) + _dot_tn(v_s, k_end_s) - _dot_tn(u_b, b_end_s)

    head_mean = head_sum * (1.0 / RW_HD)
    mean = dotf(o, head_mean)
    var = dotf(jnp.square(o - mean), head_mean)
    o = (o - mean) * lax.rsqrt(var + RW_GN_EPS) * lng_ref[...] + lnb_ref[...]
    bonus = dotf(r * k2 * rk_ref[...], head_sum) * v
    o_ref[0] = ((o + bonus) * g).astype(o_ref.dtype)

    @pl.when(ci == nc - 1)
    def _():
        s_out_ref[0] = state_ref[...]
        shift_out_ref[0] = z[T - 1:T, :]


def _rwkv7_pallas(z, s0_bd, shift_prev, mu, w0, w2, a0, a2, g2, k_k, k_a, r_k, ln_g, ln_b, *, T, out_dtype=F32):
    B, L, C = z.shape
    assert C == RW_COLS and L % T == 0 and T % RW_SUB == 0
    W = BRANCH_W
    w2p = jnp.concatenate([w2, jnp.zeros_like(w2)], axis=0)
    a2p = jnp.concatenate([jnp.zeros_like(a2), a2], axis=0)
    vec = lambda t: t.reshape(1, -1).astype(F32)
    full = lambda shape: pl.BlockSpec(shape, lambda b, c: (0,) * len(shape))
    return pl.pallas_call(
        partial(_rwkv_body, T=T),
        grid=(B, L // T),
        in_specs=[pl.BlockSpec((1, T, C), lambda b, c: (b, c, 0)),
                  pl.BlockSpec((1, 1, C), lambda b, c: (b, 0, 0)),
                  pl.BlockSpec((1, W, W), lambda b, c: (b, 0, 0)),
                  full((1, C)), full((1, W)), full((128, W)), full((1, W)), full((128, W)), full((128, W)),
                  full((1, W)), full((1, W)), full((1, W)), full((1, W)), full((1, W))],
        out_specs=[pl.BlockSpec((1, T, W), lambda b, c: (b, c, 0)),
                   pl.BlockSpec((1, W, W), lambda b, c: (b, 0, 0)),
                   pl.BlockSpec((1, 1, C), lambda b, c: (b, 0, 0))],
        out_shape=[jax.ShapeDtypeStruct((B, L, W), out_dtype),
                   jax.ShapeDtypeStruct((B, W, W), F32),
                   jax.ShapeDtypeStruct((B, 1, C), F32)],
        scratch_shapes=[pltpu.VMEM((W, W), F32), pltpu.VMEM((1, C), F32)],
        compiler_params=pltpu.CompilerParams(dimension_semantics=("arbitrary", "arbitrary")),
        name="rwkv7_chunked",
    )(z, shift_prev.reshape(B, 1, C), s0_bd, vec(mu), vec(w0), w2p, vec(a0), a2p, g2, vec(k_k), vec(k_a),
      vec(r_k), vec(ln_g), vec(ln_b))


def _state_to_bd(s):
    B = s.shape[0]
    eye = jnp.eye(RW_HEADS, dtype=s.dtype)
    return (s[:, :, :, None, :] * eye[None, :, None, :, None]).reshape(B, BRANCH_W, BRANCH_W)


def _state_from_bd(s_bd):
    B = s_bd.shape[0]
    s5 = s_bd.reshape(B, RW_HEADS, RW_HD, RW_HEADS, RW_HD)
    return jnp.stack([s5[:, h, :, h, :] for h in range(RW_HEADS)], axis=1)


def _hgrn_body(z_ref, s0_ref, loglb_ref, log1mlb_ref, ng_ref, o_ref, s_out_ref, state_ref, *, T):
    ci = pl.program_id(1)
    nc = pl.num_programs(1)
    W = BRANCH_W

    @pl.when(ci == 0)
    def _():
        state_ref[...] = s0_ref[0]

    z = z_ref[0]
    zq, zf, zi, zg = z[:, 0:W], z[:, W:2 * W], z[:, 2 * W:3 * W], z[:, 3 * W:]
    log_sig = jnp.minimum(zf, 0.0) - jnp.log(1.0 + jnp.exp(-jnp.abs(zf)))
    a = loglb_ref[...]
    bv = log1mlb_ref[...] + log_sig
    m = jnp.maximum(a, bv)
    log_f = m + jnp.log(jnp.exp(a - m) + jnp.exp(bv - m))
    k = 1.0 - jnp.exp(log_f)
    q = zq * jax.nn.sigmoid(zq)
    dotf = partial(jnp.dot, preferred_element_type=F32, precision=HI)
    ti = lax.broadcasted_iota(jnp.int32, (T, T), 0)
    tj = lax.broadcasted_iota(jnp.int32, (T, T), 1)
    b = dotf((ti >= tj).astype(F32), log_f)
    li = lax.broadcasted_iota(jnp.int32, (W, W), 0) // HG_DK
    lj = lax.broadcasted_iota(jnp.int32, (W, W), 1) // HG_DK
    same_head = li == lj
    head_sum = same_head.astype(BF16)

    t_idx = lax.broadcasted_iota(jnp.int32, (T, W), 0)
    prods = []
    for s in range(T):
        e = jnp.exp(jnp.where(t_idx >= s, b - b[s:s + 1, :], -jnp.inf))
        prods.append((q * e * k[s:s + 1, :]).astype(BF16))
    scores = jnp.dot(jnp.concatenate(prods, axis=0), head_sum, preferred_element_type=F32)
    o = jnp.zeros((T, W), F32)
    for s in range(T):
        o = o + scores[s * T:(s + 1) * T, :] * zi[s:s + 1, :]

    state = state_ref[...]
    o = o + _dot_nt((q * jnp.exp(b)).astype(BF16), state.astype(BF16))
    b_last = b[T - 1:T, :]
    k_end = (k * jnp.exp(b_last - b)).astype(BF16)
    upd = _dot_tn(zi.astype(BF16), k_end)
    state_ref[...] = state * jnp.exp(b_last) + jnp.where(same_head, upd, 0.0)

    ms = dotf(o * o, same_head.astype(F32) * (1.0 / HG_DK))
    o = o * lax.rsqrt(ms + EPS) * ng_ref[...]
    o_ref[0] = (o * (zg * jax.nn.sigmoid(zg))).astype(o_ref.dtype)

    @pl.when(ci == nc - 1)
    def _():
        s_out_ref[0] = state_ref[...]


def _hgrn2_pallas(z, s0_bd, lb, norm_g, *, T, out_dtype=F32):
    B, L, C = z.shape
    assert C == HG_COLS and L % T == 0
    W = BRANCH_W
    full = lambda shape: pl.BlockSpec(shape, lambda b, c: (0,) * len(shape))
    return pl.pallas_call(
        partial(_hgrn_body, T=T),
        grid=(B, L // T),
        in_specs=[pl.BlockSpec((1, T, C), lambda b, c: (b, c, 0)),
                  pl.BlockSpec((1, W, W), lambda b, c: (b, 0, 0)),
                  full((1, W)), full((1, W)), full((1, W))],
        out_specs=[pl.BlockSpec((1, T, W), lambda b, c: (b, c, 0)),
                   pl.BlockSpec((1, W, W), lambda b, c: (b, 0, 0))],
        out_shape=[jax.ShapeDtypeStruct((B, L, W), out_dtype),
                   jax.ShapeDtypeStruct((B, W, W), F32)],
        scratch_shapes=[pltpu.VMEM((W, W), F32)],
        compiler_params=pltpu.CompilerParams(dimension_semantics=("arbitrary", "arbitrary")),
        name="hgrn2_chunked",
    )(z, s0_bd, jnp.log(lb).reshape(1, W), jnp.log1p(-lb).reshape(1, W), jnp.tile(norm_g, HG_HEADS).reshape(1, W))


def _conformer_conv(z, buf, w, b, ln_g, ln_b):
    zf = z.astype(F32)
    val, gate = jnp.split(zf, 2, axis=-1)
    h = val * jax.nn.sigmoid(gate)
    hp = jnp.concatenate([buf.astype(F32), h], axis=1)
    y = lax.conv_general_dilated(hp, w.astype(F32)[:, None, :], window_strides=(1,), padding='VALID',
                                 dimension_numbers=('NWC', 'WIO', 'NWC'),
                                 feature_group_count=CONV_CH) + b.astype(F32)
    mu = jnp.mean(y, axis=-1, keepdims=True)
    var = jnp.mean(jnp.square(y - mu), axis=-1, keepdims=True)
    y = (y - mu) * lax.rsqrt(var + EPS) * ln_g.astype(F32) + ln_b.astype(F32)
    y = jax.nn.silu(y)
    return y.astype(z.dtype), hp[:, -(CONV_WIDTH - 1):].astype(z.dtype)


PK_PAIRS = PK_HEADS * PK_TOPK
PEER_TB = 128
SUBLANES = 8
LANES = 128
ROW_TILES = D_MODEL // LANES
PEER_VMEM_LIMIT_V7X = 52 * 1024 * 1024


def _topk_rows(s, payload, k):
    n = s.shape[0]
    rows = lax.broadcasted_iota(jnp.int32, s.shape, 0)
    vals, ids = [], []
    for _ in range(k):
        m = jnp.max(s, axis=0, keepdims=True)
        pos = jnp.min(jnp.where(s == m, rows, n), axis=0, keepdims=True)
        sel = rows == pos
        ids.append(pos if payload is None else jnp.max(jnp.where(sel, payload, -1), axis=0, keepdims=True))
        vals.append(m)
        s = jnp.where(sel, -jnp.inf, s)
    return jnp.concatenate(vals, axis=0), jnp.concatenate(ids, axis=0)


def _dot3(ah, al, bh, bl, dims):
    d = partial(lax.dot_general, dimension_numbers=dims, preferred_element_type=F32)
    return d(ah, bh) + (d(ah, bl) + d(al, bh))


def _peer_route_body(x_ref, g_ref, wqh_ref, wql_ref, qg_ref, k1_ref, k2_ref, xn_ref, idx_ref, gate_ref):
    x = x_ref[...]
    xn = x * lax.rsqrt(jnp.mean(x * x, axis=-1, keepdims=True) + EPS) * g_ref[...]
    xn_ref[...] = xn
    nn = (((1,), (0,)), ((), ()))
    nt = (((1,), (1,)), ((), ()))
    xh, xl = _split(xn)
    q = _dot3(xh, xl, wqh_ref[...], wql_ref[...], nn)
    half = PK_DQ // 2
    experts, gates = [], []
    for h in range(PK_HEADS):
        qh = q[:, h * PK_DQ:(h + 1) * PK_DQ]
        qh = qh * lax.rsqrt(jnp.mean(qh * qh, axis=-1, keepdims=True) + EPS) * qg_ref[...]
        qhh, qhl = _split(qh)
        k1h, k1l = _split(k1_ref[h])
        k2h, k2l = _split(k2_ref[h])
        s1 = _dot3(k1h, k1l, qhh[:, :half], qhl[:, :half], nt)
        s2 = _dot3(k2h, k2l, qhh[:, half:], qhl[:, half:], nt)
        v1, i1 = _topk_rows(s1, None, PK_TOPK)
        v2, i2 = _topk_rows(s2, None, PK_TOPK)
        cand_s = jnp.concatenate([v1[a:a + 1, :] + v2 for a in range(PK_TOPK)], axis=0)
        cand_i = jnp.concatenate([i1[a:a + 1, :] * PK_NKEYS + i2 for a in range(PK_TOPK)], axis=0)
        top_s, expert = _topk_rows(cand_s, cand_i, PK_TOPK)
        e = jnp.exp(top_s - jnp.max(top_s, axis=0, keepdims=True))
        gates.append(e / jnp.sum(e, axis=0, keepdims=True))
        experts.append(expert)
    expert = jnp.concatenate(experts, axis=0)
    idx_ref[...] = expert.T
    gate_ref[...] = jnp.concatenate(gates, axis=0)


def _peer_route(x2, norm_g, wq, qn_g, k1, k2):
    n_tok = x2.shape[0]
    TB = PEER_TB
    assert n_tok % TB == 0
    nb = n_tok // TB
    full = lambda shape: pl.BlockSpec(shape, lambda i: (0,) * len(shape))
    wq_hi = wq.astype(BF16)
    wq_lo = (wq - wq_hi.astype(F32)).astype(BF16)
    return pl.pallas_call(
        _peer_route_body,
        grid=(nb,),
        in_specs=[pl.BlockSpec((TB, D_MODEL), lambda i: (i, 0)), full((1, D_MODEL)),
                  full((D_MODEL, PK_HEADS * PK_DQ)), full((D_MODEL, PK_HEADS * PK_DQ)), full((1, PK_DQ)),
                  full((PK_HEADS, PK_NKEYS, PK_DQ // 2)), full((PK_HEADS, PK_NKEYS, PK_DQ // 2))],
        out_specs=[pl.BlockSpec((TB, D_MODEL), lambda i: (i, 0)),
                   pl.BlockSpec((TB, PK_PAIRS), lambda i: (i, 0)),
                   pl.BlockSpec((PK_PAIRS, TB), lambda i: (0, i))],
        out_shape=[jax.ShapeDtypeStruct((n_tok, D_MODEL), F32),
                   jax.ShapeDtypeStruct((n_tok, PK_PAIRS), jnp.int32),
                   jax.ShapeDtypeStruct((PK_PAIRS, n_tok), F32)],
        compiler_params=pltpu.CompilerParams(dimension_semantics=("arbitrary",)),
        name="peer_route",
    )(x2, norm_g.reshape(1, -1), wq_hi, wq_lo, qn_g.reshape(1, -1), k1, k2)


HALF_TILES = ROW_TILES // 2


def _halves_from_packed(tab_ref, e):
    word = tab_ref[e]
    return pltpu.bitcast(word << 16, F32), pltpu.bitcast(word & jnp.int32(-65536), F32)


def _fold_half_tiles(tiles):
    sub = lax.broadcasted_iota(jnp.int32, (SUBLANES, LANES), 0)
    order = (0, 2, 1, 3)
    lvl1 = []
    for i in range(4):
        a, b = tiles[order[i]], tiles[order[i] + 4]
        lvl1.append(jnp.concatenate([a, b], axis=0))
    m2 = (sub & 3) < 2
    lvl2 = []
    for i in range(0, 4, 2):
        a, b = lvl1[i], lvl1[i + 1]
        lvl2.append(jnp.where(m2, a, b) + jnp.where(m2, pltpu.roll(a, 6, axis=0), pltpu.roll(b, 2, axis=0)))
    m1 = (sub & 1) == 0
    a, b = lvl2
    return jnp.where(m1, a, b) + jnp.where(m1, pltpu.roll(a, 7, axis=0), pltpu.roll(b, 1, axis=0))


def _peer_act_body(idx_ref, x_ref, gate_ref, tab_ref, w_ref, part_ref):
    TB = x_ref.shape[0]

    def token(t, c):
        x = x_ref[t]
        x_lo = x[0:HALF_TILES]
        x_hi = x[HALF_TILES:]
        start = t * PK_PAIRS
        for g in range(PK_PAIRS // SUBLANES):
            tiles = []
            for j in range(SUBLANES):
                lo, hi = _halves_from_packed(tab_ref, idx_ref[0, 0, start + g * SUBLANES + j])
                tiles.append(lo * x_lo + hi * x_hi)
            part_ref[t, g * SUBLANES:(g + 1) * SUBLANES, :] = _fold_half_tiles(tiles)
        return c

    lax.fori_loop(0, TB, token, 0)
    lane = lax.broadcasted_iota(jnp.int32, (PK_PAIRS, TB), 1)
    s = jnp.zeros((PK_PAIRS, TB), F32)
    for t in range(TB):
        s = jnp.where(lane == t, jnp.sum(part_ref[t], axis=1, keepdims=True), s)
    w_ref[...] = gate_ref[...] * jax.nn.gelu(s)


def _peer_mix_body(idx_ref, w_ref, tab_ref, y_ref, wrep_ref):
    TB = y_ref.shape[0]
    w_all = w_ref[...]
    for t in range(TB):
        wrep_ref[t] = jnp.broadcast_to(w_all[:, t:t + 1], (PK_PAIRS, LANES))

    def token(t, c):
        start = t * PK_PAIRS
        acc_lo = jnp.zeros((HALF_TILES, LANES), F32)
        acc_hi = jnp.zeros((HALF_TILES, LANES), F32)
        for p in range(PK_PAIRS):
            lo, hi = _halves_from_packed(tab_ref, idx_ref[0, 0, start + p])
            w = jnp.broadcast_to(wrep_ref[t, p:p + 1, :], (HALF_TILES, LANES))
            acc_lo = acc_lo + w * lo
            acc_hi = acc_hi + w * hi
        y_ref[t] = jnp.concatenate([acc_lo, acc_hi], axis=0)
        return c

    lax.fori_loop(0, TB, token, 0)


def _pack_table(tab):
    bits = lax.bitcast_convert_type(tab.astype(BF16), jnp.uint16).astype(jnp.uint32)
    bits = bits.reshape(PK_EXPERTS, 2, HALF_TILES, LANES)
    word = bits[:, 0] | (bits[:, 1] << 16)
    return lax.bitcast_convert_type(word, jnp.int32)


def _smem_spec(TB):
    return pl.BlockSpec((1, 1, PK_PAIRS * TB), lambda i: (i, 0, 0), memory_space=pltpu.SMEM)


def _table_spec():
    return pl.BlockSpec((PK_EXPERTS, HALF_TILES, LANES), lambda i: (0, 0, 0), pipeline_mode=pl.Buffered(1))


def _peer_act(idx, xn3, gate_t, tab_packed):
    n_tok = xn3.shape[0]
    TB = PEER_TB
    nb = n_tok // TB
    return pl.pallas_call(
        _peer_act_body,
        grid=(nb,),
        in_specs=[_smem_spec(TB),
                  pl.BlockSpec((TB, ROW_TILES, LANES), lambda i: (i, 0, 0)),
                  pl.BlockSpec((PK_PAIRS, TB), lambda i: (0, i)),
                  _table_spec()],
        out_specs=pl.BlockSpec((PK_PAIRS, TB), lambda i: (0, i)),
        out_shape=jax.ShapeDtypeStruct((PK_PAIRS, n_tok), F32),
        scratch_shapes=[pltpu.VMEM((TB, PK_PAIRS, LANES), F32)],
        compiler_params=pltpu.CompilerParams(dimension_semantics=("arbitrary",),
                                             vmem_limit_bytes=PEER_VMEM_LIMIT_V7X),
        name="peer_act",
    )(idx.reshape(nb, 1, TB * PK_PAIRS), xn3, gate_t, tab_packed)


def _peer_mix(idx, w_t, tab_packed):
    n_tok = w_t.shape[1]
    TB = PEER_TB
    nb = n_tok // TB
    return pl.pallas_call(
        _peer_mix_body,
        grid=(nb,),
        in_specs=[_smem_spec(TB),
                  pl.BlockSpec((PK_PAIRS, TB), lambda i: (0, i)),
                  _table_spec()],
        out_specs=pl.BlockSpec((TB, ROW_TILES, LANES), lambda i: (i, 0, 0)),
        out_shape=jax.ShapeDtypeStruct((n_tok, ROW_TILES, LANES), F32),
        scratch_shapes=[pltpu.VMEM((TB, PK_PAIRS, LANES), F32)],
        compiler_params=pltpu.CompilerParams(dimension_semantics=("arbitrary",),
                                             vmem_limit_bytes=PEER_VMEM_LIMIT_V7X),
        name="peer_mix",
    )(idx.reshape(nb, 1, TB * PK_PAIRS), w_t, tab_packed)


def _peer_ffn_pallas(x, norm_g, wq, qn_g, sub_keys, u_packed, v_packed):
    B, L, D = x.shape
    n_tok = B * L
    pad = (-n_tok) % PEER_TB
    x2 = jnp.pad(x.reshape(n_tok, D), ((0, pad), (0, 0)))
    xn, idx, gate_t = _peer_route(x2, norm_g, wq, qn_g, sub_keys[:, 0], sub_keys[:, 1])
    w_t = _peer_act(idx, xn.reshape(-1, ROW_TILES, LANES), gate_t, u_packed)
    y = _peer_mix(idx, w_t, v_packed)
    return y.reshape(-1, D)[:n_tok].reshape(B, L, D)


def _trunk_layer(x, k_past, v_past, s_hg, buf_conv, s_rw, buf_shift, p):
    B_, L, _ = x.shape
    h = _rmsnorm(x, p['norm_mix_g'])
    z = h @ p['w_in']
    z_sb, z_hg, z_cv, z_rw, z_gate = jnp.split(z, IN_SPLITS, axis=-1)
    q, k, v = [t.reshape(B_, L, SB_HEADS, HEAD_DIM) for t in jnp.split(z_sb, 3, axis=-1)]
    qb = (q.reshape(B_, L, BRANCH_W) * (HEAD_DIM ** -0.5)).astype(BF16)
    kb = k.reshape(B_, L, BRANCH_W)
    vb = v.reshape(B_, L, BRANCH_W)
    if k_past is None:
        o_sb = _sb_attention(qb, kb.astype(BF16), vb.astype(BF16), off=0, tq=SB_TQ, tk=SB_TK, out_dtype=F32)
    else:
        past = k_past.shape[1]
        pad = (-(past + L)) % SB_TK
        kc = jnp.concatenate([k_past.reshape(B_, past, BRANCH_W), kb, jnp.zeros((B_, pad, BRANCH_W), F32)], axis=1)
        vc = jnp.concatenate([v_past.reshape(B_, past, BRANCH_W), vb, jnp.zeros((B_, pad, BRANCH_W), F32)], axis=1)
        o_sb = _sb_attention(qb, kc.astype(BF16), vc.astype(BF16), off=past, tq=L, tk=SB_TK, out_dtype=F32)
    o_hg, s_hg_bd = _hgrn2_pallas(z_hg, _state_to_bd(jnp.swapaxes(s_hg.astype(F32), 2, 3)), p['lb'], p['hg_norm_g'],
                                  T=CHUNK if L % CHUNK == 0 else L)
    s_hg_new = jnp.swapaxes(_state_from_bd(s_hg_bd), 2, 3)
    o_cv, buf_conv_new = _conformer_conv(z_cv, buf_conv, p['conv_w'], p['conv_b'], p['conv_ln_g'], p['conv_ln_b'])
    o_rw, s_rw_bd, shift_new = _rwkv7_pallas(z_rw, _state_to_bd(s_rw.astype(F32)), buf_shift, p['rw_mu'], p['rw_w0'],
                                             p['rw_w2'], p['rw_a0'], p['rw_a2'], p['rw_g2'], p['rw_k_k'], p['rw_k_a'],
                                             p['rw_r_k'], p['rw_ln_g'], p['rw_ln_b'],
                                             T=RW_CHUNK if L % RW_CHUNK == 0 else L)
    s_rw_new = _state_from_bd(s_rw_bd)
    buf_shift_new = shift_new[:, 0]
    branches = jnp.stack([o_sb, o_hg, o_cv, o_rw], axis=2)
    gates = jax.nn.sigmoid(z_gate.reshape(B_, L, N_BRANCH, D_MODEL))
    merged = jnp.sum(jnp.einsum('blnc,ncd->blnd', branches, p['w_branch']) * gates, axis=2)
    x = x + merged @ p['w_out']
    x = x + _peer_ffn_pallas(x, p['norm_ffn_g'], p['peer_wq'], p['peer_qn_g'], p['peer_keys'],
                             p['peer_u'], p['peer_v'])
    return x, (k, v, s_hg_new, buf_conv_new, s_rw_new, buf_shift_new)


def kernel(x_prompt, x_sample, cache_sb_k, cache_sb_v, state_hgrn, state_conv, state_rwkv, state_shift,
           norm_mix_g, w_in, hg_lb_logits, hg_norm_g, conv_w, conv_b, conv_ln_g, conv_ln_b,
           rw_mu, rw_w0, rw_w2, rw_a0, rw_a2, rw_g2, rw_k_k, rw_k_a, rw_r_k, rw_ln_g, rw_ln_b,
           w_branch, w_out, norm_ffn_g, peer_wq, peer_qn_g, peer_keys, peer_u, peer_v, final_norm_g):
    lb_all = jnp.cumsum(jax.nn.softmax(hg_lb_logits.astype(F32), axis=0), axis=0)
    lb_all = lb_all - lb_all[0:1]
    Bp = x_prompt.shape[0]
    hg0 = jnp.zeros((Bp, HG_HEADS, HG_DK, HG_DV), F32)
    cv0 = jnp.zeros((Bp, CONV_WIDTH - 1, CONV_CH), x_prompt.dtype)
    rw0 = jnp.zeros((Bp, RW_HEADS, RW_HD, RW_HD), F32)
    sh0 = jnp.zeros((Bp, RW_COLS), x_prompt.dtype)
    xp, xs = x_prompt, x_sample
    outs_p, outs_s = [], []
    for l in range(DEPTH):
        lp = dict(norm_mix_g=norm_mix_g[l], w_in=w_in[l], lb=lb_all[l], hg_norm_g=hg_norm_g[l],
                  conv_w=conv_w[l], conv_b=conv_b[l], conv_ln_g=conv_ln_g[l], conv_ln_b=conv_ln_b[l],
                  rw_mu=rw_mu[l], rw_w0=rw_w0[l], rw_w2=rw_w2[l], rw_a0=rw_a0[l], rw_a2=rw_a2[l],
                  rw_g2=rw_g2[l], rw_k_k=rw_k_k[l], rw_k_a=rw_k_a[l], rw_r_k=rw_r_k[l],
                  rw_ln_g=rw_ln_g[l], rw_ln_b=rw_ln_b[l], w_branch=w_branch[l], w_out=w_out[l],
                  norm_ffn_g=norm_ffn_g[l], peer_wq=peer_wq[l], peer_qn_g=peer_qn_g[l],
                  peer_keys=peer_keys[l], peer_u=_pack_table(peer_u[l]), peer_v=_pack_table(peer_v[l]))
        xp, st_p = _trunk_layer(xp, None, None, hg0, cv0, rw0, sh0, lp)
        xs, st_s = _trunk_layer(xs, cache_sb_k[l], cache_sb_v[l], state_hgrn[l], state_conv[l],
                                state_rwkv[l], state_shift[l], lp)
        outs_p.append(st_p)
        outs_s.append(st_s)

    def stk(outs, i):
        return jnp.stack([o[i] for o in outs], axis=0)

    y_prompt = _rmsnorm_pallas(xp, final_norm_g)
    y_sample = _rmsnorm_pallas(xs, final_norm_g)
    return (y_prompt, y_sample,
            stk(outs_p, 0), stk(outs_p, 1), stk(outs_p, 2), stk(outs_p, 3), stk(outs_p, 4), stk(outs_p, 5),
            stk(outs_s, 0), stk(outs_s, 1), stk(outs_s, 2), stk(outs_s, 3), stk(outs_s, 4), stk(outs_s, 5))
```

```python
from functools import partial

import jax
import jax.numpy as jnp
from jax import lax
from jax.experimental import pallas as pl
from jax.experimental.pallas import tpu as pltpu

D_MODEL = 1024
DEPTH = 2
CHUNK = 64
HEAD_DIM = 64
N_BRANCH = 4
BRANCH_W = D_MODEL // 4
SB_HEADS = BRANCH_W // HEAD_DIM
HG_HEADS = 4
HG_DK = BRANCH_W // HG_HEADS
HG_DV = BRANCH_W // HG_HEADS
CONV_CH = BRANCH_W
CONV_WIDTH = 31
RW_HEADS = 4
RW_HD = BRANCH_W // RW_HEADS
RW_DECAY_LORA = 64
RW_AAA_LORA = 64
RW_GATE_LORA = 128
SB_COLS = 3 * BRANCH_W
HG_COLS = 4 * BRANCH_W
CV_COLS = 2 * CONV_CH
RW_COLS = 3 * BRANCH_W + RW_DECAY_LORA + RW_AAA_LORA + RW_GATE_LORA
GATE_COLS = N_BRANCH * D_MODEL
IN_COLS = SB_COLS + HG_COLS + CV_COLS + RW_COLS + GATE_COLS
IN_SPLITS = (SB_COLS, SB_COLS + HG_COLS, SB_COLS + HG_COLS + CV_COLS,
             SB_COLS + HG_COLS + CV_COLS + RW_COLS)
RW_SPLITS = (BRANCH_W, 2 * BRANCH_W, 3 * BRANCH_W, 3 * BRANCH_W + RW_DECAY_LORA,
             3 * BRANCH_W + RW_DECAY_LORA + RW_AAA_LORA)
Q_BLOCK = 128
PK_HEADS = 8
PK_NKEYS = 128
PK_EXPERTS = PK_NKEYS * PK_NKEYS
PK_DQ = 256
PK_TOPK = 16
PEER_BLOCK = 128
EPS = 1e-6
RW_GN_EPS = 64e-5
SB_TQ = 256
SB_TK = 256
RW_CHUNK = 64
F32 = jnp.float32
BF16 = jnp.bfloat16


def _rmsnorm(x, g):
    xf = x.astype(F32)
    y = xf * lax.rsqrt(jnp.mean(xf * xf, axis=-1, keepdims=True) + EPS)
    return (y * g.astype(F32)).astype(x.dtype)


def _rmsnorm_body(x_ref, g_ref, o_ref):
    x = x_ref[...]
    y = x * lax.rsqrt(jnp.mean(x * x, axis=-1, keepdims=True) + EPS)
    o_ref[...] = y * g_ref[...]


def _rmsnorm_pallas(x, g, rows=512):
    shape = x.shape
    x2 = x.reshape(-1, shape[-1])
    n, d = x2.shape
    rows = min(rows, n)
    out = pl.pallas_call(
        _rmsnorm_body,
        grid=(n // rows,),
        in_specs=[pl.BlockSpec((rows, d), lambda i: (i, 0)),
                  pl.BlockSpec((1, d), lambda i: (0, 0))],
        out_specs=pl.BlockSpec((rows, d), lambda i: (i, 0)),
        out_shape=jax.ShapeDtypeStruct((n, d), x.dtype),
        name="final_rmsnorm",
    )(x2, g.reshape(1, d))
    return out.reshape(shape)


VMEM_LIMIT_V7X = 48 * 1024 * 1024


def _sb_attn_body(q_ref, k_ref, v_ref, o_ref, acc_ref, carry_ref, *, tq, tk, off, n_diag):
    qi = pl.program_id(1)
    q0 = off + qi * tq
    n_full = q0 // tk
    acc_ref[...] = jnp.zeros_like(acc_ref)
    carry_ref[...] = jnp.zeros_like(carry_ref)
    q_all = q_ref[0]
    q_heads = [q_all[:, h * HEAD_DIM:(h + 1) * HEAD_DIM] for h in range(SB_HEADS)]
    jj = lax.broadcasted_iota(jnp.int32, (tk, tk), 0)
    ss = lax.broadcasted_iota(jnp.int32, (tk, tk), 1)
    later_mat = (jj > ss).astype(BF16)

    def block(kb, masked):
        start = pl.multiple_of(kb * tk, tk)
        k_blk = k_ref[0, pl.ds(start, tk), :]
        v_blk = v_ref[0, pl.ds(start, tk), :]
        if masked:
            key_pos = start + lax.broadcasted_iota(jnp.int32, (tq, tk), 1)
            q_pos = q0 + lax.broadcasted_iota(jnp.int32, (tq, tk), 0)
            mask = key_pos < q_pos
        outs = []
        for h in range(SB_HEADS):
            kh = k_blk[:, h * HEAD_DIM:(h + 1) * HEAD_DIM]
            vh = v_blk[:, h * HEAD_DIM:(h + 1) * HEAD_DIM]
            l = lax.dot_general(q_heads[h], kh, (((1,), (1,)), ((), ())), preferred_element_type=F32)
            lsm = -(jnp.maximum(l, 0.0) + jnp.log(1.0 + jnp.exp(-jnp.abs(l))))
            if masked:
                lsm = jnp.where(mask, lsm, 0.0)
            later = jnp.dot(lsm.astype(BF16), later_mat, preferred_element_type=F32)
            carry = carry_ref[h]
            expo = l + lsm + later + jnp.concatenate([carry] * (tk // 128), axis=1)
            w = jnp.exp(expo)
            if masked:
                w = jnp.where(mask, w, 0.0)
            outs.append(jnp.dot(w.astype(BF16), vh, preferred_element_type=F32))
            row = later[:, 0:1] + lsm[:, 0:1]
            carry_ref[h] = carry + jnp.broadcast_to(row, carry.shape)
        acc_ref[...] += jnp.concatenate(outs, axis=1)

    for d in range(n_diag - 1, -1, -1):
        block(n_full + d, True)

    def full_step(i, c):
        block(n_full - 1 - i, False)
        return c

    lax.fori_loop(0, n_full, full_step, 0)
    o_ref[0] = acc_ref[...].astype(o_ref.dtype)


def _sb_attention(q, k, v, *, off, tq, tk, out_dtype=BF16):
    B, Lq, W = q.shape
    Lk = k.shape[1]
    assert W == BRANCH_W and Lq % tq == 0 and Lk % tk == 0 and tk % 128 == 0
    nq = Lq // tq
    assert nq == 1 or (tq % tk == 0 and off % tk == 0)
    n_diag = -(-((off % tk) + tq - 1) // tk)
    assert (off + Lq - 1 + tk - 1) // tk <= Lk // tk
    body = partial(_sb_attn_body, tq=tq, tk=tk, off=off, n_diag=n_diag)
    return pl.pallas_call(
        body,
        grid=(B, nq),
        in_specs=[pl.BlockSpec((1, tq, W), lambda b, i: (b, i, 0)),
                  pl.BlockSpec((1, Lk, W), lambda b, i: (b, 0, 0)),
                  pl.BlockSpec((1, Lk, W), lambda b, i: (b, 0, 0))],
        out_specs=pl.BlockSpec((1, tq, W), lambda b, i: (b, i, 0)),
        out_shape=jax.ShapeDtypeStruct((B, Lq, W), out_dtype),
        scratch_shapes=[pltpu.VMEM((tq, W), F32), pltpu.VMEM((SB_HEADS, tq, 128), F32)],
        compiler_params=pltpu.CompilerParams(dimension_semantics=("arbitrary", "arbitrary"),
                                             vmem_limit_bytes=VMEM_LIMIT_V7X),
        name="sb_attention",
    )(q, k, v)


RW_SUB = 16
HI = lax.Precision.HIGHEST


def _split(x):
    hi = x.astype(BF16)
    lo = (x - hi.astype(F32)).astype(BF16)
    return hi, lo


def _mm3(a, b):
    ah, al = _split(a)
    bh, bl = _split(b)
    d = partial(jnp.dot, preferred_element_type=F32)
    return d(ah, bh) + (d(ah, bl) + d(al, bh))


def _dot_nt(a, b):
    return lax.dot_general(a, b, (((1,), (1,)), ((), ())), preferred_element_type=F32)


def _dot_tn(a, b):
    return lax.dot_general(a, b, (((0,), (0,)), ((), ())), preferred_element_type=F32)


def _rwkv_body(z_ref, shift_ref, s0_ref, mu_ref, w0_ref, w2_ref, a0_ref, a2_ref, g2_ref, kk_ref, ka_ref,
               rk_ref, lng_ref, lnb_ref, o_ref, s_out_ref, shift_out_ref, state_ref, prev_ref, *, T):
    ci = pl.program_id(1)
    nc = pl.num_programs(1)
    W = BRANCH_W
    N = RW_HEADS * T

    @pl.when(ci == 0)
    def _():
        state_ref[...] = s0_ref[0]
        prev_ref[...] = shift_ref[0]

    z = z_ref[0]
    row = lax.broadcasted_iota(jnp.int32, (T, RW_COLS), 0)
    z_prev = jnp.where(row == 0, jnp.broadcast_to(prev_ref[...], (T, RW_COLS)), pltpu.roll(z, 1, axis=0))
    prev_ref[...] = z[T - 1:T, :]
    zs = z + (z_prev - z) * mu_ref[...]
    r = zs[:, 0:W]
    k = zs[:, W:2 * W]
    v = zs[:, 2 * W:3 * W]
    xwa = zs[:, 3 * W:3 * W + 128]
    xg = zs[:, 3 * W + 128:]
    dotf = partial(jnp.dot, preferred_element_type=F32, precision=HI)
    w_log = -jax.nn.softplus(-(w0_ref[...] + dotf(jnp.tanh(xwa), w2_ref[...]))) - 0.5
    logdec = -jnp.exp(w_log)
    a = jax.nn.sigmoid(a0_ref[...] + dotf(xwa, a2_ref[...]))
    g = dotf(jax.nn.sigmoid(xg), g2_ref[...])
    li = lax.broadcasted_iota(jnp.int32, (W, W), 0) // RW_HD
    lj = lax.broadcasted_iota(jnp.int32, (W, W), 1) // RW_HD
    head_sum = (li == lj).astype(F32)
    kk = k * kk_ref[...]
    kk = kk * lax.rsqrt(dotf(kk * kk, head_sum) + 1e-12)
    k2 = k * (1.0 + (a - 1.0) * ka_ref[...])
    beta = kk * a
    ti = lax.broadcasted_iota(jnp.int32, (T, T), 0)
    tj = lax.broadcasted_iota(jnp.int32, (T, T), 1)
    b = dotf((ti >= tj).astype(F32), logdec)
    b_last = b[T - 1:T, :]
    k_in = kk * jnp.exp(b - logdec)
    r_in = r * jnp.exp(b)
    inv_p = jnp.exp(-b)
    k_out = k2 * inv_p
    b_out = beta * inv_p
    to_end = jnp.exp(b_last - b)
    k_end = k2 * to_end
    b_end = beta * to_end

    sh = lax.broadcasted_iota(jnp.int32, (N, W), 0) // T
    sl = lax.broadcasted_iota(jnp.int32, (N, W), 1) // RW_HD
    own = sh == sl

    def stack(x, masked):
        xs = jnp.concatenate([x] * RW_HEADS, axis=0)
        return jnp.where(own, xs, 0.0) if masked else xs

    k_in_s = stack(k_in, True).astype(BF16)
    r_in_s = stack(r_in, True).astype(BF16)
    k_out_s = stack(k_out, False).astype(BF16)
    b_out_s = stack(b_out, False).astype(BF16)
    v_s = stack(v, True).astype(BF16)
    k_end_s = stack(k_end, True).astype(BF16)
    b_end_s = stack(b_end, True).astype(BF16)

    ri = lax.broadcasted_iota(jnp.int32, (N, N), 0)
    rj = lax.broadcasted_iota(jnp.int32, (N, N), 1)
    same_head = (ri // T) == (rj // T)
    strict = same_head & (ri > rj)
    incl = same_head & (ri >= rj)
    a_mat = jnp.where(strict, _dot_nt(k_in_s, b_out_s), 0.0)
    kk_mat = jnp.where(strict, _dot_nt(k_in_s, k_out_s), 0.0)
    rk_mat = jnp.where(incl, _dot_nt(r_in_s, k_out_s), 0.0)
    rb_mat = jnp.where(incl, _dot_nt(r_in_s, b_out_s), 0.0)

    eye = (ri == rj).astype(F32)
    a_bd = jnp.where((ri // RW_SUB) == (rj // RW_SUB), a_mat, 0.0)
    x = eye - a_bd
    p = _mm3(a_bd, a_bd)
    x = x + _mm3(x, p)
    p = _mm3(p, p)
    x = x + _mm3(x, p)
    p = _mm3(p, p)
    x = x + _mm3(x, p)
    size = RW_SUB
    while size < T:
        lower = ((ri // (2 * size)) == (rj // (2 * size))) & ((ri // size) > (rj // size))
        x = x - _mm3(_mm3(x, jnp.where(lower, a_mat, 0.0)), x)
        size *= 2

    state = state_ref[...]
    state_b = state.astype(BF16)
    d = partial(jnp.dot, preferred_element_type=F32)
    rhs = _dot_nt(k_in_s, state_b) + d(kk_mat.astype(BF16), v_s)
    u = _mm3(x, rhs)
    u_b = u.astype(BF16)
    o_s = _dot_nt(r_in_s, state_b) + d(rk_mat.astype(BF16), v_s) - d(rb_mat.astype(BF16), u_b)
    o = o_s[0:T]
    for h in range(1, RW_HEADS):
        o = o + o_s[h * T:(h + 1) * T]
    state_ref[...] = state * jnp.exp(b_last) + _dot_tn(v_s, k_end_s) - _dot_tn(u_b, b_end_s)

    head_mean = head_sum * (1.0 / RW_HD)
    mean = dotf(o, head_mean)
    var = dotf(jnp.square(o - mean), head_mean)
    o = (o - mean) * lax.rsqrt(var + RW_GN_EPS) * lng_ref[...] + lnb_ref[...]
    bonus = dotf(r * k2 * rk_ref[...], head_sum) * v
    o_ref[0] = ((o + bonus) * g).astype(o_ref.dtype)

    @pl.when(ci == nc - 1)
    def _():
        s_out_ref[0] = state_ref[...]
        shift_out_ref[0] = z[T - 1:T, :]


def _rwkv7_pallas(z, s0_bd, shift_prev, mu, w0, w2, a0, a2, g2, k_k, k_a, r_k, ln_g, ln_b, *, T, out_dtype=F32):
    B, L, C = z.shape
    assert C == RW_COLS and L % T == 0 and T % RW_SUB == 0
    W = BRANCH_W
    w2p = jnp.concatenate([w2, jnp.zeros_like(w2)], axis=0)
    a2p = jnp.concatenate([jnp.zeros_like(a2), a2], axis=0)
    vec = lambda t: t.reshape(1, -1).astype(F32)
    full = lambda shape: pl.BlockSpec(shape, lambda b, c: (0,) * len(shape))
    return pl.pallas_call(
        partial(_rwkv_body, T=T),
        grid=(B, L // T),
        in_specs=[pl.BlockSpec((1, T, C), lambda b, c: (b, c, 0)),
                  pl.BlockSpec((1, 1, C), lambda b, c: (b, 0, 0)),
                  pl.BlockSpec((1, W, W), lambda b, c: (b, 0, 0)),
                  full((1, C)), full((1, W)), full((128, W)), full((1, W)), full((128, W)), full((128, W)),
                  full((1, W)), full((1, W)), full((1, W)), full((1, W)), full((1, W))],
        out_specs=[pl.BlockSpec((1, T, W), lambda b, c: (b, c, 0)),
                   pl.BlockSpec((1, W, W), lambda b, c: (b, 0, 0)),
                   pl.BlockSpec((1, 1, C), lambda b, c: (b, 0, 0))],
        out_shape=[jax.ShapeDtypeStruct((B, L, W), out_dtype),
                   jax.ShapeDtypeStruct((B, W, W), F32),
                   jax.ShapeDtypeStruct((B, 1, C), F32)],
        scratch_shapes=[pltpu.VMEM((W, W), F32), pltpu.VMEM((1, C), F32)],
        compiler_params=pltpu.CompilerParams(dimension_semantics=("arbitrary", "arbitrary")),
        name="rwkv7_chunked",
    )(z, shift_prev.reshape(B, 1, C), s0_bd, vec(mu), vec(w0), w2p, vec(a0), a2p, g2, vec(k_k), vec(k_a),
      vec(r_k), vec(ln_g), vec(ln_b))


def _state_to_bd(s):
    B = s.shape[0]
    eye = jnp.eye(RW_HEADS, dtype=s.dtype)
    return (s[:, :, :, None, :] * eye[None, :, None, :, None]).reshape(B, BRANCH_W, BRANCH_W)


def _state_from_bd(s_bd):
    B = s_bd.shape[0]
    s5 = s_bd.reshape(B, RW_HEADS, RW_HD, RW_HEADS, RW_HD)
    return jnp.stack([s5[:, h, :, h, :] for h in range(RW_HEADS)], axis=1)


def _hgrn_body(z_ref, s0_ref, loglb_ref, log1mlb_ref, ng_ref, o_ref, s_out_ref, state_ref, *, T):
    ci = pl.program_id(1)
    nc = pl.num_programs(1)
    W = BRANCH_W

    @pl.when(ci == 0)
    def _():
        state_ref[...] = s0_ref[0]

    z = z_ref[0]
    zq, zf, zi, zg = z[:, 0:W], z[:, W:2 * W], z[:, 2 * W:3 * W], z[:, 3 * W:]
    log_sig = jnp.minimum(zf, 0.0) - jnp.log(1.0 + jnp.exp(-jnp.abs(zf)))
    a = loglb_ref[...]
    bv = log1mlb_ref[...] + log_sig
    m = jnp.maximum(a, bv)
    log_f = m + jnp.log(jnp.exp(a - m) + jnp.exp(bv - m))
    k = 1.0 - jnp.exp(log_f)
    q = zq * jax.nn.sigmoid(zq)
    dotf = partial(jnp.dot, preferred_element_type=F32, precision=HI)
    ti = lax.broadcasted_iota(jnp.int32, (T, T), 0)
    tj = lax.broadcasted_iota(jnp.int32, (T, T), 1)
    b = dotf((ti >= tj).astype(F32), log_f)
    li = lax.broadcasted_iota(jnp.int32, (W, W), 0) // HG_DK
    lj = lax.broadcasted_iota(jnp.int32, (W, W), 1) // HG_DK
    same_head = li == lj
    head_sum = same_head.astype(BF16)

    t_idx = lax.broadcasted_iota(jnp.int32, (T, W), 0)
    prods = []
    for s in range(T):
        e = jnp.exp(jnp.where(t_idx >= s, b - b[s:s + 1, :], -jnp.inf))
        prods.append((q * e * k[s:s + 1, :]).astype(BF16))
    scores = jnp.dot(jnp.concatenate(prods, axis=0), head_sum, preferred_element_type=F32)
    o = jnp.zeros((T, W), F32)
    for s in range(T):
        o = o + scores[s * T:(s + 1) * T, :] * zi[s:s + 1, :]

    state = state_ref[...]
    o = o + _dot_nt((q * jnp.exp(b)).astype(BF16), state.astype(BF16))
    b_last = b[T - 1:T, :]
    k_end = (k * jnp.exp(b_last - b)).astype(BF16)
    upd = _dot_tn(zi.astype(BF16), k_end)
    state_ref[...] = state * jnp.exp(b_last) + jnp.where(same_head, upd, 0.0)

    ms = dotf(o * o, same_head.astype(F32) * (1.0 / HG_DK))
    o = o * lax.rsqrt(ms + EPS) * ng_ref[...]
    o_ref[0] = (o * (zg * jax.nn.sigmoid(zg))).astype(o_ref.dtype)

    @pl.when(ci == nc - 1)
    def _():
        s_out_ref[0] = state_ref[...]


def _hgrn2_pallas(z, s0_bd, lb, norm_g, *, T, out_dtype=F32):
    B, L, C = z.shape
    assert C == HG_COLS and L % T == 0
    W = BRANCH_W
    full = lambda shape: pl.BlockSpec(shape, lambda b, c: (0,) * len(shape))
    return pl.pallas_call(
        partial(_hgrn_body, T=T),
        grid=(B, L // T),
        in_specs=[pl.BlockSpec((1, T, C), lambda b, c: (b, c, 0)),
                  pl.BlockSpec((1, W, W), lambda b, c: (b, 0, 0)),
                  full((1, W)), full((1, W)), full((1, W))],
        out_specs=[pl.BlockSpec((1, T, W), lambda b, c: (b, c, 0)),
                   pl.BlockSpec((1, W, W), lambda b, c: (b, 0, 0))],
        out_shape=[jax.ShapeDtypeStruct((B, L, W), out_dtype),
                   jax.ShapeDtypeStruct((B, W, W), F32)],
        scratch_shapes=[pltpu.VMEM((W, W), F32)],
        compiler_params=pltpu.CompilerParams(dimension_semantics=("arbitrary", "arbitrary")),
        name="hgrn2_chunked",
    )(z, s0_bd, jnp.log(lb).reshape(1, W), jnp.log1p(-lb).reshape(1, W), jnp.tile(norm_g, HG_HEADS).reshape(1, W))


PROJ_TM = 256
PROJ_VMEM_LIMIT_V7X = 48 * 1024 * 1024
CONV_CARRY = 32
CONV_TC = 512


def _in_proj_body(x_ref, g_ref, w_ref, q_ref, k_ref, v_ref, kb_ref, vb_ref, hg_ref, cv_ref, rw_ref, gate_ref):
    x = x_ref[...]
    h = (x * lax.rsqrt(jnp.mean(x * x, axis=-1, keepdims=True) + EPS) * g_ref[...]).astype(BF16)
    W = BRANCH_W

    def proj(a, b):
        return jnp.dot(h, w_ref[:, a:b], preferred_element_type=F32)

    q_ref[...] = (proj(0, W) * (HEAD_DIM ** -0.5)).astype(BF16)
    k = proj(W, 2 * W)
    v = proj(2 * W, 3 * W)
    k_ref[...] = k
    v_ref[...] = v
    kb_ref[...] = k.astype(BF16)
    vb_ref[...] = v.astype(BF16)
    c = SB_COLS
    hg_ref[...] = proj(c, c + HG_COLS)
    c += HG_COLS
    cv_ref[...] = proj(c, c + CV_COLS)
    c += CV_COLS
    rw_ref[...] = proj(c, c + RW_COLS)
    c += RW_COLS
    for n in range(N_BRANCH):
        gate_ref[:, n * D_MODEL:(n + 1) * D_MODEL] = jax.nn.sigmoid(
            proj(c + n * D_MODEL, c + (n + 1) * D_MODEL)).astype(BF16)


def _in_proj(x2, norm_g, w_in_b):
    n = x2.shape[0]
    TM = min(PROJ_TM, n)
    assert n % TM == 0
    row = lambda w: pl.BlockSpec((TM, w), lambda i: (i, 0))
    shp = lambda w, dt: jax.ShapeDtypeStruct((n, w), dt)
    W = BRANCH_W
    return pl.pallas_call(
        _in_proj_body,
        grid=(n // TM,),
        in_specs=[row(D_MODEL), pl.BlockSpec((1, D_MODEL), lambda i: (0, 0)),
                  pl.BlockSpec((D_MODEL, IN_COLS), lambda i: (0, 0), pipeline_mode=pl.Buffered(1))],
        out_specs=[row(W), row(W), row(W), row(W), row(W), row(HG_COLS), row(CV_COLS), row(RW_COLS), row(GATE_COLS)],
        out_shape=[shp(W, BF16), shp(W, F32), shp(W, F32), shp(W, BF16), shp(W, BF16),
                   shp(HG_COLS, F32), shp(CV_COLS, F32), shp(RW_COLS, F32), shp(GATE_COLS, BF16)],
        compiler_params=pltpu.CompilerParams(dimension_semantics=("arbitrary",), vmem_limit_bytes=PROJ_VMEM_LIMIT_V7X),
        name="in_proj",
    )(x2, norm_g.reshape(1, -1), w_in_b)


def _merge_body(x_ref, sb_ref, hg_ref, cv_ref, rw_ref, gate_ref, wb_ref, wo_ref, o_ref):
    merged = None
    for n, br in enumerate((sb_ref, hg_ref, cv_ref, rw_ref)):
        t = jnp.dot(br[...].astype(BF16), wb_ref[n], preferred_element_type=F32)
        t = t * gate_ref[:, n * D_MODEL:(n + 1) * D_MODEL].astype(F32)
        merged = t if merged is None else merged + t
    o_ref[...] = x_ref[...] + jnp.dot(merged.astype(BF16), wo_ref[...], preferred_element_type=F32)


def _merge(x2, o_sb, o_hg, o_cv, o_rw, gates, wb_b, wo_b):
    n = x2.shape[0]
    TM = min(PROJ_TM, n)
    row = lambda w: pl.BlockSpec((TM, w), lambda i: (i, 0))
    return pl.pallas_call(
        _merge_body,
        grid=(n // TM,),
        in_specs=[row(D_MODEL), row(BRANCH_W), row(BRANCH_W), row(BRANCH_W), row(BRANCH_W), row(GATE_COLS),
                  pl.BlockSpec((N_BRANCH, BRANCH_W, D_MODEL), lambda i: (0, 0, 0)),
                  pl.BlockSpec((D_MODEL, D_MODEL), lambda i: (0, 0))],
        out_specs=row(D_MODEL),
        out_shape=jax.ShapeDtypeStruct((n, D_MODEL), F32),
        compiler_params=pltpu.CompilerParams(dimension_semantics=("arbitrary",), vmem_limit_bytes=PROJ_VMEM_LIMIT_V7X),
        name="branch_merge",
    )(x2, o_sb, o_hg, o_cv, o_rw, gates, wb_b, wo_b)


def _conv_body(z_ref, buf_ref, w_ref, b_ref, lg_ref, lb_ref, o_ref, buf_out_ref, hp_ref, *, TC):
    ci = pl.program_id(1)
    nc = pl.num_programs(1)

    @pl.when(ci == 0)
    def _():
        hp_ref[0:CONV_CARRY, :] = buf_ref[0]

    z = z_ref[0]
    hp_ref[CONV_CARRY:, :] = z[:, 0:CONV_CH] * jax.nn.sigmoid(z[:, CONV_CH:])
    first = CONV_CARRY - (CONV_WIDTH - 1)
    y = jnp.zeros((TC, CONV_CH), F32) + b_ref[...]
    for j in range(CONV_WIDTH):
        y = y + hp_ref[first + j:first + j + TC, :] * w_ref[j:j + 1, :]
    mu = jnp.mean(y, axis=-1, keepdims=True)
    d = y - mu
    var = jnp.mean(d * d, axis=-1, keepdims=True)
    y = d * lax.rsqrt(var + EPS) * lg_ref[...] + lb_ref[...]
    o_ref[0] = (y * jax.nn.sigmoid(y)).astype(o_ref.dtype)
    tail = hp_ref[TC:TC + CONV_CARRY, :]

    @pl.when(ci == nc - 1)
    def _():
        buf_out_ref[0] = tail

    hp_ref[0:CONV_CARRY, :] = tail


def _conformer_conv_pallas(z_cv, buf, w, b, ln_g, ln_b, *, TC, out_dtype=F32):
    B, L, _ = z_cv.shape
    assert L % TC == 0 and TC % 8 == 0 and TC >= CONV_CARRY
    C = CONV_CH
    pad = CONV_CARRY - (CONV_WIDTH - 1)
    buf_p = jnp.pad(buf.astype(F32), ((0, 0), (pad, 0), (0, 0)))
    w_p = jnp.pad(w.astype(F32), ((0, 32 - CONV_WIDTH), (0, 0)))
    full = lambda shape: pl.BlockSpec(shape, lambda b, c: (0,) * len(shape))
    y, buf_new = pl.pallas_call(
        partial(_conv_body, TC=TC),
        grid=(B, L // TC),
        in_specs=[pl.BlockSpec((1, TC, 2 * C), lambda b, c: (b, c, 0)),
                  pl.BlockSpec((1, CONV_CARRY, C), lambda b, c: (b, 0, 0)),
                  full((32, C)), full((1, C)), full((1, C)), full((1, C))],
        out_specs=[pl.BlockSpec((1, TC, C), lambda b, c: (b, c, 0)),
                   pl.BlockSpec((1, CONV_CARRY, C), lambda b, c: (b, 0, 0))],
        out_shape=[jax.ShapeDtypeStruct((B, L, C), out_dtype), jax.ShapeDtypeStruct((B, CONV_CARRY, C), F32)],
        scratch_shapes=[pltpu.VMEM((CONV_CARRY + TC, C), F32)],
        compiler_params=pltpu.CompilerParams(dimension_semantics=("arbitrary", "arbitrary")),
        name="conformer_conv",
    )(z_cv, buf_p, w_p, b.reshape(1, C), ln_g.reshape(1, C), ln_b.reshape(1, C))
    return y, buf_new[:, pad:, :]


PK_PAIRS = PK_HEADS * PK_TOPK
PEER_TB = 128
SUBLANES = 8
LANES = 128
ROW_TILES = D_MODEL // LANES
PEER_VMEM_LIMIT_V7X = 52 * 1024 * 1024


def _topk_rows(s, payload, k):
    n = s.shape[0]
    rows = lax.broadcasted_iota(jnp.int32, s.shape, 0)
    vals, ids = [], []
    for _ in range(k):
        m = jnp.max(s, axis=0, keepdims=True)
        pos = jnp.min(jnp.where(s == m, rows, n), axis=0, keepdims=True)
        sel = rows == pos
        ids.append(pos if payload is None else jnp.max(jnp.where(sel, payload, -1), axis=0, keepdims=True))
        vals.append(m)
        s = jnp.where(sel, -jnp.inf, s)
    return jnp.concatenate(vals, axis=0), jnp.concatenate(ids, axis=0)


def _dot3(ah, al, bh, bl, dims):
    d = partial(lax.dot_general, dimension_numbers=dims, preferred_element_type=F32)
    return d(ah, bh) + (d(ah, bl) + d(al, bh))


def _peer_route_body(x_ref, g_ref, wqh_ref, wql_ref, qg_ref, k1_ref, k2_ref, xn_ref, idx_ref, idxc_ref, gate_ref):
    x = x_ref[...]
    xn = x * lax.rsqrt(jnp.mean(x * x, axis=-1, keepdims=True) + EPS) * g_ref[...]
    xn_ref[...] = xn
    nn = (((1,), (0,)), ((), ()))
    nt = (((1,), (1,)), ((), ()))
    xh, xl = _split(xn)
    q = _dot3(xh, xl, wqh_ref[...], wql_ref[...], nn)
    half = PK_DQ // 2
    experts, gates = [], []
    for h in range(PK_HEADS):
        qh = q[:, h * PK_DQ:(h + 1) * PK_DQ]
        qh = qh * lax.rsqrt(jnp.mean(qh * qh, axis=-1, keepdims=True) + EPS) * qg_ref[...]
        qhh, qhl = _split(qh)
        k1h, k1l = _split(k1_ref[h])
        k2h, k2l = _split(k2_ref[h])
        s1 = _dot3(k1h, k1l, qhh[:, :half], qhl[:, :half], nt)
        s2 = _dot3(k2h, k2l, qhh[:, half:], qhl[:, half:], nt)
        v1, i1 = _topk_rows(s1, None, PK_TOPK)
        v2, i2 = _topk_rows(s2, None, PK_TOPK)
        cand_s = jnp.concatenate([v1[a:a + 1, :] + v2 for a in range(PK_TOPK)], axis=0)
        cand_i = jnp.concatenate([i1[a:a + 1, :] * PK_NKEYS + i2 for a in range(PK_TOPK)], axis=0)
        top_s, expert = _topk_rows(cand_s, cand_i, PK_TOPK)
        e = jnp.exp(top_s - jnp.max(top_s, axis=0, keepdims=True))
        gates.append(e / jnp.sum(e, axis=0, keepdims=True))
        experts.append(expert)
    expert = jnp.concatenate(experts, axis=0)
    row = (expert + 1) * HALF_TILES
    pair = lax.broadcasted_iota(jnp.int32, row.shape, 0)
    idx_ref[...] = row.T
    idxc_ref[...] = (row - (pair & HALF_TILES)).T
    gate_ref[...] = jnp.concatenate(gates, axis=0)


def _peer_route(x2, norm_g, wq, qn_g, k1, k2):
    n_tok = x2.shape[0]
    TB = PEER_TB
    assert n_tok % TB == 0
    nb = n_tok // TB
    full = lambda shape: pl.BlockSpec(shape, lambda i: (0,) * len(shape))
    wq_hi = wq.astype(BF16)
    wq_lo = (wq - wq_hi.astype(F32)).astype(BF16)
    return pl.pallas_call(
        _peer_route_body,
        grid=(nb,),
        in_specs=[pl.BlockSpec((TB, D_MODEL), lambda i: (i, 0)), full((1, D_MODEL)),
                  full((D_MODEL, PK_HEADS * PK_DQ)), full((D_MODEL, PK_HEADS * PK_DQ)), full((1, PK_DQ)),
                  full((PK_HEADS, PK_NKEYS, PK_DQ // 2)), full((PK_HEADS, PK_NKEYS, PK_DQ // 2))],
        out_specs=[pl.BlockSpec((TB, D_MODEL), lambda i: (i, 0)),
                   pl.BlockSpec((TB, PK_PAIRS), lambda i: (i, 0)),
                   pl.BlockSpec((TB, PK_PAIRS), lambda i: (i, 0)),
                   pl.BlockSpec((PK_PAIRS, TB), lambda i: (0, i))],
        out_shape=[jax.ShapeDtypeStruct((n_tok, D_MODEL), F32),
                   jax.ShapeDtypeStruct((n_tok, PK_PAIRS), jnp.int32),
                   jax.ShapeDtypeStruct((n_tok, PK_PAIRS), jnp.int32),
                   jax.ShapeDtypeStruct((PK_PAIRS, n_tok), F32)],
        compiler_params=pltpu.CompilerParams(dimension_semantics=("arbitrary",)),
        name="peer_route",
    )(x2, norm_g.reshape(1, -1), wq_hi, wq_lo, qn_g.reshape(1, -1), k1, k2)


HALF_TILES = ROW_TILES // 2


def _unpack_words(word):
    return pltpu.bitcast(word << 16, F32), pltpu.bitcast(word & jnp.int32(-65536), F32)


def _fold_couples(tiles, sub):
    m2 = (sub & 3) < 2
    lvl2 = []
    for a, b in ((tiles[0], tiles[2]), (tiles[1], tiles[3])):
        lvl2.append(jnp.where(m2, a, b) + jnp.where(m2, pltpu.roll(a, 6, axis=0), pltpu.roll(b, 2, axis=0)))
    m1 = (sub & 1) == 0
    a, b = lvl2
    return jnp.where(m1, a, b) + jnp.where(m1, pltpu.roll(a, 7, axis=0), pltpu.roll(b, 1, axis=0))


def _peer_act_body(idx_ref, x_ref, gate_ref, tab_ref, w_ref, part_ref):
    TB = x_ref.shape[0]
    sub = lax.broadcasted_iota(jnp.int32, (SUBLANES, LANES), 0)
    low = sub < HALF_TILES

    def token(t, c):
        x = x_ref[t]
        x_lo = jnp.concatenate([x[0:HALF_TILES]] * 2, axis=0)
        x_hi = jnp.concatenate([x[HALF_TILES:]] * 2, axis=0)
        start = t * PK_PAIRS
        for g in range(PK_PAIRS // SUBLANES):
            tiles = []
            for j in range(HALF_TILES):
                ra = pl.multiple_of(idx_ref[0, 0, start + g * SUBLANES + j], HALF_TILES)
                rb = pl.multiple_of(idx_ref[0, 0, start + g * SUBLANES + j + HALF_TILES], HALF_TILES)
                word = jnp.where(low, tab_ref[pl.ds(ra, SUBLANES), :], tab_ref[pl.ds(rb, SUBLANES), :])
                lo, hi = _unpack_words(word)
                tiles.append(lo * x_lo + hi * x_hi)
            part_ref[t, g * SUBLANES:(g + 1) * SUBLANES, :] = _fold_couples(tiles, sub)
        return c

    lax.fori_loop(0, TB, token, 0)
    lane = lax.broadcasted_iota(jnp.int32, (PK_PAIRS, TB), 1)
    s = jnp.zeros((PK_PAIRS, TB), F32)
    for t in range(TB):
        s = jnp.where(lane == t, jnp.sum(part_ref[t], axis=1, keepdims=True), s)
    w_ref[...] = gate_ref[...] * jax.nn.gelu(s)


PEER_MIX_ACCS = 4


def _peer_mix_body(idx_ref, w_ref, tab_ref, y_ref, wrep_ref):
    TB = y_ref.shape[0]
    w_all = w_ref[...]
    for t in range(TB):
        wrep_ref[t] = jnp.broadcast_to(w_all[:, t:t + 1], (PK_PAIRS, LANES))

    def token(t, c):
        start = t * PK_PAIRS
        acc_lo = [jnp.zeros((HALF_TILES, LANES), F32) for _ in range(PEER_MIX_ACCS)]
        acc_hi = [jnp.zeros((HALF_TILES, LANES), F32) for _ in range(PEER_MIX_ACCS)]
        for p in range(PK_PAIRS):
            row = pl.multiple_of(idx_ref[0, 0, start + p], HALF_TILES)
            lo, hi = _unpack_words(tab_ref[pl.ds(row, HALF_TILES), :])
            w = jnp.broadcast_to(wrep_ref[t, p:p + 1, :], (HALF_TILES, LANES))
            acc_lo[p % PEER_MIX_ACCS] = acc_lo[p % PEER_MIX_ACCS] + w * lo
            acc_hi[p % PEER_MIX_ACCS] = acc_hi[p % PEER_MIX_ACCS] + w * hi
        y_ref[t] = jnp.concatenate([(acc_lo[0] + acc_lo[1]) + (acc_lo[2] + acc_lo[3]),
                                    (acc_hi[0] + acc_hi[1]) + (acc_hi[2] + acc_hi[3])], axis=0)
        return c

    lax.fori_loop(0, TB, token, 0)


def _pack_table(tab):
    bits = lax.bitcast_convert_type(tab.astype(BF16), jnp.uint16).astype(jnp.uint32)
    bits = bits.reshape(PK_EXPERTS, 2, HALF_TILES, LANES)
    word = bits[:, 0] | (bits[:, 1] << 16)
    rows = lax.bitcast_convert_type(word, jnp.int32).reshape(PK_EXPERTS * HALF_TILES, LANES)
    return jnp.pad(rows, ((HALF_TILES, HALF_TILES), (0, 0)))


def _smem_spec(TB):
    return pl.BlockSpec((1, 1, PK_PAIRS * TB), lambda i: (i, 0, 0), memory_space=pltpu.SMEM)


def _table_spec():
    return pl.BlockSpec(((PK_EXPERTS + 2) * HALF_TILES, LANES), lambda i: (0, 0), pipeline_mode=pl.Buffered(1))


def _peer_act(idx, xn3, gate_t, tab_packed):
    n_tok = xn3.shape[0]
    TB = PEER_TB
    nb = n_tok // TB
    return pl.pallas_call(
        _peer_act_body,
        grid=(nb,),
        in_specs=[_smem_spec(TB),
                  pl.BlockSpec((TB, ROW_TILES, LANES), lambda i: (i, 0, 0)),
                  pl.BlockSpec((PK_PAIRS, TB), lambda i: (0, i)),
                  _table_spec()],
        out_specs=pl.BlockSpec((PK_PAIRS, TB), lambda i: (0, i)),
        out_shape=jax.ShapeDtypeStruct((PK_PAIRS, n_tok), F32),
        scratch_shapes=[pltpu.VMEM((TB, PK_PAIRS, LANES), F32)],
        compiler_params=pltpu.CompilerParams(dimension_semantics=("arbitrary",),
                                             vmem_limit_bytes=PEER_VMEM_LIMIT_V7X),
        name="peer_act",
    )(idx.reshape(nb, 1, TB * PK_PAIRS), xn3, gate_t, tab_packed)


def _peer_mix(idx, w_t, tab_packed):
    n_tok = w_t.shape[1]
    TB = PEER_TB
    nb = n_tok // TB
    return pl.pallas_call(
        _peer_mix_body,
        grid=(nb,),
        in_specs=[_smem_spec(TB),
                  pl.BlockSpec((PK_PAIRS, TB), lambda i: (0, i)),
                  _table_spec()],
        out_specs=pl.BlockSpec((TB, ROW_TILES, LANES), lambda i: (i, 0, 0)),
        out_shape=jax.ShapeDtypeStruct((n_tok, ROW_TILES, LANES), F32),
        scratch_shapes=[pltpu.VMEM((TB, PK_PAIRS, LANES), F32)],
        compiler_params=pltpu.CompilerParams(dimension_semantics=("arbitrary",),
                                             vmem_limit_bytes=PEER_VMEM_LIMIT_V7X),
        name="peer_mix",
    )(idx.reshape(nb, 1, TB * PK_PAIRS), w_t, tab_packed)


def _peer_ffn_pallas(x, norm_g, wq, qn_g, sub_keys, u_packed, v_packed):
    B, L, D = x.shape
    n_tok = B * L
    pad = (-n_tok) % PEER_TB
    x2 = jnp.pad(x.reshape(n_tok, D), ((0, pad), (0, 0)))
    xn, idx, idx_couple, gate_t = _peer_route(x2, norm_g, wq, qn_g, sub_keys[:, 0], sub_keys[:, 1])
    w_t = _peer_act(idx_couple, xn.reshape(-1, ROW_TILES, LANES), gate_t, u_packed)
    y = _peer_mix(idx, w_t, v_packed)
    return y.reshape(-1, D)[:n_tok].reshape(B, L, D)


def _trunk_layer(x, k_past, v_past, s_hg, buf_conv, s_rw, buf_shift, p):
    B_, L, _ = x.shape
    n = B_ * L
    x2 = x.reshape(n, D_MODEL)
    qb, k, v, kb, vb, z_hg, z_cv, z_rw, gates = _in_proj(x2, p['norm_mix_g'], p['w_in'])
    seq = lambda t: t.reshape(B_, L, t.shape[-1])
    if k_past is None:
        o_sb = _sb_attention(seq(qb), seq(kb), seq(vb), off=0, tq=SB_TQ, tk=SB_TK, out_dtype=F32)
    else:
        past = k_past.shape[1]
        pad = jnp.zeros((B_, (-(past + L)) % SB_TK, BRANCH_W), BF16)
        kc = jnp.concatenate([k_past.reshape(B_, past, BRANCH_W).astype(BF16), seq(kb), pad], axis=1)
        vc = jnp.concatenate([v_past.reshape(B_, past, BRANCH_W).astype(BF16), seq(vb), pad], axis=1)
        o_sb = _sb_attention(seq(qb), kc, vc, off=past, tq=L, tk=SB_TK, out_dtype=F32)
    o_hg, s_hg_bd = _hgrn2_pallas(seq(z_hg), _state_to_bd(jnp.swapaxes(s_hg.astype(F32), 2, 3)), p['lb'],
                                  p['hg_norm_g'], T=CHUNK if L % CHUNK == 0 else L)
    s_hg_new = jnp.swapaxes(_state_from_bd(s_hg_bd), 2, 3)
    o_cv, buf_conv_new = _conformer_conv_pallas(seq(z_cv), buf_conv, p['conv_w'], p['conv_b'], p['conv_ln_g'],
                                                p['conv_ln_b'], TC=CONV_TC if L % CONV_TC == 0 else L)
    o_rw, s_rw_bd, shift_new = _rwkv7_pallas(seq(z_rw), _state_to_bd(s_rw.astype(F32)), buf_shift, p['rw_mu'],
                                             p['rw_w0'], p['rw_w2'], p['rw_a0'], p['rw_a2'], p['rw_g2'], p['rw_k_k'],
                                             p['rw_k_a'], p['rw_r_k'], p['rw_ln_g'], p['rw_ln_b'],
                                             T=RW_CHUNK if L % RW_CHUNK == 0 else L)
    s_rw_new = _state_from_bd(s_rw_bd)
    buf_shift_new = shift_new[:, 0]
    flat = lambda t: t.reshape(n, BRANCH_W)
    x = _merge(x2, flat(o_sb), flat(o_hg), flat(o_cv), flat(o_rw), gates, p['w_branch'], p['w_out']).reshape(x.shape)
    x = x + _peer_ffn_pallas(x, p['norm_ffn_g'], p['peer_wq'], p['peer_qn_g'], p['peer_keys'],
                             p['peer_u'], p['peer_v'])
    heads = lambda t: t.reshape(B_, L, SB_HEADS, HEAD_DIM)
    return x, (heads(k), heads(v), s_hg_new, buf_conv_new, s_rw_new, buf_shift_new)


def kernel(x_prompt, x_sample, cache_sb_k, cache_sb_v, state_hgrn, state_conv, state_rwkv, state_shift,
           norm_mix_g, w_in, hg_lb_logits, hg_norm_g, conv_w, conv_b, conv_ln_g, conv_ln_b,
           rw_mu, rw_w0, rw_w2, rw_a0, rw_a2, rw_g2, rw_k_k, rw_k_a, rw_r_k, rw_ln_g, rw_ln_b,
           w_branch, w_out, norm_ffn_g, peer_wq, peer_qn_g, peer_keys, peer_u, peer_v, final_norm_g):
    lb_all = jnp.cumsum(jax.nn.softmax(hg_lb_logits.astype(F32), axis=0), axis=0)
    lb_all = lb_all - lb_all[0:1]
    Bp = x_prompt.shape[0]
    hg0 = jnp.zeros((Bp, HG_HEADS, HG_DK, HG_DV), F32)
    cv0 = jnp.zeros((Bp, CONV_WIDTH - 1, CONV_CH), x_prompt.dtype)
    rw0 = jnp.zeros((Bp, RW_HEADS, RW_HD, RW_HD), F32)
    sh0 = jnp.zeros((Bp, RW_COLS), x_prompt.dtype)
    xp, xs = x_prompt, x_sample
    outs_p, outs_s = [], []
    for l in range(DEPTH):
        lp = dict(norm_mix_g=norm_mix_g[l], w_in=w_in[l].astype(BF16), lb=lb_all[l], hg_norm_g=hg_norm_g[l],
                  conv_w=conv_w[l], conv_b=conv_b[l], conv_ln_g=conv_ln_g[l], conv_ln_b=conv_ln_b[l],
                  rw_mu=rw_mu[l], rw_w0=rw_w0[l], rw_w2=rw_w2[l], rw_a0=rw_a0[l], rw_a2=rw_a2[l],
                  rw_g2=rw_g2[l], rw_k_k=rw_k_k[l], rw_k_a=rw_k_a[l], rw_r_k=rw_r_k[l],
                  rw_ln_g=rw_ln_g[l], rw_ln_b=rw_ln_b[l], w_branch=w_branch[l].astype(BF16),
                  w_out=w_out[l].astype(BF16),
                  norm_ffn_g=norm_ffn_g[l], peer_wq=peer_wq[l], peer_qn_g=peer_qn_g[l],
                  peer_keys=peer_keys[l], peer_u=_pack_table(peer_u[l]), peer_v=_pack_table(peer_v[l]))
        xp, st_p = _trunk_layer(xp, None, None, hg0, cv0, rw0, sh0, lp)
        xs, st_s = _trunk_layer(xs, cache_sb_k[l], cache_sb_v[l], state_hgrn[l], state_conv[l],
                                state_rwkv[l], state_shift[l], lp)
        outs_p.append(st_p)
        outs_s.append(st_s)

    def stk(outs, i):
        return jnp.stack([o[i] for o in outs], axis=0)

    y_prompt = _rmsnorm_pallas(xp, final_norm_g)
    y_sample = _rmsnorm_pallas(xs, final_norm_g)
    return (y_prompt, y_sample,
            stk(outs_p, 0), stk(outs_p, 1), stk(outs_p, 2), stk(outs_p, 3), stk(outs_p, 4), stk(outs_p, 5),
            stk(outs_s, 0), stk(outs_s, 1), stk(outs_s, 2), stk(outs_s, 3), stk(outs_s, 4), stk(outs_s, 5))
```

```python
from functools import partial

import jax
import jax.numpy as jnp
from jax import lax
from jax.experimental import pallas as pl
from jax.experimental.pallas import tpu as pltpu

D_MODEL = 1024
DEPTH = 2
CHUNK = 64
HEAD_DIM = 64
N_BRANCH = 4
BRANCH_W = D_MODEL // 4
SB_HEADS = BRANCH_W // HEAD_DIM
HG_HEADS = 4
HG_DK = BRANCH_W // HG_HEADS
HG_DV = BRANCH_W // HG_HEADS
CONV_CH = BRANCH_W
CONV_WIDTH = 31
RW_HEADS = 4
RW_HD = BRANCH_W // RW_HEADS
RW_DECAY_LORA = 64
RW_AAA_LORA = 64
RW_GATE_LORA = 128
SB_COLS = 3 * BRANCH_W
HG_COLS = 4 * BRANCH_W
CV_COLS = 2 * CONV_CH
RW_COLS = 3 * BRANCH_W + RW_DECAY_LORA + RW_AAA_LORA + RW_GATE_LORA
GATE_COLS = N_BRANCH * D_MODEL
IN_COLS = SB_COLS + HG_COLS + CV_COLS + RW_COLS + GATE_COLS
IN_SPLITS = (SB_COLS, SB_COLS + HG_COLS, SB_COLS + HG_COLS + CV_COLS,
             SB_COLS + HG_COLS + CV_COLS + RW_COLS)
RW_SPLITS = (BRANCH_W, 2 * BRANCH_W, 3 * BRANCH_W, 3 * BRANCH_W + RW_DECAY_LORA,
             3 * BRANCH_W + RW_DECAY_LORA + RW_AAA_LORA)
Q_BLOCK = 128
PK_HEADS = 8
PK_NKEYS = 128
PK_EXPERTS = PK_NKEYS * PK_NKEYS
PK_DQ = 256
PK_TOPK = 16
PEER_BLOCK = 128
EPS = 1e-6
RW_GN_EPS = 64e-5
SB_TQ = 256
SB_TK = 256
RW_CHUNK = 64
F32 = jnp.float32
BF16 = jnp.bfloat16


def _rmsnorm(x, g):
    xf = x.astype(F32)
    y = xf * lax.rsqrt(jnp.mean(xf * xf, axis=-1, keepdims=True) + EPS)
    return (y * g.astype(F32)).astype(x.dtype)


def _rmsnorm_body(x_ref, g_ref, o_ref):
    x = x_ref[...]
    y = x * lax.rsqrt(jnp.mean(x * x, axis=-1, keepdims=True) + EPS)
    o_ref[...] = y * g_ref[...]


def _rmsnorm_pallas(x, g, rows=512):
    shape = x.shape
    x2 = x.reshape(-1, shape[-1])
    n, d = x2.shape
    rows = min(rows, n)
    out = pl.pallas_call(
        _rmsnorm_body,
        grid=(n // rows,),
        in_specs=[pl.BlockSpec((rows, d), lambda i: (i, 0)),
                  pl.BlockSpec((1, d), lambda i: (0, 0))],
        out_specs=pl.BlockSpec((rows, d), lambda i: (i, 0)),
        out_shape=jax.ShapeDtypeStruct((n, d), x.dtype),
        name="final_rmsnorm",
    )(x2, g.reshape(1, d))
    return out.reshape(shape)


VMEM_LIMIT_V7X = 48 * 1024 * 1024
SB_Q_SCALE = -(HEAD_DIM ** -0.5) * 1.4426950408889634


def _sb_attn_body(q_ref, k_ref, v_ref, o_ref, acc_ref, carry_ref, *, tq, tk, off, n_diag):
    qi = pl.program_id(1)
    q0 = off + qi * tq
    n_full = q0 // tk
    acc_ref[...] = jnp.zeros_like(acc_ref)
    carry_ref[...] = jnp.zeros_like(carry_ref)
    q_all = q_ref[0]
    q_heads = [q_all[:, h * HEAD_DIM:(h + 1) * HEAD_DIM] for h in range(SB_HEADS)]
    jj = lax.broadcasted_iota(jnp.int32, (tk, tk), 0)
    ss = lax.broadcasted_iota(jnp.int32, (tk, tk), 1)
    later_mat = (jj > ss).astype(BF16)

    def block(kb, masked):
        start = pl.multiple_of(kb * tk, tk)
        k_blk = k_ref[0, pl.ds(start, tk), :]
        v_blk = v_ref[0, pl.ds(start, tk), :]
        if masked:
            key_pos = start + lax.broadcasted_iota(jnp.int32, (tq, tk), 1)
            q_pos = q0 + lax.broadcasted_iota(jnp.int32, (tq, tk), 0)
            mask = key_pos < q_pos
        dims = (((1,), (1,)), ((), ()))
        nls = [lax.dot_general(q_heads[h], k_blk[:, h * HEAD_DIM:(h + 1) * HEAD_DIM], dims,
                               preferred_element_type=F32) for h in range(SB_HEADS)]
        lsms, laters = [], []
        for h in range(SB_HEADS):
            nl = nls[h]
            neg_abs = pltpu.bitcast(pltpu.bitcast(nl, jnp.int32) | jnp.int32(-2 ** 31), F32)
            lsm = jnp.minimum(nl, 0.0) - jnp.log2(1.0 + jnp.exp2(neg_abs))
            if masked:
                lsm = jnp.where(mask, lsm, 0.0)
            lsms.append(lsm)
            laters.append(jnp.dot(lsm.astype(BF16), later_mat, preferred_element_type=F32))
        outs = []
        for h in range(SB_HEADS):
            carry = carry_ref[h]
            expo = (lsms[h] - nls[h]) + laters[h] + jnp.concatenate([carry] * (tk // 128), axis=1)
            w = jnp.exp2(expo)
            if masked:
                w = jnp.where(mask, w, 0.0)
            outs.append(jnp.dot(w.astype(BF16), v_blk[:, h * HEAD_DIM:(h + 1) * HEAD_DIM], preferred_element_type=F32))
            row = laters[h][:, 0:1] + lsms[h][:, 0:1]
            carry_ref[h] = carry + jnp.broadcast_to(row, carry.shape)
        acc_ref[...] += jnp.concatenate(outs, axis=1)

    for d in range(n_diag - 1, -1, -1):
        block(n_full + d, True)

    def full_step(i, c):
        block(n_full - 1 - i, False)
        return c

    lax.fori_loop(0, n_full, full_step, 0)
    o_ref[0] = acc_ref[...].astype(o_ref.dtype)


def _sb_attention(q, k, v, *, off, tq, tk, out_dtype=BF16):
    B, Lq, W = q.shape
    Lk = k.shape[1]
    assert W == BRANCH_W and Lq % tq == 0 and Lk % tk == 0 and tk % 128 == 0
    nq = Lq // tq
    assert nq == 1 or (tq % tk == 0 and off % tk == 0)
    n_diag = -(-((off % tk) + tq - 1) // tk)
    assert (off + Lq - 1 + tk - 1) // tk <= Lk // tk
    body = partial(_sb_attn_body, tq=tq, tk=tk, off=off, n_diag=n_diag)
    return pl.pallas_call(
        body,
        grid=(B, nq),
        in_specs=[pl.BlockSpec((1, tq, W), lambda b, i: (b, i, 0)),
                  pl.BlockSpec((1, Lk, W), lambda b, i: (b, 0, 0)),
                  pl.BlockSpec((1, Lk, W), lambda b, i: (b, 0, 0))],
        out_specs=pl.BlockSpec((1, tq, W), lambda b, i: (b, i, 0)),
        out_shape=jax.ShapeDtypeStruct((B, Lq, W), out_dtype),
        scratch_shapes=[pltpu.VMEM((tq, W), F32), pltpu.VMEM((SB_HEADS, tq, 128), F32)],
        compiler_params=pltpu.CompilerParams(dimension_semantics=("arbitrary", "arbitrary"),
                                             vmem_limit_bytes=VMEM_LIMIT_V7X),
        name="sb_attention",
    )(q, k, v)


RW_SUB = 16
HI = lax.Precision.HIGHEST


def _split(x):
    hi = x.astype(BF16)
    lo = (x - hi.astype(F32)).astype(BF16)
    return hi, lo


def _mm3(a, b):
    ah, al = _split(a)
    bh, bl = _split(b)
    d = partial(jnp.dot, preferred_element_type=F32)
    return d(ah, bh) + (d(ah, bl) + d(al, bh))


def _dot_nt(a, b):
    return lax.dot_general(a, b, (((1,), (1,)), ((), ())), preferred_element_type=F32)


def _dot_tn(a, b):
    return lax.dot_general(a, b, (((0,), (0,)), ((), ())), preferred_element_type=F32)


def _rwkv_body(z_ref, shift_ref, s0_ref, mu_ref, w0_ref, w2_ref, a0_ref, a2_ref, g2_ref, kk_ref, ka_ref,
               rk_ref, lng_ref, lnb_ref, o_ref, s_out_ref, shift_out_ref, state_ref, prev_ref, *, T):
    ci = pl.program_id(1)
    nc = pl.num_programs(1)
    W = BRANCH_W
    N = RW_HEADS * T

    @pl.when(ci == 0)
    def _():
        state_ref[...] = s0_ref[0]
        prev_ref[...] = shift_ref[0]

    z = z_ref[0]
    row = lax.broadcasted_iota(jnp.int32, (T, RW_COLS), 0)
    z_prev = jnp.where(row == 0, jnp.broadcast_to(prev_ref[...], (T, RW_COLS)), pltpu.roll(z, 1, axis=0))
    prev_ref[...] = z[T - 1:T, :]
    zs = z + (z_prev - z) * mu_ref[...]
    r = zs[:, 0:W]
    k = zs[:, W:2 * W]
    v = zs[:, 2 * W:3 * W]
    xwa = zs[:, 3 * W:3 * W + 128]
    xg = zs[:, 3 * W + 128:]
    dotf = partial(jnp.dot, preferred_element_type=F32, precision=HI)
    w_log = -jax.nn.softplus(-(w0_ref[...] + dotf(jnp.tanh(xwa), w2_ref[...]))) - 0.5
    logdec = -jnp.exp(w_log)
    a = jax.nn.sigmoid(a0_ref[...] + dotf(xwa, a2_ref[...]))
    g = dotf(jax.nn.sigmoid(xg), g2_ref[...])
    li = lax.broadcasted_iota(jnp.int32, (W, W), 0) // RW_HD
    lj = lax.broadcasted_iota(jnp.int32, (W, W), 1) // RW_HD
    head_sum = (li == lj).astype(F32)
    kk = k * kk_ref[...]
    kk = kk * lax.rsqrt(dotf(kk * kk, head_sum) + 1e-12)
    k2 = k * (1.0 + (a - 1.0) * ka_ref[...])
    beta = kk * a
    ti = lax.broadcasted_iota(jnp.int32, (T, T), 0)
    tj = lax.broadcasted_iota(jnp.int32, (T, T), 1)
    b = dotf((ti >= tj).astype(F32), logdec)
    b_last = b[T - 1:T, :]
    k_in = kk * jnp.exp(b - logdec)
    r_in = r * jnp.exp(b)
    inv_p = jnp.exp(-b)
    k_out = k2 * inv_p
    b_out = beta * inv_p
    to_end = jnp.exp(b_last - b)
    k_end = k2 * to_end
    b_end = beta * to_end

    sh = lax.broadcasted_iota(jnp.int32, (N, W), 0) // T
    sl = lax.broadcasted_iota(jnp.int32, (N, W), 1) // RW_HD
    own = sh == sl

    def stack(x, masked):
        xs = jnp.concatenate([x] * RW_HEADS, axis=0)
        return jnp.where(own, xs, 0.0) if masked else xs

    k_in_s = stack(k_in, True).astype(BF16)
    r_in_s = stack(r_in, True).astype(BF16)
    k_out_s = stack(k_out, False).astype(BF16)
    b_out_s = stack(b_out, False).astype(BF16)
    v_s = stack(v, True).astype(BF16)
    k_end_s = stack(k_end, True).astype(BF16)
    b_end_s = stack(b_end, True).astype(BF16)

    ri = lax.broadcasted_iota(jnp.int32, (N, N), 0)
    rj = lax.broadcasted_iota(jnp.int32, (N, N), 1)
    same_head = (ri // T) == (rj // T)
    strict = same_head & (ri > rj)
    incl = same_head & (ri >= rj)
    a_mat = jnp.where(strict, _dot_nt(k_in_s, b_out_s), 0.0)
    kk_mat = jnp.where(strict, _dot_nt(k_in_s, k_out_s), 0.0)
    rk_mat = jnp.where(incl, _dot_nt(r_in_s, k_out_s), 0.0)
    rb_mat = jnp.where(incl, _dot_nt(r_in_s, b_out_s), 0.0)

    eye = (ri == rj).astype(F32)
    a_bd = jnp.where((ri // RW_SUB) == (rj // RW_SUB), a_mat, 0.0)
    x = eye - a_bd
    p = _mm3(a_bd, a_bd)
    x = x + _mm3(x, p)
    p = _mm3(p, p)
    x = x + _mm3(x, p)
    p = _mm3(p, p)
    x = x + _mm3(x, p)
    size = RW_SUB
    while size < T:
        lower = ((ri // (2 * size)) == (rj // (2 * size))) & ((ri // size) > (rj // size))
        x = x - _mm3(_mm3(x, jnp.where(lower, a_mat, 0.0)), x)
        size *= 2

    state = state_ref[...]
    state_b = state.astype(BF16)
    d = partial(jnp.dot, preferred_element_type=F32)
    rhs = _dot_nt(k_in_s, state_b) + d(kk_mat.astype(BF16), v_s)
    u = _mm3(x, rhs)
    u_b = u.astype(BF16)
    o_s = _dot_nt(r_in_s, state_b) + d(rk_mat.astype(BF16), v_s) - d(rb_mat.astype(BF16), u_b)
    o = o_s[0:T]
    for h in range(1, RW_HEADS):
        o = o + o_s[h * T:(h + 1) * T]
    state_ref[...] = state * jnp.exp(b_last) + _dot_tn(v_s, k_end_s) - _dot_tn(u_b, b_end_s)

    head_mean = head_sum * (1.0 / RW_HD)
    mean = dotf(o, head_mean)
    var = dotf(jnp.square(o - mean), head_mean)
    o = (o - mean) * lax.rsqrt(var + RW_GN_EPS) * lng_ref[...] + lnb_ref[...]
    bonus = dotf(r * k2 * rk_ref[...], head_sum) * v
    o_ref[0] = ((o + bonus) * g).astype(o_ref.dtype)

    @pl.when(ci == nc - 1)
    def _():
        s_out_ref[0] = state_ref[...]
        shift_out_ref[0] = z[T - 1:T, :]


def _rwkv7_pallas(z, s0_bd, shift_prev, mu, w0, w2, a0, a2, g2, k_k, k_a, r_k, ln_g, ln_b, *, T, out_dtype=F32):
    B, L, C = z.shape
    assert C == RW_COLS and L % T == 0 and T % RW_SUB == 0
    W = BRANCH_W
    w2p = jnp.concatenate([w2, jnp.zeros_like(w2)], axis=0)
    a2p = jnp.concatenate([jnp.zeros_like(a2), a2], axis=0)
    vec = lambda t: t.reshape(1, -1).astype(F32)
    full = lambda shape: pl.BlockSpec(shape, lambda b, c: (0,) * len(shape))
    return pl.pallas_call(
        partial(_rwkv_body, T=T),
        grid=(B, L // T),
        in_specs=[pl.BlockSpec((1, T, C), lambda b, c: (b, c, 0)),
                  pl.BlockSpec((1, 1, C), lambda b, c: (b, 0, 0)),
                  pl.BlockSpec((1, W, W), lambda b, c: (b, 0, 0)),
                  full((1, C)), full((1, W)), full((128, W)), full((1, W)), full((128, W)), full((128, W)),
                  full((1, W)), full((1, W)), full((1, W)), full((1, W)), full((1, W))],
        out_specs=[pl.BlockSpec((1, T, W), lambda b, c: (b, c, 0)),
                   pl.BlockSpec((1, W, W), lambda b, c: (b, 0, 0)),
                   pl.BlockSpec((1, 1, C), lambda b, c: (b, 0, 0))],
        out_shape=[jax.ShapeDtypeStruct((B, L, W), out_dtype),
                   jax.ShapeDtypeStruct((B, W, W), F32),
                   jax.ShapeDtypeStruct((B, 1, C), F32)],
        scratch_shapes=[pltpu.VMEM((W, W), F32), pltpu.VMEM((1, C), F32)],
        compiler_params=pltpu.CompilerParams(dimension_semantics=("arbitrary", "arbitrary")),
        name="rwkv7_chunked",
    )(z, shift_prev.reshape(B, 1, C), s0_bd, vec(mu), vec(w0), w2p, vec(a0), a2p, g2, vec(k_k), vec(k_a),
      vec(r_k), vec(ln_g), vec(ln_b))


def _state_to_bd(s):
    B = s.shape[0]
    eye = jnp.eye(RW_HEADS, dtype=s.dtype)
    return (s[:, :, :, None, :] * eye[None, :, None, :, None]).reshape(B, BRANCH_W, BRANCH_W)


def _state_from_bd(s_bd):
    B = s_bd.shape[0]
    s5 = s_bd.reshape(B, RW_HEADS, RW_HD, RW_HEADS, RW_HD)
    return jnp.stack([s5[:, h, :, h, :] for h in range(RW_HEADS)], axis=1)


def _hgrn_body(z_ref, s0_ref, loglb_ref, log1mlb_ref, ng_ref, o_ref, s_out_ref, state_ref, *, T):
    ci = pl.program_id(1)
    nc = pl.num_programs(1)
    W = BRANCH_W

    @pl.when(ci == 0)
    def _():
        state_ref[...] = s0_ref[0]

    z = z_ref[0]
    zq, zf, zi, zg = z[:, 0:W], z[:, W:2 * W], z[:, 2 * W:3 * W], z[:, 3 * W:]
    log_sig = jnp.minimum(zf, 0.0) - jnp.log(1.0 + jnp.exp(-jnp.abs(zf)))
    a = loglb_ref[...]
    bv = log1mlb_ref[...] + log_sig
    m = jnp.maximum(a, bv)
    log_f = m + jnp.log(jnp.exp(a - m) + jnp.exp(bv - m))
    k = 1.0 - jnp.exp(log_f)
    q = zq * jax.nn.sigmoid(zq)
    dotf = partial(jnp.dot, preferred_element_type=F32, precision=HI)
    ti = lax.broadcasted_iota(jnp.int32, (T, T), 0)
    tj = lax.broadcasted_iota(jnp.int32, (T, T), 1)
    b = dotf((ti >= tj).astype(F32), log_f)
    li = lax.broadcasted_iota(jnp.int32, (W, W), 0) // HG_DK
    lj = lax.broadcasted_iota(jnp.int32, (W, W), 1) // HG_DK
    same_head = li == lj
    head_sum = same_head.astype(BF16)

    t_idx = lax.broadcasted_iota(jnp.int32, (T, W), 0)
    prods = []
    for s in range(T):
        e = jnp.exp(jnp.where(t_idx >= s, b - b[s:s + 1, :], -jnp.inf))
        prods.append((q * e * k[s:s + 1, :]).astype(BF16))
    scores = jnp.dot(jnp.concatenate(prods, axis=0), head_sum, preferred_element_type=F32)
    o = jnp.zeros((T, W), F32)
    for s in range(T):
        o = o + scores[s * T:(s + 1) * T, :] * zi[s:s + 1, :]

    state = state_ref[...]
    o = o + _dot_nt((q * jnp.exp(b)).astype(BF16), state.astype(BF16))
    b_last = b[T - 1:T, :]
    k_end = (k * jnp.exp(b_last - b)).astype(BF16)
    upd = _dot_tn(zi.astype(BF16), k_end)
    state_ref[...] = state * jnp.exp(b_last) + jnp.where(same_head, upd, 0.0)

    ms = dotf(o * o, same_head.astype(F32) * (1.0 / HG_DK))
    o = o * lax.rsqrt(ms + EPS) * ng_ref[...]
    o_ref[0] = (o * (zg * jax.nn.sigmoid(zg))).astype(o_ref.dtype)

    @pl.when(ci == nc - 1)
    def _():
        s_out_ref[0] = state_ref[...]


def _hgrn2_pallas(z, s0_bd, lb, norm_g, *, T, out_dtype=F32):
    B, L, C = z.shape
    assert C == HG_COLS and L % T == 0
    W = BRANCH_W
    full = lambda shape: pl.BlockSpec(shape, lambda b, c: (0,) * len(shape))
    return pl.pallas_call(
        partial(_hgrn_body, T=T),
        grid=(B, L // T),
        in_specs=[pl.BlockSpec((1, T, C), lambda b, c: (b, c, 0)),
                  pl.BlockSpec((1, W, W), lambda b, c: (b, 0, 0)),
                  full((1, W)), full((1, W)), full((1, W))],
        out_specs=[pl.BlockSpec((1, T, W), lambda b, c: (b, c, 0)),
                   pl.BlockSpec((1, W, W), lambda b, c: (b, 0, 0))],
        out_shape=[jax.ShapeDtypeStruct((B, L, W), out_dtype),
                   jax.ShapeDtypeStruct((B, W, W), F32)],
        scratch_shapes=[pltpu.VMEM((W, W), F32)],
        compiler_params=pltpu.CompilerParams(dimension_semantics=("arbitrary", "arbitrary")),
        name="hgrn2_chunked",
    )(z, s0_bd, jnp.log(lb).reshape(1, W), jnp.log1p(-lb).reshape(1, W), jnp.tile(norm_g, HG_HEADS).reshape(1, W))


PROJ_TM = 256
PROJ_VMEM_LIMIT_V7X = 48 * 1024 * 1024
CONV_CARRY = 32
CONV_TC = 512


def _in_proj_body(x_ref, g_ref, w_ref, q_ref, k_ref, v_ref, kb_ref, vb_ref, hg_ref, cv_ref, rw_ref, gate_ref):
    x = x_ref[...]
    h = (x * lax.rsqrt(jnp.mean(x * x, axis=-1, keepdims=True) + EPS) * g_ref[...]).astype(BF16)
    W = BRANCH_W

    def proj(a, b):
        return jnp.dot(h, w_ref[:, a:b], preferred_element_type=F32)

    q_ref[...] = (proj(0, W) * SB_Q_SCALE).astype(BF16)
    k = proj(W, 2 * W)
    v = proj(2 * W, 3 * W)
    k_ref[...] = k
    v_ref[...] = v
    kb_ref[...] = k.astype(BF16)
    vb_ref[...] = v.astype(BF16)
    c = SB_COLS
    hg_ref[...] = proj(c, c + HG_COLS)
    c += HG_COLS
    cv_ref[...] = proj(c, c + CV_COLS)
    c += CV_COLS
    rw_ref[...] = proj(c, c + RW_COLS)
    c += RW_COLS
    for n in range(N_BRANCH):
        gate_ref[:, n * D_MODEL:(n + 1) * D_MODEL] = jax.nn.sigmoid(
            proj(c + n * D_MODEL, c + (n + 1) * D_MODEL)).astype(BF16)


def _in_proj(x2, norm_g, w_in_b):
    n = x2.shape[0]
    TM = min(PROJ_TM, n)
    assert n % TM == 0
    row = lambda w: pl.BlockSpec((TM, w), lambda i: (i, 0))
    shp = lambda w, dt: jax.ShapeDtypeStruct((n, w), dt)
    W = BRANCH_W
    return pl.pallas_call(
        _in_proj_body,
        grid=(n // TM,),
        in_specs=[row(D_MODEL), pl.BlockSpec((1, D_MODEL), lambda i: (0, 0)),
                  pl.BlockSpec((D_MODEL, IN_COLS), lambda i: (0, 0), pipeline_mode=pl.Buffered(1))],
        out_specs=[row(W), row(W), row(W), row(W), row(W), row(HG_COLS), row(CV_COLS), row(RW_COLS), row(GATE_COLS)],
        out_shape=[shp(W, BF16), shp(W, F32), shp(W, F32), shp(W, BF16), shp(W, BF16),
                   shp(HG_COLS, F32), shp(CV_COLS, F32), shp(RW_COLS, F32), shp(GATE_COLS, BF16)],
        compiler_params=pltpu.CompilerParams(dimension_semantics=("arbitrary",), vmem_limit_bytes=PROJ_VMEM_LIMIT_V7X),
        name="in_proj",
    )(x2, norm_g.reshape(1, -1), w_in_b)


def _merge_body(x_ref, sb_ref, hg_ref, cv_ref, rw_ref, gate_ref, wb_ref, wo_ref, o_ref):
    merged = None
    for n, br in enumerate((sb_ref, hg_ref, cv_ref, rw_ref)):
        t = jnp.dot(br[...].astype(BF16), wb_ref[n], preferred_element_type=F32)
        t = t * gate_ref[:, n * D_MODEL:(n + 1) * D_MODEL].astype(F32)
        merged = t if merged is None else merged + t
    o_ref[...] = x_ref[...] + jnp.dot(merged.astype(BF16), wo_ref[...], preferred_element_type=F32)


def _merge(x2, o_sb, o_hg, o_cv, o_rw, gates, wb_b, wo_b):
    n = x2.shape[0]
    TM = min(PROJ_TM, n)
    row = lambda w: pl.BlockSpec((TM, w), lambda i: (i, 0))
    return pl.pallas_call(
        _merge_body,
        grid=(n // TM,),
        in_specs=[row(D_MODEL), row(BRANCH_W), row(BRANCH_W), row(BRANCH_W), row(BRANCH_W), row(GATE_COLS),
                  pl.BlockSpec((N_BRANCH, BRANCH_W, D_MODEL), lambda i: (0, 0, 0)),
                  pl.BlockSpec((D_MODEL, D_MODEL), lambda i: (0, 0))],
        out_specs=row(D_MODEL),
        out_shape=jax.ShapeDtypeStruct((n, D_MODEL), F32),
        compiler_params=pltpu.CompilerParams(dimension_semantics=("arbitrary",), vmem_limit_bytes=PROJ_VMEM_LIMIT_V7X),
        name="branch_merge",
    )(x2, o_sb, o_hg, o_cv, o_rw, gates, wb_b, wo_b)


def _conv_body(z_ref, buf_ref, w_ref, b_ref, lg_ref, lb_ref, o_ref, buf_out_ref, hp_ref, *, TC):
    ci = pl.program_id(1)
    nc = pl.num_programs(1)

    @pl.when(ci == 0)
    def _():
        hp_ref[0:CONV_CARRY, :] = buf_ref[0]

    z = z_ref[0]
    hp_ref[CONV_CARRY:, :] = z[:, 0:CONV_CH] * jax.nn.sigmoid(z[:, CONV_CH:])
    first = CONV_CARRY - (CONV_WIDTH - 1)
    y = jnp.zeros((TC, CONV_CH), F32) + b_ref[...]
    for j in range(CONV_WIDTH):
        y = y + hp_ref[first + j:first + j + TC, :] * w_ref[j:j + 1, :]
    mu = jnp.mean(y, axis=-1, keepdims=True)
    d = y - mu
    var = jnp.mean(d * d, axis=-1, keepdims=True)
    y = d * lax.rsqrt(var + EPS) * lg_ref[...] + lb_ref[...]
    o_ref[0] = (y * jax.nn.sigmoid(y)).astype(o_ref.dtype)
    tail = hp_ref[TC:TC + CONV_CARRY, :]

    @pl.when(ci == nc - 1)
    def _():
        buf_out_ref[0] = tail

    hp_ref[0:CONV_CARRY, :] = tail


def _conformer_conv_pallas(z_cv, buf, w, b, ln_g, ln_b, *, TC, out_dtype=F32):
    B, L, _ = z_cv.shape
    assert L % TC == 0 and TC % 8 == 0 and TC >= CONV_CARRY
    C = CONV_CH
    pad = CONV_CARRY - (CONV_WIDTH - 1)
    buf_p = jnp.pad(buf.astype(F32), ((0, 0), (pad, 0), (0, 0)))
    w_p = jnp.pad(w.astype(F32), ((0, 32 - CONV_WIDTH), (0, 0)))
    full = lambda shape: pl.BlockSpec(shape, lambda b, c: (0,) * len(shape))
    y, buf_new = pl.pallas_call(
        partial(_conv_body, TC=TC),
        grid=(B, L // TC),
        in_specs=[pl.BlockSpec((1, TC, 2 * C), lambda b, c: (b, c, 0)),
                  pl.BlockSpec((1, CONV_CARRY, C), lambda b, c: (b, 0, 0)),
                  full((32, C)), full((1, C)), full((1, C)), full((1, C))],
        out_specs=[pl.BlockSpec((1, TC, C), lambda b, c: (b, c, 0)),
                   pl.BlockSpec((1, CONV_CARRY, C), lambda b, c: (b, 0, 0))],
        out_shape=[jax.ShapeDtypeStruct((B, L, C), out_dtype), jax.ShapeDtypeStruct((B, CONV_CARRY, C), F32)],
        scratch_shapes=[pltpu.VMEM((CONV_CARRY + TC, C), F32)],
        compiler_params=pltpu.CompilerParams(dimension_semantics=("arbitrary", "arbitrary")),
        name="conformer_conv",
    )(z_cv, buf_p, w_p, b.reshape(1, C), ln_g.reshape(1, C), ln_b.reshape(1, C))
    return y, buf_new[:, pad:, :]


PK_PAIRS = PK_HEADS * PK_TOPK
PEER_TB = 128
SUBLANES = 8
LANES = 128
ROW_TILES = D_MODEL // LANES
PEER_VMEM_LIMIT_V7X = 52 * 1024 * 1024


def _topk_rows(s, payload, k):
    n = s.shape[0]
    rows = lax.broadcasted_iota(jnp.int32, s.shape, 0)
    vals, ids = [], []
    for _ in range(k):
        m = jnp.max(s, axis=0, keepdims=True)
        pos = jnp.min(jnp.where(s == m, rows, n), axis=0, keepdims=True)
        sel = rows == pos
        ids.append(pos if payload is None else jnp.max(jnp.where(sel, payload, -1), axis=0, keepdims=True))
        vals.append(m)
        s = jnp.where(sel, -jnp.inf, s)
    return jnp.concatenate(vals, axis=0), jnp.concatenate(ids, axis=0)


def _dot3(ah, al, bh, bl, dims):
    d = partial(lax.dot_general, dimension_numbers=dims, preferred_element_type=F32)
    return d(ah, bh) + (d(ah, bl) + d(al, bh))


def _peer_route_body(x_ref, g_ref, wqh_ref, wql_ref, qg_ref, k1_ref, k2_ref, xn_ref, idx_ref, idxc_ref, gate_ref):
    x = x_ref[...]
    xn = x * lax.rsqrt(jnp.mean(x * x, axis=-1, keepdims=True) + EPS) * g_ref[...]
    xn_ref[...] = xn
    nn = (((1,), (0,)), ((), ()))
    nt = (((1,), (1,)), ((), ()))
    xh, xl = _split(xn)
    q = _dot3(xh, xl, wqh_ref[...], wql_ref[...], nn)
    half = PK_DQ // 2
    experts, gates = [], []
    for h in range(PK_HEADS):
        qh = q[:, h * PK_DQ:(h + 1) * PK_DQ]
        qh = qh * lax.rsqrt(jnp.mean(qh * qh, axis=-1, keepdims=True) + EPS) * qg_ref[...]
        qhh, qhl = _split(qh)
        k1h, k1l = _split(k1_ref[h])
        k2h, k2l = _split(k2_ref[h])
        s1 = _dot3(k1h, k1l, qhh[:, :half], qhl[:, :half], nt)
        s2 = _dot3(k2h, k2l, qhh[:, half:], qhl[:, half:], nt)
        v1, i1 = _topk_rows(s1, None, PK_TOPK)
        v2, i2 = _topk_rows(s2, None, PK_TOPK)
        cand_s = jnp.concatenate([v1[a:a + 1, :] + v2 for a in range(PK_TOPK)], axis=0)
        cand_i = jnp.concatenate([i1[a:a + 1, :] * PK_NKEYS + i2 for a in range(PK_TOPK)], axis=0)
        top_s, expert = _topk_rows(cand_s, cand_i, PK_TOPK)
        e = jnp.exp(top_s - jnp.max(top_s, axis=0, keepdims=True))
        gates.append(e / jnp.sum(e, axis=0, keepdims=True))
        experts.append(expert)
    expert = jnp.concatenate(experts, axis=0)
    row = (expert + 1) * HALF_TILES
    pair = lax.broadcasted_iota(jnp.int32, row.shape, 0)
    idx_ref[...] = row.T
    idxc_ref[...] = (row - (pair & HALF_TILES)).T
    gate_ref[...] = jnp.concatenate(gates, axis=0)


def _peer_route(x2, norm_g, wq, qn_g, k1, k2):
    n_tok = x2.shape[0]
    TB = PEER_TB
    assert n_tok % TB == 0
    nb = n_tok // TB
    full = lambda shape: pl.BlockSpec(shape, lambda i: (0,) * len(shape))
    wq_hi = wq.astype(BF16)
    wq_lo = (wq - wq_hi.astype(F32)).astype(BF16)
    return pl.pallas_call(
        _peer_route_body,
        grid=(nb,),
        in_specs=[pl.BlockSpec((TB, D_MODEL), lambda i: (i, 0)), full((1, D_MODEL)),
                  full((D_MODEL, PK_HEADS * PK_DQ)), full((D_MODEL, PK_HEADS * PK_DQ)), full((1, PK_DQ)),
                  full((PK_HEADS, PK_NKEYS, PK_DQ // 2)), full((PK_HEADS, PK_NKEYS, PK_DQ // 2))],
        out_specs=[pl.BlockSpec((TB, D_MODEL), lambda i: (i, 0)),
                   pl.BlockSpec((TB, PK_PAIRS), lambda i: (i, 0)),
                   pl.BlockSpec((TB, PK_PAIRS), lambda i: (i, 0)),
                   pl.BlockSpec((PK_PAIRS, TB), lambda i: (0, i))],
        out_shape=[jax.ShapeDtypeStruct((n_tok, D_MODEL), F32),
                   jax.ShapeDtypeStruct((n_tok, PK_PAIRS), jnp.int32),
                   jax.ShapeDtypeStruct((n_tok, PK_PAIRS), jnp.int32),
                   jax.ShapeDtypeStruct((PK_PAIRS, n_tok), F32)],
        compiler_params=pltpu.CompilerParams(dimension_semantics=("arbitrary",)),
        name="peer_route",
    )(x2, norm_g.reshape(1, -1), wq_hi, wq_lo, qn_g.reshape(1, -1), k1, k2)


HALF_TILES = ROW_TILES // 2


def _unpack_words(word):
    return pltpu.bitcast(word << 16, F32), pltpu.bitcast(word & jnp.int32(-65536), F32)


def _fold_couples(tiles, sub):
    m2 = (sub & 3) < 2
    lvl2 = []
    for a, b in ((tiles[0], tiles[2]), (tiles[1], tiles[3])):
        lvl2.append(jnp.where(m2, a, b) + jnp.where(m2, pltpu.roll(a, 6, axis=0), pltpu.roll(b, 2, axis=0)))
    m1 = (sub & 1) == 0
    a, b = lvl2
    return jnp.where(m1, a, b) + jnp.where(m1, pltpu.roll(a, 7, axis=0), pltpu.roll(b, 1, axis=0))


def _peer_act_body(idx_ref, x_ref, gate_ref, tab_ref, w_ref, part_ref):
    TB = x_ref.shape[0]
    sub = lax.broadcasted_iota(jnp.int32, (SUBLANES, LANES), 0)
    low = sub < HALF_TILES

    def token(t, c):
        x = x_ref[t]
        x_lo = jnp.concatenate([x[0:HALF_TILES]] * 2, axis=0)
        x_hi = jnp.concatenate([x[HALF_TILES:]] * 2, axis=0)
        start = t * PK_PAIRS
        for g in range(PK_PAIRS // SUBLANES):
            tiles = []
            for j in range(HALF_TILES):
                ra = pl.multiple_of(idx_ref[0, 0, start + g * SUBLANES + j], HALF_TILES)
                rb = pl.multiple_of(idx_ref[0, 0, start + g * SUBLANES + j + HALF_TILES], HALF_TILES)
                word = jnp.where(low, tab_ref[pl.ds(ra, SUBLANES), :], tab_ref[pl.ds(rb, SUBLANES), :])
                lo, hi = _unpack_words(word)
                tiles.append(lo * x_lo + hi * x_hi)
            part_ref[t, g * SUBLANES:(g + 1) * SUBLANES, :] = _fold_couples(tiles, sub)
        return c

    lax.fori_loop(0, TB, token, 0)
    lane = lax.broadcasted_iota(jnp.int32, (PK_PAIRS, TB), 1)
    s = jnp.zeros((PK_PAIRS, TB), F32)
    for t in range(TB):
        s = jnp.where(lane == t, jnp.sum(part_ref[t], axis=1, keepdims=True), s)
    w_ref[...] = gate_ref[...] * jax.nn.gelu(s)


PEER_MIX_ACCS = 4


def _peer_mix_body(idx_ref, w_ref, tab_ref, y_ref, wrep_ref):
    TB = y_ref.shape[0]
    w_all = w_ref[...]
    for t in range(TB):
        wrep_ref[t] = jnp.broadcast_to(w_all[:, t:t + 1], (PK_PAIRS, LANES))

    def token(t, c):
        start = t * PK_PAIRS
        acc_lo = [jnp.zeros((HALF_TILES, LANES), F32) for _ in range(PEER_MIX_ACCS)]
        acc_hi = [jnp.zeros((HALF_TILES, LANES), F32) for _ in range(PEER_MIX_ACCS)]
        for p in range(PK_PAIRS):
            row = pl.multiple_of(idx_ref[0, 0, start + p], HALF_TILES)
            lo, hi = _unpack_words(tab_ref[pl.ds(row, HALF_TILES), :])
            w = jnp.broadcast_to(wrep_ref[t, p:p + 1, :], (HALF_TILES, LANES))
            acc_lo[p % PEER_MIX_ACCS] = acc_lo[p % PEER_MIX_ACCS] + w * lo
            acc_hi[p % PEER_MIX_ACCS] = acc_hi[p % PEER_MIX_ACCS] + w * hi
        y_ref[t] = jnp.concatenate([(acc_lo[0] + acc_lo[1]) + (acc_lo[2] + acc_lo[3]),
                                    (acc_hi[0] + acc_hi[1]) + (acc_hi[2] + acc_hi[3])], axis=0)
        return c

    lax.fori_loop(0, TB, token, 0)


def _pack_table(tab):
    bits = lax.bitcast_convert_type(tab.astype(BF16), jnp.uint16).astype(jnp.uint32)
    bits = bits.reshape(PK_EXPERTS, 2, HALF_TILES, LANES)
    word = bits[:, 0] | (bits[:, 1] << 16)
    rows = lax.bitcast_convert_type(word, jnp.int32).reshape(PK_EXPERTS * HALF_TILES, LANES)
    return jnp.pad(rows, ((HALF_TILES, HALF_TILES), (0, 0)))


def _smem_spec(TB):
    return pl.BlockSpec((1, 1, PK_PAIRS * TB), lambda i: (i, 0, 0), memory_space=pltpu.SMEM)


def _table_spec():
    return pl.BlockSpec(((PK_EXPERTS + 2) * HALF_TILES, LANES), lambda i: (0, 0), pipeline_mode=pl.Buffered(1))


def _peer_act(idx, xn3, gate_t, tab_packed):
    n_tok = xn3.shape[0]
    TB = PEER_TB
    nb = n_tok // TB
    return pl.pallas_call(
        _peer_act_body,
        grid=(nb,),
        in_specs=[_smem_spec(TB),
                  pl.BlockSpec((TB, ROW_TILES, LANES), lambda i: (i, 0, 0)),
                  pl.BlockSpec((PK_PAIRS, TB), lambda i: (0, i)),
                  _table_spec()],
        out_specs=pl.BlockSpec((PK_PAIRS, TB), lambda i: (0, i)),
        out_shape=jax.ShapeDtypeStruct((PK_PAIRS, n_tok), F32),
        scratch_shapes=[pltpu.VMEM((TB, PK_PAIRS, LANES), F32)],
        compiler_params=pltpu.CompilerParams(dimension_semantics=("arbitrary",),
                                             vmem_limit_bytes=PEER_VMEM_LIMIT_V7X),
        name="peer_act",
    )(idx.reshape(nb, 1, TB * PK_PAIRS), xn3, gate_t, tab_packed)


def _peer_mix(idx, w_t, tab_packed):
    n_tok = w_t.shape[1]
    TB = PEER_TB
    nb = n_tok // TB
    return pl.pallas_call(
        _peer_mix_body,
        grid=(nb,),
        in_specs=[_smem_spec(TB),
                  pl.BlockSpec((PK_PAIRS, TB), lambda i: (0, i)),
                  _table_spec()],
        out_specs=pl.BlockSpec((TB, ROW_TILES, LANES), lambda i: (i, 0, 0)),
        out_shape=jax.ShapeDtypeStruct((n_tok, ROW_TILES, LANES), F32),
        scratch_shapes=[pltpu.VMEM((TB, PK_PAIRS, LANES), F32)],
        compiler_params=pltpu.CompilerParams(dimension_semantics=("arbitrary",),
                                             vmem_limit_bytes=PEER_VMEM_LIMIT_V7X),
        name="peer_mix",
    )(idx.reshape(nb, 1, TB * PK_PAIRS), w_t, tab_packed)


def _peer_ffn_pallas(x, norm_g, wq, qn_g, sub_keys, u_packed, v_packed):
    B, L, D = x.shape
    n_tok = B * L
    pad = (-n_tok) % PEER_TB
    x2 = jnp.pad(x.reshape(n_tok, D), ((0, pad), (0, 0)))
    xn, idx, idx_couple, gate_t = _peer_route(x2, norm_g, wq, qn_g, sub_keys[:, 0], sub_keys[:, 1])
    w_t = _peer_act(idx_couple, xn.reshape(-1, ROW_TILES, LANES), gate_t, u_packed)
    y = _peer_mix(idx, w_t, v_packed)
    return y.reshape(-1, D)[:n_tok].reshape(B, L, D)


def _trunk_layer(x, k_past, v_past, s_hg, buf_conv, s_rw, buf_shift, p):
    B_, L, _ = x.shape
    n = B_ * L
    x2 = x.reshape(n, D_MODEL)
    qb, k, v, kb, vb, z_hg, z_cv, z_rw, gates = _in_proj(x2, p['norm_mix_g'], p['w_in'])
    seq = lambda t: t.reshape(B_, L, t.shape[-1])
    if k_past is None:
        o_sb = _sb_attention(seq(qb), seq(kb), seq(vb), off=0, tq=SB_TQ, tk=SB_TK, out_dtype=F32)
    else:
        past = k_past.shape[1]
        pad = jnp.zeros((B_, (-(past + L)) % SB_TK, BRANCH_W), BF16)
        kc = jnp.concatenate([k_past.reshape(B_, past, BRANCH_W).astype(BF16), seq(kb), pad], axis=1)
        vc = jnp.concatenate([v_past.reshape(B_, past, BRANCH_W).astype(BF16), seq(vb), pad], axis=1)
        o_sb = _sb_attention(seq(qb), kc, vc, off=past, tq=L, tk=SB_TK, out_dtype=F32)
    o_hg, s_hg_bd = _hgrn2_pallas(seq(z_hg), _state_to_bd(jnp.swapaxes(s_hg.astype(F32), 2, 3)), p['lb'],
                                  p['hg_norm_g'], T=CHUNK if L % CHUNK == 0 else L)
    s_hg_new = jnp.swapaxes(_state_from_bd(s_hg_bd), 2, 3)
    o_cv, buf_conv_new = _conformer_conv_pallas(seq(z_cv), buf_conv, p['conv_w'], p['conv_b'], p['conv_ln_g'],
                                                p['conv_ln_b'], TC=CONV_TC if L % CONV_TC == 0 else L)
    o_rw, s_rw_bd, shift_new = _rwkv7_pallas(seq(z_rw), _state_to_bd(s_rw.astype(F32)), buf_shift, p['rw_mu'],
                                             p['rw_w0'], p['rw_w2'], p['rw_a0'], p['rw_a2'], p['rw_g2'], p['rw_k_k'],
                                             p['rw_k_a'], p['rw_r_k'], p['rw_ln_g'], p['rw_ln_b'],
                                             T=RW_CHUNK if L % RW_CHUNK == 0 else L)
    s_rw_new = _state_from_bd(s_rw_bd)
    buf_shift_new = shift_new[:, 0]
    flat = lambda t: t.reshape(n, BRANCH_W)
    x = _merge(x2, flat(o_sb), flat(o_hg), flat(o_cv), flat(o_rw), gates, p['w_branch'], p['w_out']).reshape(x.shape)
    x = x + _peer_ffn_pallas(x, p['norm_ffn_g'], p['peer_wq'], p['peer_qn_g'], p['peer_keys'],
                             p['peer_u'], p['peer_v'])
    heads = lambda t: t.reshape(B_, L, SB_HEADS, HEAD_DIM)
    return x, (heads(k), heads(v), s_hg_new, buf_conv_new, s_rw_new, buf_shift_new)


def kernel(x_prompt, x_sample, cache_sb_k, cache_sb_v, state_hgrn, state_conv, state_rwkv, state_shift,
           norm_mix_g, w_in, hg_lb_logits, hg_norm_g, conv_w, conv_b, conv_ln_g, conv_ln_b,
           rw_mu, rw_w0, rw_w2, rw_a0, rw_a2, rw_g2, rw_k_k, rw_k_a, rw_r_k, rw_ln_g, rw_ln_b,
           w_branch, w_out, norm_ffn_g, peer_wq, peer_qn_g, peer_keys, peer_u, peer_v, final_norm_g):
    lb_all = jnp.cumsum(jax.nn.softmax(hg_lb_logits.astype(F32), axis=0), axis=0)
    lb_all = lb_all - lb_all[0:1]
    Bp = x_prompt.shape[0]
    hg0 = jnp.zeros((Bp, HG_HEADS, HG_DK, HG_DV), F32)
    cv0 = jnp.zeros((Bp, CONV_WIDTH - 1, CONV_CH), x_prompt.dtype)
    rw0 = jnp.zeros((Bp, RW_HEADS, RW_HD, RW_HD), F32)
    sh0 = jnp.zeros((Bp, RW_COLS), x_prompt.dtype)
    xp, xs = x_prompt, x_sample
    outs_p, outs_s = [], []
    for l in range(DEPTH):
        lp = dict(norm_mix_g=norm_mix_g[l], w_in=w_in[l].astype(BF16), lb=lb_all[l], hg_norm_g=hg_norm_g[l],
                  conv_w=conv_w[l], conv_b=conv_b[l], conv_ln_g=conv_ln_g[l], conv_ln_b=conv_ln_b[l],
                  rw_mu=rw_mu[l], rw_w0=rw_w0[l], rw_w2=rw_w2[l], rw_a0=rw_a0[l], rw_a2=rw_a2[l],
                  rw_g2=rw_g2[l], rw_k_k=rw_k_k[l], rw_k_a=rw_k_a[l], rw_r_k=rw_r_k[l],
                  rw_ln_g=rw_ln_g[l], rw_ln_b=rw_ln_b[l], w_branch=w_branch[l].astype(BF16),
                  w_out=w_out[l].astype(BF16),
                  norm_ffn_g=norm_ffn_g[l], peer_wq=peer_wq[l], peer_qn_g=peer_qn_g[l],
                  peer_keys=peer_keys[l], peer_u=_pack_table(peer_u[l]), peer_v=_pack_table(peer_v[l]))
        xp, st_p = _trunk_layer(xp, None, None, hg0, cv0, rw0, sh0, lp)
        xs, st_s = _trunk_layer(xs, cache_sb_k[l], cache_sb_v[l], state_hgrn[l], state_conv[l],
                                state_rwkv[l], state_shift[l], lp)
        outs_p.append(st_p)
        outs_s.append(st_s)

    def stk(outs, i):
        return jnp.stack([o[i] for o in outs], axis=0)

    y_prompt = _rmsnorm_pallas(xp, final_norm_g)
    y_sample = _rmsnorm_pallas(xs, final_norm_g)
    return (y_prompt, y_sample,
            stk(outs_p, 0), stk(outs_p, 1), stk(outs_p, 2), stk(outs_p, 3), stk(outs_p, 4), stk(outs_p, 5),
            stk(outs_s, 0), stk(outs_s, 1), stk(outs_s, 2), stk(outs_s, 3), stk(outs_s, 4), stk(outs_s, 5))
```

```python
from functools import partial

import jax
import jax.numpy as jnp
from jax import lax
from jax.experimental import pallas as pl
from jax.experimental.pallas import tpu as pltpu

D_MODEL = 1024
DEPTH = 2
CHUNK = 64
HEAD_DIM = 64
N_BRANCH = 4
BRANCH_W = D_MODEL // 4
SB_HEADS = BRANCH_W // HEAD_DIM
HG_HEADS = 4
HG_DK = BRANCH_W // HG_HEADS
HG_DV = BRANCH_W // HG_HEADS
CONV_CH = BRANCH_W
CONV_WIDTH = 31
RW_HEADS = 4
RW_HD = BRANCH_W // RW_HEADS
RW_DECAY_LORA = 64
RW_AAA_LORA = 64
RW_GATE_LORA = 128
SB_COLS = 3 * BRANCH_W
HG_COLS = 4 * BRANCH_W
CV_COLS = 2 * CONV_CH
RW_COLS = 3 * BRANCH_W + RW_DECAY_LORA + RW_AAA_LORA + RW_GATE_LORA
GATE_COLS = N_BRANCH * D_MODEL
IN_COLS = SB_COLS + HG_COLS + CV_COLS + RW_COLS + GATE_COLS
IN_SPLITS = (SB_COLS, SB_COLS + HG_COLS, SB_COLS + HG_COLS + CV_COLS,
             SB_COLS + HG_COLS + CV_COLS + RW_COLS)
RW_SPLITS = (BRANCH_W, 2 * BRANCH_W, 3 * BRANCH_W, 3 * BRANCH_W + RW_DECAY_LORA,
             3 * BRANCH_W + RW_DECAY_LORA + RW_AAA_LORA)
Q_BLOCK = 128
PK_HEADS = 8
PK_NKEYS = 128
PK_EXPERTS = PK_NKEYS * PK_NKEYS
PK_DQ = 256
PK_TOPK = 16
PEER_BLOCK = 128
EPS = 1e-6
RW_GN_EPS = 64e-5
SB_TQ = 256
SB_TK = 256
RW_CHUNK = 64
F32 = jnp.float32
BF16 = jnp.bfloat16


def _rmsnorm(x, g):
    xf = x.astype(F32)
    y = xf * lax.rsqrt(jnp.mean(xf * xf, axis=-1, keepdims=True) + EPS)
    return (y * g.astype(F32)).astype(x.dtype)


def _rmsnorm_body(x_ref, g_ref, o_ref):
    x = x_ref[...]
    y = x * lax.rsqrt(jnp.mean(x * x, axis=-1, keepdims=True) + EPS)
    o_ref[...] = y * g_ref[...]


def _rmsnorm_pallas(x, g, rows=512):
    shape = x.shape
    x2 = x.reshape(-1, shape[-1])
    n, d = x2.shape
    rows = min(rows, n)
    out = pl.pallas_call(
        _rmsnorm_body,
        grid=(n // rows,),
        in_specs=[pl.BlockSpec((rows, d), lambda i: (i, 0)),
                  pl.BlockSpec((1, d), lambda i: (0, 0))],
        out_specs=pl.BlockSpec((rows, d), lambda i: (i, 0)),
        out_shape=jax.ShapeDtypeStruct((n, d), x.dtype),
        name="final_rmsnorm",
    )(x2, g.reshape(1, d))
    return out.reshape(shape)


VMEM_LIMIT_V7X = 48 * 1024 * 1024
SB_Q_SCALE = -(HEAD_DIM ** -0.5) * 1.4426950408889634


def _sb_attn_body(q_ref, k_ref, v_ref, o_ref, acc_ref, carry_ref, *, tq, tk, off, n_diag):
    qi = pl.program_id(1)
    q0 = off + qi * tq
    n_full = q0 // tk
    acc_ref[...] = jnp.zeros_like(acc_ref)
    carry_ref[...] = jnp.zeros_like(carry_ref)
    q_all = q_ref[0]
    q_heads = [q_all[:, h * HEAD_DIM:(h + 1) * HEAD_DIM] for h in range(SB_HEADS)]
    jj = lax.broadcasted_iota(jnp.int32, (tk, tk), 0)
    ss = lax.broadcasted_iota(jnp.int32, (tk, tk), 1)
    later_mat = (jj > ss).astype(BF16)

    def block(kb, masked):
        start = pl.multiple_of(kb * tk, tk)
        k_blk = k_ref[0, pl.ds(start, tk), :]
        v_blk = v_ref[0, pl.ds(start, tk), :]
        if masked:
            key_pos = start + lax.broadcasted_iota(jnp.int32, (tq, tk), 1)
            q_pos = q0 + lax.broadcasted_iota(jnp.int32, (tq, tk), 0)
            mask = key_pos < q_pos
        dims = (((1,), (1,)), ((), ()))
        nls = [lax.dot_general(q_heads[h], k_blk[:, h * HEAD_DIM:(h + 1) * HEAD_DIM], dims,
                               preferred_element_type=F32) for h in range(SB_HEADS)]
        lsms, laters = [], []
        for h in range(SB_HEADS):
            nl = nls[h]
            neg_abs = pltpu.bitcast(pltpu.bitcast(nl, jnp.int32) | jnp.int32(-2 ** 31), F32)
            lsm = jnp.minimum(nl, 0.0) - jnp.log2(1.0 + jnp.exp2(neg_abs))
            if masked:
                lsm = jnp.where(mask, lsm, 0.0)
            lsms.append(lsm)
            laters.append(jnp.dot(lsm.astype(BF16), later_mat, preferred_element_type=F32))
        outs = []
        for h in range(SB_HEADS):
            carry = carry_ref[h]
            expo = (lsms[h] - nls[h]) + laters[h] + jnp.concatenate([carry] * (tk // 128), axis=1)
            w = jnp.exp2(expo)
            if masked:
                w = jnp.where(mask, w, 0.0)
            outs.append(jnp.dot(w.astype(BF16), v_blk[:, h * HEAD_DIM:(h + 1) * HEAD_DIM], preferred_element_type=F32))
            row = laters[h][:, 0:1] + lsms[h][:, 0:1]
            carry_ref[h] = carry + jnp.broadcast_to(row, carry.shape)
        acc_ref[...] += jnp.concatenate(outs, axis=1)

    for d in range(n_diag - 1, -1, -1):
        block(n_full + d, True)

    def full_step(i, c):
        block(n_full - 1 - i, False)
        return c

    lax.fori_loop(0, n_full, full_step, 0)
    o_ref[0] = acc_ref[...].astype(o_ref.dtype)


def _sb_attention(q, k, v, *, off, tq, tk, out_dtype=BF16):
    B, Lq, W = q.shape
    Lk = k.shape[1]
    assert W == BRANCH_W and Lq % tq == 0 and Lk % tk == 0 and tk % 128 == 0
    nq = Lq // tq
    assert nq == 1 or (tq % tk == 0 and off % tk == 0)
    n_diag = -(-((off % tk) + tq - 1) // tk)
    assert (off + Lq - 1 + tk - 1) // tk <= Lk // tk
    body = partial(_sb_attn_body, tq=tq, tk=tk, off=off, n_diag=n_diag)
    return pl.pallas_call(
        body,
        grid=(B, nq),
        in_specs=[pl.BlockSpec((1, tq, W), lambda b, i: (b, i, 0)),
                  pl.BlockSpec((1, Lk, W), lambda b, i: (b, 0, 0)),
                  pl.BlockSpec((1, Lk, W), lambda b, i: (b, 0, 0))],
        out_specs=pl.BlockSpec((1, tq, W), lambda b, i: (b, i, 0)),
        out_shape=jax.ShapeDtypeStruct((B, Lq, W), out_dtype),
        scratch_shapes=[pltpu.VMEM((tq, W), F32), pltpu.VMEM((SB_HEADS, tq, 128), F32)],
        compiler_params=pltpu.CompilerParams(dimension_semantics=("arbitrary", "arbitrary"),
                                             vmem_limit_bytes=VMEM_LIMIT_V7X),
        name="sb_attention",
    )(q, k, v)


RW_SUB = 16
HI = lax.Precision.HIGHEST


def _split(x):
    hi = x.astype(BF16)
    lo = (x - hi.astype(F32)).astype(BF16)
    return hi, lo


def _mm3(a, b):
    ah, al = _split(a)
    bh, bl = _split(b)
    d = partial(jnp.dot, preferred_element_type=F32)
    return d(ah, bh) + (d(ah, bl) + d(al, bh))


def _dot_nt(a, b):
    return lax.dot_general(a, b, (((1,), (1,)), ((), ())), preferred_element_type=F32)


def _dot_tn(a, b):
    return lax.dot_general(a, b, (((0,), (0,)), ((), ())), preferred_element_type=F32)


def _rwkv_body(z_ref, shift_ref, s0_ref, mu_ref, w0_ref, w2_ref, a0_ref, a2_ref, g2_ref, kk_ref, ka_ref,
               rk_ref, lng_ref, lnb_ref, o_ref, s_out_ref, shift_out_ref, state_ref, prev_ref, *, T):
    ci = pl.program_id(1)
    nc = pl.num_programs(1)
    W = BRANCH_W
    N = RW_HEADS * T

    @pl.when(ci == 0)
    def _():
        state_ref[...] = s0_ref[0]
        prev_ref[...] = shift_ref[0]

    z = z_ref[0]
    row = lax.broadcasted_iota(jnp.int32, (T, RW_COLS), 0)
    z_prev = jnp.where(row == 0, jnp.broadcast_to(prev_ref[...], (T, RW_COLS)), pltpu.roll(z, 1, axis=0))
    prev_ref[...] = z[T - 1:T, :]
    zs = z + (z_prev - z) * mu_ref[...]
    r = zs[:, 0:W]
    k = zs[:, W:2 * W]
    v = zs[:, 2 * W:3 * W]
    xwa = zs[:, 3 * W:3 * W + 128]
    xg = zs[:, 3 * W + 128:]
    dotf = partial(jnp.dot, preferred_element_type=F32, precision=HI)
    w_log = -jax.nn.softplus(-(w0_ref[...] + dotf(jnp.tanh(xwa), w2_ref[...]))) - 0.5
    logdec = -jnp.exp(w_log)
    a = jax.nn.sigmoid(a0_ref[...] + dotf(xwa, a2_ref[...]))
    g = dotf(jax.nn.sigmoid(xg), g2_ref[...])
    li = lax.broadcasted_iota(jnp.int32, (W, W), 0) // RW_HD
    lj = lax.broadcasted_iota(jnp.int32, (W, W), 1) // RW_HD
    head_sum = (li == lj).astype(F32)
    kk = k * kk_ref[...]
    kk = kk * lax.rsqrt(dotf(kk * kk, head_sum) + 1e-12)
    k2 = k * (1.0 + (a - 1.0) * ka_ref[...])
    beta = kk * a
    ti = lax.broadcasted_iota(jnp.int32, (T, T), 0)
    tj = lax.broadcasted_iota(jnp.int32, (T, T), 1)
    b = dotf((ti >= tj).astype(F32), logdec)
    b_last = b[T - 1:T, :]
    k_in = kk * jnp.exp(b - logdec)
    r_in = r * jnp.exp(b)
    inv_p = jnp.exp(-b)
    k_out = k2 * inv_p
    b_out = beta * inv_p
    to_end = jnp.exp(b_last - b)
    k_end = k2 * to_end
    b_end = beta * to_end

    sh = lax.broadcasted_iota(jnp.int32, (N, W), 0) // T
    sl = lax.broadcasted_iota(jnp.int32, (N, W), 1) // RW_HD
    own = sh == sl

    def stack(x, masked):
        xs = jnp.concatenate([x] * RW_HEADS, axis=0)
        return jnp.where(own, xs, 0.0) if masked else xs

    k_in_s = stack(k_in, True).astype(BF16)
    r_in_s = stack(r_in, True).astype(BF16)
    k_out_s = stack(k_out, False).astype(BF16)
    b_out_s = stack(b_out, False).astype(BF16)
    v_s = stack(v, True).astype(BF16)
    k_end_s = stack(k_end, True).astype(BF16)
    b_end_s = stack(b_end, True).astype(BF16)

    ri = lax.broadcasted_iota(jnp.int32, (N, N), 0)
    rj = lax.broadcasted_iota(jnp.int32, (N, N), 1)
    same_head = (ri // T) == (rj // T)
    strict = same_head & (ri > rj)
    incl = same_head & (ri >= rj)
    a_mat = jnp.where(strict, _dot_nt(k_in_s, b_out_s), 0.0)
    kk_mat = jnp.where(strict, _dot_nt(k_in_s, k_out_s), 0.0)
    rk_mat = jnp.where(incl, _dot_nt(r_in_s, k_out_s), 0.0)
    rb_mat = jnp.where(incl, _dot_nt(r_in_s, b_out_s), 0.0)

    eye = (ri == rj).astype(F32)
    a_bd = jnp.where((ri // RW_SUB) == (rj // RW_SUB), a_mat, 0.0)
    x = eye - a_bd
    p = _mm3(a_bd, a_bd)
    x = x + _mm3(x, p)
    p = _mm3(p, p)
    x = x + _mm3(x, p)
    p = _mm3(p, p)
    x = x + _mm3(x, p)
    size = RW_SUB
    while size < T:
        lower = ((ri // (2 * size)) == (rj // (2 * size))) & ((ri // size) > (rj // size))
        x = x - _mm3(_mm3(x, jnp.where(lower, a_mat, 0.0)), x)
        size *= 2

    state = state_ref[...]
    state_b = state.astype(BF16)
    d = partial(jnp.dot, preferred_element_type=F32)
    rhs = _dot_nt(k_in_s, state_b) + d(kk_mat.astype(BF16), v_s)
    u = _mm3(x, rhs)
    u_b = u.astype(BF16)
    o_s = _dot_nt(r_in_s, state_b) + d(rk_mat.astype(BF16), v_s) - d(rb_mat.astype(BF16), u_b)
    o = o_s[0:T]
    for h in range(1, RW_HEADS):
        o = o + o_s[h * T:(h + 1) * T]
    state_ref[...] = state * jnp.exp(b_last) + _dot_tn(v_s, k_end_s) - _dot_tn(u_b, b_end_s)

    head_mean = head_sum * (1.0 / RW_HD)
    mean = dotf(o, head_mean)
    var = dotf(jnp.square(o - mean), head_mean)
    o = (o - mean) * lax.rsqrt(var + RW_GN_EPS) * lng_ref[...] + lnb_ref[...]
    bonus = dotf(r * k2 * rk_ref[...], head_sum) * v
    o_ref[0] = ((o + bonus) * g).astype(o_ref.dtype)

    @pl.when(ci == nc - 1)
    def _():
        s_out_ref[0] = state_ref[...]
        shift_out_ref[0] = z[T - 1:T, :]


def _rwkv7_pallas(z, s0_bd, shift_prev, mu, w0, w2, a0, a2, g2, k_k, k_a, r_k, ln_g, ln_b, *, T, out_dtype=F32):
    B, L, C = z.shape
    assert C == RW_COLS and L % T == 0 and T % RW_SUB == 0
    W = BRANCH_W
    w2p = jnp.concatenate([w2, jnp.zeros_like(w2)], axis=0)
    a2p = jnp.concatenate([jnp.zeros_like(a2), a2], axis=0)
    vec = lambda t: t.reshape(1, -1).astype(F32)
    full = lambda shape: pl.BlockSpec(shape, lambda b, c: (0,) * len(shape))
    return pl.pallas_call(
        partial(_rwkv_body, T=T),
        grid=(B, L // T),
        in_specs=[pl.BlockSpec((1, T, C), lambda b, c: (b, c, 0)),
                  pl.BlockSpec((1, 1, C), lambda b, c: (b, 0, 0)),
                  pl.BlockSpec((1, W, W), lambda b, c: (b, 0, 0)),
                  full((1, C)), full((1, W)), full((128, W)), full((1, W)), full((128, W)), full((128, W)),
                  full((1, W)), full((1, W)), full((1, W)), full((1, W)), full((1, W))],
        out_specs=[pl.BlockSpec((1, T, W), lambda b, c: (b, c, 0)),
                   pl.BlockSpec((1, W, W), lambda b, c: (b, 0, 0)),
                   pl.BlockSpec((1, 1, C), lambda b, c: (b, 0, 0))],
        out_shape=[jax.ShapeDtypeStruct((B, L, W), out_dtype),
                   jax.ShapeDtypeStruct((B, W, W), F32),
                   jax.ShapeDtypeStruct((B, 1, C), F32)],
        scratch_shapes=[pltpu.VMEM((W, W), F32), pltpu.VMEM((1, C), F32)],
        compiler_params=pltpu.CompilerParams(dimension_semantics=("arbitrary", "arbitrary")),
        name="rwkv7_chunked",
    )(z, shift_prev.reshape(B, 1, C), s0_bd, vec(mu), vec(w0), w2p, vec(a0), a2p, g2, vec(k_k), vec(k_a),
      vec(r_k), vec(ln_g), vec(ln_b))


def _state_to_bd(s):
    B = s.shape[0]
    eye = jnp.eye(RW_HEADS, dtype=s.dtype)
    return (s[:, :, :, None, :] * eye[None, :, None, :, None]).reshape(B, BRANCH_W, BRANCH_W)


def _state_from_bd(s_bd):
    B = s_bd.shape[0]
    s5 = s_bd.reshape(B, RW_HEADS, RW_HD, RW_HEADS, RW_HD)
    return jnp.stack([s5[:, h, :, h, :] for h in range(RW_HEADS)], axis=1)


def _hgrn_body(z_ref, s0_ref, loglb_ref, log1mlb_ref, ng_ref, o_ref, s_out_ref, state_ref, *, T):
    ci = pl.program_id(1)
    nc = pl.num_programs(1)
    W = BRANCH_W

    @pl.when(ci == 0)
    def _():
        state_ref[...] = s0_ref[0]

    z = z_ref[0]
    zq, zf, zi, zg = z[:, 0:W], z[:, W:2 * W], z[:, 2 * W:3 * W], z[:, 3 * W:]
    log_sig = jnp.minimum(zf, 0.0) - jnp.log(1.0 + jnp.exp(-jnp.abs(zf)))
    a = loglb_ref[...]
    bv = log1mlb_ref[...] + log_sig
    m = jnp.maximum(a, bv)
    log_f = m + jnp.log(jnp.exp(a - m) + jnp.exp(bv - m))
    k = 1.0 - jnp.exp(log_f)
    q = zq * jax.nn.sigmoid(zq)
    dotf = partial(jnp.dot, preferred_element_type=F32, precision=HI)
    ti = lax.broadcasted_iota(jnp.int32, (T, T), 0)
    tj = lax.broadcasted_iota(jnp.int32, (T, T), 1)
    b = dotf((ti >= tj).astype(F32), log_f)
    li = lax.broadcasted_iota(jnp.int32, (W, W), 0) // HG_DK
    lj = lax.broadcasted_iota(jnp.int32, (W, W), 1) // HG_DK
    same_head = li == lj
    head_sum = same_head.astype(BF16)

    prods, first = [], []
    for s in range(T):
        r0 = (s // SUBLANES) * SUBLANES
        t_idx = r0 + lax.broadcasted_iota(jnp.int32, (T - r0, W), 0)
        e = jnp.exp(jnp.where(t_idx >= s, b[r0:, :] - b[s:s + 1, :], -jnp.inf))
        prods.append((q[r0:, :] * e * k[s:s + 1, :]).astype(BF16))
        first.append(r0)
    scores = jnp.dot(jnp.concatenate(prods, axis=0), head_sum, preferred_element_type=F32)
    groups = [jnp.zeros((SUBLANES, W), F32) for _ in range(T // SUBLANES)]
    at = 0
    for s in range(T):
        for gi in range(first[s] // SUBLANES, T // SUBLANES):
            lo = at + gi * SUBLANES - first[s]
            groups[gi] = groups[gi] + scores[lo:lo + SUBLANES, :] * zi[s:s + 1, :]
        at += T - first[s]
    o = jnp.concatenate(groups, axis=0)

    state = state_ref[...]
    o = o + _dot_nt((q * jnp.exp(b)).astype(BF16), state.astype(BF16))
    b_last = b[T - 1:T, :]
    k_end = (k * jnp.exp(b_last - b)).astype(BF16)
    upd = _dot_tn(zi.astype(BF16), k_end)
    state_ref[...] = state * jnp.exp(b_last) + jnp.where(same_head, upd, 0.0)

    ms = dotf(o * o, same_head.astype(F32) * (1.0 / HG_DK))
    o = o * lax.rsqrt(ms + EPS) * ng_ref[...]
    o_ref[0] = (o * (zg * jax.nn.sigmoid(zg))).astype(o_ref.dtype)

    @pl.when(ci == nc - 1)
    def _():
        s_out_ref[0] = state_ref[...]


def _hgrn2_pallas(z, s0_bd, lb, norm_g, *, T, out_dtype=F32):
    B, L, C = z.shape
    assert C == HG_COLS and L % T == 0
    W = BRANCH_W
    full = lambda shape: pl.BlockSpec(shape, lambda b, c: (0,) * len(shape))
    return pl.pallas_call(
        partial(_hgrn_body, T=T),
        grid=(B, L // T),
        in_specs=[pl.BlockSpec((1, T, C), lambda b, c: (b, c, 0)),
                  pl.BlockSpec((1, W, W), lambda b, c: (b, 0, 0)),
                  full((1, W)), full((1, W)), full((1, W))],
        out_specs=[pl.BlockSpec((1, T, W), lambda b, c: (b, c, 0)),
                   pl.BlockSpec((1, W, W), lambda b, c: (b, 0, 0))],
        out_shape=[jax.ShapeDtypeStruct((B, L, W), out_dtype),
                   jax.ShapeDtypeStruct((B, W, W), F32)],
        scratch_shapes=[pltpu.VMEM((W, W), F32)],
        compiler_params=pltpu.CompilerParams(dimension_semantics=("arbitrary", "arbitrary")),
        name="hgrn2_chunked",
    )(z, s0_bd, jnp.log(lb).reshape(1, W), jnp.log1p(-lb).reshape(1, W), jnp.tile(norm_g, HG_HEADS).reshape(1, W))


PROJ_TM = 256
PROJ_VMEM_LIMIT_V7X = 48 * 1024 * 1024
CONV_CARRY = 32
CONV_TC = 512


def _in_proj_body(x_ref, g_ref, w_ref, q_ref, k_ref, v_ref, kb_ref, vb_ref, hg_ref, cv_ref, rw_ref, gate_ref):
    x = x_ref[...]
    h = (x * lax.rsqrt(jnp.mean(x * x, axis=-1, keepdims=True) + EPS) * g_ref[...]).astype(BF16)
    W = BRANCH_W

    def proj(a, b):
        return jnp.dot(h, w_ref[:, a:b], preferred_element_type=F32)

    q_ref[...] = (proj(0, W) * SB_Q_SCALE).astype(BF16)
    k = proj(W, 2 * W)
    v = proj(2 * W, 3 * W)
    k_ref[...] = k
    v_ref[...] = v
    kb_ref[...] = k.astype(BF16)
    vb_ref[...] = v.astype(BF16)
    c = SB_COLS
    hg_ref[...] = proj(c, c + HG_COLS)
    c += HG_COLS
    cv_ref[...] = proj(c, c + CV_COLS)
    c += CV_COLS
    rw_ref[...] = proj(c, c + RW_COLS)
    c += RW_COLS
    for n in range(N_BRANCH):
        gate_ref[:, n * D_MODEL:(n + 1) * D_MODEL] = jax.nn.sigmoid(
            proj(c + n * D_MODEL, c + (n + 1) * D_MODEL)).astype(BF16)


def _in_proj(x2, norm_g, w_in_b):
    n = x2.shape[0]
    TM = min(PROJ_TM, n)
    assert n % TM == 0
    row = lambda w: pl.BlockSpec((TM, w), lambda i: (i, 0))
    shp = lambda w, dt: jax.ShapeDtypeStruct((n, w), dt)
    W = BRANCH_W
    return pl.pallas_call(
        _in_proj_body,
        grid=(n // TM,),
        in_specs=[row(D_MODEL), pl.BlockSpec((1, D_MODEL), lambda i: (0, 0)),
                  pl.BlockSpec((D_MODEL, IN_COLS), lambda i: (0, 0), pipeline_mode=pl.Buffered(1))],
        out_specs=[row(W), row(W), row(W), row(W), row(W), row(HG_COLS), row(CV_COLS), row(RW_COLS), row(GATE_COLS)],
        out_shape=[shp(W, BF16), shp(W, F32), shp(W, F32), shp(W, BF16), shp(W, BF16),
                   shp(HG_COLS, F32), shp(CV_COLS, F32), shp(RW_COLS, F32), shp(GATE_COLS, BF16)],
        compiler_params=pltpu.CompilerParams(dimension_semantics=("arbitrary",), vmem_limit_bytes=PROJ_VMEM_LIMIT_V7X),
        name="in_proj",
    )(x2, norm_g.reshape(1, -1), w_in_b)


def _merge_body(x_ref, sb_ref, hg_ref, cv_ref, rw_ref, gate_ref, wb_ref, wo_ref, o_ref):
    merged = None
    for n, br in enumerate((sb_ref, hg_ref, cv_ref, rw_ref)):
        t = jnp.dot(br[...].astype(BF16), wb_ref[n], preferred_element_type=F32)
        t = t * gate_ref[:, n * D_MODEL:(n + 1) * D_MODEL].astype(F32)
        merged = t if merged is None else merged + t
    o_ref[...] = x_ref[...] + jnp.dot(merged.astype(BF16), wo_ref[...], preferred_element_type=F32)


def _merge(x2, o_sb, o_hg, o_cv, o_rw, gates, wb_b, wo_b):
    n = x2.shape[0]
    TM = min(PROJ_TM, n)
    row = lambda w: pl.BlockSpec((TM, w), lambda i: (i, 0))
    return pl.pallas_call(
        _merge_body,
        grid=(n // TM,),
        in_specs=[row(D_MODEL), row(BRANCH_W), row(BRANCH_W), row(BRANCH_W), row(BRANCH_W), row(GATE_COLS),
                  pl.BlockSpec((N_BRANCH, BRANCH_W, D_MODEL), lambda i: (0, 0, 0)),
                  pl.BlockSpec((D_MODEL, D_MODEL), lambda i: (0, 0))],
        out_specs=row(D_MODEL),
        out_shape=jax.ShapeDtypeStruct((n, D_MODEL), F32),
        compiler_params=pltpu.CompilerParams(dimension_semantics=("arbitrary",), vmem_limit_bytes=PROJ_VMEM_LIMIT_V7X),
        name="branch_merge",
    )(x2, o_sb, o_hg, o_cv, o_rw, gates, wb_b, wo_b)


def _conv_body(z_ref, buf_ref, w_ref, b_ref, lg_ref, lb_ref, o_ref, buf_out_ref, hp_ref, *, TC):
    ci = pl.program_id(1)
    nc = pl.num_programs(1)

    @pl.when(ci == 0)
    def _():
        hp_ref[0:CONV_CARRY, :] = buf_ref[0]

    z = z_ref[0]
    hp_ref[CONV_CARRY:, :] = z[:, 0:CONV_CH] * jax.nn.sigmoid(z[:, CONV_CH:])
    first = CONV_CARRY - (CONV_WIDTH - 1)
    y = jnp.zeros((TC, CONV_CH), F32) + b_ref[...]
    for j in range(CONV_WIDTH):
        y = y + hp_ref[first + j:first + j + TC, :] * w_ref[j:j + 1, :]
    mu = jnp.mean(y, axis=-1, keepdims=True)
    d = y - mu
    var = jnp.mean(d * d, axis=-1, keepdims=True)
    y = d * lax.rsqrt(var + EPS) * lg_ref[...] + lb_ref[...]
    o_ref[0] = (y * jax.nn.sigmoid(y)).astype(o_ref.dtype)
    tail = hp_ref[TC:TC + CONV_CARRY, :]

    @pl.when(ci == nc - 1)
    def _():
        buf_out_ref[0] = tail

    hp_ref[0:CONV_CARRY, :] = tail


def _conformer_conv_pallas(z_cv, buf, w, b, ln_g, ln_b, *, TC, out_dtype=F32):
    B, L, _ = z_cv.shape
    assert L % TC == 0 and TC % 8 == 0 and TC >= CONV_CARRY
    C = CONV_CH
    pad = CONV_CARRY - (CONV_WIDTH - 1)
    buf_p = jnp.pad(buf.astype(F32), ((0, 0), (pad, 0), (0, 0)))
    w_p = jnp.pad(w.astype(F32), ((0, 32 - CONV_WIDTH), (0, 0)))
    full = lambda shape: pl.BlockSpec(shape, lambda b, c: (0,) * len(shape))
    y, buf_new = pl.pallas_call(
        partial(_conv_body, TC=TC),
        grid=(B, L // TC),
        in_specs=[pl.BlockSpec((1, TC, 2 * C), lambda b, c: (b, c, 0)),
                  pl.BlockSpec((1, CONV_CARRY, C), lambda b, c: (b, 0, 0)),
                  full((32, C)), full((1, C)), full((1, C)), full((1, C))],
        out_specs=[pl.BlockSpec((1, TC, C), lambda b, c: (b, c, 0)),
                   pl.BlockSpec((1, CONV_CARRY, C), lambda b, c: (b, 0, 0))],
        out_shape=[jax.ShapeDtypeStruct((B, L, C), out_dtype), jax.ShapeDtypeStruct((B, CONV_CARRY, C), F32)],
        scratch_shapes=[pltpu.VMEM((CONV_CARRY + TC, C), F32)],
        compiler_params=pltpu.CompilerParams(dimension_semantics=("arbitrary", "arbitrary")),
        name="conformer_conv",
    )(z_cv, buf_p, w_p, b.reshape(1, C), ln_g.reshape(1, C), ln_b.reshape(1, C))
    return y, buf_new[:, pad:, :]


PK_PAIRS = PK_HEADS * PK_TOPK
PEER_TB = 128
SUBLANES = 8
LANES = 128
ROW_TILES = D_MODEL // LANES
PEER_VMEM_LIMIT_V7X = 52 * 1024 * 1024


def _topk_rows(s, payload, k):
    n = s.shape[0]
    rows = lax.broadcasted_iota(jnp.int32, s.shape, 0).astype(F32)
    vals, ids = [], []
    for _ in range(k):
        m = jnp.max(s, axis=0, keepdims=True)
        pos = jnp.min(jnp.where(s == m, rows, float(n)), axis=0, keepdims=True)
        sel = rows == pos
        ids.append(pos if payload is None else jnp.max(jnp.where(sel, payload, -1.0), axis=0, keepdims=True))
        vals.append(m)
        s = jnp.where(sel, -jnp.inf, s)
    return jnp.concatenate(vals, axis=0), jnp.concatenate(ids, axis=0)


def _dot3(ah, al, bh, bl, dims):
    d = partial(lax.dot_general, dimension_numbers=dims, preferred_element_type=F32)
    return d(ah, bh) + (d(ah, bl) + d(al, bh))


def _peer_route_body(x_ref, g_ref, wqh_ref, wql_ref, qg_ref, k1_ref, k2_ref, xn_ref, idx_ref, idxc_ref, gate_ref):
    x = x_ref[...]
    xn = x * lax.rsqrt(jnp.mean(x * x, axis=-1, keepdims=True) + EPS) * g_ref[...]
    xn_ref[...] = xn
    nn = (((1,), (0,)), ((), ()))
    nt = (((1,), (1,)), ((), ()))
    xh, xl = _split(xn)
    q = _dot3(xh, xl, wqh_ref[...], wql_ref[...], nn)
    half = PK_DQ // 2
    experts, gates = [], []
    for h in range(PK_HEADS):
        qh = q[:, h * PK_DQ:(h + 1) * PK_DQ]
        qh = qh * lax.rsqrt(jnp.mean(qh * qh, axis=-1, keepdims=True) + EPS) * qg_ref[...]
        qhh, qhl = _split(qh)
        k1h, k1l = _split(k1_ref[h])
        k2h, k2l = _split(k2_ref[h])
        s1 = _dot3(k1h, k1l, qhh[:, :half], qhl[:, :half], nt)
        s2 = _dot3(k2h, k2l, qhh[:, half:], qhl[:, half:], nt)
        v1, i1 = _topk_rows(s1, None, PK_TOPK)
        v2, i2 = _topk_rows(s2, None, PK_TOPK)
        cand_s = jnp.concatenate([v1[a:a + 1, :] + v2 for a in range(PK_TOPK)], axis=0)
        top_s, pos = _topk_rows(cand_s, None, PK_TOPK)
        pa = jnp.floor(pos * (1.0 / PK_TOPK))
        pb = pos - pa * PK_TOPK
        e1 = jnp.zeros_like(pos)
        e2 = jnp.zeros_like(pos)
        for a in range(PK_TOPK):
            e1 = jnp.where(pa == a, i1[a:a + 1, :], e1)
            e2 = jnp.where(pb == a, i2[a:a + 1, :], e2)
        expert = e1 * float(PK_NKEYS) + e2
        e = jnp.exp(top_s - jnp.max(top_s, axis=0, keepdims=True))
        gates.append(e / jnp.sum(e, axis=0, keepdims=True))
        experts.append(expert)
    expert = jnp.concatenate(experts, axis=0).astype(jnp.int32)
    row = (expert + 1) * HALF_TILES
    pair = lax.broadcasted_iota(jnp.int32, row.shape, 0)
    idx_ref[...] = row.T
    idxc_ref[...] = (row - (pair & HALF_TILES)).T
    gate_ref[...] = jnp.concatenate(gates, axis=0)


def _peer_route(x2, norm_g, wq, qn_g, k1, k2):
    n_tok = x2.shape[0]
    TB = PEER_TB
    assert n_tok % TB == 0
    nb = n_tok // TB
    full = lambda shape: pl.BlockSpec(shape, lambda i: (0,) * len(shape))
    wq_hi = wq.astype(BF16)
    wq_lo = (wq - wq_hi.astype(F32)).astype(BF16)
    return pl.pallas_call(
        _peer_route_body,
        grid=(nb,),
        in_specs=[pl.BlockSpec((TB, D_MODEL), lambda i: (i, 0)), full((1, D_MODEL)),
                  full((D_MODEL, PK_HEADS * PK_DQ)), full((D_MODEL, PK_HEADS * PK_DQ)), full((1, PK_DQ)),
                  full((PK_HEADS, PK_NKEYS, PK_DQ // 2)), full((PK_HEADS, PK_NKEYS, PK_DQ // 2))],
        out_specs=[pl.BlockSpec((TB, D_MODEL), lambda i: (i, 0)),
                   pl.BlockSpec((TB, PK_PAIRS), lambda i: (i, 0)),
                   pl.BlockSpec((TB, PK_PAIRS), lambda i: (i, 0)),
                   pl.BlockSpec((PK_PAIRS, TB), lambda i: (0, i))],
        out_shape=[jax.ShapeDtypeStruct((n_tok, D_MODEL), F32),
                   jax.ShapeDtypeStruct((n_tok, PK_PAIRS), jnp.int32),
                   jax.ShapeDtypeStruct((n_tok, PK_PAIRS), jnp.int32),
                   jax.ShapeDtypeStruct((PK_PAIRS, n_tok), F32)],
        compiler_params=pltpu.CompilerParams(dimension_semantics=("arbitrary",)),
        name="peer_route",
    )(x2, norm_g.reshape(1, -1), wq_hi, wq_lo, qn_g.reshape(1, -1), k1, k2)


HALF_TILES = ROW_TILES // 2


def _unpack_words(word):
    return pltpu.bitcast(word << 16, F32), pltpu.bitcast(word & jnp.int32(-65536), F32)


def _fold_couples(tiles, sub):
    m2 = (sub & 3) < 2
    lvl2 = []
    for a, b in ((tiles[0], tiles[2]), (tiles[1], tiles[3])):
        lvl2.append(jnp.where(m2, a, b) + jnp.where(m2, pltpu.roll(a, 6, axis=0), pltpu.roll(b, 2, axis=0)))
    m1 = (sub & 1) == 0
    a, b = lvl2
    return jnp.where(m1, a, b) + jnp.where(m1, pltpu.roll(a, 7, axis=0), pltpu.roll(b, 1, axis=0))


def _peer_act_body(idx_ref, x_ref, gate_ref, tab_ref, w_ref, part_ref):
    TB = x_ref.shape[0]
    sub = lax.broadcasted_iota(jnp.int32, (SUBLANES, LANES), 0)
    low = sub < HALF_TILES

    def token(t, c):
        x = x_ref[t]
        x_lo = jnp.concatenate([x[0:HALF_TILES]] * 2, axis=0)
        x_hi = jnp.concatenate([x[HALF_TILES:]] * 2, axis=0)
        start = t * PK_PAIRS
        for g in range(PK_PAIRS // SUBLANES):
            tiles = []
            for j in range(HALF_TILES):
                ra = pl.multiple_of(idx_ref[0, 0, start + g * SUBLANES + j], HALF_TILES)
                rb = pl.multiple_of(idx_ref[0, 0, start + g * SUBLANES + j + HALF_TILES], HALF_TILES)
                word = jnp.where(low, tab_ref[pl.ds(ra, SUBLANES), :], tab_ref[pl.ds(rb, SUBLANES), :])
                lo, hi = _unpack_words(word)
                tiles.append(lo * x_lo + hi * x_hi)
            part_ref[t, g * SUBLANES:(g + 1) * SUBLANES, :] = _fold_couples(tiles, sub)
        return c

    lax.fori_loop(0, TB, token, 0)
    lane = lax.broadcasted_iota(jnp.int32, (PK_PAIRS, TB), 1)
    s = jnp.zeros((PK_PAIRS, TB), F32)
    for t in range(TB):
        s = jnp.where(lane == t, jnp.sum(part_ref[t], axis=1, keepdims=True), s)
    w_ref[...] = gate_ref[...] * jax.nn.gelu(s)


PEER_MIX_ACCS = 4


def _peer_mix_body(idx_ref, w_ref, tab_ref, y_ref, wrep_ref):
    TB = y_ref.shape[0]
    w_all = w_ref[...]
    for t in range(TB):
        wrep_ref[t] = jnp.broadcast_to(w_all[:, t:t + 1], (PK_PAIRS, LANES))

    def token(t, c):
        start = t * PK_PAIRS
        acc_lo = [jnp.zeros((HALF_TILES, LANES), F32) for _ in range(PEER_MIX_ACCS)]
        acc_hi = [jnp.zeros((HALF_TILES, LANES), F32) for _ in range(PEER_MIX_ACCS)]
        for p in range(PK_PAIRS):
            row = pl.multiple_of(idx_ref[0, 0, start + p], HALF_TILES)
            lo, hi = _unpack_words(tab_ref[pl.ds(row, HALF_TILES), :])
            w = jnp.broadcast_to(wrep_ref[t, p:p + 1, :], (HALF_TILES, LANES))
            acc_lo[p % PEER_MIX_ACCS] = acc_lo[p % PEER_MIX_ACCS] + w * lo
            acc_hi[p % PEER_MIX_ACCS] = acc_hi[p % PEER_MIX_ACCS] + w * hi
        y_ref[t] = jnp.concatenate([(acc_lo[0] + acc_lo[1]) + (acc_lo[2] + acc_lo[3]),
                                    (acc_hi[0] + acc_hi[1]) + (acc_hi[2] + acc_hi[3])], axis=0)
        return c

    lax.fori_loop(0, TB, token, 0)


def _pack_table(tab):
    bits = lax.bitcast_convert_type(tab.astype(BF16), jnp.uint16).astype(jnp.uint32)
    bits = bits.reshape(PK_EXPERTS, 2, HALF_TILES, LANES)
    word = bits[:, 0] | (bits[:, 1] << 16)
    rows = lax.bitcast_convert_type(word, jnp.int32).reshape(PK_EXPERTS * HALF_TILES, LANES)
    return jnp.pad(rows, ((HALF_TILES, HALF_TILES), (0, 0)))


def _smem_spec(TB):
    return pl.BlockSpec((1, 1, PK_PAIRS * TB), lambda i: (i, 0, 0), memory_space=pltpu.SMEM)


def _table_spec():
    return pl.BlockSpec(((PK_EXPERTS + 2) * HALF_TILES, LANES), lambda i: (0, 0), pipeline_mode=pl.Buffered(1))


def _peer_act(idx, xn3, gate_t, tab_packed):
    n_tok = xn3.shape[0]
    TB = PEER_TB
    nb = n_tok // TB
    return pl.pallas_call(
        _peer_act_body,
        grid=(nb,),
        in_specs=[_smem_spec(TB),
                  pl.BlockSpec((TB, ROW_TILES, LANES), lambda i: (i, 0, 0)),
                  pl.BlockSpec((PK_PAIRS, TB), lambda i: (0, i)),
                  _table_spec()],
        out_specs=pl.BlockSpec((PK_PAIRS, TB), lambda i: (0, i)),
        out_shape=jax.ShapeDtypeStruct((PK_PAIRS, n_tok), F32),
        scratch_shapes=[pltpu.VMEM((TB, PK_PAIRS, LANES), F32)],
        compiler_params=pltpu.CompilerParams(dimension_semantics=("arbitrary",),
                                             vmem_limit_bytes=PEER_VMEM_LIMIT_V7X),
        name="peer_act",
    )(idx.reshape(nb, 1, TB * PK_PAIRS), xn3, gate_t, tab_packed)


def _peer_mix(idx, w_t, tab_packed):
    n_tok = w_t.shape[1]
    TB = PEER_TB
    nb = n_tok // TB
    return pl.pallas_call(
        _peer_mix_body,
        grid=(nb,),
        in_specs=[_smem_spec(TB),
                  pl.BlockSpec((PK_PAIRS, TB), lambda i: (0, i)),
                  _table_spec()],
        out_specs=pl.BlockSpec((TB, ROW_TILES, LANES), lambda i: (i, 0, 0)),
        out_shape=jax.ShapeDtypeStruct((n_tok, ROW_TILES, LANES), F32),
        scratch_shapes=[pltpu.VMEM((TB, PK_PAIRS, LANES), F32)],
        compiler_params=pltpu.CompilerParams(dimension_semantics=("arbitrary",),
                                             vmem_limit_bytes=PEER_VMEM_LIMIT_V7X),
        name="peer_mix",
    )(idx.reshape(nb, 1, TB * PK_PAIRS), w_t, tab_packed)


def _peer_ffn_pallas(x, norm_g, wq, qn_g, sub_keys, u_packed, v_packed):
    B, L, D = x.shape
    n_tok = B * L
    pad = (-n_tok) % PEER_TB
    x2 = jnp.pad(x.reshape(n_tok, D), ((0, pad), (0, 0)))
    xn, idx, idx_couple, gate_t = _peer_route(x2, norm_g, wq, qn_g, sub_keys[:, 0], sub_keys[:, 1])
    w_t = _peer_act(idx_couple, xn.reshape(-1, ROW_TILES, LANES), gate_t, u_packed)
    y = _peer_mix(idx, w_t, v_packed)
    return y.reshape(-1, D)[:n_tok].reshape(B, L, D)


def _trunk_layer(x, k_past, v_past, s_hg, buf_conv, s_rw, buf_shift, p):
    B_, L, _ = x.shape
    n = B_ * L
    x2 = x.reshape(n, D_MODEL)
    qb, k, v, kb, vb, z_hg, z_cv, z_rw, gates = _in_proj(x2, p['norm_mix_g'], p['w_in'])
    seq = lambda t: t.reshape(B_, L, t.shape[-1])
    if k_past is None:
        o_sb = _sb_attention(seq(qb), seq(kb), seq(vb), off=0, tq=SB_TQ, tk=SB_TK, out_dtype=F32)
    else:
        past = k_past.shape[1]
        pad = jnp.zeros((B_, (-(past + L)) % SB_TK, BRANCH_W), BF16)
        kc = jnp.concatenate([k_past.reshape(B_, past, BRANCH_W).astype(BF16), seq(kb), pad], axis=1)
        vc = jnp.concatenate([v_past.reshape(B_, past, BRANCH_W).astype(BF16), seq(vb), pad], axis=1)
        o_sb = _sb_attention(seq(qb), kc, vc, off=past, tq=L, tk=SB_TK, out_dtype=F32)
    o_hg, s_hg_bd = _hgrn2_pallas(seq(z_hg), _state_to_bd(jnp.swapaxes(s_hg.astype(F32), 2, 3)), p['lb'],
                                  p['hg_norm_g'], T=CHUNK if L % CHUNK == 0 else L)
    s_hg_new = jnp.swapaxes(_state_from_bd(s_hg_bd), 2, 3)
    o_cv, buf_conv_new = _conformer_conv_pallas(seq(z_cv), buf_conv, p['conv_w'], p['conv_b'], p['conv_ln_g'],
                                                p['conv_ln_b'], TC=CONV_TC if L % CONV_TC == 0 else L)
    o_rw, s_rw_bd, shift_new = _rwkv7_pallas(seq(z_rw), _state_to_bd(s_rw.astype(F32)), buf_shift, p['rw_mu'],
                                             p['rw_w0'], p['rw_w2'], p['rw_a0'], p['rw_a2'], p['rw_g2'], p['rw_k_k'],
                                             p['rw_k_a'], p['rw_r_k'], p['rw_ln_g'], p['rw_ln_b'],
                                             T=RW_CHUNK if L % RW_CHUNK == 0 else L)
    s_rw_new = _state_from_bd(s_rw_bd)
    buf_shift_new = shift_new[:, 0]
    flat = lambda t: t.reshape(n, BRANCH_W)
    x = _merge(x2, flat(o_sb), flat(o_hg), flat(o_cv), flat(o_rw), gates, p['w_branch'], p['w_out']).reshape(x.shape)
    x = x + _peer_ffn_pallas(x, p['norm_ffn_g'], p['peer_wq'], p['peer_qn_g'], p['peer_keys'],
                             p['peer_u'], p['peer_v'])
    heads = lambda t: t.reshape(B_, L, SB_HEADS, HEAD_DIM)
    return x, (heads(k), heads(v), s_hg_new, buf_conv_new, s_rw_new, buf_shift_new)


def kernel(x_prompt, x_sample, cache_sb_k, cache_sb_v, state_hgrn, state_conv, state_rwkv, state_shift,
           norm_mix_g, w_in, hg_lb_logits, hg_norm_g, conv_w, conv_b, conv_ln_g, conv_ln_b,
           rw_mu, rw_w0, rw_w2, rw_a0, rw_a2, rw_g2, rw_k_k, rw_k_a, rw_r_k, rw_ln_g, rw_ln_b,
           w_branch, w_out, norm_ffn_g, peer_wq, peer_qn_g, peer_keys, peer_u, peer_v, final_norm_g):
    lb_all = jnp.cumsum(jax.nn.softmax(hg_lb_logits.astype(F32), axis=0), axis=0)
    lb_all = lb_all - lb_all[0:1]
    Bp = x_prompt.shape[0]
    hg0 = jnp.zeros((Bp, HG_HEADS, HG_DK, HG_DV), F32)
    cv0 = jnp.zeros((Bp, CONV_WIDTH - 1, CONV_CH), x_prompt.dtype)
    rw0 = jnp.zeros((Bp, RW_HEADS, RW_HD, RW_HD), F32)
    sh0 = jnp.zeros((Bp, RW_COLS), x_prompt.dtype)
    xp, xs = x_prompt, x_sample
    outs_p, outs_s = [], []
    for l in range(DEPTH):
        lp = dict(norm_mix_g=norm_mix_g[l], w_in=w_in[l].astype(BF16), lb=lb_all[l], hg_norm_g=hg_norm_g[l],
                  conv_w=conv_w[l], conv_b=conv_b[l], conv_ln_g=conv_ln_g[l], conv_ln_b=conv_ln_b[l],
                  rw_mu=rw_mu[l], rw_w0=rw_w0[l], rw_w2=rw_w2[l], rw_a0=rw_a0[l], rw_a2=rw_a2[l],
                  rw_g2=rw_g2[l], rw_k_k=rw_k_k[l], rw_k_a=rw_k_a[l], rw_r_k=rw_r_k[l],
                  rw_ln_g=rw_ln_g[l], rw_ln_b=rw_ln_b[l], w_branch=w_branch[l].astype(BF16),
                  w_out=w_out[l].astype(BF16),
                  norm_ffn_g=norm_ffn_g[l], peer_wq=peer_wq[l], peer_qn_g=peer_qn_g[l],
                  peer_keys=peer_keys[l], peer_u=_pack_table(peer_u[l]), peer_v=_pack_table(peer_v[l]))
        xp, st_p = _trunk_layer(xp, None, None, hg0, cv0, rw0, sh0, lp)
        xs, st_s = _trunk_layer(xs, cache_sb_k[l], cache_sb_v[l], state_hgrn[l], state_conv[l],
                                state_rwkv[l], state_shift[l], lp)
        outs_p.append(st_p)
        outs_s.append(st_s)

    def stk(outs, i):
        return jnp.stack([o[i] for o in outs], axis=0)

    y_prompt = _rmsnorm_pallas(xp, final_norm_g)
    y_sample = _rmsnorm_pallas(xs, final_norm_g)
    return (y_prompt, y_sample,
            stk(outs_p, 0), stk(outs_p, 1), stk(outs_p, 2), stk(outs_p, 3), stk(outs_p, 4), stk(outs_p, 5),
            stk(outs_s, 0), stk(outs_s, 1), stk(outs_s, 2), stk(outs_s, 3), stk(outs_s, 4), stk(outs_s, 5))
```

```python
from functools import partial

import jax
import jax.numpy as jnp
from jax import lax
from jax.experimental import pallas as pl
from jax.experimental.pallas import tpu as pltpu

D_MODEL = 1024
DEPTH = 2
CHUNK = 64
HEAD_DIM = 64
N_BRANCH = 4
BRANCH_W = D_MODEL // 4
SB_HEADS = BRANCH_W // HEAD_DIM
HG_HEADS = 4
HG_DK = BRANCH_W // HG_HEADS
HG_DV = BRANCH_W // HG_HEADS
CONV_CH = BRANCH_W
CONV_WIDTH = 31
RW_HEADS = 4
RW_HD = BRANCH_W // RW_HEADS
RW_DECAY_LORA = 64
RW_AAA_LORA = 64
RW_GATE_LORA = 128
SB_COLS = 3 * BRANCH_W
HG_COLS = 4 * BRANCH_W
CV_COLS = 2 * CONV_CH
RW_COLS = 3 * BRANCH_W + RW_DECAY_LORA + RW_AAA_LORA + RW_GATE_LORA
GATE_COLS = N_BRANCH * D_MODEL
IN_COLS = SB_COLS + HG_COLS + CV_COLS + RW_COLS + GATE_COLS
PK_HEADS = 8
PK_NKEYS = 128
PK_EXPERTS = PK_NKEYS * PK_NKEYS
PK_DQ = 256
PK_TOPK = 16
EPS = 1e-6
RW_GN_EPS = 64e-5
SB_TQ = 256
SB_TK = 256
RW_CHUNK = 64
F32 = jnp.float32
BF16 = jnp.bfloat16


def _rmsnorm_body(x_ref, g_ref, o_ref):
    x = x_ref[...]
    y = x * lax.rsqrt(jnp.mean(x * x, axis=-1, keepdims=True) + EPS)
    o_ref[...] = y * g_ref[...]


def _rmsnorm_pallas(x, g, rows=512):
    shape = x.shape
    x2 = x.reshape(-1, shape[-1])
    n, d = x2.shape
    rows = min(rows, n)
    out = pl.pallas_call(
        _rmsnorm_body,
        grid=(n // rows,),
        in_specs=[pl.BlockSpec((rows, d), lambda i: (i, 0)),
                  pl.BlockSpec((1, d), lambda i: (0, 0))],
        out_specs=pl.BlockSpec((rows, d), lambda i: (i, 0)),
        out_shape=jax.ShapeDtypeStruct((n, d), x.dtype),
        name="final_rmsnorm",
    )(x2, g.reshape(1, d))
    return out.reshape(shape)


VMEM_LIMIT_V7X = 48 * 1024 * 1024
SB_Q_SCALE = -(HEAD_DIM ** -0.5) * 1.4426950408889634


def _sb_attn_body(q_ref, k_ref, v_ref, o_ref, acc_ref, carry_ref, *, tq, tk, off, n_diag):
    qi = pl.program_id(1)
    q0 = off + qi * tq
    n_full = q0 // tk
    acc_ref[...] = jnp.zeros_like(acc_ref)
    carry_ref[...] = jnp.zeros_like(carry_ref)
    q_all = q_ref[0]
    q_heads = [q_all[:, h * HEAD_DIM:(h + 1) * HEAD_DIM] for h in range(SB_HEADS)]
    jj = lax.broadcasted_iota(jnp.int32, (tk, tk), 0)
    ss = lax.broadcasted_iota(jnp.int32, (tk, tk), 1)
    later_mat = (jj > ss).astype(BF16)

    def block(kb, masked):
        start = pl.multiple_of(kb * tk, tk)
        k_blk = k_ref[0, pl.ds(start, tk), :]
        v_blk = v_ref[0, pl.ds(start, tk), :]
        if masked:
            key_pos = start + lax.broadcasted_iota(jnp.int32, (tq, tk), 1)
            q_pos = q0 + lax.broadcasted_iota(jnp.int32, (tq, tk), 0)
            mask = key_pos < q_pos
        dims = (((1,), (1,)), ((), ()))
        nls = [lax.dot_general(q_heads[h], k_blk[:, h * HEAD_DIM:(h + 1) * HEAD_DIM], dims,
                               preferred_element_type=F32) for h in range(SB_HEADS)]
        lsms, laters = [], []
        for h in range(SB_HEADS):
            nl = nls[h]
            neg_abs = pltpu.bitcast(pltpu.bitcast(nl, jnp.int32) | jnp.int32(-2 ** 31), F32)
            lsm = jnp.minimum(nl, 0.0) - jnp.log2(1.0 + jnp.exp2(neg_abs))
            if masked:
                lsm = jnp.where(mask, lsm, 0.0)
            lsms.append(lsm)
            laters.append(jnp.dot(lsm.astype(BF16), later_mat, preferred_element_type=F32))
        outs = []
        for h in range(SB_HEADS):
            carry = carry_ref[h]
            expo = (lsms[h] - nls[h]) + laters[h] + jnp.concatenate([carry] * (tk // 128), axis=1)
            w = jnp.exp2(expo)
            if masked:
                w = jnp.where(mask, w, 0.0)
            outs.append(jnp.dot(w.astype(BF16), v_blk[:, h * HEAD_DIM:(h + 1) * HEAD_DIM], preferred_element_type=F32))
            row = laters[h][:, 0:1] + lsms[h][:, 0:1]
            carry_ref[h] = carry + jnp.broadcast_to(row, carry.shape)
        acc_ref[...] += jnp.concatenate(outs, axis=1)

    for d in range(n_diag - 1, -1, -1):
        block(n_full + d, True)

    def full_step(i, c):
        block(n_full - 1 - i, False)
        return c

    lax.fori_loop(0, n_full, full_step, 0)
    o_ref[0] = acc_ref[...].astype(o_ref.dtype)


def _sb_attention(q, k, v, *, off, tq, tk, out_dtype=BF16):
    B, Lq, W = q.shape
    Lk = k.shape[1]
    assert W == BRANCH_W and Lq % tq == 0 and Lk % tk == 0 and tk % 128 == 0
    nq = Lq // tq
    assert nq == 1 or (tq % tk == 0 and off % tk == 0)
    n_diag = -(-((off % tk) + tq - 1) // tk)
    assert (off + Lq - 1 + tk - 1) // tk <= Lk // tk
    body = partial(_sb_attn_body, tq=tq, tk=tk, off=off, n_diag=n_diag)
    return pl.pallas_call(
        body,
        grid=(B, nq),
        in_specs=[pl.BlockSpec((1, tq, W), lambda b, i: (b, i, 0)),
                  pl.BlockSpec((1, Lk, W), lambda b, i: (b, 0, 0)),
                  pl.BlockSpec((1, Lk, W), lambda b, i: (b, 0, 0))],
        out_specs=pl.BlockSpec((1, tq, W), lambda b, i: (b, i, 0)),
        out_shape=jax.ShapeDtypeStruct((B, Lq, W), out_dtype),
        scratch_shapes=[pltpu.VMEM((tq, W), F32), pltpu.VMEM((SB_HEADS, tq, 128), F32)],
        compiler_params=pltpu.CompilerParams(dimension_semantics=("arbitrary", "arbitrary"),
                                             vmem_limit_bytes=VMEM_LIMIT_V7X),
        name="sb_attention",
    )(q, k, v)


RW_SUB = 16
HI = lax.Precision.HIGHEST


def _split(x):
    hi = x.astype(BF16)
    lo = (x - hi.astype(F32)).astype(BF16)
    return hi, lo


def _mm3(a, b):
    ah, al = _split(a)
    bh, bl = _split(b)
    d = partial(jnp.dot, preferred_element_type=F32)
    return d(ah, bh) + (d(ah, bl) + d(al, bh))


def _dot_nt(a, b):
    return lax.dot_general(a, b, (((1,), (1,)), ((), ())), preferred_element_type=F32)


def _dot_tn(a, b):
    return lax.dot_general(a, b, (((0,), (0,)), ((), ())), preferred_element_type=F32)


def _rwkv_body(z_ref, shift_ref, s0_ref, mu_ref, w0_ref, w2_ref, a0_ref, a2_ref, g2_ref, kk_ref, ka_ref,
               rk_ref, lng_ref, lnb_ref, o_ref, s_out_ref, shift_out_ref, state_ref, prev_ref, *, T):
    ci = pl.program_id(1)
    nc = pl.num_programs(1)
    W = BRANCH_W
    N = RW_HEADS * T

    @pl.when(ci == 0)
    def _():
        state_ref[...] = s0_ref[0]
        prev_ref[...] = shift_ref[0]

    z = z_ref[0]
    row = lax.broadcasted_iota(jnp.int32, (T, RW_COLS), 0)
    z_prev = jnp.where(row == 0, jnp.broadcast_to(prev_ref[...], (T, RW_COLS)), pltpu.roll(z, 1, axis=0))
    prev_ref[...] = z[T - 1:T, :]
    zs = z + (z_prev - z) * mu_ref[...]
    r = zs[:, 0:W]
    k = zs[:, W:2 * W]
    v = zs[:, 2 * W:3 * W]
    xwa = zs[:, 3 * W:3 * W + 128]
    xg = zs[:, 3 * W + 128:]
    dotf = partial(jnp.dot, preferred_element_type=F32, precision=HI)
    w_log = -jax.nn.softplus(-(w0_ref[...] + dotf(jnp.tanh(xwa), w2_ref[...]))) - 0.5
    logdec = -jnp.exp(w_log)
    a = jax.nn.sigmoid(a0_ref[...] + dotf(xwa, a2_ref[...]))
    g = dotf(jax.nn.sigmoid(xg), g2_ref[...])
    li = lax.broadcasted_iota(jnp.int32, (W, W), 0) // RW_HD
    lj = lax.broadcasted_iota(jnp.int32, (W, W), 1) // RW_HD
    head_sum = (li == lj).astype(F32)
    kk = k * kk_ref[...]
    kk = kk * lax.rsqrt(dotf(kk * kk, head_sum) + 1e-12)
    k2 = k * (1.0 + (a - 1.0) * ka_ref[...])
    beta = kk * a
    ti = lax.broadcasted_iota(jnp.int32, (T, T), 0)
    tj = lax.broadcasted_iota(jnp.int32, (T, T), 1)
    b = dotf((ti >= tj).astype(F32), logdec)
    b_last = b[T - 1:T, :]
    k_in = kk * jnp.exp(b - logdec)
    r_in = r * jnp.exp(b)
    inv_p = jnp.exp(-b)
    k_out = k2 * inv_p
    b_out = beta * inv_p
    to_end = jnp.exp(b_last - b)
    k_end = k2 * to_end
    b_end = beta * to_end

    sh = lax.broadcasted_iota(jnp.int32, (N, W), 0) // T
    sl = lax.broadcasted_iota(jnp.int32, (N, W), 1) // RW_HD
    own = sh == sl

    def stack(x, masked):
        xs = jnp.concatenate([x] * RW_HEADS, axis=0)
        return jnp.where(own, xs, 0.0) if masked else xs

    k_in_s = stack(k_in, True).astype(BF16)
    r_in_s = stack(r_in, True).astype(BF16)
    k_out_s = stack(k_out, False).astype(BF16)
    b_out_s = stack(b_out, False).astype(BF16)
    v_s = stack(v, True).astype(BF16)
    k_end_s = stack(k_end, True).astype(BF16)
    b_end_s = stack(b_end, True).astype(BF16)

    ri = lax.broadcasted_iota(jnp.int32, (N, N), 0)
    rj = lax.broadcasted_iota(jnp.int32, (N, N), 1)
    same_head = (ri // T) == (rj // T)
    strict = same_head & (ri > rj)
    incl = same_head & (ri >= rj)
    a_mat = jnp.where(strict, _dot_nt(k_in_s, b_out_s), 0.0)
    kk_mat = jnp.where(strict, _dot_nt(k_in_s, k_out_s), 0.0)
    rk_mat = jnp.where(incl, _dot_nt(r_in_s, k_out_s), 0.0)
    rb_mat = jnp.where(incl, _dot_nt(r_in_s, b_out_s), 0.0)

    eye = (ri == rj).astype(F32)
    a_bd = jnp.where((ri // RW_SUB) == (rj // RW_SUB), a_mat, 0.0)
    x = eye - a_bd
    p = _mm3(a_bd, a_bd)
    x = x + _mm3(x, p)
    p = _mm3(p, p)
    x = x + _mm3(x, p)
    p = _mm3(p, p)
    x = x + _mm3(x, p)
    size = RW_SUB
    while size < T:
        lower = ((ri // (2 * size)) == (rj // (2 * size))) & ((ri // size) > (rj // size))
        x = x - _mm3(_mm3(x, jnp.where(lower, a_mat, 0.0)), x)
        size *= 2

    state = state_ref[...]
    state_b = state.astype(BF16)
    d = partial(jnp.dot, preferred_element_type=F32)
    rhs = _dot_nt(k_in_s, state_b) + d(kk_mat.astype(BF16), v_s)
    u = _mm3(x, rhs)
    u_b = u.astype(BF16)
    o_s = _dot_nt(r_in_s, state_b) + d(rk_mat.astype(BF16), v_s) - d(rb_mat.astype(BF16), u_b)
    o = o_s[0:T]
    for h in range(1, RW_HEADS):
        o = o + o_s[h * T:(h + 1) * T]
    state_ref[...] = state * jnp.exp(b_last) + _dot_tn(v_s, k_end_s) - _dot_tn(u_b, b_end_s)

    head_mean = head_sum * (1.0 / RW_HD)
    mean = dotf(o, head_mean)
    var = dotf(jnp.square(o - mean), head_mean)
    o = (o - mean) * lax.rsqrt(var + RW_GN_EPS) * lng_ref[...] + lnb_ref[...]
    bonus = dotf(r * k2 * rk_ref[...], head_sum) * v
    o_ref[0] = ((o + bonus) * g).astype(o_ref.dtype)

    @pl.when(ci == nc - 1)
    def _():
        s_out_ref[0] = state_ref[...]
        shift_out_ref[0] = z[T - 1:T, :]


def _rwkv7_pallas(z, s0_bd, shift_prev, mu, w0, w2, a0, a2, g2, k_k, k_a, r_k, ln_g, ln_b, *, T, out_dtype=F32):
    B, L, C = z.shape
    assert C == RW_COLS and L % T == 0 and T % RW_SUB == 0
    W = BRANCH_W
    w2p = jnp.concatenate([w2, jnp.zeros_like(w2)], axis=0)
    a2p = jnp.concatenate([jnp.zeros_like(a2), a2], axis=0)
    vec = lambda t: t.reshape(1, -1).astype(F32)
    full = lambda shape: pl.BlockSpec(shape, lambda b, c: (0,) * len(shape))
    return pl.pallas_call(
        partial(_rwkv_body, T=T),
        grid=(B, L // T),
        in_specs=[pl.BlockSpec((1, T, C), lambda b, c: (b, c, 0)),
                  pl.BlockSpec((1, 1, C), lambda b, c: (b, 0, 0)),
                  pl.BlockSpec((1, W, W), lambda b, c: (b, 0, 0)),
                  full((1, C)), full((1, W)), full((128, W)), full((1, W)), full((128, W)), full((128, W)),
                  full((1, W)), full((1, W)), full((1, W)), full((1, W)), full((1, W))],
        out_specs=[pl.BlockSpec((1, T, W), lambda b, c: (b, c, 0)),
                   pl.BlockSpec((1, W, W), lambda b, c: (b, 0, 0)),
                   pl.BlockSpec((1, 1, C), lambda b, c: (b, 0, 0))],
        out_shape=[jax.ShapeDtypeStruct((B, L, W), out_dtype),
                   jax.ShapeDtypeStruct((B, W, W), F32),
                   jax.ShapeDtypeStruct((B, 1, C), F32)],
        scratch_shapes=[pltpu.VMEM((W, W), F32), pltpu.VMEM((1, C), F32)],
        compiler_params=pltpu.CompilerParams(dimension_semantics=("arbitrary", "arbitrary")),
        name="rwkv7_chunked",
    )(z, shift_prev.reshape(B, 1, C), s0_bd, vec(mu), vec(w0), w2p, vec(a0), a2p, g2, vec(k_k), vec(k_a),
      vec(r_k), vec(ln_g), vec(ln_b))


def _state_to_bd(s):
    B = s.shape[0]
    eye = jnp.eye(RW_HEADS, dtype=s.dtype)
    return (s[:, :, :, None, :] * eye[None, :, None, :, None]).reshape(B, BRANCH_W, BRANCH_W)


def _state_from_bd(s_bd):
    B = s_bd.shape[0]
    s5 = s_bd.reshape(B, RW_HEADS, RW_HD, RW_HEADS, RW_HD)
    return jnp.stack([s5[:, h, :, h, :] for h in range(RW_HEADS)], axis=1)


def _hgrn_body(z_ref, s0_ref, loglb_ref, log1mlb_ref, ng_ref, o_ref, s_out_ref, state_ref, *, T):
    ci = pl.program_id(1)
    nc = pl.num_programs(1)
    W = BRANCH_W

    @pl.when(ci == 0)
    def _():
        state_ref[...] = s0_ref[0]

    z = z_ref[0]
    zq, zf, zi, zg = z[:, 0:W], z[:, W:2 * W], z[:, 2 * W:3 * W], z[:, 3 * W:]
    log_sig = jnp.minimum(zf, 0.0) - jnp.log(1.0 + jnp.exp(-jnp.abs(zf)))
    a = loglb_ref[...]
    bv = log1mlb_ref[...] + log_sig
    m = jnp.maximum(a, bv)
    log_f = m + jnp.log(jnp.exp(a - m) + jnp.exp(bv - m))
    k = 1.0 - jnp.exp(log_f)
    q = zq * jax.nn.sigmoid(zq)
    dotf = partial(jnp.dot, preferred_element_type=F32, precision=HI)
    ti = lax.broadcasted_iota(jnp.int32, (T, T), 0)
    tj = lax.broadcasted_iota(jnp.int32, (T, T), 1)
    b = dotf((ti >= tj).astype(F32), log_f)
    li = lax.broadcasted_iota(jnp.int32, (W, W), 0) // HG_DK
    lj = lax.broadcasted_iota(jnp.int32, (W, W), 1) // HG_DK
    same_head = li == lj
    head_sum = same_head.astype(BF16)

    prods, first = [], []
    for s in range(T):
        r0 = (s // SUBLANES) * SUBLANES
        t_idx = r0 + lax.broadcasted_iota(jnp.int32, (T - r0, W), 0)
        e = jnp.exp(jnp.where(t_idx >= s, b[r0:, :] - b[s:s + 1, :], -jnp.inf))
        prods.append((q[r0:, :] * e * k[s:s + 1, :]).astype(BF16))
        first.append(r0)
    scores = jnp.dot(jnp.concatenate(prods, axis=0), head_sum, preferred_element_type=F32)
    groups = [jnp.zeros((SUBLANES, W), F32) for _ in range(T // SUBLANES)]
    at = 0
    for s in range(T):
        for gi in range(first[s] // SUBLANES, T // SUBLANES):
            lo = at + gi * SUBLANES - first[s]
            groups[gi] = groups[gi] + scores[lo:lo + SUBLANES, :] * zi[s:s + 1, :]
        at += T - first[s]
    o = jnp.concatenate(groups, axis=0)

    state = state_ref[...]
    o = o + _dot_nt((q * jnp.exp(b)).astype(BF16), state.astype(BF16))
    b_last = b[T - 1:T, :]
    k_end = (k * jnp.exp(b_last - b)).astype(BF16)
    upd = _dot_tn(zi.astype(BF16), k_end)
    state_ref[...] = state * jnp.exp(b_last) + jnp.where(same_head, upd, 0.0)

    ms = dotf(o * o, same_head.astype(F32) * (1.0 / HG_DK))
    o = o * lax.rsqrt(ms + EPS) * ng_ref[...]
    o_ref[0] = (o * (zg * jax.nn.sigmoid(zg))).astype(o_ref.dtype)

    @pl.when(ci == nc - 1)
    def _():
        s_out_ref[0] = state_ref[...]


def _hgrn2_pallas(z, s0_bd, lb, norm_g, *, T, out_dtype=F32):
    B, L, C = z.shape
    assert C == HG_COLS and L % T == 0
    W = BRANCH_W
    full = lambda shape: pl.BlockSpec(shape, lambda b, c: (0,) * len(shape))
    return pl.pallas_call(
        partial(_hgrn_body, T=T),
        grid=(B, L // T),
        in_specs=[pl.BlockSpec((1, T, C), lambda b, c: (b, c, 0)),
                  pl.BlockSpec((1, W, W), lambda b, c: (b, 0, 0)),
                  full((1, W)), full((1, W)), full((1, W))],
        out_specs=[pl.BlockSpec((1, T, W), lambda b, c: (b, c, 0)),
                   pl.BlockSpec((1, W, W), lambda b, c: (b, 0, 0))],
        out_shape=[jax.ShapeDtypeStruct((B, L, W), out_dtype),
                   jax.ShapeDtypeStruct((B, W, W), F32)],
        scratch_shapes=[pltpu.VMEM((W, W), F32)],
        compiler_params=pltpu.CompilerParams(dimension_semantics=("arbitrary", "arbitrary")),
        name="hgrn2_chunked",
    )(z, s0_bd, jnp.log(lb).reshape(1, W), jnp.log1p(-lb).reshape(1, W), jnp.tile(norm_g, HG_HEADS).reshape(1, W))


PROJ_TM = 256
PROJ_VMEM_LIMIT_V7X = 48 * 1024 * 1024
CONV_CARRY = 32
CONV_TC = 512


def _in_proj_body(x_ref, g_ref, w_ref, q_ref, k_ref, v_ref, kb_ref, vb_ref, hg_ref, cv_ref, rw_ref, gate_ref):
    x = x_ref[...]
    h = (x * lax.rsqrt(jnp.mean(x * x, axis=-1, keepdims=True) + EPS) * g_ref[...]).astype(BF16)
    W = BRANCH_W

    def proj(a, b):
        return jnp.dot(h, w_ref[:, a:b], preferred_element_type=F32)

    q_ref[...] = (proj(0, W) * SB_Q_SCALE).astype(BF16)
    k = proj(W, 2 * W)
    v = proj(2 * W, 3 * W)
    k_ref[...] = k
    v_ref[...] = v
    kb_ref[...] = k.astype(BF16)
    vb_ref[...] = v.astype(BF16)
    c = SB_COLS
    hg_ref[...] = proj(c, c + HG_COLS)
    c += HG_COLS
    cv_ref[...] = proj(c, c + CV_COLS)
    c += CV_COLS
    rw_ref[...] = proj(c, c + RW_COLS)
    c += RW_COLS
    for n in range(N_BRANCH):
        gate_ref[:, n * D_MODEL:(n + 1) * D_MODEL] = jax.nn.sigmoid(
            proj(c + n * D_MODEL, c + (n + 1) * D_MODEL)).astype(BF16)


def _in_proj(x2, norm_g, w_in_b):
    n = x2.shape[0]
    TM = min(PROJ_TM, n)
    assert n % TM == 0
    row = lambda w: pl.BlockSpec((TM, w), lambda i: (i, 0))
    shp = lambda w, dt: jax.ShapeDtypeStruct((n, w), dt)
    W = BRANCH_W
    return pl.pallas_call(
        _in_proj_body,
        grid=(n // TM,),
        in_specs=[row(D_MODEL), pl.BlockSpec((1, D_MODEL), lambda i: (0, 0)),
                  pl.BlockSpec((D_MODEL, IN_COLS), lambda i: (0, 0), pipeline_mode=pl.Buffered(1))],
        out_specs=[row(W), row(W), row(W), row(W), row(W), row(HG_COLS), row(CV_COLS), row(RW_COLS), row(GATE_COLS)],
        out_shape=[shp(W, BF16), shp(W, F32), shp(W, F32), shp(W, BF16), shp(W, BF16),
                   shp(HG_COLS, F32), shp(CV_COLS, F32), shp(RW_COLS, F32), shp(GATE_COLS, BF16)],
        compiler_params=pltpu.CompilerParams(dimension_semantics=("arbitrary",), vmem_limit_bytes=PROJ_VMEM_LIMIT_V7X),
        name="in_proj",
    )(x2, norm_g.reshape(1, -1), w_in_b)


def _merge_body(x_ref, sb_ref, hg_ref, cv_ref, rw_ref, gate_ref, wb_ref, wo_ref, o_ref):
    merged = None
    for n, br in enumerate((sb_ref, hg_ref, cv_ref, rw_ref)):
        t = jnp.dot(br[...].astype(BF16), wb_ref[n], preferred_element_type=F32)
        t = t * gate_ref[:, n * D_MODEL:(n + 1) * D_MODEL].astype(F32)
        merged = t if merged is None else merged + t
    o_ref[...] = x_ref[...] + jnp.dot(merged.astype(BF16), wo_ref[...], preferred_element_type=F32)


def _merge(x2, o_sb, o_hg, o_cv, o_rw, gates, wb_b, wo_b):
    n = x2.shape[0]
    TM = min(PROJ_TM, n)
    row = lambda w: pl.BlockSpec((TM, w), lambda i: (i, 0))
    return pl.pallas_call(
        _merge_body,
        grid=(n // TM,),
        in_specs=[row(D_MODEL), row(BRANCH_W), row(BRANCH_W), row(BRANCH_W), row(BRANCH_W), row(GATE_COLS),
                  pl.BlockSpec((N_BRANCH, BRANCH_W, D_MODEL), lambda i: (0, 0, 0)),
                  pl.BlockSpec((D_MODEL, D_MODEL), lambda i: (0, 0))],
        out_specs=row(D_MODEL),
        out_shape=jax.ShapeDtypeStruct((n, D_MODEL), F32),
        compiler_params=pltpu.CompilerParams(dimension_semantics=("arbitrary",), vmem_limit_bytes=PROJ_VMEM_LIMIT_V7X),
        name="branch_merge",
    )(x2, o_sb, o_hg, o_cv, o_rw, gates, wb_b, wo_b)


def _conv_body(z_ref, buf_ref, w_ref, b_ref, lg_ref, lb_ref, o_ref, buf_out_ref, hp_ref, *, TC):
    ci = pl.program_id(1)
    nc = pl.num_programs(1)

    @pl.when(ci == 0)
    def _():
        hp_ref[0:CONV_CARRY, :] = buf_ref[0]

    z = z_ref[0]
    hp_ref[CONV_CARRY:, :] = z[:, 0:CONV_CH] * jax.nn.sigmoid(z[:, CONV_CH:])
    first = CONV_CARRY - (CONV_WIDTH - 1)
    y = jnp.zeros((TC, CONV_CH), F32) + b_ref[...]
    for j in range(CONV_WIDTH):
        y = y + hp_ref[first + j:first + j + TC, :] * w_ref[j:j + 1, :]
    mu = jnp.mean(y, axis=-1, keepdims=True)
    d = y - mu
    var = jnp.mean(d * d, axis=-1, keepdims=True)
    y = d * lax.rsqrt(var + EPS) * lg_ref[...] + lb_ref[...]
    o_ref[0] = (y * jax.nn.sigmoid(y)).astype(o_ref.dtype)
    tail = hp_ref[TC:TC + CONV_CARRY, :]

    @pl.when(ci == nc - 1)
    def _():
        buf_out_ref[0] = tail

    hp_ref[0:CONV_CARRY, :] = tail


def _conformer_conv_pallas(z_cv, buf, w, b, ln_g, ln_b, *, TC, out_dtype=F32):
    B, L, _ = z_cv.shape
    assert L % TC == 0 and TC % 8 == 0 and TC >= CONV_CARRY
    C = CONV_CH
    pad = CONV_CARRY - (CONV_WIDTH - 1)
    buf_p = jnp.pad(buf.astype(F32), ((0, 0), (pad, 0), (0, 0)))
    w_p = jnp.pad(w.astype(F32), ((0, 32 - CONV_WIDTH), (0, 0)))
    full = lambda shape: pl.BlockSpec(shape, lambda b, c: (0,) * len(shape))
    y, buf_new = pl.pallas_call(
        partial(_conv_body, TC=TC),
        grid=(B, L // TC),
        in_specs=[pl.BlockSpec((1, TC, 2 * C), lambda b, c: (b, c, 0)),
                  pl.BlockSpec((1, CONV_CARRY, C), lambda b, c: (b, 0, 0)),
                  full((32, C)), full((1, C)), full((1, C)), full((1, C))],
        out_specs=[pl.BlockSpec((1, TC, C), lambda b, c: (b, c, 0)),
                   pl.BlockSpec((1, CONV_CARRY, C), lambda b, c: (b, 0, 0))],
        out_shape=[jax.ShapeDtypeStruct((B, L, C), out_dtype), jax.ShapeDtypeStruct((B, CONV_CARRY, C), F32)],
        scratch_shapes=[pltpu.VMEM((CONV_CARRY + TC, C), F32)],
        compiler_params=pltpu.CompilerParams(dimension_semantics=("arbitrary", "arbitrary")),
        name="conformer_conv",
    )(z_cv, buf_p, w_p, b.reshape(1, C), ln_g.reshape(1, C), ln_b.reshape(1, C))
    return y, buf_new[:, pad:, :]


PK_PAIRS = PK_HEADS * PK_TOPK
PEER_HEAD_GROUP = 4
PEER_TB = 128
SUBLANES = 8
LANES = 128
ROW_TILES = D_MODEL // LANES
PEER_VMEM_LIMIT_V7X = 52 * 1024 * 1024


def _topk_rows(scores, payloads, k):
    n = scores[0].shape[0]
    rows = lax.broadcasted_iota(jnp.int32, scores[0].shape, 0).astype(F32)
    scores = list(scores)
    vals = [[] for _ in scores]
    ids = [[] for _ in scores]
    for _ in range(k):
        for c, s in enumerate(scores):
            m = jnp.max(s, axis=0, keepdims=True)
            pos = jnp.min(jnp.where(s == m, rows, float(n)), axis=0, keepdims=True)
            sel = rows == pos
            ids[c].append(pos if payloads[c] is None
                          else jnp.max(jnp.where(sel, payloads[c], -1.0), axis=0, keepdims=True))
            vals[c].append(m)
            scores[c] = jnp.where(sel, -jnp.inf, s)
    return [(jnp.concatenate(v, axis=0), jnp.concatenate(i, axis=0)) for v, i in zip(vals, ids)]


def _dot3(ah, al, bh, bl, dims):
    d = partial(lax.dot_general, dimension_numbers=dims, preferred_element_type=F32)
    return d(ah, bh) + (d(ah, bl) + d(al, bh))


def _candidate_tables(tb):
    groups = [[0], [1], [2], [3, 4], [5, 6, 7], list(range(8, 16))]
    rows = []
    for grp in groups:
        part = [(a, b) for a in grp for b in range(PK_TOPK // (a + 1))]
        rows += part + [None] * ((-len(part)) % SUBLANES)
    n = len(rows)
    sel1 = [[1.0 if (r is not None and r[0] == j) else 0.0 for j in range(PK_TOPK)] for r in rows]
    sel2 = [[1.0 if (r is not None and r[1] == j) else 0.0 for j in range(PK_TOPK)] for r in rows]
    cpos = [[-1.0 if r is None else float(r[0] * PK_TOPK + r[1])] * tb for r in rows]
    return jnp.array(sel1, BF16), jnp.array(sel2, BF16), jnp.array(cpos, F32), n


def _pick_rows(sel, v):
    hi = v.astype(BF16)
    r1 = v - hi.astype(F32)
    mid = r1.astype(BF16)
    lo = (r1 - mid.astype(F32)).astype(BF16)
    d = partial(jnp.dot, preferred_element_type=F32)
    return (d(sel, hi) + d(sel, mid)) + d(sel, lo)


def _peer_route_body(x_ref, g_ref, wqh_ref, wql_ref, qg_ref, k1_ref, k2_ref, sel1_ref, sel2_ref, cpos_ref,
                     xn_ref, idx_ref, idxc_ref, gate_ref):
    x = x_ref[...]
    xn = x * lax.rsqrt(jnp.mean(x * x, axis=-1, keepdims=True) + EPS) * g_ref[...]
    xn_ref[...] = xn
    nn = (((1,), (0,)), ((), ()))
    nt = (((1,), (1,)), ((), ()))
    xh, xl = _split(xn)
    q = _dot3(xh, xl, wqh_ref[...], wql_ref[...], nn)
    half = PK_DQ // 2
    experts, gates = [], []
    for h0 in range(0, PK_HEADS, PEER_HEAD_GROUP):
        heads = range(h0, h0 + PEER_HEAD_GROUP)
        subs = []
        for h in heads:
            qh = q[:, h * PK_DQ:(h + 1) * PK_DQ]
            qh = qh * lax.rsqrt(jnp.mean(qh * qh, axis=-1, keepdims=True) + EPS) * qg_ref[...]
            qhh, qhl = _split(qh)
            k1h, k1l = _split(k1_ref[h])
            k2h, k2l = _split(k2_ref[h])
            subs.append(_dot3(k1h, k1l, qhh[:, :half], qhl[:, :half], nt))
            subs.append(_dot3(k2h, k2l, qhh[:, half:], qhl[:, half:], nt))
        tops = _topk_rows(subs, [None] * len(subs), PK_TOPK)
        cands = []
        for j in range(PEER_HEAD_GROUP):
            (v1, _), (v2, _) = tops[2 * j], tops[2 * j + 1]
            cands.append(jnp.where(cpos_ref[...] < 0.0, -jnp.inf,
                                   _pick_rows(sel1_ref[...], v1) + _pick_rows(sel2_ref[...], v2)))
        best = _topk_rows(cands, [cpos_ref[...]] * PEER_HEAD_GROUP, PK_TOPK)
        for j in range(PEER_HEAD_GROUP):
            i1, i2 = tops[2 * j][1], tops[2 * j + 1][1]
            top_s, pos = best[j]
            pa = jnp.floor(pos * (1.0 / PK_TOPK))
            pb = pos - pa * PK_TOPK
            e1 = jnp.zeros_like(pos)
            e2 = jnp.zeros_like(pos)
            for a in range(PK_TOPK):
                e1 = jnp.where(pa == a, i1[a:a + 1, :], e1)
                e2 = jnp.where(pb == a, i2[a:a + 1, :], e2)
            e = jnp.exp(top_s - jnp.max(top_s, axis=0, keepdims=True))
            gates.append(e / jnp.sum(e, axis=0, keepdims=True))
            experts.append(e1 * float(PK_NKEYS) + e2)
    expert = jnp.concatenate(experts, axis=0).astype(jnp.int32)
    row = (expert + 1) * HALF_TILES
    pair = lax.broadcasted_iota(jnp.int32, row.shape, 0)
    idx_ref[...] = row.T
    idxc_ref[...] = (row - (pair & HALF_TILES)).T
    gate_ref[...] = jnp.concatenate(gates, axis=0)


def _peer_route(x2, norm_g, wq, qn_g, k1, k2):
    n_tok = x2.shape[0]
    TB = PEER_TB
    assert n_tok % TB == 0
    nb = n_tok // TB
    full = lambda shape: pl.BlockSpec(shape, lambda i: (0,) * len(shape))
    wq_hi = wq.astype(BF16)
    wq_lo = (wq - wq_hi.astype(F32)).astype(BF16)
    sel1, sel2, cpos, n_cand = _candidate_tables(TB)
    return pl.pallas_call(
        _peer_route_body,
        grid=(nb,),
        in_specs=[pl.BlockSpec((TB, D_MODEL), lambda i: (i, 0)), full((1, D_MODEL)),
                  full((D_MODEL, PK_HEADS * PK_DQ)), full((D_MODEL, PK_HEADS * PK_DQ)), full((1, PK_DQ)),
                  full((PK_HEADS, PK_NKEYS, PK_DQ // 2)), full((PK_HEADS, PK_NKEYS, PK_DQ // 2)),
                  full((n_cand, PK_TOPK)), full((n_cand, PK_TOPK)), full((n_cand, TB))],
        out_specs=[pl.BlockSpec((TB, D_MODEL), lambda i: (i, 0)),
                   pl.BlockSpec((TB, PK_PAIRS), lambda i: (i, 0)),
                   pl.BlockSpec((TB, PK_PAIRS), lambda i: (i, 0)),
                   pl.BlockSpec((PK_PAIRS, TB), lambda i: (0, i))],
        out_shape=[jax.ShapeDtypeStruct((n_tok, D_MODEL), F32),
                   jax.ShapeDtypeStruct((n_tok, PK_PAIRS), jnp.int32),
                   jax.ShapeDtypeStruct((n_tok, PK_PAIRS), jnp.int32),
                   jax.ShapeDtypeStruct((PK_PAIRS, n_tok), F32)],
        compiler_params=pltpu.CompilerParams(dimension_semantics=("arbitrary",)),
        name="peer_route",
    )(x2, norm_g.reshape(1, -1), wq_hi, wq_lo, qn_g.reshape(1, -1), k1, k2, sel1, sel2, cpos)


HALF_TILES = ROW_TILES // 2


def _unpack_words(word):
    return pltpu.bitcast(word << 16, F32), pltpu.bitcast(word & jnp.int32(-65536), F32)


def _fold_couples(tiles, sub):
    m2 = (sub & 3) < 2
    lvl2 = []
    for a, b in ((tiles[0], tiles[2]), (tiles[1], tiles[3])):
        lvl2.append(jnp.where(m2, a, b) + jnp.where(m2, pltpu.roll(a, 6, axis=0), pltpu.roll(b, 2, axis=0)))
    m1 = (sub & 1) == 0
    a, b = lvl2
    return jnp.where(m1, a, b) + jnp.where(m1, pltpu.roll(a, 7, axis=0), pltpu.roll(b, 1, axis=0))


def _peer_act_body(idx_ref, x_ref, gate_ref, tab_ref, w_ref, part_ref):
    TB = x_ref.shape[0]
    sub = lax.broadcasted_iota(jnp.int32, (SUBLANES, LANES), 0)
    low = sub < HALF_TILES

    def token(t, c):
        x = x_ref[t]
        x_lo = jnp.concatenate([x[0:HALF_TILES]] * 2, axis=0)
        x_hi = jnp.concatenate([x[HALF_TILES:]] * 2, axis=0)
        start = t * PK_PAIRS
        for g in range(PK_PAIRS // SUBLANES):
            tiles = []
            for j in range(HALF_TILES):
                ra = pl.multiple_of(idx_ref[0, 0, start + g * SUBLANES + j], HALF_TILES)
                rb = pl.multiple_of(idx_ref[0, 0, start + g * SUBLANES + j + HALF_TILES], HALF_TILES)
                word = jnp.where(low, tab_ref[pl.ds(ra, SUBLANES), :], tab_ref[pl.ds(rb, SUBLANES), :])
                lo, hi = _unpack_words(word)
                tiles.append(lo * x_lo + hi * x_hi)
            part_ref[t, g * SUBLANES:(g + 1) * SUBLANES, :] = _fold_couples(tiles, sub)
        return c

    lax.fori_loop(0, TB, token, 0)
    lane = lax.broadcasted_iota(jnp.int32, (PK_PAIRS, TB), 1)
    s = jnp.zeros((PK_PAIRS, TB), F32)
    for t in range(TB):
        s = jnp.where(lane == t, jnp.sum(part_ref[t], axis=1, keepdims=True), s)
    w_ref[...] = gate_ref[...] * jax.nn.gelu(s)


PEER_MIX_ACCS = 4


def _peer_mix_body(idx_ref, w_ref, tab_ref, y_ref, wrep_ref):
    TB = y_ref.shape[0]
    w_all = w_ref[...]
    for t in range(TB):
        wrep_ref[t] = jnp.broadcast_to(w_all[:, t:t + 1], (PK_PAIRS, LANES))

    def token(t, c):
        start = t * PK_PAIRS
        acc_lo = [jnp.zeros((HALF_TILES, LANES), F32) for _ in range(PEER_MIX_ACCS)]
        acc_hi = [jnp.zeros((HALF_TILES, LANES), F32) for _ in range(PEER_MIX_ACCS)]
        for p in range(PK_PAIRS):
            row = pl.multiple_of(idx_ref[0, 0, start + p], HALF_TILES)
            lo, hi = _unpack_words(tab_ref[pl.ds(row, HALF_TILES), :])
            w = jnp.broadcast_to(wrep_ref[t, p:p + 1, :], (HALF_TILES, LANES))
            acc_lo[p % PEER_MIX_ACCS] = acc_lo[p % PEER_MIX_ACCS] + w * lo
            acc_hi[p % PEER_MIX_ACCS] = acc_hi[p % PEER_MIX_ACCS] + w * hi
        y_ref[t] = jnp.concatenate([(acc_lo[0] + acc_lo[1]) + (acc_lo[2] + acc_lo[3]),
                                    (acc_hi[0] + acc_hi[1]) + (acc_hi[2] + acc_hi[3])], axis=0)
        return c

    lax.fori_loop(0, TB, token, 0)


def _pack_table(tab):
    bits = lax.bitcast_convert_type(tab.astype(BF16), jnp.uint16).astype(jnp.uint32)
    bits = bits.reshape(PK_EXPERTS, 2, HALF_TILES, LANES)
    word = bits[:, 0] | (bits[:, 1] << 16)
    rows = lax.bitcast_convert_type(word, jnp.int32).reshape(PK_EXPERTS * HALF_TILES, LANES)
    return jnp.pad(rows, ((HALF_TILES, HALF_TILES), (0, 0)))


def _smem_spec(TB):
    return pl.BlockSpec((1, 1, PK_PAIRS * TB), lambda i: (i, 0, 0), memory_space=pltpu.SMEM)


def _table_spec():
    return pl.BlockSpec(((PK_EXPERTS + 2) * HALF_TILES, LANES), lambda i: (0, 0), pipeline_mode=pl.Buffered(1))


def _peer_act(idx, xn3, gate_t, tab_packed):
    n_tok = xn3.shape[0]
    TB = PEER_TB
    nb = n_tok // TB
    return pl.pallas_call(
        _peer_act_body,
        grid=(nb,),
        in_specs=[_smem_spec(TB),
                  pl.BlockSpec((TB, ROW_TILES, LANES), lambda i: (i, 0, 0)),
                  pl.BlockSpec((PK_PAIRS, TB), lambda i: (0, i)),
                  _table_spec()],
        out_specs=pl.BlockSpec((PK_PAIRS, TB), lambda i: (0, i)),
        out_shape=jax.ShapeDtypeStruct((PK_PAIRS, n_tok), F32),
        scratch_shapes=[pltpu.VMEM((TB, PK_PAIRS, LANES), F32)],
        compiler_params=pltpu.CompilerParams(dimension_semantics=("arbitrary",),
                                             vmem_limit_bytes=PEER_VMEM_LIMIT_V7X),
        name="peer_act",
    )(idx.reshape(nb, 1, TB * PK_PAIRS), xn3, gate_t, tab_packed)


def _peer_mix(idx, w_t, tab_packed):
    n_tok = w_t.shape[1]
    TB = PEER_TB
    nb = n_tok // TB
    return pl.pallas_call(
        _peer_mix_body,
        grid=(nb,),
        in_specs=[_smem_spec(TB),
                  pl.BlockSpec((PK_PAIRS, TB), lambda i: (0, i)),
                  _table_spec()],
        out_specs=pl.BlockSpec((TB, ROW_TILES, LANES), lambda i: (i, 0, 0)),
        out_shape=jax.ShapeDtypeStruct((n_tok, ROW_TILES, LANES), F32),
        scratch_shapes=[pltpu.VMEM((TB, PK_PAIRS, LANES), F32)],
        compiler_params=pltpu.CompilerParams(dimension_semantics=("arbitrary",),
                                             vmem_limit_bytes=PEER_VMEM_LIMIT_V7X),
        name="peer_mix",
    )(idx.reshape(nb, 1, TB * PK_PAIRS), w_t, tab_packed)


def _peer_ffn_pallas(x, norm_g, wq, qn_g, sub_keys, u_packed, v_packed):
    B, L, D = x.shape
    n_tok = B * L
    pad = (-n_tok) % PEER_TB
    x2 = jnp.pad(x.reshape(n_tok, D), ((0, pad), (0, 0)))
    xn, idx, idx_couple, gate_t = _peer_route(x2, norm_g, wq, qn_g, sub_keys[:, 0], sub_keys[:, 1])
    w_t = _peer_act(idx_couple, xn.reshape(-1, ROW_TILES, LANES), gate_t, u_packed)
    y = _peer_mix(idx, w_t, v_packed)
    return y.reshape(-1, D)[:n_tok].reshape(B, L, D)


def _trunk_layer(x, k_past, v_past, s_hg, buf_conv, s_rw, buf_shift, p):
    B_, L, _ = x.shape
    n = B_ * L
    x2 = x.reshape(n, D_MODEL)
    qb, k, v, kb, vb, z_hg, z_cv, z_rw, gates = _in_proj(x2, p['norm_mix_g'], p['w_in'])
    seq = lambda t: t.reshape(B_, L, t.shape[-1])
    if k_past is None:
        o_sb = _sb_attention(seq(qb), seq(kb), seq(vb), off=0, tq=SB_TQ, tk=SB_TK, out_dtype=F32)
    else:
        past = k_past.shape[1]
        pad = jnp.zeros((B_, (-(past + L)) % SB_TK, BRANCH_W), BF16)
        kc = jnp.concatenate([k_past.reshape(B_, past, BRANCH_W).astype(BF16), seq(kb), pad], axis=1)
        vc = jnp.concatenate([v_past.reshape(B_, past, BRANCH_W).astype(BF16), seq(vb), pad], axis=1)
        o_sb = _sb_attention(seq(qb), kc, vc, off=past, tq=L, tk=SB_TK, out_dtype=F32)
    o_hg, s_hg_bd = _hgrn2_pallas(seq(z_hg), _state_to_bd(jnp.swapaxes(s_hg.astype(F32), 2, 3)), p['lb'],
                                  p['hg_norm_g'], T=CHUNK if L % CHUNK == 0 else L)
    s_hg_new = jnp.swapaxes(_state_from_bd(s_hg_bd), 2, 3)
    o_cv, buf_conv_new = _conformer_conv_pallas(seq(z_cv), buf_conv, p['conv_w'], p['conv_b'], p['conv_ln_g'],
                                                p['conv_ln_b'], TC=CONV_TC if L % CONV_TC == 0 else L)
    o_rw, s_rw_bd, shift_new = _rwkv7_pallas(seq(z_rw), _state_to_bd(s_rw.astype(F32)), buf_shift, p['rw_mu'],
                                             p['rw_w0'], p['rw_w2'], p['rw_a0'], p['rw_a2'], p['rw_g2'], p['rw_k_k'],
                                             p['rw_k_a'], p['rw_r_k'], p['rw_ln_g'], p['rw_ln_b'],
                                             T=RW_CHUNK if L % RW_CHUNK == 0 else L)
    s_rw_new = _state_from_bd(s_rw_bd)
    buf_shift_new = shift_new[:, 0]
    flat = lambda t: t.reshape(n, BRANCH_W)
    x = _merge(x2, flat(o_sb), flat(o_hg), flat(o_cv), flat(o_rw), gates, p['w_branch'], p['w_out']).reshape(x.shape)
    x = x + _peer_ffn_pallas(x, p['norm_ffn_g'], p['peer_wq'], p['peer_qn_g'], p['peer_keys'],
                             p['peer_u'], p['peer_v'])
    heads = lambda t: t.reshape(B_, L, SB_HEADS, HEAD_DIM)
    return x, (heads(k), heads(v), s_hg_new, buf_conv_new, s_rw_new, buf_shift_new)


def kernel(x_prompt, x_sample, cache_sb_k, cache_sb_v, state_hgrn, state_conv, state_rwkv, state_shift,
           norm_mix_g, w_in, hg_lb_logits, hg_norm_g, conv_w, conv_b, conv_ln_g, conv_ln_b,
           rw_mu, rw_w0, rw_w2, rw_a0, rw_a2, rw_g2, rw_k_k, rw_k_a, rw_r_k, rw_ln_g, rw_ln_b,
           w_branch, w_out, norm_ffn_g, peer_wq, peer_qn_g, peer_keys, peer_u, peer_v, final_norm_g):
    lb_all = jnp.cumsum(jax.nn.softmax(hg_lb_logits.astype(F32), axis=0), axis=0)
    lb_all = lb_all - lb_all[0:1]
    Bp = x_prompt.shape[0]
    hg0 = jnp.zeros((Bp, HG_HEADS, HG_DK, HG_DV), F32)
    cv0 = jnp.zeros((Bp, CONV_WIDTH - 1, CONV_CH), x_prompt.dtype)
    rw0 = jnp.zeros((Bp, RW_HEADS, RW_HD, RW_HD), F32)
    sh0 = jnp.zeros((Bp, RW_COLS), x_prompt.dtype)
    xp, xs = x_prompt, x_sample
    outs_p, outs_s = [], []
    for l in range(DEPTH):
        lp = dict(norm_mix_g=norm_mix_g[l], w_in=w_in[l].astype(BF16), lb=lb_all[l], hg_norm_g=hg_norm_g[l],
                  conv_w=conv_w[l], conv_b=conv_b[l], conv_ln_g=conv_ln_g[l], conv_ln_b=conv_ln_b[l],
                  rw_mu=rw_mu[l], rw_w0=rw_w0[l], rw_w2=rw_w2[l], rw_a0=rw_a0[l], rw_a2=rw_a2[l],
                  rw_g2=rw_g2[l], rw_k_k=rw_k_k[l], rw_k_a=rw_k_a[l], rw_r_k=rw_r_k[l],
                  rw_ln_g=rw_ln_g[l], rw_ln_b=rw_ln_b[l], w_branch=w_branch[l].astype(BF16),
                  w_out=w_out[l].astype(BF16),
                  norm_ffn_g=norm_ffn_g[l], peer_wq=peer_wq[l], peer_qn_g=peer_qn_g[l],
                  peer_keys=peer_keys[l], peer_u=_pack_table(peer_u[l]), peer_v=_pack_table(peer_v[l]))
        xp, st_p = _trunk_layer(xp, None, None, hg0, cv0, rw0, sh0, lp)
        xs, st_s = _trunk_layer(xs, cache_sb_k[l], cache_sb_v[l], state_hgrn[l], state_conv[l],
                                state_rwkv[l], state_shift[l], lp)
        outs_p.append(st_p)
        outs_s.append(st_s)

    def stk(outs, i):
        return jnp.stack([o[i] for o in outs], axis=0)

    y_prompt = _rmsnorm_pallas(xp, final_norm_g)
    y_sample = _rmsnorm_pallas(xs, final_norm_g)
    return (y_prompt, y_sample,
            stk(outs_p, 0), stk(outs_p, 1), stk(outs_p, 2), stk(outs_p, 3), stk(outs_p, 4), stk(outs_p, 5),
            stk(outs_s, 0), stk(outs_s, 1), stk(outs_s, 2), stk(outs_s, 3), stk(outs_s, 4), stk(outs_s, 5))
```

```python
from functools import partial

import jax
import jax.numpy as jnp
from jax import lax
from jax.experimental import pallas as pl
from jax.experimental.pallas import tpu as pltpu

D_MODEL = 1024
DEPTH = 2
CHUNK = 64
HEAD_DIM = 64
N_BRANCH = 4
BRANCH_W = D_MODEL // 4
SB_HEADS = BRANCH_W // HEAD_DIM
HG_HEADS = 4
HG_DK = BRANCH_W // HG_HEADS
HG_DV = BRANCH_W // HG_HEADS
CONV_CH = BRANCH_W
CONV_WIDTH = 31
RW_HEADS = 4
RW_HD = BRANCH_W // RW_HEADS
RW_DECAY_LORA = 64
RW_AAA_LORA = 64
RW_GATE_LORA = 128
SB_COLS = 3 * BRANCH_W
HG_COLS = 4 * BRANCH_W
CV_COLS = 2 * CONV_CH
RW_COLS = 3 * BRANCH_W + RW_DECAY_LORA + RW_AAA_LORA + RW_GATE_LORA
GATE_COLS = N_BRANCH * D_MODEL
IN_COLS = SB_COLS + HG_COLS + CV_COLS + RW_COLS + GATE_COLS
PK_HEADS = 8
PK_NKEYS = 128
PK_EXPERTS = PK_NKEYS * PK_NKEYS
PK_DQ = 256
PK_TOPK = 16
EPS = 1e-6
RW_GN_EPS = 64e-5
SB_TQ = 256
SB_TK = 256
RW_CHUNK = 64
F32 = jnp.float32
BF16 = jnp.bfloat16


def _rmsnorm_body(x_ref, g_ref, o_ref):
    x = x_ref[...]
    y = x * lax.rsqrt(jnp.mean(x * x, axis=-1, keepdims=True) + EPS)
    o_ref[...] = y * g_ref[...]


def _rmsnorm_pallas(x, g, rows=512):
    shape = x.shape
    x2 = x.reshape(-1, shape[-1])
    n, d = x2.shape
    rows = min(rows, n)
    out = pl.pallas_call(
        _rmsnorm_body,
        grid=(n // rows,),
        in_specs=[pl.BlockSpec((rows, d), lambda i: (i, 0)),
                  pl.BlockSpec((1, d), lambda i: (0, 0))],
        out_specs=pl.BlockSpec((rows, d), lambda i: (i, 0)),
        out_shape=jax.ShapeDtypeStruct((n, d), x.dtype),
        name="final_rmsnorm",
    )(x2, g.reshape(1, d))
    return out.reshape(shape)


VMEM_LIMIT_V7X = 48 * 1024 * 1024
SB_Q_SCALE = -(HEAD_DIM ** -0.5) * 1.4426950408889634


def _sb_attn_body(q_ref, k_ref, v_ref, o_ref, acc_ref, carry_ref, *, tq, tk, off, n_diag):
    qi = pl.program_id(1)
    q0 = off + qi * tq
    n_full = q0 // tk
    acc_ref[...] = jnp.zeros_like(acc_ref)
    carry_ref[...] = jnp.zeros_like(carry_ref)
    q_all = q_ref[0]
    q_heads = [q_all[:, h * HEAD_DIM:(h + 1) * HEAD_DIM] for h in range(SB_HEADS)]
    jj = lax.broadcasted_iota(jnp.int32, (tk, tk), 0)
    ss = lax.broadcasted_iota(jnp.int32, (tk, tk), 1)
    later_mat = (jj > ss).astype(BF16)

    def block(kb, masked):
        start = pl.multiple_of(kb * tk, tk)
        k_blk = k_ref[0, pl.ds(start, tk), :]
        v_blk = v_ref[0, pl.ds(start, tk), :]
        if masked:
            key_pos = start + lax.broadcasted_iota(jnp.int32, (tq, tk), 1)
            q_pos = q0 + lax.broadcasted_iota(jnp.int32, (tq, tk), 0)
            mask = key_pos < q_pos
        dims = (((1,), (1,)), ((), ()))
        nls = [lax.dot_general(q_heads[h], k_blk[:, h * HEAD_DIM:(h + 1) * HEAD_DIM], dims,
                               preferred_element_type=F32) for h in range(SB_HEADS)]
        lsms, laters = [], []
        for h in range(SB_HEADS):
            nl = nls[h]
            neg_abs = pltpu.bitcast(pltpu.bitcast(nl, jnp.int32) | jnp.int32(-2 ** 31), F32)
            lsm = jnp.minimum(nl, 0.0) - jnp.log2(1.0 + jnp.exp2(neg_abs))
            if masked:
                lsm = jnp.where(mask, lsm, 0.0)
            lsms.append(lsm)
            laters.append(jnp.dot(lsm.astype(BF16), later_mat, preferred_element_type=F32))
        outs = []
        for h in range(SB_HEADS):
            carry = carry_ref[h]
            expo = (lsms[h] - nls[h]) + laters[h] + jnp.concatenate([carry] * (tk // 128), axis=1)
            w = jnp.exp2(expo)
            if masked:
                w = jnp.where(mask, w, 0.0)
            outs.append(jnp.dot(w.astype(BF16), v_blk[:, h * HEAD_DIM:(h + 1) * HEAD_DIM], preferred_element_type=F32))
            row = laters[h][:, 0:1] + lsms[h][:, 0:1]
            carry_ref[h] = carry + jnp.broadcast_to(row, carry.shape)
        acc_ref[...] += jnp.concatenate(outs, axis=1)

    for d in range(n_diag - 1, -1, -1):
        block(n_full + d, True)

    def full_step(i, c):
        block(n_full - 1 - i, False)
        return c

    lax.fori_loop(0, n_full, full_step, 0)
    o_ref[0] = acc_ref[...].astype(o_ref.dtype)


def _sb_attention(q, k, v, *, off, tq, tk, out_dtype=BF16):
    B, Lq, W = q.shape
    Lk = k.shape[1]
    assert W == BRANCH_W and Lq % tq == 0 and Lk % tk == 0 and tk % 128 == 0
    nq = Lq // tq
    assert nq == 1 or (tq % tk == 0 and off % tk == 0)
    n_diag = -(-((off % tk) + tq - 1) // tk)
    assert (off + Lq - 1 + tk - 1) // tk <= Lk // tk
    body = partial(_sb_attn_body, tq=tq, tk=tk, off=off, n_diag=n_diag)
    return pl.pallas_call(
        body,
        grid=(B, nq),
        in_specs=[pl.BlockSpec((1, tq, W), lambda b, i: (b, i, 0)),
                  pl.BlockSpec((1, Lk, W), lambda b, i: (b, 0, 0)),
                  pl.BlockSpec((1, Lk, W), lambda b, i: (b, 0, 0))],
        out_specs=pl.BlockSpec((1, tq, W), lambda b, i: (b, i, 0)),
        out_shape=jax.ShapeDtypeStruct((B, Lq, W), out_dtype),
        scratch_shapes=[pltpu.VMEM((tq, W), F32), pltpu.VMEM((SB_HEADS, tq, 128), F32)],
        compiler_params=pltpu.CompilerParams(dimension_semantics=("arbitrary", "arbitrary"),
                                             vmem_limit_bytes=VMEM_LIMIT_V7X),
        name="sb_attention",
    )(q, k, v)


RW_SUB = 16


def _split(x):
    hi = x.astype(BF16)
    lo = (x - hi.astype(F32)).astype(BF16)
    return hi, lo


def _mm3(a, b):
    ah, al = _split(a)
    bh, bl = _split(b)
    d = partial(jnp.dot, preferred_element_type=F32)
    return d(ah, bh) + (d(ah, bl) + d(al, bh))


def _sum01(x, m01):
    xh, xl = _split(x)
    return jnp.dot(xh, m01, preferred_element_type=F32) + jnp.dot(xl, m01, preferred_element_type=F32)


def _sum01_left(m01, x):
    xh, xl = _split(x)
    return jnp.dot(m01, xh, preferred_element_type=F32) + jnp.dot(m01, xl, preferred_element_type=F32)


def _dot_nt(a, b):
    return lax.dot_general(a, b, (((1,), (1,)), ((), ())), preferred_element_type=F32)


def _dot_tn(a, b):
    return lax.dot_general(a, b, (((0,), (0,)), ((), ())), preferred_element_type=F32)


def _rwkv_body(z_ref, shift_ref, s0_ref, mu_ref, w0_ref, w2_ref, a0_ref, a2_ref, g2_ref, kk_ref, ka_ref,
               rk_ref, lng_ref, lnb_ref, o_ref, s_out_ref, shift_out_ref, state_ref, prev_ref, *, T):
    ci = pl.program_id(1)
    nc = pl.num_programs(1)
    W = BRANCH_W
    N = RW_HEADS * T

    @pl.when(ci == 0)
    def _():
        state_ref[...] = s0_ref[0]
        prev_ref[...] = shift_ref[0]

    z = z_ref[0]
    row = lax.broadcasted_iota(jnp.int32, (T, RW_COLS), 0)
    z_prev = jnp.where(row == 0, jnp.broadcast_to(prev_ref[...], (T, RW_COLS)), pltpu.roll(z, 1, axis=0))
    prev_ref[...] = z[T - 1:T, :]
    zs = z + (z_prev - z) * mu_ref[...]
    r = zs[:, 0:W]
    k = zs[:, W:2 * W]
    v = zs[:, 2 * W:3 * W]
    xwa = zs[:, 3 * W:3 * W + 128]
    xg = zs[:, 3 * W + 128:]
    w_log = -jax.nn.softplus(-(w0_ref[...] + _mm3(jnp.tanh(xwa), w2_ref[...]))) - 0.5
    logdec = -jnp.exp(w_log)
    a = jax.nn.sigmoid(a0_ref[...] + _mm3(xwa, a2_ref[...]))
    g = _mm3(jax.nn.sigmoid(xg), g2_ref[...])
    li = lax.broadcasted_iota(jnp.int32, (W, W), 0) // RW_HD
    lj = lax.broadcasted_iota(jnp.int32, (W, W), 1) // RW_HD
    head_sum = (li == lj).astype(BF16)
    kk = k * kk_ref[...]
    kk = kk * lax.rsqrt(_sum01(kk * kk, head_sum) + 1e-12)
    k2 = k * (1.0 + (a - 1.0) * ka_ref[...])
    beta = kk * a
    ti = lax.broadcasted_iota(jnp.int32, (T, T), 0)
    tj = lax.broadcasted_iota(jnp.int32, (T, T), 1)
    b = _sum01_left((ti >= tj).astype(BF16), logdec)
    b_last = b[T - 1:T, :]
    k_in = kk * jnp.exp(b - logdec)
    r_in = r * jnp.exp(b)
    inv_p = jnp.exp(-b)
    k_out = k2 * inv_p
    b_out = beta * inv_p
    to_end = jnp.exp(b_last - b)
    k_end = k2 * to_end
    b_end = beta * to_end

    sh = lax.broadcasted_iota(jnp.int32, (N, W), 0) // T
    sl = lax.broadcasted_iota(jnp.int32, (N, W), 1) // RW_HD
    own = sh == sl

    def stack(x, masked):
        xs = jnp.concatenate([x] * RW_HEADS, axis=0)
        return jnp.where(own, xs, 0.0) if masked else xs

    k_in_s = stack(k_in, True).astype(BF16)
    r_in_s = stack(r_in, True).astype(BF16)
    k_out_s = stack(k_out, False).astype(BF16)
    b_out_s = stack(b_out, False).astype(BF16)
    v_s = stack(v, True).astype(BF16)
    k_end_s = stack(k_end, True).astype(BF16)
    b_end_s = stack(b_end, True).astype(BF16)

    ri = lax.broadcasted_iota(jnp.int32, (N, N), 0)
    rj = lax.broadcasted_iota(jnp.int32, (N, N), 1)
    same_head = (ri // T) == (rj // T)
    strict = same_head & (ri > rj)
    incl = same_head & (ri >= rj)
    a_mat = jnp.where(strict, _dot_nt(k_in_s, b_out_s), 0.0)
    kk_mat = jnp.where(strict, _dot_nt(k_in_s, k_out_s), 0.0)
    rk_mat = jnp.where(incl, _dot_nt(r_in_s, k_out_s), 0.0)
    rb_mat = jnp.where(incl, _dot_nt(r_in_s, b_out_s), 0.0)

    eye = (ri == rj).astype(F32)
    a_bd = jnp.where((ri // RW_SUB) == (rj // RW_SUB), a_mat, 0.0)
    x = eye - a_bd
    p = _mm3(a_bd, a_bd)
    x = x + _mm3(x, p)
    p = _mm3(p, p)
    x = x + _mm3(x, p)
    p = _mm3(p, p)
    x = x + _mm3(x, p)
    size = RW_SUB
    while size < T:
        lower = ((ri // (2 * size)) == (rj // (2 * size))) & ((ri // size) > (rj // size))
        x = x - _mm3(_mm3(x, jnp.where(lower, a_mat, 0.0)), x)
        size *= 2

    state = state_ref[...]
    state_b = state.astype(BF16)
    d = partial(jnp.dot, preferred_element_type=F32)
    rhs = _dot_nt(k_in_s, state_b) + d(kk_mat.astype(BF16), v_s)
    u = _mm3(x, rhs)
    u_b = u.astype(BF16)
    o_s = _dot_nt(r_in_s, state_b) + d(rk_mat.astype(BF16), v_s) - d(rb_mat.astype(BF16), u_b)
    o = o_s[0:T]
    for h in range(1, RW_HEADS):
        o = o + o_s[h * T:(h + 1) * T]
    state_ref[...] = state * jnp.exp(b_last) + _dot_tn(v_s, k_end_s) - _dot_tn(u_b, b_end_s)

    mean = _sum01(o, head_sum) * (1.0 / RW_HD)
    var = _sum01(jnp.square(o - mean), head_sum) * (1.0 / RW_HD)
    o = (o - mean) * lax.rsqrt(var + RW_GN_EPS) * lng_ref[...] + lnb_ref[...]
    bonus = _sum01(r * k2 * rk_ref[...], head_sum) * v
    o_ref[0] = ((o + bonus) * g).astype(o_ref.dtype)

    @pl.when(ci == nc - 1)
    def _():
        s_out_ref[0] = state_ref[...]
        shift_out_ref[0] = z[T - 1:T, :]


def _rwkv7_pallas(z, s0_bd, shift_prev, mu, w0, w2, a0, a2, g2, k_k, k_a, r_k, ln_g, ln_b, *, T, out_dtype=F32):
    B, L, C = z.shape
    assert C == RW_COLS and L % T == 0 and T % RW_SUB == 0
    W = BRANCH_W
    w2p = jnp.concatenate([w2, jnp.zeros_like(w2)], axis=0)
    a2p = jnp.concatenate([jnp.zeros_like(a2), a2], axis=0)
    vec = lambda t: t.reshape(1, -1).astype(F32)
    full = lambda shape: pl.BlockSpec(shape, lambda b, c: (0,) * len(shape))
    return pl.pallas_call(
        partial(_rwkv_body, T=T),
        grid=(B, L // T),
        in_specs=[pl.BlockSpec((1, T, C), lambda b, c: (b, c, 0)),
                  pl.BlockSpec((1, 1, C), lambda b, c: (b, 0, 0)),
                  pl.BlockSpec((1, W, W), lambda b, c: (b, 0, 0)),
                  full((1, C)), full((1, W)), full((128, W)), full((1, W)), full((128, W)), full((128, W)),
                  full((1, W)), full((1, W)), full((1, W)), full((1, W)), full((1, W))],
        out_specs=[pl.BlockSpec((1, T, W), lambda b, c: (b, c, 0)),
                   pl.BlockSpec((1, W, W), lambda b, c: (b, 0, 0)),
                   pl.BlockSpec((1, 1, C), lambda b, c: (b, 0, 0))],
        out_shape=[jax.ShapeDtypeStruct((B, L, W), out_dtype),
                   jax.ShapeDtypeStruct((B, W, W), F32),
                   jax.ShapeDtypeStruct((B, 1, C), F32)],
        scratch_shapes=[pltpu.VMEM((W, W), F32), pltpu.VMEM((1, C), F32)],
        compiler_params=pltpu.CompilerParams(dimension_semantics=("arbitrary", "arbitrary")),
        name="rwkv7_chunked",
    )(z, shift_prev.reshape(B, 1, C), s0_bd, vec(mu), vec(w0), w2p, vec(a0), a2p, g2, vec(k_k), vec(k_a),
      vec(r_k), vec(ln_g), vec(ln_b))


def _state_to_bd(s):
    B = s.shape[0]
    eye = jnp.eye(RW_HEADS, dtype=s.dtype)
    return (s[:, :, :, None, :] * eye[None, :, None, :, None]).reshape(B, BRANCH_W, BRANCH_W)


def _state_from_bd(s_bd):
    B = s_bd.shape[0]
    s5 = s_bd.reshape(B, RW_HEADS, RW_HD, RW_HEADS, RW_HD)
    return jnp.stack([s5[:, h, :, h, :] for h in range(RW_HEADS)], axis=1)


def _hgrn_body(z_ref, s0_ref, loglb_ref, log1mlb_ref, ng_ref, o_ref, s_out_ref, state_ref, *, T):
    ci = pl.program_id(1)
    nc = pl.num_programs(1)
    W = BRANCH_W

    @pl.when(ci == 0)
    def _():
        state_ref[...] = s0_ref[0]

    z = z_ref[0]
    zq, zf, zi, zg = z[:, 0:W], z[:, W:2 * W], z[:, 2 * W:3 * W], z[:, 3 * W:]
    log_sig = jnp.minimum(zf, 0.0) - jnp.log(1.0 + jnp.exp(-jnp.abs(zf)))
    a = loglb_ref[...]
    bv = log1mlb_ref[...] + log_sig
    m = jnp.maximum(a, bv)
    log_f = m + jnp.log(jnp.exp(a - m) + jnp.exp(bv - m))
    k = 1.0 - jnp.exp(log_f)
    q = zq * jax.nn.sigmoid(zq)
    ti = lax.broadcasted_iota(jnp.int32, (T, T), 0)
    tj = lax.broadcasted_iota(jnp.int32, (T, T), 1)
    b = _sum01_left((ti >= tj).astype(BF16), log_f)
    li = lax.broadcasted_iota(jnp.int32, (W, W), 0) // HG_DK
    lj = lax.broadcasted_iota(jnp.int32, (W, W), 1) // HG_DK
    same_head = li == lj
    head_sum = same_head.astype(BF16)

    prods, first = [], []
    for s in range(T):
        r0 = (s // SUBLANES) * SUBLANES
        t_idx = r0 + lax.broadcasted_iota(jnp.int32, (T - r0, W), 0)
        e = jnp.exp(jnp.where(t_idx >= s, b[r0:, :] - b[s:s + 1, :], -jnp.inf))
        prods.append((q[r0:, :] * e * k[s:s + 1, :]).astype(BF16))
        first.append(r0)
    scores = jnp.dot(jnp.concatenate(prods, axis=0), head_sum, preferred_element_type=F32)
    groups = [jnp.zeros((SUBLANES, W), F32) for _ in range(T // SUBLANES)]
    at = 0
    for s in range(T):
        for gi in range(first[s] // SUBLANES, T // SUBLANES):
            lo = at + gi * SUBLANES - first[s]
            groups[gi] = groups[gi] + scores[lo:lo + SUBLANES, :] * zi[s:s + 1, :]
        at += T - first[s]
    o = jnp.concatenate(groups, axis=0)

    state = state_ref[...]
    o = o + _dot_nt((q * jnp.exp(b)).astype(BF16), state.astype(BF16))
    b_last = b[T - 1:T, :]
    k_end = (k * jnp.exp(b_last - b)).astype(BF16)
    upd = _dot_tn(zi.astype(BF16), k_end)
    state_ref[...] = state * jnp.exp(b_last) + jnp.where(same_head, upd, 0.0)

    ms = _sum01(o * o, head_sum) * (1.0 / HG_DK)
    o = o * lax.rsqrt(ms + EPS) * ng_ref[...]
    o_ref[0] = (o * (zg * jax.nn.sigmoid(zg))).astype(o_ref.dtype)

    @pl.when(ci == nc - 1)
    def _():
        s_out_ref[0] = state_ref[...]


def _hgrn2_pallas(z, s0_bd, lb, norm_g, *, T, out_dtype=F32):
    B, L, C = z.shape
    assert C == HG_COLS and L % T == 0
    W = BRANCH_W
    full = lambda shape: pl.BlockSpec(shape, lambda b, c: (0,) * len(shape))
    return pl.pallas_call(
        partial(_hgrn_body, T=T),
        grid=(B, L // T),
        in_specs=[pl.BlockSpec((1, T, C), lambda b, c: (b, c, 0)),
                  pl.BlockSpec((1, W, W), lambda b, c: (b, 0, 0)),
                  full((1, W)), full((1, W)), full((1, W))],
        out_specs=[pl.BlockSpec((1, T, W), lambda b, c: (b, c, 0)),
                   pl.BlockSpec((1, W, W), lambda b, c: (b, 0, 0))],
        out_shape=[jax.ShapeDtypeStruct((B, L, W), out_dtype),
                   jax.ShapeDtypeStruct((B, W, W), F32)],
        scratch_shapes=[pltpu.VMEM((W, W), F32)],
        compiler_params=pltpu.CompilerParams(dimension_semantics=("arbitrary", "arbitrary")),
        name="hgrn2_chunked",
    )(z, s0_bd, jnp.log(lb).reshape(1, W), jnp.log1p(-lb).reshape(1, W), jnp.tile(norm_g, HG_HEADS).reshape(1, W))


PROJ_TM = 256
PROJ_VMEM_LIMIT_V7X = 48 * 1024 * 1024
CONV_CARRY = 32
CONV_TC = 512


def _in_proj_body(x_ref, g_ref, w_ref, q_ref, k_ref, v_ref, kb_ref, vb_ref, hg_ref, cv_ref, rw_ref, gate_ref):
    x = x_ref[...]
    h = (x * lax.rsqrt(jnp.mean(x * x, axis=-1, keepdims=True) + EPS) * g_ref[...]).astype(BF16)
    W = BRANCH_W

    def proj(a, b):
        return jnp.dot(h, w_ref[:, a:b], preferred_element_type=F32)

    q_ref[...] = (proj(0, W) * SB_Q_SCALE).astype(BF16)
    k = proj(W, 2 * W)
    v = proj(2 * W, 3 * W)
    k_ref[...] = k
    v_ref[...] = v
    kb_ref[...] = k.astype(BF16)
    vb_ref[...] = v.astype(BF16)
    c = SB_COLS
    hg_ref[...] = proj(c, c + HG_COLS)
    c += HG_COLS
    cv_ref[...] = proj(c, c + CV_COLS)
    c += CV_COLS
    rw_ref[...] = proj(c, c + RW_COLS)
    c += RW_COLS
    for n in range(N_BRANCH):
        gate_ref[:, n * D_MODEL:(n + 1) * D_MODEL] = jax.nn.sigmoid(
            proj(c + n * D_MODEL, c + (n + 1) * D_MODEL)).astype(BF16)


def _in_proj(x2, norm_g, w_in_b):
    n = x2.shape[0]
    TM = min(PROJ_TM, n)
    assert n % TM == 0
    row = lambda w: pl.BlockSpec((TM, w), lambda i: (i, 0))
    shp = lambda w, dt: jax.ShapeDtypeStruct((n, w), dt)
    W = BRANCH_W
    return pl.pallas_call(
        _in_proj_body,
        grid=(n // TM,),
        in_specs=[row(D_MODEL), pl.BlockSpec((1, D_MODEL), lambda i: (0, 0)),
                  pl.BlockSpec((D_MODEL, IN_COLS), lambda i: (0, 0), pipeline_mode=pl.Buffered(1))],
        out_specs=[row(W), row(W), row(W), row(W), row(W), row(HG_COLS), row(CV_COLS), row(RW_COLS), row(GATE_COLS)],
        out_shape=[shp(W, BF16), shp(W, F32), shp(W, F32), shp(W, BF16), shp(W, BF16),
                   shp(HG_COLS, F32), shp(CV_COLS, F32), shp(RW_COLS, F32), shp(GATE_COLS, BF16)],
        compiler_params=pltpu.CompilerParams(dimension_semantics=("arbitrary",), vmem_limit_bytes=PROJ_VMEM_LIMIT_V7X),
        name="in_proj",
    )(x2, norm_g.reshape(1, -1), w_in_b)


def _merge_body(x_ref, sb_ref, hg_ref, cv_ref, rw_ref, gate_ref, wb_ref, wo_ref, o_ref):
    merged = None
    for n, br in enumerate((sb_ref, hg_ref, cv_ref, rw_ref)):
        t = jnp.dot(br[...].astype(BF16), wb_ref[n], preferred_element_type=F32)
        t = t * gate_ref[:, n * D_MODEL:(n + 1) * D_MODEL].astype(F32)
        merged = t if merged is None else merged + t
    o_ref[...] = x_ref[...] + jnp.dot(merged.astype(BF16), wo_ref[...], preferred_element_type=F32)


def _merge(x2, o_sb, o_hg, o_cv, o_rw, gates, wb_b, wo_b):
    n = x2.shape[0]
    TM = min(PROJ_TM, n)
    row = lambda w: pl.BlockSpec((TM, w), lambda i: (i, 0))
    return pl.pallas_call(
        _merge_body,
        grid=(n // TM,),
        in_specs=[row(D_MODEL), row(BRANCH_W), row(BRANCH_W), row(BRANCH_W), row(BRANCH_W), row(GATE_COLS),
                  pl.BlockSpec((N_BRANCH, BRANCH_W, D_MODEL), lambda i: (0, 0, 0)),
                  pl.BlockSpec((D_MODEL, D_MODEL), lambda i: (0, 0))],
        out_specs=row(D_MODEL),
        out_shape=jax.ShapeDtypeStruct((n, D_MODEL), F32),
        compiler_params=pltpu.CompilerParams(dimension_semantics=("arbitrary",), vmem_limit_bytes=PROJ_VMEM_LIMIT_V7X),
        name="branch_merge",
    )(x2, o_sb, o_hg, o_cv, o_rw, gates, wb_b, wo_b)


def _conv_body(z_ref, buf_ref, w_ref, b_ref, lg_ref, lb_ref, o_ref, buf_out_ref, hp_ref, *, TC):
    ci = pl.program_id(1)
    nc = pl.num_programs(1)

    @pl.when(ci == 0)
    def _():
        hp_ref[0:CONV_CARRY, :] = buf_ref[0]

    z = z_ref[0]
    hp_ref[CONV_CARRY:, :] = z[:, 0:CONV_CH] * jax.nn.sigmoid(z[:, CONV_CH:])
    first = CONV_CARRY - (CONV_WIDTH - 1)
    y = jnp.zeros((TC, CONV_CH), F32) + b_ref[...]
    for j in range(CONV_WIDTH):
        y = y + hp_ref[first + j:first + j + TC, :] * w_ref[j:j + 1, :]
    mu = jnp.mean(y, axis=-1, keepdims=True)
    d = y - mu
    var = jnp.mean(d * d, axis=-1, keepdims=True)
    y = d * lax.rsqrt(var + EPS) * lg_ref[...] + lb_ref[...]
    o_ref[0] = (y * jax.nn.sigmoid(y)).astype(o_ref.dtype)
    tail = hp_ref[TC:TC + CONV_CARRY, :]

    @pl.when(ci == nc - 1)
    def _():
        buf_out_ref[0] = tail

    hp_ref[0:CONV_CARRY, :] = tail


def _conformer_conv_pallas(z_cv, buf, w, b, ln_g, ln_b, *, TC, out_dtype=F32):
    B, L, _ = z_cv.shape
    assert L % TC == 0 and TC % 8 == 0 and TC >= CONV_CARRY
    C = CONV_CH
    pad = CONV_CARRY - (CONV_WIDTH - 1)
    buf_p = jnp.pad(buf.astype(F32), ((0, 0), (pad, 0), (0, 0)))
    w_p = jnp.pad(w.astype(F32), ((0, 32 - CONV_WIDTH), (0, 0)))
    full = lambda shape: pl.BlockSpec(shape, lambda b, c: (0,) * len(shape))
    y, buf_new = pl.pallas_call(
        partial(_conv_body, TC=TC),
        grid=(B, L // TC),
        in_specs=[pl.BlockSpec((1, TC, 2 * C), lambda b, c: (b, c, 0)),
                  pl.BlockSpec((1, CONV_CARRY, C), lambda b, c: (b, 0, 0)),
                  full((32, C)), full((1, C)), full((1, C)), full((1, C))],
        out_specs=[pl.BlockSpec((1, TC, C), lambda b, c: (b, c, 0)),
                   pl.BlockSpec((1, CONV_CARRY, C), lambda b, c: (b, 0, 0))],
        out_shape=[jax.ShapeDtypeStruct((B, L, C), out_dtype), jax.ShapeDtypeStruct((B, CONV_CARRY, C), F32)],
        scratch_shapes=[pltpu.VMEM((CONV_CARRY + TC, C), F32)],
        compiler_params=pltpu.CompilerParams(dimension_semantics=("arbitrary", "arbitrary")),
        name="conformer_conv",
    )(z_cv, buf_p, w_p, b.reshape(1, C), ln_g.reshape(1, C), ln_b.reshape(1, C))
    return y, buf_new[:, pad:, :]


PK_PAIRS = PK_HEADS * PK_TOPK
PEER_HEAD_GROUP = 4
PEER_TB = 128
SUBLANES = 8
LANES = 128
ROW_TILES = D_MODEL // LANES
PEER_VMEM_LIMIT_V7X = 52 * 1024 * 1024


def _topk_rows(scores, payloads, k):
    n = scores[0].shape[0]
    rows = lax.broadcasted_iota(jnp.int32, scores[0].shape, 0).astype(F32)
    scores = list(scores)
    vals = [[] for _ in scores]
    ids = [[] for _ in scores]
    for _ in range(k):
        for c, s in enumerate(scores):
            m = jnp.max(s, axis=0, keepdims=True)
            pos = jnp.min(jnp.where(s == m, rows, float(n)), axis=0, keepdims=True)
            sel = rows == pos
            ids[c].append(pos if payloads[c] is None
                          else jnp.max(jnp.where(sel, payloads[c], -1.0), axis=0, keepdims=True))
            vals[c].append(m)
            scores[c] = jnp.where(sel, -jnp.inf, s)
    return [(jnp.concatenate(v, axis=0), jnp.concatenate(i, axis=0)) for v, i in zip(vals, ids)]


def _dot3(ah, al, bh, bl, dims):
    d = partial(lax.dot_general, dimension_numbers=dims, preferred_element_type=F32)
    return d(ah, bh) + (d(ah, bl) + d(al, bh))


def _candidate_tables(tb):
    groups = [[0], [1], [2], [3, 4], [5, 6, 7], list(range(8, 16))]
    rows = []
    for grp in groups:
        part = [(a, b) for a in grp for b in range(PK_TOPK // (a + 1))]
        rows += part + [None] * ((-len(part)) % SUBLANES)
    n = len(rows)
    sel1 = [[1.0 if (r is not None and r[0] == j) else 0.0 for j in range(PK_TOPK)] for r in rows]
    sel2 = [[1.0 if (r is not None and r[1] == j) else 0.0 for j in range(PK_TOPK)] for r in rows]
    cpos = [[-1.0 if r is None else float(r[0] * PK_TOPK + r[1])] * tb for r in rows]
    return jnp.array(sel1, BF16), jnp.array(sel2, BF16), jnp.array(cpos, F32), n


def _pick_rows(sel, v):
    hi = v.astype(BF16)
    r1 = v - hi.astype(F32)
    mid = r1.astype(BF16)
    lo = (r1 - mid.astype(F32)).astype(BF16)
    d = partial(jnp.dot, preferred_element_type=F32)
    return (d(sel, hi) + d(sel, mid)) + d(sel, lo)


def _peer_route_body(x_ref, g_ref, wqh_ref, wql_ref, qg_ref, k1_ref, k2_ref, sel1_ref, sel2_ref, cpos_ref,
                     xn_ref, idx_ref, idxc_ref, gate_ref):
    x = x_ref[...]
    xn = x * lax.rsqrt(jnp.mean(x * x, axis=-1, keepdims=True) + EPS) * g_ref[...]
    xn_ref[...] = xn
    nn = (((1,), (0,)), ((), ()))
    nt = (((1,), (1,)), ((), ()))
    xh, xl = _split(xn)
    q = _dot3(xh, xl, wqh_ref[...], wql_ref[...], nn)
    half = PK_DQ // 2
    experts, gates = [], []
    for h0 in range(0, PK_HEADS, PEER_HEAD_GROUP):
        heads = range(h0, h0 + PEER_HEAD_GROUP)
        subs = []
        for h in heads:
            qh = q[:, h * PK_DQ:(h + 1) * PK_DQ]
            qh = qh * lax.rsqrt(jnp.mean(qh * qh, axis=-1, keepdims=True) + EPS) * qg_ref[...]
            qhh, qhl = _split(qh)
            k1h, k1l = _split(k1_ref[h])
            k2h, k2l = _split(k2_ref[h])
            subs.append(_dot3(k1h, k1l, qhh[:, :half], qhl[:, :half], nt))
            subs.append(_dot3(k2h, k2l, qhh[:, half:], qhl[:, half:], nt))
        tops = _topk_rows(subs, [None] * len(subs), PK_TOPK)
        cands = []
        for j in range(PEER_HEAD_GROUP):
            (v1, _), (v2, _) = tops[2 * j], tops[2 * j + 1]
            cands.append(jnp.where(cpos_ref[...] < 0.0, -jnp.inf,
                                   _pick_rows(sel1_ref[...], v1) + _pick_rows(sel2_ref[...], v2)))
        best = _topk_rows(cands, [cpos_ref[...]] * PEER_HEAD_GROUP, PK_TOPK)
        for j in range(PEER_HEAD_GROUP):
            i1, i2 = tops[2 * j][1], tops[2 * j + 1][1]
            top_s, pos = best[j]
            pa = jnp.floor(pos * (1.0 / PK_TOPK))
            pb = pos - pa * PK_TOPK
            e1 = jnp.zeros_like(pos)
            e2 = jnp.zeros_like(pos)
            for a in range(PK_TOPK):
                e1 = jnp.where(pa == a, i1[a:a + 1, :], e1)
                e2 = jnp.where(pb == a, i2[a:a + 1, :], e2)
            e = jnp.exp(top_s - jnp.max(top_s, axis=0, keepdims=True))
            gates.append(e / jnp.sum(e, axis=0, keepdims=True))
            experts.append(e1 * float(PK_NKEYS) + e2)
    expert = jnp.concatenate(experts, axis=0).astype(jnp.int32)
    row = (expert + 1) * HALF_TILES
    pair = lax.broadcasted_iota(jnp.int32, row.shape, 0)
    idx_ref[...] = row.T
    idxc_ref[...] = (row - (pair & HALF_TILES)).T
    gate_ref[...] = jnp.concatenate(gates, axis=0)


def _peer_route(x2, norm_g, wq, qn_g, k1, k2):
    n_tok = x2.shape[0]
    TB = PEER_TB
    assert n_tok % TB == 0
    nb = n_tok // TB
    full = lambda shape: pl.BlockSpec(shape, lambda i: (0,) * len(shape))
    wq_hi = wq.astype(BF16)
    wq_lo = (wq - wq_hi.astype(F32)).astype(BF16)
    sel1, sel2, cpos, n_cand = _candidate_tables(TB)
    return pl.pallas_call(
        _peer_route_body,
        grid=(nb,),
        in_specs=[pl.BlockSpec((TB, D_MODEL), lambda i: (i, 0)), full((1, D_MODEL)),
                  full((D_MODEL, PK_HEADS * PK_DQ)), full((D_MODEL, PK_HEADS * PK_DQ)), full((1, PK_DQ)),
                  full((PK_HEADS, PK_NKEYS, PK_DQ // 2)), full((PK_HEADS, PK_NKEYS, PK_DQ // 2)),
                  full((n_cand, PK_TOPK)), full((n_cand, PK_TOPK)), full((n_cand, TB))],
        out_specs=[pl.BlockSpec((TB, D_MODEL), lambda i: (i, 0)),
                   pl.BlockSpec((TB, PK_PAIRS), lambda i: (i, 0)),
                   pl.BlockSpec((TB, PK_PAIRS), lambda i: (i, 0)),
                   pl.BlockSpec((PK_PAIRS, TB), lambda i: (0, i))],
        out_shape=[jax.ShapeDtypeStruct((n_tok, D_MODEL), F32),
                   jax.ShapeDtypeStruct((n_tok, PK_PAIRS), jnp.int32),
                   jax.ShapeDtypeStruct((n_tok, PK_PAIRS), jnp.int32),
                   jax.ShapeDtypeStruct((PK_PAIRS, n_tok), F32)],
        compiler_params=pltpu.CompilerParams(dimension_semantics=("arbitrary",)),
        name="peer_route",
    )(x2, norm_g.reshape(1, -1), wq_hi, wq_lo, qn_g.reshape(1, -1), k1, k2, sel1, sel2, cpos)


HALF_TILES = ROW_TILES // 2


def _unpack_words(word):
    return pltpu.bitcast(word << 16, F32), pltpu.bitcast(word & jnp.int32(-65536), F32)


def _fold_couples(tiles, sub):
    m2 = (sub & 3) < 2
    lvl2 = []
    for a, b in ((tiles[0], tiles[2]), (tiles[1], tiles[3])):
        lvl2.append(jnp.where(m2, a, b) + jnp.where(m2, pltpu.roll(a, 6, axis=0), pltpu.roll(b, 2, axis=0)))
    m1 = (sub & 1) == 0
    a, b = lvl2
    return jnp.where(m1, a, b) + jnp.where(m1, pltpu.roll(a, 7, axis=0), pltpu.roll(b, 1, axis=0))


def _peer_act_body(idx_ref, x_ref, gate_ref, tab_ref, w_ref, part_ref):
    TB = x_ref.shape[0]
    sub = lax.broadcasted_iota(jnp.int32, (SUBLANES, LANES), 0)
    low = sub < HALF_TILES

    def token(t, c):
        x = x_ref[t]
        x_lo = jnp.concatenate([x[0:HALF_TILES]] * 2, axis=0)
        x_hi = jnp.concatenate([x[HALF_TILES:]] * 2, axis=0)
        start = t * PK_PAIRS
        for g in range(PK_PAIRS // SUBLANES):
            tiles = []
            for j in range(HALF_TILES):
                ra = pl.multiple_of(idx_ref[0, 0, start + g * SUBLANES + j], HALF_TILES)
                rb = pl.multiple_of(idx_ref[0, 0, start + g * SUBLANES + j + HALF_TILES], HALF_TILES)
                word = jnp.where(low, tab_ref[pl.ds(ra, SUBLANES), :], tab_ref[pl.ds(rb, SUBLANES), :])
                lo, hi = _unpack_words(word)
                tiles.append(lo * x_lo + hi * x_hi)
            part_ref[t, g * SUBLANES:(g + 1) * SUBLANES, :] = _fold_couples(tiles, sub)
        return c

    lax.fori_loop(0, TB, token, 0)
    lane = lax.broadcasted_iota(jnp.int32, (PK_PAIRS, TB), 1)
    s = jnp.zeros((PK_PAIRS, TB), F32)
    for t in range(TB):
        s = jnp.where(lane == t, jnp.sum(part_ref[t], axis=1, keepdims=True), s)
    w_ref[...] = gate_ref[...] * jax.nn.gelu(s)


PEER_MIX_ACCS = 4


def _peer_mix_body(idx_ref, w_ref, tab_ref, y_ref, wrep_ref):
    TB = y_ref.shape[0]
    w_all = w_ref[...]
    for t in range(TB):
        wrep_ref[t] = jnp.broadcast_to(w_all[:, t:t + 1], (PK_PAIRS, LANES))

    def token(t, c):
        start = t * PK_PAIRS
        acc_lo = [jnp.zeros((HALF_TILES, LANES), F32) for _ in range(PEER_MIX_ACCS)]
        acc_hi = [jnp.zeros((HALF_TILES, LANES), F32) for _ in range(PEER_MIX_ACCS)]
        for p in range(PK_PAIRS):
            row = pl.multiple_of(idx_ref[0, 0, start + p], HALF_TILES)
            lo, hi = _unpack_words(tab_ref[pl.ds(row, HALF_TILES), :])
            w = jnp.broadcast_to(wrep_ref[t, p:p + 1, :], (HALF_TILES, LANES))
            acc_lo[p % PEER_MIX_ACCS] = acc_lo[p % PEER_MIX_ACCS] + w * lo
            acc_hi[p % PEER_MIX_ACCS] = acc_hi[p % PEER_MIX_ACCS] + w * hi
        y_ref[t] = jnp.concatenate([(acc_lo[0] + acc_lo[1]) + (acc_lo[2] + acc_lo[3]),
                                    (acc_hi[0] + acc_hi[1]) + (acc_hi[2] + acc_hi[3])], axis=0)
        return c

    lax.fori_loop(0, TB, token, 0)


def _pack_table(tab):
    bits = lax.bitcast_convert_type(tab.astype(BF16), jnp.uint16).astype(jnp.uint32)
    bits = bits.reshape(PK_EXPERTS, 2, HALF_TILES, LANES)
    word = bits[:, 0] | (bits[:, 1] << 16)
    rows = lax.bitcast_convert_type(word, jnp.int32).reshape(PK_EXPERTS * HALF_TILES, LANES)
    return jnp.pad(rows, ((HALF_TILES, HALF_TILES), (0, 0)))


def _smem_spec(TB):
    return pl.BlockSpec((1, 1, PK_PAIRS * TB), lambda i: (i, 0, 0), memory_space=pltpu.SMEM)


def _table_spec():
    return pl.BlockSpec(((PK_EXPERTS + 2) * HALF_TILES, LANES), lambda i: (0, 0), pipeline_mode=pl.Buffered(1))


def _peer_act(idx, xn3, gate_t, tab_packed):
    n_tok = xn3.shape[0]
    TB = PEER_TB
    nb = n_tok // TB
    return pl.pallas_call(
        _peer_act_body,
        grid=(nb,),
        in_specs=[_smem_spec(TB),
                  pl.BlockSpec((TB, ROW_TILES, LANES), lambda i: (i, 0, 0)),
                  pl.BlockSpec((PK_PAIRS, TB), lambda i: (0, i)),
                  _table_spec()],
        out_specs=pl.BlockSpec((PK_PAIRS, TB), lambda i: (0, i)),
        out_shape=jax.ShapeDtypeStruct((PK_PAIRS, n_tok), F32),
        scratch_shapes=[pltpu.VMEM((TB, PK_PAIRS, LANES), F32)],
        compiler_params=pltpu.CompilerParams(dimension_semantics=("arbitrary",),
                                             vmem_limit_bytes=PEER_VMEM_LIMIT_V7X),
        name="peer_act",
    )(idx.reshape(nb, 1, TB * PK_PAIRS), xn3, gate_t, tab_packed)


def _peer_mix(idx, w_t, tab_packed):
    n_tok = w_t.shape[1]
    TB = PEER_TB
    nb = n_tok // TB
    return pl.pallas_call(
        _peer_mix_body,
        grid=(nb,),
        in_specs=[_smem_spec(TB),
                  pl.BlockSpec((PK_PAIRS, TB), lambda i: (0, i)),
                  _table_spec()],
        out_specs=pl.BlockSpec((TB, ROW_TILES, LANES), lambda i: (i, 0, 0)),
        out_shape=jax.ShapeDtypeStruct((n_tok, ROW_TILES, LANES), F32),
        scratch_shapes=[pltpu.VMEM((TB, PK_PAIRS, LANES), F32)],
        compiler_params=pltpu.CompilerParams(dimension_semantics=("arbitrary",),
                                             vmem_limit_bytes=PEER_VMEM_LIMIT_V7X),
        name="peer_mix",
    )(idx.reshape(nb, 1, TB * PK_PAIRS), w_t, tab_packed)


def _peer_ffn_pallas(x, norm_g, wq, qn_g, sub_keys, u_packed, v_packed):
    B, L, D = x.shape
    n_tok = B * L
    pad = (-n_tok) % PEER_TB
    x2 = jnp.pad(x.reshape(n_tok, D), ((0, pad), (0, 0)))
    xn, idx, idx_couple, gate_t = _peer_route(x2, norm_g, wq, qn_g, sub_keys[:, 0], sub_keys[:, 1])
    w_t = _peer_act(idx_couple, xn.reshape(-1, ROW_TILES, LANES), gate_t, u_packed)
    y = _peer_mix(idx, w_t, v_packed)
    return y.reshape(-1, D)[:n_tok].reshape(B, L, D)


def _trunk_layer(x, k_past, v_past, s_hg, buf_conv, s_rw, buf_shift, p):
    B_, L, _ = x.shape
    n = B_ * L
    x2 = x.reshape(n, D_MODEL)
    qb, k, v, kb, vb, z_hg, z_cv, z_rw, gates = _in_proj(x2, p['norm_mix_g'], p['w_in'])
    seq = lambda t: t.reshape(B_, L, t.shape[-1])
    if k_past is None:
        o_sb = _sb_attention(seq(qb), seq(kb), seq(vb), off=0, tq=SB_TQ, tk=SB_TK, out_dtype=F32)
    else:
        past = k_past.shape[1]
        pad = jnp.zeros((B_, (-(past + L)) % SB_TK, BRANCH_W), BF16)
        kc = jnp.concatenate([k_past.reshape(B_, past, BRANCH_W).astype(BF16), seq(kb), pad], axis=1)
        vc = jnp.concatenate([v_past.reshape(B_, past, BRANCH_W).astype(BF16), seq(vb), pad], axis=1)
        o_sb = _sb_attention(seq(qb), kc, vc, off=past, tq=L, tk=SB_TK, out_dtype=F32)
    o_hg, s_hg_bd = _hgrn2_pallas(seq(z_hg), _state_to_bd(jnp.swapaxes(s_hg.astype(F32), 2, 3)), p['lb'],
                                  p['hg_norm_g'], T=CHUNK if L % CHUNK == 0 else L)
    s_hg_new = jnp.swapaxes(_state_from_bd(s_hg_bd), 2, 3)
    o_cv, buf_conv_new = _conformer_conv_pallas(seq(z_cv), buf_conv, p['conv_w'], p['conv_b'], p['conv_ln_g'],
                                                p['conv_ln_b'], TC=CONV_TC if L % CONV_TC == 0 else L)
    o_rw, s_rw_bd, shift_new = _rwkv7_pallas(seq(z_rw), _state_to_bd(s_rw.astype(F32)), buf_shift, p['rw_mu'],
                                             p['rw_w0'], p['rw_w2'], p['rw_a0'], p['rw_a2'], p['rw_g2'], p['rw_k_k'],
                                             p['rw_k_a'], p['rw_r_k'], p['rw_ln_g'], p['rw_ln_b'],
                                             T=RW_CHUNK if L % RW_CHUNK == 0 else L)
    s_rw_new = _state_from_bd(s_rw_bd)
    buf_shift_new = shift_new[:, 0]
    flat = lambda t: t.reshape(n, BRANCH_W)
    x = _merge(x2, flat(o_sb), flat(o_hg), flat(o_cv), flat(o_rw), gates, p['w_branch'], p['w_out']).reshape(x.shape)
    x = x + _peer_ffn_pallas(x, p['norm_ffn_g'], p['peer_wq'], p['peer_qn_g'], p['peer_keys'],
                             p['peer_u'], p['peer_v'])
    heads = lambda t: t.reshape(B_, L, SB_HEADS, HEAD_DIM)
    return x, (heads(k), heads(v), s_hg_new, buf_conv_new, s_rw_new, buf_shift_new)


def kernel(x_prompt, x_sample, cache_sb_k, cache_sb_v, state_hgrn, state_conv, state_rwkv, state_shift,
           norm_mix_g, w_in, hg_lb_logits, hg_norm_g, conv_w, conv_b, conv_ln_g, conv_ln_b,
           rw_mu, rw_w0, rw_w2, rw_a0, rw_a2, rw_g2, rw_k_k, rw_k_a, rw_r_k, rw_ln_g, rw_ln_b,
           w_branch, w_out, norm_ffn_g, peer_wq, peer_qn_g, peer_keys, peer_u, peer_v, final_norm_g):
    lb_all = jnp.cumsum(jax.nn.softmax(hg_lb_logits.astype(F32), axis=0), axis=0)
    lb_all = lb_all - lb_all[0:1]
    Bp = x_prompt.shape[0]
    hg0 = jnp.zeros((Bp, HG_HEADS, HG_DK, HG_DV), F32)
    cv0 = jnp.zeros((Bp, CONV_WIDTH - 1, CONV_CH), x_prompt.dtype)
    rw0 = jnp.zeros((Bp, RW_HEADS, RW_HD, RW_HD), F32)
    sh0 = jnp.zeros((Bp, RW_COLS), x_prompt.dtype)
    xp, xs = x_prompt, x_sample
    outs_p, outs_s = [], []
    for l in range(DEPTH):
        lp = dict(norm_mix_g=norm_mix_g[l], w_in=w_in[l].astype(BF16), lb=lb_all[l], hg_norm_g=hg_norm_g[l],
                  conv_w=conv_w[l], conv_b=conv_b[l], conv_ln_g=conv_ln_g[l], conv_ln_b=conv_ln_b[l],
                  rw_mu=rw_mu[l], rw_w0=rw_w0[l], rw_w2=rw_w2[l], rw_a0=rw_a0[l], rw_a2=rw_a2[l],
                  rw_g2=rw_g2[l], rw_k_k=rw_k_k[l], rw_k_a=rw_k_a[l], rw_r_k=rw_r_k[l],
                  rw_ln_g=rw_ln_g[l], rw_ln_b=rw_ln_b[l], w_branch=w_branch[l].astype(BF16),
                  w_out=w_out[l].astype(BF16),
                  norm_ffn_g=norm_ffn_g[l], peer_wq=peer_wq[l], peer_qn_g=peer_qn_g[l],
                  peer_keys=peer_keys[l], peer_u=_pack_table(peer_u[l]), peer_v=_pack_table(peer_v[l]))
        xp, st_p = _trunk_layer(xp, None, None, hg0, cv0, rw0, sh0, lp)
        xs, st_s = _trunk_layer(xs, cache_sb_k[l], cache_sb_v[l], state_hgrn[l], state_conv[l],
                                state_rwkv[l], state_shift[l], lp)
        outs_p.append(st_p)
        outs_s.append(st_s)

    def stk(outs, i):
        return jnp.stack([o[i] for o in outs], axis=0)

    y_prompt = _rmsnorm_pallas(xp, final_norm_g)
    y_sample = _rmsnorm_pallas(xs, final_norm_g)
    return (y_prompt, y_sample,
            stk(outs_p, 0), stk(outs_p, 1), stk(outs_p, 2), stk(outs_p, 3), stk(outs_p, 4), stk(outs_p, 5),
            stk(outs_s, 0), stk(outs_s, 1), stk(outs_s, 2), stk(outs_s, 3), stk(outs_s, 4), stk(outs_s, 5))
```

```python
from functools import partial

import jax
import jax.numpy as jnp
from jax import lax
from jax.experimental import pallas as pl
from jax.experimental.pallas import tpu as pltpu

D_MODEL = 1024
DEPTH = 2
CHUNK = 64
HEAD_DIM = 64
N_BRANCH = 4
BRANCH_W = D_MODEL // 4
SB_HEADS = BRANCH_W // HEAD_DIM
HG_HEADS = 4
HG_DK = BRANCH_W // HG_HEADS
HG_DV = BRANCH_W // HG_HEADS
CONV_CH = BRANCH_W
CONV_WIDTH = 31
RW_HEADS = 4
RW_HD = BRANCH_W // RW_HEADS
RW_DECAY_LORA = 64
RW_AAA_LORA = 64
RW_GATE_LORA = 128
SB_COLS = 3 * BRANCH_W
HG_COLS = 4 * BRANCH_W
CV_COLS = 2 * CONV_CH
RW_COLS = 3 * BRANCH_W + RW_DECAY_LORA + RW_AAA_LORA + RW_GATE_LORA
GATE_COLS = N_BRANCH * D_MODEL
IN_COLS = SB_COLS + HG_COLS + CV_COLS + RW_COLS + GATE_COLS
PK_HEADS = 8
PK_NKEYS = 128
PK_EXPERTS = PK_NKEYS * PK_NKEYS
PK_DQ = 256
PK_TOPK = 16
EPS = 1e-6
RW_GN_EPS = 64e-5
SB_TQ = 256
SB_TK = 256
RW_CHUNK = 64
F32 = jnp.float32
BF16 = jnp.bfloat16
SUBLANES = 8
LANES = 128


def _rmsnorm_body(x_ref, g_ref, o_ref):
    x = x_ref[...]
    y = x * lax.rsqrt(jnp.mean(x * x, axis=-1, keepdims=True) + EPS)
    o_ref[...] = y * g_ref[...]


def _rmsnorm_pallas(x, g, rows=512):
    shape = x.shape
    x2 = x.reshape(-1, shape[-1])
    n, d = x2.shape
    rows = min(rows, n)
    out = pl.pallas_call(
        _rmsnorm_body,
        grid=(n // rows,),
        in_specs=[pl.BlockSpec((rows, d), lambda i: (i, 0)),
                  pl.BlockSpec((1, d), lambda i: (0, 0))],
        out_specs=pl.BlockSpec((rows, d), lambda i: (i, 0)),
        out_shape=jax.ShapeDtypeStruct((n, d), x.dtype),
        name="final_rmsnorm",
    )(x2, g.reshape(1, d))
    return out.reshape(shape)


VMEM_LIMIT_V7X = 48 * 1024 * 1024
SB_Q_SCALE = -(HEAD_DIM ** -0.5) * 1.4426950408889634


def _sb_attn_body(q_ref, k_ref, v_ref, o_ref, acc_ref, carry_ref, *, tq, tk, off, n_diag):
    qi = pl.program_id(1)
    q0 = off + qi * tq
    n_full = q0 // tk
    acc_ref[...] = jnp.zeros_like(acc_ref)
    carry_ref[...] = jnp.zeros_like(carry_ref)
    q_all = q_ref[0]
    lane = lax.broadcasted_iota(jnp.int32, (tq, LANES), 1)
    own = [(lane // HEAD_DIM) == (h % 2) for h in range(SB_HEADS)]
    q_heads = [jnp.where(own[h], q_all[:, (h // 2) * LANES:(h // 2 + 1) * LANES], jnp.zeros((), BF16))
               for h in range(SB_HEADS)]
    jj = lax.broadcasted_iota(jnp.int32, (tk, tk), 0)
    ss = lax.broadcasted_iota(jnp.int32, (tk, tk), 1)
    later_mat = (jj > ss).astype(BF16)

    def block(kb, masked):
        start = pl.multiple_of(kb * tk, tk)
        k_blk = k_ref[0, pl.ds(start, tk), :]
        v_blk = v_ref[0, pl.ds(start, tk), :]
        if masked:
            key_pos = start + lax.broadcasted_iota(jnp.int32, (tq, tk), 1)
            q_pos = q0 + lax.broadcasted_iota(jnp.int32, (tq, tk), 0)
            mask = key_pos < q_pos
        dims = (((1,), (1,)), ((), ()))
        nls = [lax.dot_general(q_heads[h], k_blk[:, (h // 2) * LANES:(h // 2 + 1) * LANES], dims,
                               preferred_element_type=F32) for h in range(SB_HEADS)]
        lsms, laters = [], []
        for h in range(SB_HEADS):
            nl = nls[h]
            neg_abs = pltpu.bitcast(pltpu.bitcast(nl, jnp.int32) | jnp.int32(-2 ** 31), F32)
            lsm = jnp.minimum(nl, 0.0) - jnp.log2(1.0 + jnp.exp2(neg_abs))
            if masked:
                lsm = jnp.where(mask, lsm, 0.0)
            lsms.append(lsm)
            laters.append(jnp.dot(lsm.astype(BF16), later_mat, preferred_element_type=F32))
        for h in range(SB_HEADS):
            carry = carry_ref[h]
            expo = (lsms[h] - nls[h]) + laters[h] + jnp.concatenate([carry] * (tk // LANES), axis=1)
            w = jnp.exp2(expo)
            if masked:
                w = jnp.where(mask, w, 0.0)
            acc_ref[h] += jnp.dot(w.astype(BF16), v_blk[:, (h // 2) * LANES:(h // 2 + 1) * LANES],
                                  preferred_element_type=F32)
            row = laters[h][:, 0:1] + lsms[h][:, 0:1]
            carry_ref[h] = carry + jnp.broadcast_to(row, carry.shape)

    for d in range(n_diag - 1, -1, -1):
        block(n_full + d, True)

    def full_step(i, c):
        block(n_full - 1 - i, False)
        return c

    lax.fori_loop(0, n_full, full_step, 0)
    o_ref[0] = jnp.concatenate([jnp.where(own[2 * p], acc_ref[2 * p], acc_ref[2 * p + 1])
                                for p in range(SB_HEADS // 2)], axis=1).astype(o_ref.dtype)


def _sb_attention(q, k, v, *, off, tq, tk, out_dtype=BF16):
    B, Lq, W = q.shape
    Lk = k.shape[1]
    assert W == BRANCH_W and Lq % tq == 0 and Lk % tk == 0 and tk % 128 == 0
    nq = Lq // tq
    assert nq == 1 or (tq % tk == 0 and off % tk == 0)
    n_diag = -(-((off % tk) + tq - 1) // tk)
    assert (off + Lq - 1 + tk - 1) // tk <= Lk // tk
    body = partial(_sb_attn_body, tq=tq, tk=tk, off=off, n_diag=n_diag)
    return pl.pallas_call(
        body,
        grid=(B, nq),
        in_specs=[pl.BlockSpec((1, tq, W), lambda b, i: (b, i, 0)),
                  pl.BlockSpec((1, Lk, W), lambda b, i: (b, 0, 0)),
                  pl.BlockSpec((1, Lk, W), lambda b, i: (b, 0, 0))],
        out_specs=pl.BlockSpec((1, tq, W), lambda b, i: (b, i, 0)),
        out_shape=jax.ShapeDtypeStruct((B, Lq, W), out_dtype),
        scratch_shapes=[pltpu.VMEM((SB_HEADS, tq, LANES), F32), pltpu.VMEM((SB_HEADS, tq, LANES), F32)],
        compiler_params=pltpu.CompilerParams(dimension_semantics=("arbitrary", "arbitrary"),
                                             vmem_limit_bytes=VMEM_LIMIT_V7X),
        name="sb_attention",
    )(q, k, v)


RW_SUB = 16


def _split(x):
    hi = x.astype(BF16)
    lo = (x - hi.astype(F32)).astype(BF16)
    return hi, lo


def _mm3(a, b):
    ah, al = _split(a)
    bh, bl = _split(b)
    d = partial(jnp.dot, preferred_element_type=F32)
    return d(ah, bh) + (d(ah, bl) + d(al, bh))


def _sum01(x, m01):
    xh, xl = _split(x)
    return jnp.dot(xh, m01, preferred_element_type=F32) + jnp.dot(xl, m01, preferred_element_type=F32)


def _sum01_left(m01, x):
    xh, xl = _split(x)
    return jnp.dot(m01, xh, preferred_element_type=F32) + jnp.dot(m01, xl, preferred_element_type=F32)


def _dot_nt(a, b):
    return lax.dot_general(a, b, (((1,), (1,)), ((), ())), preferred_element_type=F32)


def _dot_tn(a, b):
    return lax.dot_general(a, b, (((0,), (0,)), ((), ())), preferred_element_type=F32)


def _rwkv_body(z_ref, shift_ref, s0_ref, mu_ref, w0_ref, w2_ref, a0_ref, a2_ref, g2_ref, kk_ref, ka_ref,
               rk_ref, lng_ref, lnb_ref, o_ref, s_out_ref, shift_out_ref, state_ref, prev_ref, *, T):
    ci = pl.program_id(1)
    nc = pl.num_programs(1)
    W = BRANCH_W
    N = RW_HEADS * T

    @pl.when(ci == 0)
    def _():
        state_ref[...] = s0_ref[0]
        prev_ref[...] = shift_ref[0]

    z = z_ref[0]
    row = lax.broadcasted_iota(jnp.int32, (T, RW_COLS), 0)
    z_prev = jnp.where(row == 0, jnp.broadcast_to(prev_ref[...], (T, RW_COLS)), pltpu.roll(z, 1, axis=0))
    prev_ref[...] = z[T - 1:T, :]
    zs = z + (z_prev - z) * mu_ref[...]
    r = zs[:, 0:W]
    k = zs[:, W:2 * W]
    v = zs[:, 2 * W:3 * W]
    xwa = zs[:, 3 * W:3 * W + 128]
    xg = zs[:, 3 * W + 128:]
    w_log = -jax.nn.softplus(-(w0_ref[...] + _mm3(jnp.tanh(xwa), w2_ref[...]))) - 0.5
    logdec = -jnp.exp(w_log)
    a = jax.nn.sigmoid(a0_ref[...] + _mm3(xwa, a2_ref[...]))
    g = _mm3(jax.nn.sigmoid(xg), g2_ref[...])
    li = lax.broadcasted_iota(jnp.int32, (W, W), 0) // RW_HD
    lj = lax.broadcasted_iota(jnp.int32, (W, W), 1) // RW_HD
    head_sum = (li == lj).astype(BF16)
    kk = k * kk_ref[...]
    kk = kk * lax.rsqrt(_sum01(kk * kk, head_sum) + 1e-12)
    k2 = k * (1.0 + (a - 1.0) * ka_ref[...])
    beta = kk * a
    ti = lax.broadcasted_iota(jnp.int32, (T, T), 0)
    tj = lax.broadcasted_iota(jnp.int32, (T, T), 1)
    b = _sum01_left((ti >= tj).astype(BF16), logdec)
    b_last = b[T - 1:T, :]
    k_in = kk * jnp.exp(b - logdec)
    r_in = r * jnp.exp(b)
    inv_p = jnp.exp(-b)
    k_out = k2 * inv_p
    b_out = beta * inv_p
    to_end = jnp.exp(b_last - b)
    k_end = k2 * to_end
    b_end = beta * to_end

    sh = lax.broadcasted_iota(jnp.int32, (N, W), 0) // T
    sl = lax.broadcasted_iota(jnp.int32, (N, W), 1) // RW_HD
    own = sh == sl

    def stack(x, masked):
        xs = jnp.concatenate([x] * RW_HEADS, axis=0)
        return jnp.where(own, xs, 0.0) if masked else xs

    k_in_s = stack(k_in, True).astype(BF16)
    r_in_s = stack(r_in, True).astype(BF16)
    k_out_s = stack(k_out, False).astype(BF16)
    b_out_s = stack(b_out, False).astype(BF16)
    v_s = stack(v, True).astype(BF16)
    k_end_s = stack(k_end, True).astype(BF16)
    b_end_s = stack(b_end, True).astype(BF16)

    ri = lax.broadcasted_iota(jnp.int32, (N, N), 0)
    rj = lax.broadcasted_iota(jnp.int32, (N, N), 1)
    same_head = (ri // T) == (rj // T)
    strict = same_head & (ri > rj)
    incl = same_head & (ri >= rj)
    a_mat = jnp.where(strict, _dot_nt(k_in_s, b_out_s), 0.0)
    kk_mat = jnp.where(strict, _dot_nt(k_in_s, k_out_s), 0.0)
    rk_mat = jnp.where(incl, _dot_nt(r_in_s, k_out_s), 0.0)
    rb_mat = jnp.where(incl, _dot_nt(r_in_s, b_out_s), 0.0)

    eye = (ri == rj).astype(F32)
    a_bd = jnp.where((ri // RW_SUB) == (rj // RW_SUB), a_mat, 0.0)
    x = eye - a_bd
    p = _mm3(a_bd, a_bd)
    x = x + _mm3(x, p)
    p = _mm3(p, p)
    x = x + _mm3(x, p)
    p = _mm3(p, p)
    x = x + _mm3(x, p)
    size = RW_SUB
    while size < T:
        lower = ((ri // (2 * size)) == (rj // (2 * size))) & ((ri // size) > (rj // size))
        x = x - _mm3(_mm3(x, jnp.where(lower, a_mat, 0.0)), x)
        size *= 2

    state = state_ref[...]
    state_b = state.astype(BF16)
    d = partial(jnp.dot, preferred_element_type=F32)
    rhs = _dot_nt(k_in_s, state_b) + d(kk_mat.astype(BF16), v_s)
    u = _mm3(x, rhs)
    u_b = u.astype(BF16)
    o_s = _dot_nt(r_in_s, state_b) + d(rk_mat.astype(BF16), v_s) - d(rb_mat.astype(BF16), u_b)
    o = o_s[0:T]
    for h in range(1, RW_HEADS):
        o = o + o_s[h * T:(h + 1) * T]
    state_ref[...] = state * jnp.exp(b_last) + _dot_tn(v_s, k_end_s) - _dot_tn(u_b, b_end_s)

    mean = _sum01(o, head_sum) * (1.0 / RW_HD)
    var = _sum01(jnp.square(o - mean), head_sum) * (1.0 / RW_HD)
    o = (o - mean) * lax.rsqrt(var + RW_GN_EPS) * lng_ref[...] + lnb_ref[...]
    bonus = _sum01(r * k2 * rk_ref[...], head_sum) * v
    o_ref[0] = ((o + bonus) * g).astype(o_ref.dtype)

    @pl.when(ci == nc - 1)
    def _():
        s_out_ref[0] = state_ref[...]
        shift_out_ref[0] = z[T - 1:T, :]


def _rwkv7_pallas(z, s0_bd, shift_prev, mu, w0, w2, a0, a2, g2, k_k, k_a, r_k, ln_g, ln_b, *, T, out_dtype=F32):
    B, L, C = z.shape
    assert C == RW_COLS and L % T == 0 and T % RW_SUB == 0
    W = BRANCH_W
    w2p = jnp.concatenate([w2, jnp.zeros_like(w2)], axis=0)
    a2p = jnp.concatenate([jnp.zeros_like(a2), a2], axis=0)
    vec = lambda t: t.reshape(1, -1).astype(F32)
    full = lambda shape: pl.BlockSpec(shape, lambda b, c: (0,) * len(shape))
    return pl.pallas_call(
        partial(_rwkv_body, T=T),
        grid=(B, L // T),
        in_specs=[pl.BlockSpec((1, T, C), lambda b, c: (b, c, 0)),
                  pl.BlockSpec((1, 1, C), lambda b, c: (b, 0, 0)),
                  pl.BlockSpec((1, W, W), lambda b, c: (b, 0, 0)),
                  full((1, C)), full((1, W)), full((128, W)), full((1, W)), full((128, W)), full((128, W)),
                  full((1, W)), full((1, W)), full((1, W)), full((1, W)), full((1, W))],
        out_specs=[pl.BlockSpec((1, T, W), lambda b, c: (b, c, 0)),
                   pl.BlockSpec((1, W, W), lambda b, c: (b, 0, 0)),
                   pl.BlockSpec((1, 1, C), lambda b, c: (b, 0, 0))],
        out_shape=[jax.ShapeDtypeStruct((B, L, W), out_dtype),
                   jax.ShapeDtypeStruct((B, W, W), F32),
                   jax.ShapeDtypeStruct((B, 1, C), F32)],
        scratch_shapes=[pltpu.VMEM((W, W), F32), pltpu.VMEM((1, C), F32)],
        compiler_params=pltpu.CompilerParams(dimension_semantics=("arbitrary", "arbitrary")),
        name="rwkv7_chunked",
    )(z, shift_prev.reshape(B, 1, C), s0_bd, vec(mu), vec(w0), w2p, vec(a0), a2p, g2, vec(k_k), vec(k_a),
      vec(r_k), vec(ln_g), vec(ln_b))


def _state_to_bd(s):
    B = s.shape[0]
    eye = jnp.eye(RW_HEADS, dtype=s.dtype)
    return (s[:, :, :, None, :] * eye[None, :, None, :, None]).reshape(B, BRANCH_W, BRANCH_W)


def _state_from_bd(s_bd):
    B = s_bd.shape[0]
    s5 = s_bd.reshape(B, RW_HEADS, RW_HD, RW_HEADS, RW_HD)
    return jnp.stack([s5[:, h, :, h, :] for h in range(RW_HEADS)], axis=1)


def _hgrn_body(z_ref, s0_ref, loglb_ref, log1mlb_ref, ng_ref, o_ref, s_out_ref, state_ref, *, T):
    ci = pl.program_id(1)
    nc = pl.num_programs(1)
    W = BRANCH_W

    @pl.when(ci == 0)
    def _():
        state_ref[...] = s0_ref[0]

    z = z_ref[0]
    zq, zf, zi, zg = z[:, 0:W], z[:, W:2 * W], z[:, 2 * W:3 * W], z[:, 3 * W:]
    log_sig = jnp.minimum(zf, 0.0) - jnp.log(1.0 + jnp.exp(-jnp.abs(zf)))
    a = loglb_ref[...]
    bv = log1mlb_ref[...] + log_sig
    m = jnp.maximum(a, bv)
    log_f = m + jnp.log(jnp.exp(a - m) + jnp.exp(bv - m))
    k = 1.0 - jnp.exp(log_f)
    q = zq * jax.nn.sigmoid(zq)
    ti = lax.broadcasted_iota(jnp.int32, (T, T), 0)
    tj = lax.broadcasted_iota(jnp.int32, (T, T), 1)
    b = _sum01_left((ti >= tj).astype(BF16), log_f)
    li = lax.broadcasted_iota(jnp.int32, (W, W), 0) // HG_DK
    lj = lax.broadcasted_iota(jnp.int32, (W, W), 1) // HG_DK
    same_head = li == lj
    head_sum = same_head.astype(BF16)

    prods, first = [], []
    for s in range(T):
        r0 = (s // SUBLANES) * SUBLANES
        t_idx = r0 + lax.broadcasted_iota(jnp.int32, (T - r0, W), 0)
        e = jnp.exp(jnp.where(t_idx >= s, b[r0:, :] - b[s:s + 1, :], -jnp.inf))
        prods.append((q[r0:, :] * e * k[s:s + 1, :]).astype(BF16))
        first.append(r0)
    scores = jnp.dot(jnp.concatenate(prods, axis=0), head_sum, preferred_element_type=F32)
    groups = [jnp.zeros((SUBLANES, W), F32) for _ in range(T // SUBLANES)]
    at = 0
    for s in range(T):
        for gi in range(first[s] // SUBLANES, T // SUBLANES):
            lo = at + gi * SUBLANES - first[s]
            groups[gi] = groups[gi] + scores[lo:lo + SUBLANES, :] * zi[s:s + 1, :]
        at += T - first[s]
    o = jnp.concatenate(groups, axis=0)

    state = state_ref[...]
    o = o + _dot_nt((q * jnp.exp(b)).astype(BF16), state.astype(BF16))
    b_last = b[T - 1:T, :]
    k_end = (k * jnp.exp(b_last - b)).astype(BF16)
    upd = _dot_tn(zi.astype(BF16), k_end)
    state_ref[...] = state * jnp.exp(b_last) + jnp.where(same_head, upd, 0.0)

    ms = _sum01(o * o, head_sum) * (1.0 / HG_DK)
    o = o * lax.rsqrt(ms + EPS) * ng_ref[...]
    o_ref[0] = (o * (zg * jax.nn.sigmoid(zg))).astype(o_ref.dtype)

    @pl.when(ci == nc - 1)
    def _():
        s_out_ref[0] = state_ref[...]


def _hgrn2_pallas(z, s0_bd, lb, norm_g, *, T, out_dtype=F32):
    B, L, C = z.shape
    assert C == HG_COLS and L % T == 0
    W = BRANCH_W
    full = lambda shape: pl.BlockSpec(shape, lambda b, c: (0,) * len(shape))
    return pl.pallas_call(
        partial(_hgrn_body, T=T),
        grid=(B, L // T),
        in_specs=[pl.BlockSpec((1, T, C), lambda b, c: (b, c, 0)),
                  pl.BlockSpec((1, W, W), lambda b, c: (b, 0, 0)),
                  full((1, W)), full((1, W)), full((1, W))],
        out_specs=[pl.BlockSpec((1, T, W), lambda b, c: (b, c, 0)),
                   pl.BlockSpec((1, W, W), lambda b, c: (b, 0, 0))],
        out_shape=[jax.ShapeDtypeStruct((B, L, W), out_dtype),
                   jax.ShapeDtypeStruct((B, W, W), F32)],
        scratch_shapes=[pltpu.VMEM((W, W), F32)],
        compiler_params=pltpu.CompilerParams(dimension_semantics=("arbitrary", "arbitrary")),
        name="hgrn2_chunked",
    )(z, s0_bd, jnp.log(lb).reshape(1, W), jnp.log1p(-lb).reshape(1, W), jnp.tile(norm_g, HG_HEADS).reshape(1, W))


PROJ_TM = 256
PROJ_VMEM_LIMIT_V7X = 48 * 1024 * 1024
CONV_CARRY = 32
CONV_TC = 512


def _in_proj_body(x_ref, g_ref, w_ref, q_ref, k_ref, v_ref, kb_ref, vb_ref, hg_ref, cv_ref, rw_ref, gate_ref):
    x = x_ref[...]
    h = (x * lax.rsqrt(jnp.mean(x * x, axis=-1, keepdims=True) + EPS) * g_ref[...]).astype(BF16)
    W = BRANCH_W

    def proj(a, b):
        return jnp.dot(h, w_ref[:, a:b], preferred_element_type=F32)

    q_ref[...] = (proj(0, W) * SB_Q_SCALE).astype(BF16)
    k = proj(W, 2 * W)
    v = proj(2 * W, 3 * W)
    k_ref[...] = k
    v_ref[...] = v
    kb_ref[...] = k.astype(BF16)
    vb_ref[...] = v.astype(BF16)
    c = SB_COLS
    hg_ref[...] = proj(c, c + HG_COLS)
    c += HG_COLS
    cv_ref[...] = proj(c, c + CV_COLS)
    c += CV_COLS
    rw_ref[...] = proj(c, c + RW_COLS)
    c += RW_COLS
    for n in range(N_BRANCH):
        gate_ref[:, n * D_MODEL:(n + 1) * D_MODEL] = jax.nn.sigmoid(
            proj(c + n * D_MODEL, c + (n + 1) * D_MODEL)).astype(BF16)


def _in_proj(x2, norm_g, w_in_b):
    n = x2.shape[0]
    TM = min(PROJ_TM, n)
    assert n % TM == 0
    row = lambda w: pl.BlockSpec((TM, w), lambda i: (i, 0))
    shp = lambda w, dt: jax.ShapeDtypeStruct((n, w), dt)
    W = BRANCH_W
    return pl.pallas_call(
        _in_proj_body,
        grid=(n // TM,),
        in_specs=[row(D_MODEL), pl.BlockSpec((1, D_MODEL), lambda i: (0, 0)),
                  pl.BlockSpec((D_MODEL, IN_COLS), lambda i: (0, 0), pipeline_mode=pl.Buffered(1))],
        out_specs=[row(W), row(W), row(W), row(W), row(W), row(HG_COLS), row(CV_COLS), row(RW_COLS), row(GATE_COLS)],
        out_shape=[shp(W, BF16), shp(W, F32), shp(W, F32), shp(W, BF16), shp(W, BF16),
                   shp(HG_COLS, F32), shp(CV_COLS, F32), shp(RW_COLS, F32), shp(GATE_COLS, BF16)],
        compiler_params=pltpu.CompilerParams(dimension_semantics=("arbitrary",), vmem_limit_bytes=PROJ_VMEM_LIMIT_V7X),
        name="in_proj",
    )(x2, norm_g.reshape(1, -1), w_in_b)


def _merge_body(x_ref, sb_ref, hg_ref, cv_ref, rw_ref, gate_ref, wb_ref, wo_ref, o_ref):
    merged = None
    for n, br in enumerate((sb_ref, hg_ref, cv_ref, rw_ref)):
        t = jnp.dot(br[...].astype(BF16), wb_ref[n], preferred_element_type=F32)
        t = t * gate_ref[:, n * D_MODEL:(n + 1) * D_MODEL].astype(F32)
        merged = t if merged is None else merged + t
    o_ref[...] = x_ref[...] + jnp.dot(merged.astype(BF16), wo_ref[...], preferred_element_type=F32)


def _merge(x2, o_sb, o_hg, o_cv, o_rw, gates, wb_b, wo_b):
    n = x2.shape[0]
    TM = min(PROJ_TM, n)
    row = lambda w: pl.BlockSpec((TM, w), lambda i: (i, 0))
    return pl.pallas_call(
        _merge_body,
        grid=(n // TM,),
        in_specs=[row(D_MODEL), row(BRANCH_W), row(BRANCH_W), row(BRANCH_W), row(BRANCH_W), row(GATE_COLS),
                  pl.BlockSpec((N_BRANCH, BRANCH_W, D_MODEL), lambda i: (0, 0, 0)),
                  pl.BlockSpec((D_MODEL, D_MODEL), lambda i: (0, 0))],
        out_specs=row(D_MODEL),
        out_shape=jax.ShapeDtypeStruct((n, D_MODEL), F32),
        compiler_params=pltpu.CompilerParams(dimension_semantics=("arbitrary",), vmem_limit_bytes=PROJ_VMEM_LIMIT_V7X),
        name="branch_merge",
    )(x2, o_sb, o_hg, o_cv, o_rw, gates, wb_b, wo_b)


def _conv_body(z_ref, buf_ref, w_ref, b_ref, lg_ref, lb_ref, o_ref, buf_out_ref, hp_ref, *, TC):
    ci = pl.program_id(1)
    nc = pl.num_programs(1)

    @pl.when(ci == 0)
    def _():
        hp_ref[0:CONV_CARRY, :] = buf_ref[0]

    z = z_ref[0]
    hp_ref[CONV_CARRY:, :] = z[:, 0:CONV_CH] * jax.nn.sigmoid(z[:, CONV_CH:])
    first = CONV_CARRY - (CONV_WIDTH - 1)
    y = jnp.zeros((TC, CONV_CH), F32) + b_ref[...]
    for j in range(CONV_WIDTH):
        y = y + hp_ref[first + j:first + j + TC, :] * w_ref[j:j + 1, :]
    mu = jnp.mean(y, axis=-1, keepdims=True)
    d = y - mu
    var = jnp.mean(d * d, axis=-1, keepdims=True)
    y = d * lax.rsqrt(var + EPS) * lg_ref[...] + lb_ref[...]
    o_ref[0] = (y * jax.nn.sigmoid(y)).astype(o_ref.dtype)
    tail = hp_ref[TC:TC + CONV_CARRY, :]

    @pl.when(ci == nc - 1)
    def _():
        buf_out_ref[0] = tail

    hp_ref[0:CONV_CARRY, :] = tail


def _conformer_conv_pallas(z_cv, buf, w, b, ln_g, ln_b, *, TC, out_dtype=F32):
    B, L, _ = z_cv.shape
    assert L % TC == 0 and TC % 8 == 0 and TC >= CONV_CARRY
    C = CONV_CH
    pad = CONV_CARRY - (CONV_WIDTH - 1)
    buf_p = jnp.pad(buf.astype(F32), ((0, 0), (pad, 0), (0, 0)))
    w_p = jnp.pad(w.astype(F32), ((0, 32 - CONV_WIDTH), (0, 0)))
    full = lambda shape: pl.BlockSpec(shape, lambda b, c: (0,) * len(shape))
    y, buf_new = pl.pallas_call(
        partial(_conv_body, TC=TC),
        grid=(B, L // TC),
        in_specs=[pl.BlockSpec((1, TC, 2 * C), lambda b, c: (b, c, 0)),
                  pl.BlockSpec((1, CONV_CARRY, C), lambda b, c: (b, 0, 0)),
                  full((32, C)), full((1, C)), full((1, C)), full((1, C))],
        out_specs=[pl.BlockSpec((1, TC, C), lambda b, c: (b, c, 0)),
                   pl.BlockSpec((1, CONV_CARRY, C), lambda b, c: (b, 0, 0))],
        out_shape=[jax.ShapeDtypeStruct((B, L, C), out_dtype), jax.ShapeDtypeStruct((B, CONV_CARRY, C), F32)],
        scratch_shapes=[pltpu.VMEM((CONV_CARRY + TC, C), F32)],
        compiler_params=pltpu.CompilerParams(dimension_semantics=("arbitrary", "arbitrary")),
        name="conformer_conv",
    )(z_cv, buf_p, w_p, b.reshape(1, C), ln_g.reshape(1, C), ln_b.reshape(1, C))
    return y, buf_new[:, pad:, :]


PK_PAIRS = PK_HEADS * PK_TOPK
PEER_HEAD_GROUP = 4
PEER_TB = 128
ROW_TILES = D_MODEL // LANES
PEER_VMEM_LIMIT_V7X = 52 * 1024 * 1024


def _topk_rows(scores, payloads, k):
    n = scores[0].shape[0]
    rows = lax.broadcasted_iota(jnp.int32, scores[0].shape, 0).astype(F32)
    scores = list(scores)
    vals = [[] for _ in scores]
    ids = [[] for _ in scores]
    for _ in range(k):
        for c, s in enumerate(scores):
            m = jnp.max(s, axis=0, keepdims=True)
            pos = jnp.min(jnp.where(s == m, rows, float(n)), axis=0, keepdims=True)
            sel = rows == pos
            ids[c].append(pos if payloads[c] is None
                          else jnp.max(jnp.where(sel, payloads[c], -1.0), axis=0, keepdims=True))
            vals[c].append(m)
            scores[c] = jnp.where(sel, -jnp.inf, s)
    return [(jnp.concatenate(v, axis=0), jnp.concatenate(i, axis=0)) for v, i in zip(vals, ids)]


def _dot3(ah, al, bh, bl, dims):
    d = partial(lax.dot_general, dimension_numbers=dims, preferred_element_type=F32)
    return d(ah, bh) + (d(ah, bl) + d(al, bh))


def _candidate_tables(tb):
    groups = [[0], [1], [2], [3, 4], [5, 6, 7], list(range(8, 16))]
    rows = []
    for grp in groups:
        part = [(a, b) for a in grp for b in range(PK_TOPK // (a + 1))]
        rows += part + [None] * ((-len(part)) % SUBLANES)
    n = len(rows)
    sel1 = [[1.0 if (r is not None and r[0] == j) else 0.0 for j in range(PK_TOPK)] for r in rows]
    sel2 = [[1.0 if (r is not None and r[1] == j) else 0.0 for j in range(PK_TOPK)] for r in rows]
    cpos = [[-1.0 if r is None else float(r[0] * PK_TOPK + r[1])] * tb for r in rows]
    return jnp.array(sel1, BF16), jnp.array(sel2, BF16), jnp.array(cpos, F32), n


def _pick_rows(sel, v):
    hi = v.astype(BF16)
    r1 = v - hi.astype(F32)
    mid = r1.astype(BF16)
    lo = (r1 - mid.astype(F32)).astype(BF16)
    d = partial(jnp.dot, preferred_element_type=F32)
    return (d(sel, hi) + d(sel, mid)) + d(sel, lo)


def _peer_route_body(x_ref, g_ref, wqh_ref, wql_ref, qg_ref, k1_ref, k2_ref, sel1_ref, sel2_ref, cpos_ref,
                     xn_ref, idx_ref, idxc_ref, gate_ref):
    x = x_ref[...]
    xn = x * lax.rsqrt(jnp.mean(x * x, axis=-1, keepdims=True) + EPS) * g_ref[...]
    xn_ref[...] = xn
    nn = (((1,), (0,)), ((), ()))
    nt = (((1,), (1,)), ((), ()))
    xh, xl = _split(xn)
    q = _dot3(xh, xl, wqh_ref[...], wql_ref[...], nn)
    half = PK_DQ // 2
    experts, gates = [], []
    for h0 in range(0, PK_HEADS, PEER_HEAD_GROUP):
        heads = range(h0, h0 + PEER_HEAD_GROUP)
        subs = []
        for h in heads:
            qh = q[:, h * PK_DQ:(h + 1) * PK_DQ]
            qh = qh * lax.rsqrt(jnp.mean(qh * qh, axis=-1, keepdims=True) + EPS) * qg_ref[...]
            qhh, qhl = _split(qh)
            k1h, k1l = _split(k1_ref[h])
            k2h, k2l = _split(k2_ref[h])
            subs.append(_dot3(k1h, k1l, qhh[:, :half], qhl[:, :half], nt))
            subs.append(_dot3(k2h, k2l, qhh[:, half:], qhl[:, half:], nt))
        tops = _topk_rows(subs, [None] * len(subs), PK_TOPK)
        cands = []
        for j in range(PEER_HEAD_GROUP):
            (v1, _), (v2, _) = tops[2 * j], tops[2 * j + 1]
            cands.append(jnp.where(cpos_ref[...] < 0.0, -jnp.inf,
                                   _pick_rows(sel1_ref[...], v1) + _pick_rows(sel2_ref[...], v2)))
        best = _topk_rows(cands, [cpos_ref[...]] * PEER_HEAD_GROUP, PK_TOPK)
        for j in range(PEER_HEAD_GROUP):
            i1, i2 = tops[2 * j][1], tops[2 * j + 1][1]
            top_s, pos = best[j]
            pa = jnp.floor(pos * (1.0 / PK_TOPK))
            pb = pos - pa * PK_TOPK
            e1 = jnp.zeros_like(pos)
            e2 = jnp.zeros_like(pos)
            for a in range(PK_TOPK):
                e1 = jnp.where(pa == a, i1[a:a + 1, :], e1)
                e2 = jnp.where(pb == a, i2[a:a + 1, :], e2)
            e = jnp.exp(top_s - jnp.max(top_s, axis=0, keepdims=True))
            gates.append(e / jnp.sum(e, axis=0, keepdims=True))
            experts.append(e1 * float(PK_NKEYS) + e2)
    expert = jnp.concatenate(experts, axis=0).astype(jnp.int32)
    row = (expert + 1) * HALF_TILES
    pair = lax.broadcasted_iota(jnp.int32, row.shape, 0)
    idx_ref[...] = row.T
    idxc_ref[...] = (row - (pair & HALF_TILES)).T
    gate_ref[...] = jnp.concatenate(gates, axis=0)


def _peer_route(x2, norm_g, wq, qn_g, k1, k2):
    n_tok = x2.shape[0]
    TB = PEER_TB
    assert n_tok % TB == 0
    nb = n_tok // TB
    full = lambda shape: pl.BlockSpec(shape, lambda i: (0,) * len(shape))
    wq_hi = wq.astype(BF16)
    wq_lo = (wq - wq_hi.astype(F32)).astype(BF16)
    sel1, sel2, cpos, n_cand = _candidate_tables(TB)
    return pl.pallas_call(
        _peer_route_body,
        grid=(nb,),
        in_specs=[pl.BlockSpec((TB, D_MODEL), lambda i: (i, 0)), full((1, D_MODEL)),
                  full((D_MODEL, PK_HEADS * PK_DQ)), full((D_MODEL, PK_HEADS * PK_DQ)), full((1, PK_DQ)),
                  full((PK_HEADS, PK_NKEYS, PK_DQ // 2)), full((PK_HEADS, PK_NKEYS, PK_DQ // 2)),
                  full((n_cand, PK_TOPK)), full((n_cand, PK_TOPK)), full((n_cand, TB))],
        out_specs=[pl.BlockSpec((TB, D_MODEL), lambda i: (i, 0)),
                   pl.BlockSpec((TB, PK_PAIRS), lambda i: (i, 0)),
                   pl.BlockSpec((TB, PK_PAIRS), lambda i: (i, 0)),
                   pl.BlockSpec((PK_PAIRS, TB), lambda i: (0, i))],
        out_shape=[jax.ShapeDtypeStruct((n_tok, D_MODEL), F32),
                   jax.ShapeDtypeStruct((n_tok, PK_PAIRS), jnp.int32),
                   jax.ShapeDtypeStruct((n_tok, PK_PAIRS), jnp.int32),
                   jax.ShapeDtypeStruct((PK_PAIRS, n_tok), F32)],
        compiler_params=pltpu.CompilerParams(dimension_semantics=("arbitrary",)),
        name="peer_route",
    )(x2, norm_g.reshape(1, -1), wq_hi, wq_lo, qn_g.reshape(1, -1), k1, k2, sel1, sel2, cpos)


HALF_TILES = ROW_TILES // 2


def _unpack_words(word):
    return pltpu.bitcast(word << 16, F32), pltpu.bitcast(word & jnp.int32(-65536), F32)


def _fold_couples(tiles, sub):
    m2 = (sub & 3) < 2
    lvl2 = []
    for a, b in ((tiles[0], tiles[2]), (tiles[1], tiles[3])):
        lvl2.append(jnp.where(m2, a, b) + jnp.where(m2, pltpu.roll(a, 6, axis=0), pltpu.roll(b, 2, axis=0)))
    m1 = (sub & 1) == 0
    a, b = lvl2
    return jnp.where(m1, a, b) + jnp.where(m1, pltpu.roll(a, 7, axis=0), pltpu.roll(b, 1, axis=0))


def _peer_act_body(idx_ref, x_ref, gate_ref, tab_ref, w_ref, part_ref):
    TB = x_ref.shape[0]
    sub = lax.broadcasted_iota(jnp.int32, (SUBLANES, LANES), 0)
    low = sub < HALF_TILES

    def token(t, c):
        x = x_ref[pl.ds(t, 1), :]
        tiles_x = [x[:, j * LANES:(j + 1) * LANES] for j in range(ROW_TILES)]
        x_lo = jnp.concatenate(tiles_x[:HALF_TILES] * 2, axis=0)
        x_hi = jnp.concatenate(tiles_x[HALF_TILES:] * 2, axis=0)
        start = t * PK_PAIRS
        for g in range(PK_PAIRS // SUBLANES):
            tiles = []
            for j in range(HALF_TILES):
                ra = pl.multiple_of(idx_ref[0, 0, start + g * SUBLANES + j], HALF_TILES)
                rb = pl.multiple_of(idx_ref[0, 0, start + g * SUBLANES + j + HALF_TILES], HALF_TILES)
                word = jnp.where(low, tab_ref[pl.ds(ra, SUBLANES), :], tab_ref[pl.ds(rb, SUBLANES), :])
                lo, hi = _unpack_words(word)
                tiles.append(lo * x_lo + hi * x_hi)
            part_ref[t, g * SUBLANES:(g + 1) * SUBLANES, :] = _fold_couples(tiles, sub)
        return c

    lax.fori_loop(0, TB, token, 0)
    lane = lax.broadcasted_iota(jnp.int32, (PK_PAIRS, TB), 1)
    s = jnp.zeros((PK_PAIRS, TB), F32)
    for t in range(TB):
        s = jnp.where(lane == t, jnp.sum(part_ref[t], axis=1, keepdims=True), s)
    w_ref[...] = gate_ref[...] * jax.nn.gelu(s)


PEER_MIX_ACCS = 4


def _peer_mix_body(idx_ref, w_ref, tab_ref, y_ref, wrep_ref):
    TB = y_ref.shape[0]
    w_all = w_ref[...]
    for t in range(TB):
        wrep_ref[t] = jnp.broadcast_to(w_all[:, t:t + 1], (PK_PAIRS, LANES))

    def token(t, c):
        start = t * PK_PAIRS
        acc_lo = [jnp.zeros((HALF_TILES, LANES), F32) for _ in range(PEER_MIX_ACCS)]
        acc_hi = [jnp.zeros((HALF_TILES, LANES), F32) for _ in range(PEER_MIX_ACCS)]
        for p in range(PK_PAIRS):
            row = pl.multiple_of(idx_ref[0, 0, start + p], HALF_TILES)
            lo, hi = _unpack_words(tab_ref[pl.ds(row, HALF_TILES), :])
            w = jnp.broadcast_to(wrep_ref[t, p:p + 1, :], (HALF_TILES, LANES))
            acc_lo[p % PEER_MIX_ACCS] = acc_lo[p % PEER_MIX_ACCS] + w * lo
            acc_hi[p % PEER_MIX_ACCS] = acc_hi[p % PEER_MIX_ACCS] + w * hi
        y_ref[t] = jnp.concatenate([(acc_lo[0] + acc_lo[1]) + (acc_lo[2] + acc_lo[3]),
                                    (acc_hi[0] + acc_hi[1]) + (acc_hi[2] + acc_hi[3])], axis=0)
        return c

    lax.fori_loop(0, TB, token, 0)


def _pack_table(tab):
    bits = lax.bitcast_convert_type(tab.astype(BF16), jnp.uint16).astype(jnp.uint32)
    bits = bits.reshape(PK_EXPERTS, 2, HALF_TILES, LANES)
    word = bits[:, 0] | (bits[:, 1] << 16)
    rows = lax.bitcast_convert_type(word, jnp.int32).reshape(PK_EXPERTS * HALF_TILES, LANES)
    return jnp.pad(rows, ((HALF_TILES, HALF_TILES), (0, 0)))


def _smem_spec(TB):
    return pl.BlockSpec((1, 1, PK_PAIRS * TB), lambda i: (i, 0, 0), memory_space=pltpu.SMEM)


def _table_spec():
    return pl.BlockSpec(((PK_EXPERTS + 2) * HALF_TILES, LANES), lambda i: (0, 0), pipeline_mode=pl.Buffered(1))


def _peer_act(idx, xn, gate_t, tab_packed):
    n_tok = xn.shape[0]
    TB = PEER_TB
    nb = n_tok // TB
    return pl.pallas_call(
        _peer_act_body,
        grid=(nb,),
        in_specs=[_smem_spec(TB),
                  pl.BlockSpec((TB, D_MODEL), lambda i: (i, 0)),
                  pl.BlockSpec((PK_PAIRS, TB), lambda i: (0, i)),
                  _table_spec()],
        out_specs=pl.BlockSpec((PK_PAIRS, TB), lambda i: (0, i)),
        out_shape=jax.ShapeDtypeStruct((PK_PAIRS, n_tok), F32),
        scratch_shapes=[pltpu.VMEM((TB, PK_PAIRS, LANES), F32)],
        compiler_params=pltpu.CompilerParams(dimension_semantics=("arbitrary",),
                                             vmem_limit_bytes=PEER_VMEM_LIMIT_V7X),
        name="peer_act",
    )(idx.reshape(nb, 1, TB * PK_PAIRS), xn, gate_t, tab_packed)


def _peer_mix(idx, w_t, tab_packed):
    n_tok = w_t.shape[1]
    TB = PEER_TB
    nb = n_tok // TB
    return pl.pallas_call(
        _peer_mix_body,
        grid=(nb,),
        in_specs=[_smem_spec(TB),
                  pl.BlockSpec((PK_PAIRS, TB), lambda i: (0, i)),
                  _table_spec()],
        out_specs=pl.BlockSpec((TB, ROW_TILES, LANES), lambda i: (i, 0, 0)),
        out_shape=jax.ShapeDtypeStruct((n_tok, ROW_TILES, LANES), F32),
        scratch_shapes=[pltpu.VMEM((TB, PK_PAIRS, LANES), F32)],
        compiler_params=pltpu.CompilerParams(dimension_semantics=("arbitrary",),
                                             vmem_limit_bytes=PEER_VMEM_LIMIT_V7X),
        name="peer_mix",
    )(idx.reshape(nb, 1, TB * PK_PAIRS), w_t, tab_packed)


def _peer_ffn_pallas(x, norm_g, wq, qn_g, sub_keys, u_packed, v_packed):
    B, L, D = x.shape
    n_tok = B * L
    pad = (-n_tok) % PEER_TB
    x2 = jnp.pad(x.reshape(n_tok, D), ((0, pad), (0, 0)))
    xn, idx, idx_couple, gate_t = _peer_route(x2, norm_g, wq, qn_g, sub_keys[:, 0], sub_keys[:, 1])
    w_t = _peer_act(idx_couple, xn, gate_t, u_packed)
    y = _peer_mix(idx, w_t, v_packed)
    return y.reshape(-1, D)[:n_tok].reshape(B, L, D)


def _trunk_layer(x, k_past, v_past, s_hg, buf_conv, s_rw, buf_shift, p):
    B_, L, _ = x.shape
    n = B_ * L
    x2 = x.reshape(n, D_MODEL)
    qb, k, v, kb, vb, z_hg, z_cv, z_rw, gates = _in_proj(x2, p['norm_mix_g'], p['w_in'])
    seq = lambda t: t.reshape(B_, L, t.shape[-1])
    if k_past is None:
        o_sb = _sb_attention(seq(qb), seq(kb), seq(vb), off=0, tq=SB_TQ, tk=SB_TK, out_dtype=F32)
    else:
        past = k_past.shape[1]
        pad = jnp.zeros((B_, (-(past + L)) % SB_TK, BRANCH_W), BF16)
        kc = jnp.concatenate([k_past.reshape(B_, past, BRANCH_W).astype(BF16), seq(kb), pad], axis=1)
        vc = jnp.concatenate([v_past.reshape(B_, past, BRANCH_W).astype(BF16), seq(vb), pad], axis=1)
        o_sb = _sb_attention(seq(qb), kc, vc, off=past, tq=L, tk=SB_TK, out_dtype=F32)
    o_hg, s_hg_bd = _hgrn2_pallas(seq(z_hg), _state_to_bd(jnp.swapaxes(s_hg.astype(F32), 2, 3)), p['lb'],
                                  p['hg_norm_g'], T=CHUNK if L % CHUNK == 0 else L)
    s_hg_new = jnp.swapaxes(_state_from_bd(s_hg_bd), 2, 3)
    o_cv, buf_conv_new = _conformer_conv_pallas(seq(z_cv), buf_conv, p['conv_w'], p['conv_b'], p['conv_ln_g'],
                                                p['conv_ln_b'], TC=CONV_TC if L % CONV_TC == 0 else L)
    o_rw, s_rw_bd, shift_new = _rwkv7_pallas(seq(z_rw), _state_to_bd(s_rw.astype(F32)), buf_shift, p['rw_mu'],
                                             p['rw_w0'], p['rw_w2'], p['rw_a0'], p['rw_a2'], p['rw_g2'], p['rw_k_k'],
                                             p['rw_k_a'], p['rw_r_k'], p['rw_ln_g'], p['rw_ln_b'],
                                             T=RW_CHUNK if L % RW_CHUNK == 0 else L)
    s_rw_new = _state_from_bd(s_rw_bd)
    buf_shift_new = shift_new[:, 0]
    flat = lambda t: t.reshape(n, BRANCH_W)
    x = _merge(x2, flat(o_sb), flat(o_hg), flat(o_cv), flat(o_rw), gates, p['w_branch'], p['w_out']).reshape(x.shape)
    x = x + _peer_ffn_pallas(x, p['norm_ffn_g'], p['peer_wq'], p['peer_qn_g'], p['peer_keys'],
                             p['peer_u'], p['peer_v'])
    heads = lambda t: t.reshape(B_, L, SB_HEADS, HEAD_DIM)
    return x, (heads(k), heads(v), s_hg_new, buf_conv_new, s_rw_new, buf_shift_new)


def kernel(x_prompt, x_sample, cache_sb_k, cache_sb_v, state_hgrn, state_conv, state_rwkv, state_shift,
           norm_mix_g, w_in, hg_lb_logits, hg_norm_g, conv_w, conv_b, conv_ln_g, conv_ln_b,
           rw_mu, rw_w0, rw_w2, rw_a0, rw_a2, rw_g2, rw_k_k, rw_k_a, rw_r_k, rw_ln_g, rw_ln_b,
           w_branch, w_out, norm_ffn_g, peer_wq, peer_qn_g, peer_keys, peer_u, peer_v, final_norm_g):
    lb_all = jnp.cumsum(jax.nn.softmax(hg_lb_logits.astype(F32), axis=0), axis=0)
    lb_all = lb_all - lb_all[0:1]
    Bp = x_prompt.shape[0]
    hg0 = jnp.zeros((Bp, HG_HEADS, HG_DK, HG_DV), F32)
    cv0 = jnp.zeros((Bp, CONV_WIDTH - 1, CONV_CH), x_prompt.dtype)
    rw0 = jnp.zeros((Bp, RW_HEADS, RW_HD, RW_HD), F32)
    sh0 = jnp.zeros((Bp, RW_COLS), x_prompt.dtype)
    xp, xs = x_prompt, x_sample
    outs_p, outs_s = [], []
    for l in range(DEPTH):
        lp = dict(norm_mix_g=norm_mix_g[l], w_in=w_in[l].astype(BF16), lb=lb_all[l], hg_norm_g=hg_norm_g[l],
                  conv_w=conv_w[l], conv_b=conv_b[l], conv_ln_g=conv_ln_g[l], conv_ln_b=conv_ln_b[l],
                  rw_mu=rw_mu[l], rw_w0=rw_w0[l], rw_w2=rw_w2[l], rw_a0=rw_a0[l], rw_a2=rw_a2[l],
                  rw_g2=rw_g2[l], rw_k_k=rw_k_k[l], rw_k_a=rw_k_a[l], rw_r_k=rw_r_k[l],
                  rw_ln_g=rw_ln_g[l], rw_ln_b=rw_ln_b[l], w_branch=w_branch[l].astype(BF16),
                  w_out=w_out[l].astype(BF16),
                  norm_ffn_g=norm_ffn_g[l], peer_wq=peer_wq[l], peer_qn_g=peer_qn_g[l],
                  peer_keys=peer_keys[l], peer_u=_pack_table(peer_u[l]), peer_v=_pack_table(peer_v[l]))
        xp, st_p = _trunk_layer(xp, None, None, hg0, cv0, rw0, sh0, lp)
        xs, st_s = _trunk_layer(xs, cache_sb_k[l], cache_sb_v[l], state_hgrn[l], state_conv[l],
                                state_rwkv[l], state_shift[l], lp)
        outs_p.append(st_p)
        outs_s.append(st_s)

    def stk(outs, i):
        return jnp.stack([o[i] for o in outs], axis=0)

    y_prompt = _rmsnorm_pallas(xp, final_norm_g)
    y_sample = _rmsnorm_pallas(xs, final_norm_g)
    return (y_prompt, y_sample,
            stk(outs_p, 0), stk(outs_p, 1), stk(outs_p, 2), stk(outs_p, 3), stk(outs_p, 4), stk(outs_p, 5),
            stk(outs_s, 0), stk(outs_s, 1), stk(outs_s, 2), stk(outs_s, 3), stk(outs_s, 4), stk(outs_s, 5))
```

```python
from functools import partial

import jax
import jax.numpy as jnp
from jax import lax
from jax.experimental import pallas as pl
from jax.experimental.pallas import tpu as pltpu

D_MODEL = 1024
DEPTH = 2
CHUNK = 64
HEAD_DIM = 64
N_BRANCH = 4
BRANCH_W = D_MODEL // 4
SB_HEADS = BRANCH_W // HEAD_DIM
HG_HEADS = 4
HG_DK = BRANCH_W // HG_HEADS
HG_DV = BRANCH_W // HG_HEADS
CONV_CH = BRANCH_W
CONV_WIDTH = 31
RW_HEADS = 4
RW_HD = BRANCH_W // RW_HEADS
RW_DECAY_LORA = 64
RW_AAA_LORA = 64
RW_GATE_LORA = 128
SB_COLS = 3 * BRANCH_W
HG_COLS = 4 * BRANCH_W
CV_COLS = 2 * CONV_CH
RW_COLS = 3 * BRANCH_W + RW_DECAY_LORA + RW_AAA_LORA + RW_GATE_LORA
GATE_COLS = N_BRANCH * D_MODEL
IN_COLS = SB_COLS + HG_COLS + CV_COLS + RW_COLS + GATE_COLS
PK_HEADS = 8
PK_NKEYS = 128
PK_EXPERTS = PK_NKEYS * PK_NKEYS
PK_DQ = 256
PK_TOPK = 16
EPS = 1e-6
RW_GN_EPS = 64e-5
SB_TQ = 256
SB_TK = 256
RW_CHUNK = 64
F32 = jnp.float32
BF16 = jnp.bfloat16
SUBLANES = 8
LANES = 128


def _rmsnorm_body(x_ref, g_ref, o_ref):
    x = x_ref[...]
    y = x * lax.rsqrt(jnp.mean(x * x, axis=-1, keepdims=True) + EPS)
    o_ref[...] = y * g_ref[...]


def _rmsnorm_pallas(x, g, rows=512):
    shape = x.shape
    x2 = x.reshape(-1, shape[-1])
    n, d = x2.shape
    rows = min(rows, n)
    out = pl.pallas_call(
        _rmsnorm_body,
        grid=(n // rows,),
        in_specs=[pl.BlockSpec((rows, d), lambda i: (i, 0)),
                  pl.BlockSpec((1, d), lambda i: (0, 0))],
        out_specs=pl.BlockSpec((rows, d), lambda i: (i, 0)),
        out_shape=jax.ShapeDtypeStruct((n, d), x.dtype),
        name="final_rmsnorm",
    )(x2, g.reshape(1, d))
    return out.reshape(shape)


VMEM_LIMIT_V7X = 48 * 1024 * 1024
SB_Q_SCALE = -(HEAD_DIM ** -0.5) * 1.4426950408889634


def _sb_attn_body(q_ref, k_ref, v_ref, o_ref, acc_ref, carry_ref, *, tq, tk, off, n_diag):
    qi = pl.program_id(1)
    q0 = off + qi * tq
    n_full = q0 // tk
    acc_ref[...] = jnp.zeros_like(acc_ref)
    carry_ref[...] = jnp.zeros_like(carry_ref)
    q_all = q_ref[0]
    lane = lax.broadcasted_iota(jnp.int32, (tq, LANES), 1)
    own = [(lane // HEAD_DIM) == (h % 2) for h in range(SB_HEADS)]
    q_heads = [jnp.where(own[h], q_all[:, (h // 2) * LANES:(h // 2 + 1) * LANES], jnp.zeros((), BF16))
               for h in range(SB_HEADS)]
    jj = lax.broadcasted_iota(jnp.int32, (tk, tk), 0)
    ss = lax.broadcasted_iota(jnp.int32, (tk, tk), 1)
    later_mat = (jj > ss).astype(BF16)

    def block(kb, masked):
        start = pl.multiple_of(kb * tk, tk)
        k_blk = k_ref[0, pl.ds(start, tk), :]
        v_blk = v_ref[0, pl.ds(start, tk), :]
        if masked:
            key_pos = start + lax.broadcasted_iota(jnp.int32, (tq, tk), 1)
            q_pos = q0 + lax.broadcasted_iota(jnp.int32, (tq, tk), 0)
            mask = key_pos < q_pos
        dims = (((1,), (1,)), ((), ()))
        nls = [lax.dot_general(q_heads[h], k_blk[:, (h // 2) * LANES:(h // 2 + 1) * LANES], dims,
                               preferred_element_type=F32) for h in range(SB_HEADS)]
        lsms, laters = [], []
        for h in range(SB_HEADS):
            nl = nls[h]
            neg_abs = pltpu.bitcast(pltpu.bitcast(nl, jnp.int32) | jnp.int32(-2 ** 31), F32)
            lsm = jnp.minimum(nl, 0.0) - jnp.log2(1.0 + jnp.exp2(neg_abs))
            if masked:
                lsm = jnp.where(mask, lsm, 0.0)
            lsms.append(lsm)
            laters.append(jnp.dot(lsm.astype(BF16), later_mat, preferred_element_type=F32))
        for h in range(SB_HEADS):
            carry = carry_ref[h]
            expo = (lsms[h] - nls[h]) + laters[h] + jnp.concatenate([carry] * (tk // LANES), axis=1)
            w = jnp.exp2(expo)
            if masked:
                w = jnp.where(mask, w, 0.0)
            acc_ref[h] += jnp.dot(w.astype(BF16), v_blk[:, (h // 2) * LANES:(h // 2 + 1) * LANES],
                                  preferred_element_type=F32)
            row = laters[h][:, 0:1] + lsms[h][:, 0:1]
            carry_ref[h] = carry + jnp.broadcast_to(row, carry.shape)

    for d in range(n_diag - 1, -1, -1):
        block(n_full + d, True)

    def full_step(i, c):
        block(n_full - 1 - i, False)
        return c

    lax.fori_loop(0, n_full, full_step, 0)
    o_ref[0] = jnp.concatenate([jnp.where(own[2 * p], acc_ref[2 * p], acc_ref[2 * p + 1])
                                for p in range(SB_HEADS // 2)], axis=1).astype(o_ref.dtype)


def _sb_attention(q, k, v, *, off, tq, tk, out_dtype=BF16):
    B, Lq, W = q.shape
    Lk = k.shape[1]
    assert W == BRANCH_W and Lq % tq == 0 and Lk % tk == 0 and tk % 128 == 0
    nq = Lq // tq
    assert nq == 1 or (tq % tk == 0 and off % tk == 0)
    n_diag = -(-((off % tk) + tq - 1) // tk)
    assert (off + Lq - 1 + tk - 1) // tk <= Lk // tk
    body = partial(_sb_attn_body, tq=tq, tk=tk, off=off, n_diag=n_diag)
    return pl.pallas_call(
        body,
        grid=(B, nq),
        in_specs=[pl.BlockSpec((1, tq, W), lambda b, i: (b, i, 0)),
                  pl.BlockSpec((1, Lk, W), lambda b, i: (b, 0, 0)),
                  pl.BlockSpec((1, Lk, W), lambda b, i: (b, 0, 0))],
        out_specs=pl.BlockSpec((1, tq, W), lambda b, i: (b, i, 0)),
        out_shape=jax.ShapeDtypeStruct((B, Lq, W), out_dtype),
        scratch_shapes=[pltpu.VMEM((SB_HEADS, tq, LANES), F32), pltpu.VMEM((SB_HEADS, tq, LANES), F32)],
        compiler_params=pltpu.CompilerParams(dimension_semantics=("arbitrary", "arbitrary"),
                                             vmem_limit_bytes=VMEM_LIMIT_V7X),
        name="sb_attention",
    )(q, k, v)


RW_SUB = 16


def _split(x):
    hi = x.astype(BF16)
    lo = (x - hi.astype(F32)).astype(BF16)
    return hi, lo


def _mm3(a, b):
    ah, al = _split(a)
    bh, bl = _split(b)
    d = partial(jnp.dot, preferred_element_type=F32)
    return d(ah, bh) + (d(ah, bl) + d(al, bh))


def _sum01(x, m01):
    xh, xl = _split(x)
    return jnp.dot(xh, m01, preferred_element_type=F32) + jnp.dot(xl, m01, preferred_element_type=F32)


def _sum01_left(m01, x):
    xh, xl = _split(x)
    return jnp.dot(m01, xh, preferred_element_type=F32) + jnp.dot(m01, xl, preferred_element_type=F32)


def _dot_nt(a, b):
    return lax.dot_general(a, b, (((1,), (1,)), ((), ())), preferred_element_type=F32)


def _dot_tn(a, b):
    return lax.dot_general(a, b, (((0,), (0,)), ((), ())), preferred_element_type=F32)


def _rwkv_body(z_ref, shift_ref, s0_ref, mu_ref, w0_ref, w2_ref, a0_ref, a2_ref, g2_ref, kk_ref, ka_ref,
               rk_ref, lng_ref, lnb_ref, o_ref, s_out_ref, shift_out_ref, state_ref, prev_ref, *, T):
    ci = pl.program_id(1)
    nc = pl.num_programs(1)
    W = BRANCH_W
    N = RW_HEADS * T

    @pl.when(ci == 0)
    def _():
        state_ref[...] = s0_ref[0]
        prev_ref[...] = shift_ref[0]

    z = z_ref[0]
    row = lax.broadcasted_iota(jnp.int32, (T, RW_COLS), 0)
    z_prev = jnp.where(row == 0, jnp.broadcast_to(prev_ref[...], (T, RW_COLS)), pltpu.roll(z, 1, axis=0))
    prev_ref[...] = z[T - 1:T, :]
    zs = z + (z_prev - z) * mu_ref[...]
    r = zs[:, 0:W]
    k = zs[:, W:2 * W]
    v = zs[:, 2 * W:3 * W]
    xwa = zs[:, 3 * W:3 * W + 128]
    xg = zs[:, 3 * W + 128:]
    w_log = -jax.nn.softplus(-(w0_ref[...] + _mm3(jnp.tanh(xwa), w2_ref[...]))) - 0.5
    logdec = -jnp.exp(w_log)
    a = jax.nn.sigmoid(a0_ref[...] + _mm3(xwa, a2_ref[...]))
    g = _mm3(jax.nn.sigmoid(xg), g2_ref[...])
    li = lax.broadcasted_iota(jnp.int32, (W, W), 0) // RW_HD
    lj = lax.broadcasted_iota(jnp.int32, (W, W), 1) // RW_HD
    head_sum = (li == lj).astype(BF16)
    kk = k * kk_ref[...]
    kk = kk * lax.rsqrt(_sum01(kk * kk, head_sum) + 1e-12)
    k2 = k * (1.0 + (a - 1.0) * ka_ref[...])
    beta = kk * a
    ti = lax.broadcasted_iota(jnp.int32, (T, T), 0)
    tj = lax.broadcasted_iota(jnp.int32, (T, T), 1)
    b = _sum01_left((ti >= tj).astype(BF16), logdec)
    b_last = b[T - 1:T, :]
    k_in = kk * jnp.exp(b - logdec)
    r_in = r * jnp.exp(b)
    inv_p = jnp.exp(-b)
    k_out = k2 * inv_p
    b_out = beta * inv_p
    to_end = jnp.exp(b_last - b)
    k_end = k2 * to_end
    b_end = beta * to_end

    sh = lax.broadcasted_iota(jnp.int32, (N, W), 0) // T
    sl = lax.broadcasted_iota(jnp.int32, (N, W), 1) // RW_HD
    own = sh == sl

    def stack(x, masked):
        xs = jnp.concatenate([x] * RW_HEADS, axis=0)
        return jnp.where(own, xs, 0.0) if masked else xs

    k_in_s = stack(k_in, True).astype(BF16)
    r_in_s = stack(r_in, True).astype(BF16)
    k_out_s = stack(k_out, False).astype(BF16)
    b_out_s = stack(b_out, False).astype(BF16)
    v_s = stack(v, True).astype(BF16)
    k_end_s = stack(k_end, True).astype(BF16)
    b_end_s = stack(b_end, True).astype(BF16)

    ri = lax.broadcasted_iota(jnp.int32, (N, N), 0)
    rj = lax.broadcasted_iota(jnp.int32, (N, N), 1)
    same_head = (ri // T) == (rj // T)
    strict = same_head & (ri > rj)
    incl = same_head & (ri >= rj)
    a_mat = jnp.where(strict, _dot_nt(k_in_s, b_out_s), 0.0)
    kk_mat = jnp.where(strict, _dot_nt(k_in_s, k_out_s), 0.0)
    rk_mat = jnp.where(incl, _dot_nt(r_in_s, k_out_s), 0.0)
    rb_mat = jnp.where(incl, _dot_nt(r_in_s, b_out_s), 0.0)

    eye = (ri == rj).astype(F32)
    a_bd = jnp.where((ri // RW_SUB) == (rj // RW_SUB), a_mat, 0.0)
    x = eye - a_bd
    p = _mm3(a_bd, a_bd)
    x = x + _mm3(x, p)
    p = _mm3(p, p)
    x = x + _mm3(x, p)
    p = _mm3(p, p)
    x = x + _mm3(x, p)
    size = RW_SUB
    while size < T:
        lower = ((ri // (2 * size)) == (rj // (2 * size))) & ((ri // size) > (rj // size))
        x = x - _mm3(_mm3(x, jnp.where(lower, a_mat, 0.0)), x)
        size *= 2

    state = state_ref[...]
    state_b = state.astype(BF16)
    d = partial(jnp.dot, preferred_element_type=F32)
    rhs = _dot_nt(k_in_s, state_b) + d(kk_mat.astype(BF16), v_s)
    u = _mm3(x, rhs)
    u_b = u.astype(BF16)
    o_s = _dot_nt(r_in_s, state_b) + d(rk_mat.astype(BF16), v_s) - d(rb_mat.astype(BF16), u_b)
    o = o_s[0:T]
    for h in range(1, RW_HEADS):
        o = o + o_s[h * T:(h + 1) * T]
    state_ref[...] = state * jnp.exp(b_last) + _dot_tn(v_s, k_end_s) - _dot_tn(u_b, b_end_s)

    mean = _sum01(o, head_sum) * (1.0 / RW_HD)
    var = _sum01(jnp.square(o - mean), head_sum) * (1.0 / RW_HD)
    o = (o - mean) * lax.rsqrt(var + RW_GN_EPS) * lng_ref[...] + lnb_ref[...]
    bonus = _sum01(r * k2 * rk_ref[...], head_sum) * v
    o_ref[0] = ((o + bonus) * g).astype(o_ref.dtype)

    @pl.when(ci == nc - 1)
    def _():
        s_out_ref[0] = state_ref[...]
        shift_out_ref[0] = z[T - 1:T, :]


def _rwkv7_pallas(z, s0_bd, shift_prev, mu, w0, w2, a0, a2, g2, k_k, k_a, r_k, ln_g, ln_b, *, T, out_dtype=F32):
    B, L, C = z.shape
    assert C == RW_COLS and L % T == 0 and T % RW_SUB == 0
    W = BRANCH_W
    w2p = jnp.concatenate([w2, jnp.zeros_like(w2)], axis=0)
    a2p = jnp.concatenate([jnp.zeros_like(a2), a2], axis=0)
    vec = lambda t: t.reshape(1, -1).astype(F32)
    full = lambda shape: pl.BlockSpec(shape, lambda b, c: (0,) * len(shape))
    return pl.pallas_call(
        partial(_rwkv_body, T=T),
        grid=(B, L // T),
        in_specs=[pl.BlockSpec((1, T, C), lambda b, c: (b, c, 0)),
                  pl.BlockSpec((1, 1, C), lambda b, c: (b, 0, 0)),
                  pl.BlockSpec((1, W, W), lambda b, c: (b, 0, 0)),
                  full((1, C)), full((1, W)), full((128, W)), full((1, W)), full((128, W)), full((128, W)),
                  full((1, W)), full((1, W)), full((1, W)), full((1, W)), full((1, W))],
        out_specs=[pl.BlockSpec((1, T, W), lambda b, c: (b, c, 0)),
                   pl.BlockSpec((1, W, W), lambda b, c: (b, 0, 0)),
                   pl.BlockSpec((1, 1, C), lambda b, c: (b, 0, 0))],
        out_shape=[jax.ShapeDtypeStruct((B, L, W), out_dtype),
                   jax.ShapeDtypeStruct((B, W, W), F32),
                   jax.ShapeDtypeStruct((B, 1, C), F32)],
        scratch_shapes=[pltpu.VMEM((W, W), F32), pltpu.VMEM((1, C), F32)],
        compiler_params=pltpu.CompilerParams(dimension_semantics=("arbitrary", "arbitrary")),
        name="rwkv7_chunked",
    )(z, shift_prev.reshape(B, 1, C), s0_bd, vec(mu), vec(w0), w2p, vec(a0), a2p, g2, vec(k_k), vec(k_a),
      vec(r_k), vec(ln_g), vec(ln_b))


def _state_to_bd(s):
    B = s.shape[0]
    eye = jnp.eye(RW_HEADS, dtype=s.dtype)
    return (s[:, :, :, None, :] * eye[None, :, None, :, None]).reshape(B, BRANCH_W, BRANCH_W)


def _state_from_bd(s_bd):
    B = s_bd.shape[0]
    s5 = s_bd.reshape(B, RW_HEADS, RW_HD, RW_HEADS, RW_HD)
    return jnp.stack([s5[:, h, :, h, :] for h in range(RW_HEADS)], axis=1)


def _hgrn_body(z_ref, s0_ref, loglb_ref, log1mlb_ref, ng_ref, o_ref, s_out_ref, state_ref, *, T):
    ci = pl.program_id(1)
    nc = pl.num_programs(1)
    W = BRANCH_W

    @pl.when(ci == 0)
    def _():
        state_ref[...] = s0_ref[0]

    z = z_ref[0]
    zq, zf, zi, zg = z[:, 0:W], z[:, W:2 * W], z[:, 2 * W:3 * W], z[:, 3 * W:]
    log_sig = jnp.minimum(zf, 0.0) - jnp.log(1.0 + jnp.exp(-jnp.abs(zf)))
    a = loglb_ref[...]
    bv = log1mlb_ref[...] + log_sig
    m = jnp.maximum(a, bv)
    log_f = m + jnp.log(jnp.exp(a - m) + jnp.exp(bv - m))
    k = 1.0 - jnp.exp(log_f)
    q = zq * jax.nn.sigmoid(zq)
    ti = lax.broadcasted_iota(jnp.int32, (T, T), 0)
    tj = lax.broadcasted_iota(jnp.int32, (T, T), 1)
    b = _sum01_left((ti >= tj).astype(BF16), log_f)
    li = lax.broadcasted_iota(jnp.int32, (W, W), 0) // HG_DK
    lj = lax.broadcasted_iota(jnp.int32, (W, W), 1) // HG_DK
    same_head = li == lj
    head_sum = same_head.astype(BF16)

    prods, first = [], []
    for s in range(T):
        r0 = (s // SUBLANES) * SUBLANES
        t_idx = r0 + lax.broadcasted_iota(jnp.int32, (T - r0, W), 0)
        e = jnp.exp(jnp.where(t_idx >= s, b[r0:, :] - b[s:s + 1, :], -jnp.inf))
        prods.append((q[r0:, :] * e * k[s:s + 1, :]).astype(BF16))
        first.append(r0)
    scores = jnp.dot(jnp.concatenate(prods, axis=0), head_sum, preferred_element_type=F32)
    groups = [jnp.zeros((SUBLANES, W), F32) for _ in range(T // SUBLANES)]
    at = 0
    for s in range(T):
        for gi in range(first[s] // SUBLANES, T // SUBLANES):
            lo = at + gi * SUBLANES - first[s]
            groups[gi] = groups[gi] + scores[lo:lo + SUBLANES, :] * zi[s:s + 1, :]
        at += T - first[s]
    o = jnp.concatenate(groups, axis=0)

    state = state_ref[...]
    o = o + _dot_nt((q * jnp.exp(b)).astype(BF16), state.astype(BF16))
    b_last = b[T - 1:T, :]
    k_end = (k * jnp.exp(b_last - b)).astype(BF16)
    upd = _dot_tn(zi.astype(BF16), k_end)
    state_ref[...] = state * jnp.exp(b_last) + jnp.where(same_head, upd, 0.0)

    ms = _sum01(o * o, head_sum) * (1.0 / HG_DK)
    o = o * lax.rsqrt(ms + EPS) * ng_ref[...]
    o_ref[0] = (o * (zg * jax.nn.sigmoid(zg))).astype(o_ref.dtype)

    @pl.when(ci == nc - 1)
    def _():
        s_out_ref[0] = state_ref[...]


def _hgrn2_pallas(z, s0_bd, lb, norm_g, *, T, out_dtype=F32):
    B, L, C = z.shape
    assert C == HG_COLS and L % T == 0
    W = BRANCH_W
    full = lambda shape: pl.BlockSpec(shape, lambda b, c: (0,) * len(shape))
    return pl.pallas_call(
        partial(_hgrn_body, T=T),
        grid=(B, L // T),
        in_specs=[pl.BlockSpec((1, T, C), lambda b, c: (b, c, 0)),
                  pl.BlockSpec((1, W, W), lambda b, c: (b, 0, 0)),
                  full((1, W)), full((1, W)), full((1, W))],
        out_specs=[pl.BlockSpec((1, T, W), lambda b, c: (b, c, 0)),
                   pl.BlockSpec((1, W, W), lambda b, c: (b, 0, 0))],
        out_shape=[jax.ShapeDtypeStruct((B, L, W), out_dtype),
                   jax.ShapeDtypeStruct((B, W, W), F32)],
        scratch_shapes=[pltpu.VMEM((W, W), F32)],
        compiler_params=pltpu.CompilerParams(dimension_semantics=("arbitrary", "arbitrary")),
        name="hgrn2_chunked",
    )(z, s0_bd, jnp.log(lb).reshape(1, W), jnp.log1p(-lb).reshape(1, W), jnp.tile(norm_g, HG_HEADS).reshape(1, W))


PROJ_TM = 256
PROJ_VMEM_LIMIT_V7X = 48 * 1024 * 1024
CONV_CARRY = 32
CONV_TC = 512


def _in_proj_body(x_ref, g_ref, w_ref, q_ref, k_ref, v_ref, kb_ref, vb_ref, hg_ref, cv_ref, rw_ref, gate_ref):
    x = x_ref[...]
    h = (x * lax.rsqrt(jnp.mean(x * x, axis=-1, keepdims=True) + EPS) * g_ref[...]).astype(BF16)
    W = BRANCH_W

    def proj(a, b):
        return jnp.dot(h, w_ref[:, a:b], preferred_element_type=F32)

    q_ref[...] = (proj(0, W) * SB_Q_SCALE).astype(BF16)
    k = proj(W, 2 * W)
    v = proj(2 * W, 3 * W)
    k_ref[...] = k
    v_ref[...] = v
    kb_ref[...] = k.astype(BF16)
    vb_ref[...] = v.astype(BF16)
    c = SB_COLS
    hg_ref[...] = proj(c, c + HG_COLS)
    c += HG_COLS
    cv_ref[...] = proj(c, c + CV_COLS)
    c += CV_COLS
    rw_ref[...] = proj(c, c + RW_COLS)
    c += RW_COLS
    for n in range(N_BRANCH):
        gate_ref[:, n * D_MODEL:(n + 1) * D_MODEL] = jax.nn.sigmoid(
            proj(c + n * D_MODEL, c + (n + 1) * D_MODEL)).astype(BF16)


def _in_proj(x2, norm_g, w_in_b):
    n = x2.shape[0]
    TM = min(PROJ_TM, n)
    assert n % TM == 0
    row = lambda w: pl.BlockSpec((TM, w), lambda i: (i, 0))
    shp = lambda w, dt: jax.ShapeDtypeStruct((n, w), dt)
    W = BRANCH_W
    return pl.pallas_call(
        _in_proj_body,
        grid=(n // TM,),
        in_specs=[row(D_MODEL), pl.BlockSpec((1, D_MODEL), lambda i: (0, 0)),
                  pl.BlockSpec((D_MODEL, IN_COLS), lambda i: (0, 0), pipeline_mode=pl.Buffered(1))],
        out_specs=[row(W), row(W), row(W), row(W), row(W), row(HG_COLS), row(CV_COLS), row(RW_COLS), row(GATE_COLS)],
        out_shape=[shp(W, BF16), shp(W, F32), shp(W, F32), shp(W, BF16), shp(W, BF16),
                   shp(HG_COLS, F32), shp(CV_COLS, F32), shp(RW_COLS, F32), shp(GATE_COLS, BF16)],
        compiler_params=pltpu.CompilerParams(dimension_semantics=("arbitrary",), vmem_limit_bytes=PROJ_VMEM_LIMIT_V7X),
        name="in_proj",
    )(x2, norm_g.reshape(1, -1), w_in_b)


def _merge_body(x_ref, sb_ref, hg_ref, cv_ref, rw_ref, gate_ref, wb_ref, wo_ref, o_ref):
    merged = None
    for n, br in enumerate((sb_ref, hg_ref, cv_ref, rw_ref)):
        t = jnp.dot(br[...].astype(BF16), wb_ref[n], preferred_element_type=F32)
        t = t * gate_ref[:, n * D_MODEL:(n + 1) * D_MODEL].astype(F32)
        merged = t if merged is None else merged + t
    o_ref[...] = x_ref[...] + jnp.dot(merged.astype(BF16), wo_ref[...], preferred_element_type=F32)


def _merge(x2, o_sb, o_hg, o_cv, o_rw, gates, wb_b, wo_b):
    n = x2.shape[0]
    TM = min(PROJ_TM, n)
    row = lambda w: pl.BlockSpec((TM, w), lambda i: (i, 0))
    return pl.pallas_call(
        _merge_body,
        grid=(n // TM,),
        in_specs=[row(D_MODEL), row(BRANCH_W), row(BRANCH_W), row(BRANCH_W), row(BRANCH_W), row(GATE_COLS),
                  pl.BlockSpec((N_BRANCH, BRANCH_W, D_MODEL), lambda i: (0, 0, 0)),
                  pl.BlockSpec((D_MODEL, D_MODEL), lambda i: (0, 0))],
        out_specs=row(D_MODEL),
        out_shape=jax.ShapeDtypeStruct((n, D_MODEL), F32),
        compiler_params=pltpu.CompilerParams(dimension_semantics=("arbitrary",), vmem_limit_bytes=PROJ_VMEM_LIMIT_V7X),
        name="branch_merge",
    )(x2, o_sb, o_hg, o_cv, o_rw, gates, wb_b, wo_b)


def _conv_body(z_ref, buf_ref, w_ref, b_ref, lg_ref, lb_ref, o_ref, buf_out_ref, hp_ref, *, TC):
    ci = pl.program_id(1)
    nc = pl.num_programs(1)

    @pl.when(ci == 0)
    def _():
        hp_ref[0:CONV_CARRY, :] = buf_ref[0]

    z = z_ref[0]
    hp_ref[CONV_CARRY:, :] = z[:, 0:CONV_CH] * jax.nn.sigmoid(z[:, CONV_CH:])
    first = CONV_CARRY - (CONV_WIDTH - 1)
    y = jnp.zeros((TC, CONV_CH), F32) + b_ref[...]
    for j in range(CONV_WIDTH):
        y = y + hp_ref[first + j:first + j + TC, :] * w_ref[j:j + 1, :]
    mu = jnp.mean(y, axis=-1, keepdims=True)
    d = y - mu
    var = jnp.mean(d * d, axis=-1, keepdims=True)
    y = d * lax.rsqrt(var + EPS) * lg_ref[...] + lb_ref[...]
    o_ref[0] = (y * jax.nn.sigmoid(y)).astype(o_ref.dtype)
    tail = hp_ref[TC:TC + CONV_CARRY, :]

    @pl.when(ci == nc - 1)
    def _():
        buf_out_ref[0] = tail

    hp_ref[0:CONV_CARRY, :] = tail


def _conformer_conv_pallas(z_cv, buf, w, b, ln_g, ln_b, *, TC, out_dtype=F32):
    B, L, _ = z_cv.shape
    assert L % TC == 0 and TC % 8 == 0 and TC >= CONV_CARRY
    C = CONV_CH
    pad = CONV_CARRY - (CONV_WIDTH - 1)
    buf_p = jnp.pad(buf.astype(F32), ((0, 0), (pad, 0), (0, 0)))
    w_p = jnp.pad(w.astype(F32), ((0, 32 - CONV_WIDTH), (0, 0)))
    full = lambda shape: pl.BlockSpec(shape, lambda b, c: (0,) * len(shape))
    y, buf_new = pl.pallas_call(
        partial(_conv_body, TC=TC),
        grid=(B, L // TC),
        in_specs=[pl.BlockSpec((1, TC, 2 * C), lambda b, c: (b, c, 0)),
                  pl.BlockSpec((1, CONV_CARRY, C), lambda b, c: (b, 0, 0)),
                  full((32, C)), full((1, C)), full((1, C)), full((1, C))],
        out_specs=[pl.BlockSpec((1, TC, C), lambda b, c: (b, c, 0)),
                   pl.BlockSpec((1, CONV_CARRY, C), lambda b, c: (b, 0, 0))],
        out_shape=[jax.ShapeDtypeStruct((B, L, C), out_dtype), jax.ShapeDtypeStruct((B, CONV_CARRY, C), F32)],
        scratch_shapes=[pltpu.VMEM((CONV_CARRY + TC, C), F32)],
        compiler_params=pltpu.CompilerParams(dimension_semantics=("arbitrary", "arbitrary")),
        name="conformer_conv",
    )(z_cv, buf_p, w_p, b.reshape(1, C), ln_g.reshape(1, C), ln_b.reshape(1, C))
    return y, buf_new[:, pad:, :]


PK_PAIRS = PK_HEADS * PK_TOPK
PEER_HEAD_GROUP = 4
PEER_TB = 128
ROW_TILES = D_MODEL // LANES
PEER_VMEM_LIMIT_V7X = 52 * 1024 * 1024


def _topk_rows(scores, payloads, k):
    n = scores[0].shape[0]
    rows = lax.broadcasted_iota(jnp.int32, scores[0].shape, 0).astype(F32)
    scores = list(scores)
    vals = [[] for _ in scores]
    ids = [[] for _ in scores]
    for _ in range(k):
        for c, s in enumerate(scores):
            m = jnp.max(s, axis=0, keepdims=True)
            pos = jnp.min(jnp.where(s == m, rows, float(n)), axis=0, keepdims=True)
            sel = rows == pos
            ids[c].append(pos if payloads[c] is None
                          else jnp.max(jnp.where(sel, payloads[c], -1.0), axis=0, keepdims=True))
            vals[c].append(m)
            scores[c] = jnp.where(sel, -jnp.inf, s)
    return [(jnp.concatenate(v, axis=0), jnp.concatenate(i, axis=0)) for v, i in zip(vals, ids)]


def _dot3(ah, al, bh, bl, dims):
    d = partial(lax.dot_general, dimension_numbers=dims, preferred_element_type=F32)
    return d(ah, bh) + (d(ah, bl) + d(al, bh))


def _candidate_tables(tb):
    groups = [[0], [1], [2], [3, 4], [5, 6, 7], list(range(8, 16))]
    rows = []
    for grp in groups:
        part = [(a, b) for a in grp for b in range(PK_TOPK // (a + 1))]
        rows += part + [None] * ((-len(part)) % SUBLANES)
    n = len(rows)
    sel1 = [[1.0 if (r is not None and r[0] == j) else 0.0 for j in range(PK_TOPK)] for r in rows]
    sel2 = [[1.0 if (r is not None and r[1] == j) else 0.0 for j in range(PK_TOPK)] for r in rows]
    cpos = [[-1.0 if r is None else float(r[0] * PK_TOPK + r[1])] * tb for r in rows]
    return jnp.array(sel1, BF16), jnp.array(sel2, BF16), jnp.array(cpos, F32), n


def _pick_rows(sel, v):
    hi = v.astype(BF16)
    r1 = v - hi.astype(F32)
    mid = r1.astype(BF16)
    lo = (r1 - mid.astype(F32)).astype(BF16)
    d = partial(jnp.dot, preferred_element_type=F32)
    return (d(sel, hi) + d(sel, mid)) + d(sel, lo)


def _peer_route_body(x_ref, g_ref, wqh_ref, wql_ref, qg_ref, k1_ref, k2_ref, sel1_ref, sel2_ref, cpos_ref,
                     xn_ref, idx_ref, idxc_ref, gate_ref):
    x = x_ref[...]
    xn = x * lax.rsqrt(jnp.mean(x * x, axis=-1, keepdims=True) + EPS) * g_ref[...]
    xn_ref[...] = xn
    nn = (((1,), (0,)), ((), ()))
    nt = (((1,), (1,)), ((), ()))
    xh, xl = _split(xn)
    q = _dot3(xh, xl, wqh_ref[...], wql_ref[...], nn)
    half = PK_DQ // 2
    experts, gates = [], []
    for h0 in range(0, PK_HEADS, PEER_HEAD_GROUP):
        heads = range(h0, h0 + PEER_HEAD_GROUP)
        subs = []
        for h in heads:
            qh = q[:, h * PK_DQ:(h + 1) * PK_DQ]
            qh = qh * lax.rsqrt(jnp.mean(qh * qh, axis=-1, keepdims=True) + EPS) * qg_ref[...]
            qhh, qhl = _split(qh)
            k1h, k1l = _split(k1_ref[h])
            k2h, k2l = _split(k2_ref[h])
            subs.append(_dot3(k1h, k1l, qhh[:, :half], qhl[:, :half], nt))
            subs.append(_dot3(k2h, k2l, qhh[:, half:], qhl[:, half:], nt))
        tops = _topk_rows(subs, [None] * len(subs), PK_TOPK)
        cands = []
        for j in range(PEER_HEAD_GROUP):
            (v1, _), (v2, _) = tops[2 * j], tops[2 * j + 1]
            cands.append(jnp.where(cpos_ref[...] < 0.0, -jnp.inf,
                                   _pick_rows(sel1_ref[...], v1) + _pick_rows(sel2_ref[...], v2)))
        best = _topk_rows(cands, [cpos_ref[...]] * PEER_HEAD_GROUP, PK_TOPK)
        for j in range(PEER_HEAD_GROUP):
            i1, i2 = tops[2 * j][1], tops[2 * j + 1][1]
            top_s, pos = best[j]
            pa = jnp.floor(pos * (1.0 / PK_TOPK))
            pb = pos - pa * PK_TOPK
            e1 = jnp.zeros_like(pos)
            e2 = jnp.zeros_like(pos)
            for a in range(PK_TOPK):
                e1 = jnp.where(pa == a, i1[a:a + 1, :], e1)
                e2 = jnp.where(pb == a, i2[a:a + 1, :], e2)
            e = jnp.exp(top_s - jnp.max(top_s, axis=0, keepdims=True))
            gates.append(e / jnp.sum(e, axis=0, keepdims=True))
            experts.append(e1 * float(PK_NKEYS) + e2)
    expert = jnp.concatenate(experts, axis=0).astype(jnp.int32)
    row = (expert + 1) * HALF_TILES
    pair = lax.broadcasted_iota(jnp.int32, row.shape, 0)
    idx_ref[...] = row.T
    idxc_ref[...] = (row - (pair & HALF_TILES)).T
    gate_ref[...] = jnp.concatenate(gates, axis=0)


def _peer_route(x2, norm_g, wq, qn_g, k1, k2):
    n_tok = x2.shape[0]
    TB = PEER_TB
    assert n_tok % TB == 0
    nb = n_tok // TB
    full = lambda shape: pl.BlockSpec(shape, lambda i: (0,) * len(shape))
    wq_hi = wq.astype(BF16)
    wq_lo = (wq - wq_hi.astype(F32)).astype(BF16)
    sel1, sel2, cpos, n_cand = _candidate_tables(TB)
    return pl.pallas_call(
        _peer_route_body,
        grid=(nb,),
        in_specs=[pl.BlockSpec((TB, D_MODEL), lambda i: (i, 0)), full((1, D_MODEL)),
                  full((D_MODEL, PK_HEADS * PK_DQ)), full((D_MODEL, PK_HEADS * PK_DQ)), full((1, PK_DQ)),
                  full((PK_HEADS, PK_NKEYS, PK_DQ // 2)), full((PK_HEADS, PK_NKEYS, PK_DQ // 2)),
                  full((n_cand, PK_TOPK)), full((n_cand, PK_TOPK)), full((n_cand, TB))],
        out_specs=[pl.BlockSpec((TB, D_MODEL), lambda i: (i, 0)),
                   pl.BlockSpec((TB, PK_PAIRS), lambda i: (i, 0)),
                   pl.BlockSpec((TB, PK_PAIRS), lambda i: (i, 0)),
                   pl.BlockSpec((PK_PAIRS, TB), lambda i: (0, i))],
        out_shape=[jax.ShapeDtypeStruct((n_tok, D_MODEL), F32),
                   jax.ShapeDtypeStruct((n_tok, PK_PAIRS), jnp.int32),
                   jax.ShapeDtypeStruct((n_tok, PK_PAIRS), jnp.int32),
                   jax.ShapeDtypeStruct((PK_PAIRS, n_tok), F32)],
        compiler_params=pltpu.CompilerParams(dimension_semantics=("arbitrary",)),
        name="peer_route",
    )(x2, norm_g.reshape(1, -1), wq_hi, wq_lo, qn_g.reshape(1, -1), k1, k2, sel1, sel2, cpos)


HALF_TILES = ROW_TILES // 2


def _unpack_words(word):
    return pltpu.bitcast(word << 16, F32), pltpu.bitcast(word & jnp.int32(-65536), F32)


def _fold_couples(tiles, sub):
    m2 = (sub & 3) < 2
    lvl2 = []
    for a, b in ((tiles[0], tiles[2]), (tiles[1], tiles[3])):
        lvl2.append(jnp.where(m2, a, b) + jnp.where(m2, pltpu.roll(a, 6, axis=0), pltpu.roll(b, 2, axis=0)))
    m1 = (sub & 1) == 0
    a, b = lvl2
    return jnp.where(m1, a, b) + jnp.where(m1, pltpu.roll(a, 7, axis=0), pltpu.roll(b, 1, axis=0))


def _peer_act_body(idx_ref, x_ref, gate_ref, tab_ref, w_ref, part_ref):
    TB = x_ref.shape[0]
    sub = lax.broadcasted_iota(jnp.int32, (SUBLANES, LANES), 0)
    low = sub < HALF_TILES

    def token(t, c):
        x = x_ref[pl.ds(t, 1), :]
        tiles_x = [x[:, j * LANES:(j + 1) * LANES] for j in range(ROW_TILES)]
        x_lo = jnp.concatenate(tiles_x[:HALF_TILES] * 2, axis=0)
        x_hi = jnp.concatenate(tiles_x[HALF_TILES:] * 2, axis=0)
        start = t * PK_PAIRS
        for g in range(PK_PAIRS // SUBLANES):
            tiles = []
            for j in range(HALF_TILES):
                ra = pl.multiple_of(idx_ref[0, 0, start + g * SUBLANES + j], HALF_TILES)
                rb = pl.multiple_of(idx_ref[0, 0, start + g * SUBLANES + j + HALF_TILES], HALF_TILES)
                word = jnp.where(low, tab_ref[pl.ds(ra, SUBLANES), :], tab_ref[pl.ds(rb, SUBLANES), :])
                lo, hi = _unpack_words(word)
                tiles.append(lo * x_lo + hi * x_hi)
            part_ref[t, g * SUBLANES:(g + 1) * SUBLANES, :] = _fold_couples(tiles, sub)
        return c

    lax.fori_loop(0, TB, token, 0)
    lane = lax.broadcasted_iota(jnp.int32, (PK_PAIRS, TB), 1)
    s = jnp.zeros((PK_PAIRS, TB), F32)
    ones = jnp.ones((LANES, LANES), BF16)
    for t in range(TB):
        tot = jnp.dot(part_ref[t].astype(BF16), ones, preferred_element_type=F32)
        s = jnp.where(lane == t, tot, s)
    w_ref[...] = gate_ref[...] * jax.nn.gelu(s)


PEER_MIX_ACCS = 4


def _peer_mix_body(idx_ref, w_ref, tab_ref, y_ref, wrep_ref):
    TB = y_ref.shape[0]
    w_all = w_ref[...]
    for t in range(TB):
        wrep_ref[t] = jnp.broadcast_to(w_all[:, t:t + 1], (PK_PAIRS, LANES))

    def token(t, c):
        start = t * PK_PAIRS
        acc_lo = [jnp.zeros((HALF_TILES, LANES), F32) for _ in range(PEER_MIX_ACCS)]
        acc_hi = [jnp.zeros((HALF_TILES, LANES), F32) for _ in range(PEER_MIX_ACCS)]
        for p in range(PK_PAIRS):
            row = pl.multiple_of(idx_ref[0, 0, start + p], HALF_TILES)
            lo, hi = _unpack_words(tab_ref[pl.ds(row, HALF_TILES), :])
            w = jnp.broadcast_to(wrep_ref[t, p:p + 1, :], (HALF_TILES, LANES))
            acc_lo[p % PEER_MIX_ACCS] = acc_lo[p % PEER_MIX_ACCS] + w * lo
            acc_hi[p % PEER_MIX_ACCS] = acc_hi[p % PEER_MIX_ACCS] + w * hi
        y_ref[t] = jnp.concatenate([(acc_lo[0] + acc_lo[1]) + (acc_lo[2] + acc_lo[3]),
                                    (acc_hi[0] + acc_hi[1]) + (acc_hi[2] + acc_hi[3])], axis=0)
        return c

    lax.fori_loop(0, TB, token, 0)


def _pack_table(tab):
    bits = lax.bitcast_convert_type(tab.astype(BF16), jnp.uint16).astype(jnp.uint32)
    bits = bits.reshape(PK_EXPERTS, 2, HALF_TILES, LANES)
    word = bits[:, 0] | (bits[:, 1] << 16)
    rows = lax.bitcast_convert_type(word, jnp.int32).reshape(PK_EXPERTS * HALF_TILES, LANES)
    return jnp.pad(rows, ((HALF_TILES, HALF_TILES), (0, 0)))


def _smem_spec(TB):
    return pl.BlockSpec((1, 1, PK_PAIRS * TB), lambda i: (i, 0, 0), memory_space=pltpu.SMEM)


def _table_spec():
    return pl.BlockSpec(((PK_EXPERTS + 2) * HALF_TILES, LANES), lambda i: (0, 0), pipeline_mode=pl.Buffered(1))


def _peer_act(idx, xn, gate_t, tab_packed):
    n_tok = xn.shape[0]
    TB = PEER_TB
    nb = n_tok // TB
    return pl.pallas_call(
        _peer_act_body,
        grid=(nb,),
        in_specs=[_smem_spec(TB),
                  pl.BlockSpec((TB, D_MODEL), lambda i: (i, 0)),
                  pl.BlockSpec((PK_PAIRS, TB), lambda i: (0, i)),
                  _table_spec()],
        out_specs=pl.BlockSpec((PK_PAIRS, TB), lambda i: (0, i)),
        out_shape=jax.ShapeDtypeStruct((PK_PAIRS, n_tok), F32),
        scratch_shapes=[pltpu.VMEM((TB, PK_PAIRS, LANES), F32)],
        compiler_params=pltpu.CompilerParams(dimension_semantics=("arbitrary",),
                                             vmem_limit_bytes=PEER_VMEM_LIMIT_V7X),
        name="peer_act",
    )(idx.reshape(nb, 1, TB * PK_PAIRS), xn, gate_t, tab_packed)


def _peer_mix(idx, w_t, tab_packed):
    n_tok = w_t.shape[1]
    TB = PEER_TB
    nb = n_tok // TB
    return pl.pallas_call(
        _peer_mix_body,
        grid=(nb,),
        in_specs=[_smem_spec(TB),
                  pl.BlockSpec((PK_PAIRS, TB), lambda i: (0, i)),
                  _table_spec()],
        out_specs=pl.BlockSpec((TB, ROW_TILES, LANES), lambda i: (i, 0, 0)),
        out_shape=jax.ShapeDtypeStruct((n_tok, ROW_TILES, LANES), F32),
        scratch_shapes=[pltpu.VMEM((TB, PK_PAIRS, LANES), F32)],
        compiler_params=pltpu.CompilerParams(dimension_semantics=("arbitrary",),
                                             vmem_limit_bytes=PEER_VMEM_LIMIT_V7X),
        name="peer_mix",
    )(idx.reshape(nb, 1, TB * PK_PAIRS), w_t, tab_packed)


def _peer_ffn_pallas(x, norm_g, wq, qn_g, sub_keys, u_packed, v_packed):
    B, L, D = x.shape
    n_tok = B * L
    pad = (-n_tok) % PEER_TB
    x2 = jnp.pad(x.reshape(n_tok, D), ((0, pad), (0, 0)))
    xn, idx, idx_couple, gate_t = _peer_route(x2, norm_g, wq, qn_g, sub_keys[:, 0], sub_keys[:, 1])
    w_t = _peer_act(idx_couple, xn, gate_t, u_packed)
    y = _peer_mix(idx, w_t, v_packed)
    return y.reshape(-1, D)[:n_tok].reshape(B, L, D)


def _trunk_layer(x, k_past, v_past, s_hg, buf_conv, s_rw, buf_shift, p):
    B_, L, _ = x.shape
    n = B_ * L
    x2 = x.reshape(n, D_MODEL)
    qb, k, v, kb, vb, z_hg, z_cv, z_rw, gates = _in_proj(x2, p['norm_mix_g'], p['w_in'])
    seq = lambda t: t.reshape(B_, L, t.shape[-1])
    if k_past is None:
        o_sb = _sb_attention(seq(qb), seq(kb), seq(vb), off=0, tq=SB_TQ, tk=SB_TK, out_dtype=F32)
    else:
        past = k_past.shape[1]
        pad = jnp.zeros((B_, (-(past + L)) % SB_TK, BRANCH_W), BF16)
        kc = jnp.concatenate([k_past.reshape(B_, past, BRANCH_W).astype(BF16), seq(kb), pad], axis=1)
        vc = jnp.concatenate([v_past.reshape(B_, past, BRANCH_W).astype(BF16), seq(vb), pad], axis=1)
        o_sb = _sb_attention(seq(qb), kc, vc, off=past, tq=L, tk=SB_TK, out_dtype=F32)
    o_hg, s_hg_bd = _hgrn2_pallas(seq(z_hg), _state_to_bd(jnp.swapaxes(s_hg.astype(F32), 2, 3)), p['lb'],
                                  p['hg_norm_g'], T=CHUNK if L % CHUNK == 0 else L)
    s_hg_new = jnp.swapaxes(_state_from_bd(s_hg_bd), 2, 3)
    o_cv, buf_conv_new = _conformer_conv_pallas(seq(z_cv), buf_conv, p['conv_w'], p['conv_b'], p['conv_ln_g'],
                                                p['conv_ln_b'], TC=CONV_TC if L % CONV_TC == 0 else L)
    o_rw, s_rw_bd, shift_new = _rwkv7_pallas(seq(z_rw), _state_to_bd(s_rw.astype(F32)), buf_shift, p['rw_mu'],
                                             p['rw_w0'], p['rw_w2'], p['rw_a0'], p['rw_a2'], p['rw_g2'], p['rw_k_k'],
                                             p['rw_k_a'], p['rw_r_k'], p['rw_ln_g'], p['rw_ln_b'],
                                             T=RW_CHUNK if L % RW_CHUNK == 0 else L)
    s_rw_new = _state_from_bd(s_rw_bd)
    buf_shift_new = shift_new[:, 0]
    flat = lambda t: t.reshape(n, BRANCH_W)
    x = _merge(x2, flat(o_sb), flat(o_hg), flat(o_cv), flat(o_rw), gates, p['w_branch'], p['w_out']).reshape(x.shape)
    x = x + _peer_ffn_pallas(x, p['norm_ffn_g'], p['peer_wq'], p['peer_qn_g'], p['peer_keys'],
                             p['peer_u'], p['peer_v'])
    heads = lambda t: t.reshape(B_, L, SB_HEADS, HEAD_DIM)
    return x, (heads(k), heads(v), s_hg_new, buf_conv_new, s_rw_new, buf_shift_new)


def kernel(x_prompt, x_sample, cache_sb_k, cache_sb_v, state_hgrn, state_conv, state_rwkv, state_shift,
           norm_mix_g, w_in, hg_lb_logits, hg_norm_g, conv_w, conv_b, conv_ln_g, conv_ln_b,
           rw_mu, rw_w0, rw_w2, rw_a0, rw_a2, rw_g2, rw_k_k, rw_k_a, rw_r_k, rw_ln_g, rw_ln_b,
           w_branch, w_out, norm_ffn_g, peer_wq, peer_qn_g, peer_keys, peer_u, peer_v, final_norm_g):
    lb_all = jnp.cumsum(jax.nn.softmax(hg_lb_logits.astype(F32), axis=0), axis=0)
    lb_all = lb_all - lb_all[0:1]
    Bp = x_prompt.shape[0]
    hg0 = jnp.zeros((Bp, HG_HEADS, HG_DK, HG_DV), F32)
    cv0 = jnp.zeros((Bp, CONV_WIDTH - 1, CONV_CH), x_prompt.dtype)
    rw0 = jnp.zeros((Bp, RW_HEADS, RW_HD, RW_HD), F32)
    sh0 = jnp.zeros((Bp, RW_COLS), x_prompt.dtype)
    xp, xs = x_prompt, x_sample
    outs_p, outs_s = [], []
    for l in range(DEPTH):
        lp = dict(norm_mix_g=norm_mix_g[l], w_in=w_in[l].astype(BF16), lb=lb_all[l], hg_norm_g=hg_norm_g[l],
                  conv_w=conv_w[l], conv_b=conv_b[l], conv_ln_g=conv_ln_g[l], conv_ln_b=conv_ln_b[l],
                  rw_mu=rw_mu[l], rw_w0=rw_w0[l], rw_w2=rw_w2[l], rw_a0=rw_a0[l], rw_a2=rw_a2[l],
                  rw_g2=rw_g2[l], rw_k_k=rw_k_k[l], rw_k_a=rw_k_a[l], rw_r_k=rw_r_k[l],
                  rw_ln_g=rw_ln_g[l], rw_ln_b=rw_ln_b[l], w_branch=w_branch[l].astype(BF16),
                  w_out=w_out[l].astype(BF16),
                  norm_ffn_g=norm_ffn_g[l], peer_wq=peer_wq[l], peer_qn_g=peer_qn_g[l],
                  peer_keys=peer_keys[l], peer_u=_pack_table(peer_u[l]), peer_v=_pack_table(peer_v[l]))
        xp, st_p = _trunk_layer(xp, None, None, hg0, cv0, rw0, sh0, lp)
        xs, st_s = _trunk_layer(xs, cache_sb_k[l], cache_sb_v[l], state_hgrn[l], state_conv[l],
                                state_rwkv[l], state_shift[l], lp)
        outs_p.append(st_p)
        outs_s.append(st_s)

    def stk(outs, i):
        return jnp.stack([o[i] for o in outs], axis=0)

    y_prompt = _rmsnorm_pallas(xp, final_norm_g)
    y_sample = _rmsnorm_pallas(xs, final_norm_g)
    return (y_prompt, y_sample,
            stk(outs_p, 0), stk(outs_p, 1), stk(outs_p, 2), stk(outs_p, 3), stk(outs_p, 4), stk(outs_p, 5),
            stk(outs_s, 0), stk(outs_s, 1), stk(outs_s, 2), stk(outs_s, 3), stk(outs_s, 4), stk(outs_s, 5))
```

```python
from functools import partial

import jax
import jax.numpy as jnp
from jax import lax
from jax.experimental import pallas as pl
from jax.experimental.pallas import tpu as pltpu

D_MODEL = 1024
DEPTH = 2
CHUNK = 64
HEAD_DIM = 64
N_BRANCH = 4
BRANCH_W = D_MODEL // 4
SB_HEADS = BRANCH_W // HEAD_DIM
HG_HEADS = 4
HG_DK = BRANCH_W // HG_HEADS
HG_DV = BRANCH_W // HG_HEADS
CONV_CH = BRANCH_W
CONV_WIDTH = 31
RW_HEADS = 4
RW_HD = BRANCH_W // RW_HEADS
RW_DECAY_LORA = 64
RW_AAA_LORA = 64
RW_GATE_LORA = 128
SB_COLS = 3 * BRANCH_W
HG_COLS = 4 * BRANCH_W
CV_COLS = 2 * CONV_CH
RW_COLS = 3 * BRANCH_W + RW_DECAY_LORA + RW_AAA_LORA + RW_GATE_LORA
GATE_COLS = N_BRANCH * D_MODEL
IN_COLS = SB_COLS + HG_COLS + CV_COLS + RW_COLS + GATE_COLS
PK_HEADS = 8
PK_NKEYS = 128
PK_EXPERTS = PK_NKEYS * PK_NKEYS
PK_DQ = 256
PK_TOPK = 16
EPS = 1e-6
RW_GN_EPS = 64e-5
SB_TQ = 256
SB_TK = 256
RW_CHUNK = 64
F32 = jnp.float32
BF16 = jnp.bfloat16
SUBLANES = 8
LANES = 128


def _rmsnorm_body(x_ref, g_ref, o_ref):
    x = x_ref[...]
    y = x * lax.rsqrt(jnp.mean(x * x, axis=-1, keepdims=True) + EPS)
    o_ref[...] = y * g_ref[...]


def _rmsnorm_pallas(x, g, rows=512):
    shape = x.shape
    x2 = x.reshape(-1, shape[-1])
    n, d = x2.shape
    rows = min(rows, n)
    out = pl.pallas_call(
        _rmsnorm_body,
        grid=(n // rows,),
        in_specs=[pl.BlockSpec((rows, d), lambda i: (i, 0)),
                  pl.BlockSpec((1, d), lambda i: (0, 0))],
        out_specs=pl.BlockSpec((rows, d), lambda i: (i, 0)),
        out_shape=jax.ShapeDtypeStruct((n, d), x.dtype),
        name="final_rmsnorm",
    )(x2, g.reshape(1, d))
    return out.reshape(shape)


VMEM_LIMIT_V7X = 48 * 1024 * 1024
SB_Q_SCALE = -(HEAD_DIM ** -0.5) * 1.4426950408889634


def _sb_attn_body(q_ref, k_ref, v_ref, o_ref, acc_ref, carry_ref, *, tq, tk, off, n_diag):
    qi = pl.program_id(1)
    q0 = off + qi * tq
    n_full = q0 // tk
    acc_ref[...] = jnp.zeros_like(acc_ref)
    carry_ref[...] = jnp.zeros_like(carry_ref)
    q_all = q_ref[0]
    lane = lax.broadcasted_iota(jnp.int32, (tq, LANES), 1)
    own = [(lane // HEAD_DIM) == (h % 2) for h in range(SB_HEADS)]
    q_heads = [jnp.where(own[h], q_all[:, (h // 2) * LANES:(h // 2 + 1) * LANES], jnp.zeros((), BF16))
               for h in range(SB_HEADS)]
    jj = lax.broadcasted_iota(jnp.int32, (tk, tk), 0)
    ss = lax.broadcasted_iota(jnp.int32, (tk, tk), 1)
    later_mat = (jj > ss).astype(BF16)

    def block(kb, masked):
        start = pl.multiple_of(kb * tk, tk)
        k_blk = k_ref[0, pl.ds(start, tk), :]
        v_blk = v_ref[0, pl.ds(start, tk), :]
        if masked:
            key_pos = start + lax.broadcasted_iota(jnp.int32, (tq, tk), 1)
            q_pos = q0 + lax.broadcasted_iota(jnp.int32, (tq, tk), 0)
            mask = key_pos < q_pos
        dims = (((1,), (1,)), ((), ()))
        nls = [lax.dot_general(q_heads[h], k_blk[:, (h // 2) * LANES:(h // 2 + 1) * LANES], dims,
                               preferred_element_type=F32) for h in range(SB_HEADS)]
        lsms, laters = [], []
        for h in range(SB_HEADS):
            nl = nls[h]
            neg_abs = pltpu.bitcast(pltpu.bitcast(nl, jnp.int32) | jnp.int32(-2 ** 31), F32)
            lsm = jnp.minimum(nl, 0.0) - jnp.log2(1.0 + jnp.exp2(neg_abs))
            if masked:
                lsm = jnp.where(mask, lsm, 0.0)
            lsms.append(lsm)
            laters.append(jnp.dot(lsm.astype(BF16), later_mat, preferred_element_type=F32))
        for h in range(SB_HEADS):
            carry = carry_ref[h]
            expo = (lsms[h] - nls[h]) + laters[h] + jnp.concatenate([carry] * (tk // LANES), axis=1)
            w = jnp.exp2(expo)
            if masked:
                w = jnp.where(mask, w, 0.0)
            acc_ref[h] += jnp.dot(w.astype(BF16), v_blk[:, (h // 2) * LANES:(h // 2 + 1) * LANES],
                                  preferred_element_type=F32)
            row = laters[h][:, 0:1] + lsms[h][:, 0:1]
            carry_ref[h] = carry + jnp.broadcast_to(row, carry.shape)

    for d in range(n_diag - 1, -1, -1):
        block(n_full + d, True)

    def full_step(i, c):
        block(n_full - 1 - i, False)
        return c

    lax.fori_loop(0, n_full, full_step, 0)
    o_ref[0] = jnp.concatenate([jnp.where(own[2 * p], acc_ref[2 * p], acc_ref[2 * p + 1])
                                for p in range(SB_HEADS // 2)], axis=1).astype(o_ref.dtype)


def _sb_attention(q, k, v, *, off, tq, tk, out_dtype=BF16):
    B, Lq, W = q.shape
    Lk = k.shape[1]
    assert W == BRANCH_W and Lq % tq == 0 and Lk % tk == 0 and tk % 128 == 0
    nq = Lq // tq
    assert nq == 1 or (tq % tk == 0 and off % tk == 0)
    n_diag = -(-((off % tk) + tq - 1) // tk)
    assert (off + Lq - 1 + tk - 1) // tk <= Lk // tk
    body = partial(_sb_attn_body, tq=tq, tk=tk, off=off, n_diag=n_diag)
    return pl.pallas_call(
        body,
        grid=(B, nq),
        in_specs=[pl.BlockSpec((1, tq, W), lambda b, i: (b, i, 0)),
                  pl.BlockSpec((1, Lk, W), lambda b, i: (b, 0, 0)),
                  pl.BlockSpec((1, Lk, W), lambda b, i: (b, 0, 0))],
        out_specs=pl.BlockSpec((1, tq, W), lambda b, i: (b, i, 0)),
        out_shape=jax.ShapeDtypeStruct((B, Lq, W), out_dtype),
        scratch_shapes=[pltpu.VMEM((SB_HEADS, tq, LANES), F32), pltpu.VMEM((SB_HEADS, tq, LANES), F32)],
        compiler_params=pltpu.CompilerParams(dimension_semantics=("arbitrary", "arbitrary"),
                                             vmem_limit_bytes=VMEM_LIMIT_V7X),
        name="sb_attention",
    )(q, k, v)


RW_SUB = 16


def _split(x):
    hi = x.astype(BF16)
    lo = (x - hi.astype(F32)).astype(BF16)
    return hi, lo


def _mm3(a, b):
    ah, al = _split(a)
    bh, bl = _split(b)
    d = partial(jnp.dot, preferred_element_type=F32)
    return d(ah, bh) + (d(ah, bl) + d(al, bh))


def _sum01(x, m01):
    xh, xl = _split(x)
    return jnp.dot(xh, m01, preferred_element_type=F32) + jnp.dot(xl, m01, preferred_element_type=F32)


def _sum01_left(m01, x):
    xh, xl = _split(x)
    return jnp.dot(m01, xh, preferred_element_type=F32) + jnp.dot(m01, xl, preferred_element_type=F32)


def _dot_nt(a, b):
    return lax.dot_general(a, b, (((1,), (1,)), ((), ())), preferred_element_type=F32)


def _dot_tn(a, b):
    return lax.dot_general(a, b, (((0,), (0,)), ((), ())), preferred_element_type=F32)


def _rwkv_body(z_ref, shift_ref, s0_ref, mu_ref, w0_ref, w2_ref, a0_ref, a2_ref, g2_ref, kk_ref, ka_ref,
               rk_ref, lng_ref, lnb_ref, o_ref, s_out_ref, shift_out_ref, state_ref, prev_ref, *, T):
    ci = pl.program_id(1)
    nc = pl.num_programs(1)
    W = BRANCH_W
    N = RW_HEADS * T

    @pl.when(ci == 0)
    def _():
        state_ref[...] = s0_ref[0]
        prev_ref[...] = shift_ref[0]

    z = z_ref[0]
    row = lax.broadcasted_iota(jnp.int32, (T, RW_COLS), 0)
    z_prev = jnp.where(row == 0, jnp.broadcast_to(prev_ref[...], (T, RW_COLS)), pltpu.roll(z, 1, axis=0))
    prev_ref[...] = z[T - 1:T, :]
    zs = z + (z_prev - z) * mu_ref[...]
    r = zs[:, 0:W]
    k = zs[:, W:2 * W]
    v = zs[:, 2 * W:3 * W]
    xwa = zs[:, 3 * W:3 * W + 128]
    xg = zs[:, 3 * W + 128:]
    w_log = -jax.nn.softplus(-(w0_ref[...] + _mm3(jnp.tanh(xwa), w2_ref[...]))) - 0.5
    logdec = -jnp.exp(w_log)
    a = jax.nn.sigmoid(a0_ref[...] + _mm3(xwa, a2_ref[...]))
    g = _mm3(jax.nn.sigmoid(xg), g2_ref[...])
    li = lax.broadcasted_iota(jnp.int32, (W, W), 0) // RW_HD
    lj = lax.broadcasted_iota(jnp.int32, (W, W), 1) // RW_HD
    head_sum = (li == lj).astype(BF16)
    kk = k * kk_ref[...]
    kk = kk * lax.rsqrt(_sum01(kk * kk, head_sum) + 1e-12)
    k2 = k * (1.0 + (a - 1.0) * ka_ref[...])
    beta = kk * a
    ti = lax.broadcasted_iota(jnp.int32, (T, T), 0)
    tj = lax.broadcasted_iota(jnp.int32, (T, T), 1)
    b = _sum01_left((ti >= tj).astype(BF16), logdec)
    b_last = b[T - 1:T, :]
    k_in = kk * jnp.exp(b - logdec)
    r_in = r * jnp.exp(b)
    inv_p = jnp.exp(-b)
    k_out = k2 * inv_p
    b_out = beta * inv_p
    to_end = jnp.exp(b_last - b)
    k_end = k2 * to_end
    b_end = beta * to_end

    sh = lax.broadcasted_iota(jnp.int32, (N, W), 0) // T
    sl = lax.broadcasted_iota(jnp.int32, (N, W), 1) // RW_HD
    own = sh == sl

    def stack(x, masked):
        xs = jnp.concatenate([x] * RW_HEADS, axis=0)
        return jnp.where(own, xs, 0.0) if masked else xs

    k_in_s = stack(k_in, True).astype(BF16)
    r_in_s = stack(r_in, True).astype(BF16)
    k_out_s = stack(k_out, False).astype(BF16)
    b_out_s = stack(b_out, False).astype(BF16)
    v_s = stack(v, True).astype(BF16)
    k_end_s = stack(k_end, True).astype(BF16)
    b_end_s = stack(b_end, True).astype(BF16)

    ri = lax.broadcasted_iota(jnp.int32, (N, N), 0)
    rj = lax.broadcasted_iota(jnp.int32, (N, N), 1)
    same_head = (ri // T) == (rj // T)
    strict = same_head & (ri > rj)
    incl = same_head & (ri >= rj)
    a_mat = jnp.where(strict, _dot_nt(k_in_s, b_out_s), 0.0)
    kk_mat = jnp.where(strict, _dot_nt(k_in_s, k_out_s), 0.0)
    rk_mat = jnp.where(incl, _dot_nt(r_in_s, k_out_s), 0.0)
    rb_mat = jnp.where(incl, _dot_nt(r_in_s, b_out_s), 0.0)

    eye = (ri == rj).astype(F32)
    a_bd = jnp.where((ri // RW_SUB) == (rj // RW_SUB), a_mat, 0.0)
    x = eye - a_bd
    p = _mm3(a_bd, a_bd)
    x = x + _mm3(x, p)
    p = _mm3(p, p)
    x = x + _mm3(x, p)
    p = _mm3(p, p)
    x = x + _mm3(x, p)
    size = RW_SUB
    while size < T:
        lower = ((ri // (2 * size)) == (rj // (2 * size))) & ((ri // size) > (rj // size))
        x = x - _mm3(_mm3(x, jnp.where(lower, a_mat, 0.0)), x)
        size *= 2

    state = state_ref[...]
    state_b = state.astype(BF16)
    d = partial(jnp.dot, preferred_element_type=F32)
    rhs = _dot_nt(k_in_s, state_b) + d(kk_mat.astype(BF16), v_s)
    u = _mm3(x, rhs)
    u_b = u.astype(BF16)
    o_s = _dot_nt(r_in_s, state_b) + d(rk_mat.astype(BF16), v_s) - d(rb_mat.astype(BF16), u_b)
    o = o_s[0:T]
    for h in range(1, RW_HEADS):
        o = o + o_s[h * T:(h + 1) * T]
    state_ref[...] = state * jnp.exp(b_last) + _dot_tn(v_s, k_end_s) - _dot_tn(u_b, b_end_s)

    mean = _sum01(o, head_sum) * (1.0 / RW_HD)
    var = _sum01(jnp.square(o - mean), head_sum) * (1.0 / RW_HD)
    o = (o - mean) * lax.rsqrt(var + RW_GN_EPS) * lng_ref[...] + lnb_ref[...]
    bonus = _sum01(r * k2 * rk_ref[...], head_sum) * v
    o_ref[0] = ((o + bonus) * g).astype(o_ref.dtype)

    @pl.when(ci == nc - 1)
    def _():
        s_out_ref[0] = state_ref[...]
        shift_out_ref[0] = z[T - 1:T, :]


def _rwkv7_pallas(z, s0_bd, shift_prev, mu, w0, w2, a0, a2, g2, k_k, k_a, r_k, ln_g, ln_b, *, T, out_dtype=F32):
    B, L, C = z.shape
    assert C == RW_COLS and L % T == 0 and T % RW_SUB == 0
    W = BRANCH_W
    w2p = jnp.concatenate([w2, jnp.zeros_like(w2)], axis=0)
    a2p = jnp.concatenate([jnp.zeros_like(a2), a2], axis=0)
    vec = lambda t: t.reshape(1, -1).astype(F32)
    full = lambda shape: pl.BlockSpec(shape, lambda b, c: (0,) * len(shape))
    return pl.pallas_call(
        partial(_rwkv_body, T=T),
        grid=(B, L // T),
        in_specs=[pl.BlockSpec((1, T, C), lambda b, c: (b, c, 0)),
                  pl.BlockSpec((1, 1, C), lambda b, c: (b, 0, 0)),
                  pl.BlockSpec((1, W, W), lambda b, c: (b, 0, 0)),
                  full((1, C)), full((1, W)), full((128, W)), full((1, W)), full((128, W)), full((128, W)),
                  full((1, W)), full((1, W)), full((1, W)), full((1, W)), full((1, W))],
        out_specs=[pl.BlockSpec((1, T, W), lambda b, c: (b, c, 0)),
                   pl.BlockSpec((1, W, W), lambda b, c: (b, 0, 0)),
                   pl.BlockSpec((1, 1, C), lambda b, c: (b, 0, 0))],
        out_shape=[jax.ShapeDtypeStruct((B, L, W), out_dtype),
                   jax.ShapeDtypeStruct((B, W, W), F32),
                   jax.ShapeDtypeStruct((B, 1, C), F32)],
        scratch_shapes=[pltpu.VMEM((W, W), F32), pltpu.VMEM((1, C), F32)],
        compiler_params=pltpu.CompilerParams(dimension_semantics=("arbitrary", "arbitrary")),
        name="rwkv7_chunked",
    )(z, shift_prev.reshape(B, 1, C), s0_bd, vec(mu), vec(w0), w2p, vec(a0), a2p, g2, vec(k_k), vec(k_a),
      vec(r_k), vec(ln_g), vec(ln_b))


def _state_to_bd(s):
    B = s.shape[0]
    eye = jnp.eye(RW_HEADS, dtype=s.dtype)
    return (s[:, :, :, None, :] * eye[None, :, None, :, None]).reshape(B, BRANCH_W, BRANCH_W)


def _state_from_bd(s_bd):
    B = s_bd.shape[0]
    s5 = s_bd.reshape(B, RW_HEADS, RW_HD, RW_HEADS, RW_HD)
    return jnp.stack([s5[:, h, :, h, :] for h in range(RW_HEADS)], axis=1)


def _hgrn_body(z_ref, s0_ref, loglb_ref, log1mlb_ref, ng_ref, o_ref, s_out_ref, state_ref, *, T):
    ci = pl.program_id(1)
    nc = pl.num_programs(1)
    W = BRANCH_W

    @pl.when(ci == 0)
    def _():
        state_ref[...] = s0_ref[0]

    z = z_ref[0]
    zq, zf, zi, zg = z[:, 0:W], z[:, W:2 * W], z[:, 2 * W:3 * W], z[:, 3 * W:]
    log_sig = jnp.minimum(zf, 0.0) - jnp.log(1.0 + jnp.exp(-jnp.abs(zf)))
    a = loglb_ref[...]
    bv = log1mlb_ref[...] + log_sig
    m = jnp.maximum(a, bv)
    log_f = m + jnp.log(jnp.exp(a - m) + jnp.exp(bv - m))
    k = 1.0 - jnp.exp(log_f)
    q = zq * jax.nn.sigmoid(zq)
    ti = lax.broadcasted_iota(jnp.int32, (T, T), 0)
    tj = lax.broadcasted_iota(jnp.int32, (T, T), 1)
    b = _sum01_left((ti >= tj).astype(BF16), log_f)
    li = lax.broadcasted_iota(jnp.int32, (W, W), 0) // HG_DK
    lj = lax.broadcasted_iota(jnp.int32, (W, W), 1) // HG_DK
    same_head = li == lj
    head_sum = same_head.astype(BF16)

    prods, first = [], []
    for s in range(T):
        r0 = (s // SUBLANES) * SUBLANES
        t_idx = r0 + lax.broadcasted_iota(jnp.int32, (T - r0, W), 0)
        e = jnp.exp(jnp.where(t_idx >= s, b[r0:, :] - b[s:s + 1, :], -jnp.inf))
        prods.append((q[r0:, :] * e * k[s:s + 1, :]).astype(BF16))
        first.append(r0)
    scores = jnp.dot(jnp.concatenate(prods, axis=0), head_sum, preferred_element_type=F32)
    groups = [jnp.zeros((SUBLANES, W), F32) for _ in range(T // SUBLANES)]
    at = 0
    for s in range(T):
        for gi in range(first[s] // SUBLANES, T // SUBLANES):
            lo = at + gi * SUBLANES - first[s]
            groups[gi] = groups[gi] + scores[lo:lo + SUBLANES, :] * zi[s:s + 1, :]
        at += T - first[s]
    o = jnp.concatenate(groups, axis=0)

    state = state_ref[...]
    o = o + _dot_nt((q * jnp.exp(b)).astype(BF16), state.astype(BF16))
    b_last = b[T - 1:T, :]
    k_end = (k * jnp.exp(b_last - b)).astype(BF16)
    upd = _dot_tn(zi.astype(BF16), k_end)
    state_ref[...] = state * jnp.exp(b_last) + jnp.where(same_head, upd, 0.0)

    ms = _sum01(o * o, head_sum) * (1.0 / HG_DK)
    o = o * lax.rsqrt(ms + EPS) * ng_ref[...]
    o_ref[0] = (o * (zg * jax.nn.sigmoid(zg))).astype(o_ref.dtype)

    @pl.when(ci == nc - 1)
    def _():
        s_out_ref[0] = state_ref[...]


def _hgrn2_pallas(z, s0_bd, lb, norm_g, *, T, out_dtype=F32):
    B, L, C = z.shape
    assert C == HG_COLS and L % T == 0
    W = BRANCH_W
    full = lambda shape: pl.BlockSpec(shape, lambda b, c: (0,) * len(shape))
    return pl.pallas_call(
        partial(_hgrn_body, T=T),
        grid=(B, L // T),
        in_specs=[pl.BlockSpec((1, T, C), lambda b, c: (b, c, 0)),
                  pl.BlockSpec((1, W, W), lambda b, c: (b, 0, 0)),
                  full((1, W)), full((1, W)), full((1, W))],
        out_specs=[pl.BlockSpec((1, T, W), lambda b, c: (b, c, 0)),
                   pl.BlockSpec((1, W, W), lambda b, c: (b, 0, 0))],
        out_shape=[jax.ShapeDtypeStruct((B, L, W), out_dtype),
                   jax.ShapeDtypeStruct((B, W, W), F32)],
        scratch_shapes=[pltpu.VMEM((W, W), F32)],
        compiler_params=pltpu.CompilerParams(dimension_semantics=("arbitrary", "arbitrary")),
        name="hgrn2_chunked",
    )(z, s0_bd, jnp.log(lb).reshape(1, W), jnp.log1p(-lb).reshape(1, W), jnp.tile(norm_g, HG_HEADS).reshape(1, W))


PROJ_TM = 256
PROJ_VMEM_LIMIT_V7X = 48 * 1024 * 1024
CONV_CARRY = 32
CONV_TC = 512


def _in_proj_body(x_ref, g_ref, w_ref, q_ref, k_ref, v_ref, kb_ref, vb_ref, hg_ref, cv_ref, rw_ref, gate_ref):
    x = x_ref[...]
    h = (x * lax.rsqrt(jnp.mean(x * x, axis=-1, keepdims=True) + EPS) * g_ref[...]).astype(BF16)
    W = BRANCH_W

    def proj(a, b):
        return jnp.dot(h, w_ref[:, a:b], preferred_element_type=F32)

    q_ref[...] = (proj(0, W) * SB_Q_SCALE).astype(BF16)
    k = proj(W, 2 * W)
    v = proj(2 * W, 3 * W)
    k_ref[...] = k
    v_ref[...] = v
    kb_ref[...] = k.astype(BF16)
    vb_ref[...] = v.astype(BF16)
    c = SB_COLS
    hg_ref[...] = proj(c, c + HG_COLS)
    c += HG_COLS
    cv_ref[...] = proj(c, c + CV_COLS)
    c += CV_COLS
    rw_ref[...] = proj(c, c + RW_COLS)
    c += RW_COLS
    for n in range(N_BRANCH):
        gate_ref[:, n * D_MODEL:(n + 1) * D_MODEL] = jax.nn.sigmoid(
            proj(c + n * D_MODEL, c + (n + 1) * D_MODEL)).astype(BF16)


def _in_proj(x2, norm_g, w_in_b):
    n = x2.shape[0]
    TM = min(PROJ_TM, n)
    assert n % TM == 0
    row = lambda w: pl.BlockSpec((TM, w), lambda i: (i, 0))
    shp = lambda w, dt: jax.ShapeDtypeStruct((n, w), dt)
    W = BRANCH_W
    return pl.pallas_call(
        _in_proj_body,
        grid=(n // TM,),
        in_specs=[row(D_MODEL), pl.BlockSpec((1, D_MODEL), lambda i: (0, 0)),
                  pl.BlockSpec((D_MODEL, IN_COLS), lambda i: (0, 0), pipeline_mode=pl.Buffered(1))],
        out_specs=[row(W), row(W), row(W), row(W), row(W), row(HG_COLS), row(CV_COLS), row(RW_COLS), row(GATE_COLS)],
        out_shape=[shp(W, BF16), shp(W, F32), shp(W, F32), shp(W, BF16), shp(W, BF16),
                   shp(HG_COLS, F32), shp(CV_COLS, F32), shp(RW_COLS, F32), shp(GATE_COLS, BF16)],
        compiler_params=pltpu.CompilerParams(dimension_semantics=("arbitrary",), vmem_limit_bytes=PROJ_VMEM_LIMIT_V7X),
        name="in_proj",
    )(x2, norm_g.reshape(1, -1), w_in_b)


def _merge_body(x_ref, sb_ref, hg_ref, cv_ref, rw_ref, gate_ref, wb_ref, wo_ref, o_ref):
    merged = None
    for n, br in enumerate((sb_ref, hg_ref, cv_ref, rw_ref)):
        t = jnp.dot(br[...].astype(BF16), wb_ref[n], preferred_element_type=F32)
        t = t * gate_ref[:, n * D_MODEL:(n + 1) * D_MODEL].astype(F32)
        merged = t if merged is None else merged + t
    o_ref[...] = x_ref[...] + jnp.dot(merged.astype(BF16), wo_ref[...], preferred_element_type=F32)


def _merge(x2, o_sb, o_hg, o_cv, o_rw, gates, wb_b, wo_b):
    n = x2.shape[0]
    TM = min(PROJ_TM, n)
    row = lambda w: pl.BlockSpec((TM, w), lambda i: (i, 0))
    return pl.pallas_call(
        _merge_body,
        grid=(n // TM,),
        in_specs=[row(D_MODEL), row(BRANCH_W), row(BRANCH_W), row(BRANCH_W), row(BRANCH_W), row(GATE_COLS),
                  pl.BlockSpec((N_BRANCH, BRANCH_W, D_MODEL), lambda i: (0, 0, 0)),
                  pl.BlockSpec((D_MODEL, D_MODEL), lambda i: (0, 0))],
        out_specs=row(D_MODEL),
        out_shape=jax.ShapeDtypeStruct((n, D_MODEL), F32),
        compiler_params=pltpu.CompilerParams(dimension_semantics=("arbitrary",), vmem_limit_bytes=PROJ_VMEM_LIMIT_V7X),
        name="branch_merge",
    )(x2, o_sb, o_hg, o_cv, o_rw, gates, wb_b, wo_b)


def _conv_body(z_ref, buf_ref, w_ref, b_ref, lg_ref, lb_ref, o_ref, buf_out_ref, hp_ref, *, TC):
    ci = pl.program_id(1)
    nc = pl.num_programs(1)

    @pl.when(ci == 0)
    def _():
        hp_ref[0:CONV_CARRY, :] = buf_ref[0]

    z = z_ref[0]
    hp_ref[CONV_CARRY:, :] = z[:, 0:CONV_CH] * jax.nn.sigmoid(z[:, CONV_CH:])
    first = CONV_CARRY - (CONV_WIDTH - 1)
    y = jnp.zeros((TC, CONV_CH), F32) + b_ref[...]
    for j in range(CONV_WIDTH):
        y = y + hp_ref[first + j:first + j + TC, :] * w_ref[j:j + 1, :]
    mu = jnp.mean(y, axis=-1, keepdims=True)
    d = y - mu
    var = jnp.mean(d * d, axis=-1, keepdims=True)
    y = d * lax.rsqrt(var + EPS) * lg_ref[...] + lb_ref[...]
    o_ref[0] = (y * jax.nn.sigmoid(y)).astype(o_ref.dtype)
    tail = hp_ref[TC:TC + CONV_CARRY, :]

    @pl.when(ci == nc - 1)
    def _():
        buf_out_ref[0] = tail

    hp_ref[0:CONV_CARRY, :] = tail


def _conformer_conv_pallas(z_cv, buf, w, b, ln_g, ln_b, *, TC, out_dtype=F32):
    B, L, _ = z_cv.shape
    assert L % TC == 0 and TC % 8 == 0 and TC >= CONV_CARRY
    C = CONV_CH
    pad = CONV_CARRY - (CONV_WIDTH - 1)
    buf_p = jnp.pad(buf.astype(F32), ((0, 0), (pad, 0), (0, 0)))
    w_p = jnp.pad(w.astype(F32), ((0, 32 - CONV_WIDTH), (0, 0)))
    full = lambda shape: pl.BlockSpec(shape, lambda b, c: (0,) * len(shape))
    y, buf_new = pl.pallas_call(
        partial(_conv_body, TC=TC),
        grid=(B, L // TC),
        in_specs=[pl.BlockSpec((1, TC, 2 * C), lambda b, c: (b, c, 0)),
                  pl.BlockSpec((1, CONV_CARRY, C), lambda b, c: (b, 0, 0)),
                  full((32, C)), full((1, C)), full((1, C)), full((1, C))],
        out_specs=[pl.BlockSpec((1, TC, C), lambda b, c: (b, c, 0)),
                   pl.BlockSpec((1, CONV_CARRY, C), lambda b, c: (b, 0, 0))],
        out_shape=[jax.ShapeDtypeStruct((B, L, C), out_dtype), jax.ShapeDtypeStruct((B, CONV_CARRY, C), F32)],
        scratch_shapes=[pltpu.VMEM((CONV_CARRY + TC, C), F32)],
        compiler_params=pltpu.CompilerParams(dimension_semantics=("arbitrary", "arbitrary")),
        name="conformer_conv",
    )(z_cv, buf_p, w_p, b.reshape(1, C), ln_g.reshape(1, C), ln_b.reshape(1, C))
    return y, buf_new[:, pad:, :]


PK_PAIRS = PK_HEADS * PK_TOPK
PEER_HEAD_GROUP = 4
PEER_TB = 128
ROW_TILES = D_MODEL // LANES
PEER_VMEM_LIMIT_V7X = 52 * 1024 * 1024


def _topk_rows(scores, payloads, k):
    n = scores[0].shape[0]
    rows = lax.broadcasted_iota(jnp.int32, scores[0].shape, 0).astype(F32)
    scores = list(scores)
    vals = [[] for _ in scores]
    ids = [[] for _ in scores]
    for _ in range(k):
        for c, s in enumerate(scores):
            m = jnp.max(s, axis=0, keepdims=True)
            pos = jnp.min(jnp.where(s == m, rows, float(n)), axis=0, keepdims=True)
            sel = rows == pos
            ids[c].append(pos if payloads[c] is None
                          else jnp.max(jnp.where(sel, payloads[c], -1.0), axis=0, keepdims=True))
            vals[c].append(m)
            scores[c] = jnp.where(sel, -jnp.inf, s)
    return [(jnp.concatenate(v, axis=0), jnp.concatenate(i, axis=0)) for v, i in zip(vals, ids)]


def _dot3(ah, al, bh, bl, dims):
    d = partial(lax.dot_general, dimension_numbers=dims, preferred_element_type=F32)
    return d(ah, bh) + (d(ah, bl) + d(al, bh))


def _candidate_tables(tb):
    groups = [[0], [1], [2], [3, 4], [5, 6, 7], list(range(8, 16))]
    rows = []
    for grp in groups:
        part = [(a, b) for a in grp for b in range(PK_TOPK // (a + 1))]
        rows += part + [None] * ((-len(part)) % SUBLANES)
    n = len(rows)
    sel1 = [[1.0 if (r is not None and r[0] == j) else 0.0 for j in range(PK_TOPK)] for r in rows]
    sel2 = [[1.0 if (r is not None and r[1] == j) else 0.0 for j in range(PK_TOPK)] for r in rows]
    cpos = [[-1.0 if r is None else float(r[0] * PK_TOPK + r[1])] * tb for r in rows]
    return jnp.array(sel1, BF16), jnp.array(sel2, BF16), jnp.array(cpos, F32), n


def _pick_rows(sel, v):
    hi = v.astype(BF16)
    r1 = v - hi.astype(F32)
    mid = r1.astype(BF16)
    lo = (r1 - mid.astype(F32)).astype(BF16)
    d = partial(jnp.dot, preferred_element_type=F32)
    return (d(sel, hi) + d(sel, mid)) + d(sel, lo)


def _peer_route_body(x_ref, g_ref, wqh_ref, wql_ref, qg_ref, k1_ref, k2_ref, sel1_ref, sel2_ref, cpos_ref,
                     xn_ref, idx_ref, idxc_ref, gate_ref):
    TB = x_ref.shape[0]
    x = x_ref[...]
    xn = x * lax.rsqrt(jnp.mean(x * x, axis=-1, keepdims=True) + EPS) * g_ref[...]
    xn_ref[...] = xn
    nn = (((1,), (0,)), ((), ()))
    nt = (((1,), (1,)), ((), ()))
    xh, xl = _split(xn)
    q = _dot3(xh, xl, wqh_ref[...], wql_ref[...], nn)
    half = PK_DQ // 2
    experts, gates = [], []
    for h0 in range(0, PK_HEADS, PEER_HEAD_GROUP):
        heads = range(h0, h0 + PEER_HEAD_GROUP)
        subs = []
        for h in heads:
            qh = q[:, h * PK_DQ:(h + 1) * PK_DQ]
            qh = qh * lax.rsqrt(jnp.mean(qh * qh, axis=-1, keepdims=True) + EPS) * qg_ref[...]
            qhh, qhl = _split(qh)
            k1h, k1l = _split(k1_ref[h])
            k2h, k2l = _split(k2_ref[h])
            subs.append(_dot3(k1h, k1l, qhh[:, :half], qhl[:, :half], nt))
            subs.append(_dot3(k2h, k2l, qhh[:, half:], qhl[:, half:], nt))
        tops = _topk_rows(subs, [None] * len(subs), PK_TOPK)
        cands = []
        for j in range(PEER_HEAD_GROUP):
            (v1, _), (v2, _) = tops[2 * j], tops[2 * j + 1]
            cands.append(jnp.where(cpos_ref[...] < 0.0, -jnp.inf,
                                   _pick_rows(sel1_ref[...], v1) + _pick_rows(sel2_ref[...], v2)))
        best = _topk_rows(cands, [cpos_ref[...]] * PEER_HEAD_GROUP, PK_TOPK)
        for j in range(PEER_HEAD_GROUP):
            i1, i2 = tops[2 * j][1], tops[2 * j + 1][1]
            top_s, pos = best[j]
            pa = jnp.floor(pos * (1.0 / PK_TOPK))
            pb = pos - pa * PK_TOPK
            e1 = jnp.zeros_like(pos)
            e2 = jnp.zeros_like(pos)
            for a in range(PK_TOPK):
                e1 = jnp.where(pa == a, i1[a:a + 1, :], e1)
                e2 = jnp.where(pb == a, i2[a:a + 1, :], e2)
            e = jnp.exp(top_s - jnp.max(top_s, axis=0, keepdims=True))
            gates.append(e / jnp.sum(e, axis=0, keepdims=True))
            experts.append(e1 * float(PK_NKEYS) + e2)
    expert = jnp.concatenate(experts, axis=0).astype(jnp.int32)
    row = (expert + 1) * HALF_TILES
    pair = lax.broadcasted_iota(jnp.int32, row.shape, 0)
    row_t = row.T
    rowc_t = (row - (pair & HALF_TILES)).T
    for t in range(TB):
        idx_ref[0, :, t * PK_PAIRS:(t + 1) * PK_PAIRS] = row_t[t:t + 1, :]
        idxc_ref[0, :, t * PK_PAIRS:(t + 1) * PK_PAIRS] = rowc_t[t:t + 1, :]
    gate_ref[...] = jnp.concatenate(gates, axis=0)


def _peer_route(x2, norm_g, wq, qn_g, k1, k2):
    n_tok = x2.shape[0]
    TB = PEER_TB
    assert n_tok % TB == 0
    nb = n_tok // TB
    full = lambda shape: pl.BlockSpec(shape, lambda i: (0,) * len(shape))
    wq_hi = wq.astype(BF16)
    wq_lo = (wq - wq_hi.astype(F32)).astype(BF16)
    sel1, sel2, cpos, n_cand = _candidate_tables(TB)
    return pl.pallas_call(
        _peer_route_body,
        grid=(nb,),
        in_specs=[pl.BlockSpec((TB, D_MODEL), lambda i: (i, 0)), full((1, D_MODEL)),
                  full((D_MODEL, PK_HEADS * PK_DQ)), full((D_MODEL, PK_HEADS * PK_DQ)), full((1, PK_DQ)),
                  full((PK_HEADS, PK_NKEYS, PK_DQ // 2)), full((PK_HEADS, PK_NKEYS, PK_DQ // 2)),
                  full((n_cand, PK_TOPK)), full((n_cand, PK_TOPK)), full((n_cand, TB))],
        out_specs=[pl.BlockSpec((TB, D_MODEL), lambda i: (i, 0)),
                   pl.BlockSpec((1, 1, TB * PK_PAIRS), lambda i: (i, 0, 0)),
                   pl.BlockSpec((1, 1, TB * PK_PAIRS), lambda i: (i, 0, 0)),
                   pl.BlockSpec((PK_PAIRS, TB), lambda i: (0, i))],
        out_shape=[jax.ShapeDtypeStruct((n_tok, D_MODEL), F32),
                   jax.ShapeDtypeStruct((nb, 1, TB * PK_PAIRS), jnp.int32),
                   jax.ShapeDtypeStruct((nb, 1, TB * PK_PAIRS), jnp.int32),
                   jax.ShapeDtypeStruct((PK_PAIRS, n_tok), F32)],
        compiler_params=pltpu.CompilerParams(dimension_semantics=("arbitrary",)),
        name="peer_route",
    )(x2, norm_g.reshape(1, -1), wq_hi, wq_lo, qn_g.reshape(1, -1), k1, k2, sel1, sel2, cpos)


HALF_TILES = ROW_TILES // 2


def _unpack_words(word):
    return pltpu.bitcast(word << 16, F32), pltpu.bitcast(word & jnp.int32(-65536), F32)


def _fold_couples(tiles, sub):
    m2 = (sub & 3) < 2
    lvl2 = []
    for a, b in ((tiles[0], tiles[2]), (tiles[1], tiles[3])):
        lvl2.append(jnp.where(m2, a, b) + jnp.where(m2, pltpu.roll(a, 6, axis=0), pltpu.roll(b, 2, axis=0)))
    m1 = (sub & 1) == 0
    a, b = lvl2
    return jnp.where(m1, a, b) + jnp.where(m1, pltpu.roll(a, 7, axis=0), pltpu.roll(b, 1, axis=0))


def _peer_act_body(idx_ref, x_ref, gate_ref, tab_ref, w_ref, part_ref):
    TB = x_ref.shape[0]
    sub = lax.broadcasted_iota(jnp.int32, (SUBLANES, LANES), 0)
    low = sub < HALF_TILES

    def token(t, c):
        x = x_ref[pl.ds(t, 1), :]
        tiles_x = [x[:, j * LANES:(j + 1) * LANES] for j in range(ROW_TILES)]
        x_lo = jnp.concatenate(tiles_x[:HALF_TILES] * 2, axis=0)
        x_hi = jnp.concatenate(tiles_x[HALF_TILES:] * 2, axis=0)
        start = t * PK_PAIRS
        for g in range(PK_PAIRS // SUBLANES):
            tiles = []
            for j in range(HALF_TILES):
                ra = pl.multiple_of(idx_ref[0, 0, start + g * SUBLANES + j], HALF_TILES)
                rb = pl.multiple_of(idx_ref[0, 0, start + g * SUBLANES + j + HALF_TILES], HALF_TILES)
                word = jnp.where(low, tab_ref[pl.ds(ra, SUBLANES), :], tab_ref[pl.ds(rb, SUBLANES), :])
                lo, hi = _unpack_words(word)
                tiles.append(lo * x_lo + hi * x_hi)
            part_ref[t, g * SUBLANES:(g + 1) * SUBLANES, :] = _fold_couples(tiles, sub)
        return c

    lax.fori_loop(0, TB, token, 0)
    lane = lax.broadcasted_iota(jnp.int32, (PK_PAIRS, TB), 1)
    s = jnp.zeros((PK_PAIRS, TB), F32)
    ones = jnp.ones((LANES, LANES), BF16)
    for t in range(TB):
        tot = jnp.dot(part_ref[t].astype(BF16), ones, preferred_element_type=F32)
        s = jnp.where(lane == t, tot, s)
    w_ref[...] = gate_ref[...] * jax.nn.gelu(s)


PEER_MIX_ACCS = 4


def _peer_mix_body(idx_ref, w_ref, tab_ref, y_ref, wrep_ref):
    TB = y_ref.shape[0]
    w_all = w_ref[...]
    for t in range(TB):
        wrep_ref[t] = jnp.broadcast_to(w_all[:, t:t + 1], (PK_PAIRS, LANES))

    def token(t, c):
        start = t * PK_PAIRS
        acc_lo = [jnp.zeros((HALF_TILES, LANES), F32) for _ in range(PEER_MIX_ACCS)]
        acc_hi = [jnp.zeros((HALF_TILES, LANES), F32) for _ in range(PEER_MIX_ACCS)]
        for p in range(PK_PAIRS):
            row = pl.multiple_of(idx_ref[0, 0, start + p], HALF_TILES)
            lo, hi = _unpack_words(tab_ref[pl.ds(row, HALF_TILES), :])
            w = jnp.broadcast_to(wrep_ref[t, p:p + 1, :], (HALF_TILES, LANES))
            acc_lo[p % PEER_MIX_ACCS] = acc_lo[p % PEER_MIX_ACCS] + w * lo
            acc_hi[p % PEER_MIX_ACCS] = acc_hi[p % PEER_MIX_ACCS] + w * hi
        y_ref[t] = jnp.concatenate([(acc_lo[0] + acc_lo[1]) + (acc_lo[2] + acc_lo[3]),
                                    (acc_hi[0] + acc_hi[1]) + (acc_hi[2] + acc_hi[3])], axis=0)
        return c

    lax.fori_loop(0, TB, token, 0)


def _pack_table(tab):
    bits = lax.bitcast_convert_type(tab.astype(BF16), jnp.uint16).astype(jnp.uint32)
    bits = bits.reshape(PK_EXPERTS, 2, HALF_TILES, LANES)
    word = bits[:, 0] | (bits[:, 1] << 16)
    rows = lax.bitcast_convert_type(word, jnp.int32).reshape(PK_EXPERTS * HALF_TILES, LANES)
    return jnp.pad(rows, ((HALF_TILES, HALF_TILES), (0, 0)))


def _smem_spec(TB):
    return pl.BlockSpec((1, 1, PK_PAIRS * TB), lambda i: (i, 0, 0), memory_space=pltpu.SMEM)


def _table_spec():
    return pl.BlockSpec(((PK_EXPERTS + 2) * HALF_TILES, LANES), lambda i: (0, 0), pipeline_mode=pl.Buffered(1))


def _peer_act(idx, xn, gate_t, tab_packed):
    n_tok = xn.shape[0]
    TB = PEER_TB
    nb = n_tok // TB
    return pl.pallas_call(
        _peer_act_body,
        grid=(nb,),
        in_specs=[_smem_spec(TB),
                  pl.BlockSpec((TB, D_MODEL), lambda i: (i, 0)),
                  pl.BlockSpec((PK_PAIRS, TB), lambda i: (0, i)),
                  _table_spec()],
        out_specs=pl.BlockSpec((PK_PAIRS, TB), lambda i: (0, i)),
        out_shape=jax.ShapeDtypeStruct((PK_PAIRS, n_tok), F32),
        scratch_shapes=[pltpu.VMEM((TB, PK_PAIRS, LANES), F32)],
        compiler_params=pltpu.CompilerParams(dimension_semantics=("arbitrary",),
                                             vmem_limit_bytes=PEER_VMEM_LIMIT_V7X),
        name="peer_act",
    )(idx, xn, gate_t, tab_packed)


def _peer_mix(idx, w_t, tab_packed):
    n_tok = w_t.shape[1]
    TB = PEER_TB
    nb = n_tok // TB
    return pl.pallas_call(
        _peer_mix_body,
        grid=(nb,),
        in_specs=[_smem_spec(TB),
                  pl.BlockSpec((PK_PAIRS, TB), lambda i: (0, i)),
                  _table_spec()],
        out_specs=pl.BlockSpec((TB, ROW_TILES, LANES), lambda i: (i, 0, 0)),
        out_shape=jax.ShapeDtypeStruct((n_tok, ROW_TILES, LANES), F32),
        scratch_shapes=[pltpu.VMEM((TB, PK_PAIRS, LANES), F32)],
        compiler_params=pltpu.CompilerParams(dimension_semantics=("arbitrary",),
                                             vmem_limit_bytes=PEER_VMEM_LIMIT_V7X),
        name="peer_mix",
    )(idx, w_t, tab_packed)


def _peer_ffn_pallas(x, norm_g, wq, qn_g, sub_keys, u_packed, v_packed):
    B, L, D = x.shape
    n_tok = B * L
    pad = (-n_tok) % PEER_TB
    x2 = jnp.pad(x.reshape(n_tok, D), ((0, pad), (0, 0)))
    xn, idx, idx_couple, gate_t = _peer_route(x2, norm_g, wq, qn_g, sub_keys[:, 0], sub_keys[:, 1])
    w_t = _peer_act(idx_couple, xn, gate_t, u_packed)
    y = _peer_mix(idx, w_t, v_packed)
    return y.reshape(-1, D)[:n_tok].reshape(B, L, D)


def _trunk_layer(x, k_past, v_past, s_hg, buf_conv, s_rw, buf_shift, p):
    B_, L, _ = x.shape
    n = B_ * L
    x2 = x.reshape(n, D_MODEL)
    qb, k, v, kb, vb, z_hg, z_cv, z_rw, gates = _in_proj(x2, p['norm_mix_g'], p['w_in'])
    seq = lambda t: t.reshape(B_, L, t.shape[-1])
    if k_past is None:
        o_sb = _sb_attention(seq(qb), seq(kb), seq(vb), off=0, tq=SB_TQ, tk=SB_TK, out_dtype=F32)
    else:
        past = k_past.shape[1]
        pad = jnp.zeros((B_, (-(past + L)) % SB_TK, BRANCH_W), BF16)
        kc = jnp.concatenate([k_past.reshape(B_, past, BRANCH_W).astype(BF16), seq(kb), pad], axis=1)
        vc = jnp.concatenate([v_past.reshape(B_, past, BRANCH_W).astype(BF16), seq(vb), pad], axis=1)
        o_sb = _sb_attention(seq(qb), kc, vc, off=past, tq=L, tk=SB_TK, out_dtype=F32)
    o_hg, s_hg_bd = _hgrn2_pallas(seq(z_hg), _state_to_bd(jnp.swapaxes(s_hg.astype(F32), 2, 3)), p['lb'],
                                  p['hg_norm_g'], T=CHUNK if L % CHUNK == 0 else L)
    s_hg_new = jnp.swapaxes(_state_from_bd(s_hg_bd), 2, 3)
    o_cv, buf_conv_new = _conformer_conv_pallas(seq(z_cv), buf_conv, p['conv_w'], p['conv_b'], p['conv_ln_g'],
                                                p['conv_ln_b'], TC=CONV_TC if L % CONV_TC == 0 else L)
    o_rw, s_rw_bd, shift_new = _rwkv7_pallas(seq(z_rw), _state_to_bd(s_rw.astype(F32)), buf_shift, p['rw_mu'],
                                             p['rw_w0'], p['rw_w2'], p['rw_a0'], p['rw_a2'], p['rw_g2'], p['rw_k_k'],
                                             p['rw_k_a'], p['rw_r_k'], p['rw_ln_g'], p['rw_ln_b'],
                                             T=RW_CHUNK if L % RW_CHUNK == 0 else L)
    s_rw_new = _state_from_bd(s_rw_bd)
    buf_shift_new = shift_new[:, 0]
    flat = lambda t: t.reshape(n, BRANCH_W)
    x = _merge(x2, flat(o_sb), flat(o_hg), flat(o_cv), flat(o_rw), gates, p['w_branch'], p['w_out']).reshape(x.shape)
    x = x + _peer_ffn_pallas(x, p['norm_ffn_g'], p['peer_wq'], p['peer_qn_g'], p['peer_keys'],
                             p['peer_u'], p['peer_v'])
    heads = lambda t: t.reshape(B_, L, SB_HEADS, HEAD_DIM)
    return x, (heads(k), heads(v), s_hg_new, buf_conv_new, s_rw_new, buf_shift_new)


def kernel(x_prompt, x_sample, cache_sb_k, cache_sb_v, state_hgrn, state_conv, state_rwkv, state_shift,
           norm_mix_g, w_in, hg_lb_logits, hg_norm_g, conv_w, conv_b, conv_ln_g, conv_ln_b,
           rw_mu, rw_w0, rw_w2, rw_a0, rw_a2, rw_g2, rw_k_k, rw_k_a, rw_r_k, rw_ln_g, rw_ln_b,
           w_branch, w_out, norm_ffn_g, peer_wq, peer_qn_g, peer_keys, peer_u, peer_v, final_norm_g):
    lb_all = jnp.cumsum(jax.nn.softmax(hg_lb_logits.astype(F32), axis=0), axis=0)
    lb_all = lb_all - lb_all[0:1]
    Bp = x_prompt.shape[0]
    hg0 = jnp.zeros((Bp, HG_HEADS, HG_DK, HG_DV), F32)
    cv0 = jnp.zeros((Bp, CONV_WIDTH - 1, CONV_CH), x_prompt.dtype)
    rw0 = jnp.zeros((Bp, RW_HEADS, RW_HD, RW_HD), F32)
    sh0 = jnp.zeros((Bp, RW_COLS), x_prompt.dtype)
    xp, xs = x_prompt, x_sample
    outs_p, outs_s = [], []
    for l in range(DEPTH):
        lp = dict(norm_mix_g=norm_mix_g[l], w_in=w_in[l].astype(BF16), lb=lb_all[l], hg_norm_g=hg_norm_g[l],
                  conv_w=conv_w[l], conv_b=conv_b[l], conv_ln_g=conv_ln_g[l], conv_ln_b=conv_ln_b[l],
                  rw_mu=rw_mu[l], rw_w0=rw_w0[l], rw_w2=rw_w2[l], rw_a0=rw_a0[l], rw_a2=rw_a2[l],
                  rw_g2=rw_g2[l], rw_k_k=rw_k_k[l], rw_k_a=rw_k_a[l], rw_r_k=rw_r_k[l],
                  rw_ln_g=rw_ln_g[l], rw_ln_b=rw_ln_b[l], w_branch=w_branch[l].astype(BF16),
                  w_out=w_out[l].astype(BF16),
                  norm_ffn_g=norm_ffn_g[l], peer_wq=peer_wq[l], peer_qn_g=peer_qn_g[l],
                  peer_keys=peer_keys[l], peer_u=_pack_table(peer_u[l]), peer_v=_pack_table(peer_v[l]))
        xp, st_p = _trunk_layer(xp, None, None, hg0, cv0, rw0, sh0, lp)
        xs, st_s = _trunk_layer(xs, cache_sb_k[l], cache_sb_v[l], state_hgrn[l], state_conv[l],
                                state_rwkv[l], state_shift[l], lp)
        outs_p.append(st_p)
        outs_s.append(st_s)

    def stk(outs, i):
        return jnp.stack([o[i] for o in outs], axis=0)

    y_prompt = _rmsnorm_pallas(xp, final_norm_g)
    y_sample = _rmsnorm_pallas(xs, final_norm_g)
    return (y_prompt, y_sample,
            stk(outs_p, 0), stk(outs_p, 1), stk(outs_p, 2), stk(outs_p, 3), stk(outs_p, 4), stk(outs_p, 5),
            stk(outs_s, 0), stk(outs_s, 1), stk(outs_s, 2), stk(outs_s, 3), stk(outs_s, 4), stk(outs_s, 5))
```

```python
from functools import partial

import jax
import jax.numpy as jnp
from jax import lax
from jax.experimental import pallas as pl
from jax.experimental.pallas import tpu as pltpu

D_MODEL = 1024
DEPTH = 2
CHUNK = 64
HEAD_DIM = 64
N_BRANCH = 4
BRANCH_W = D_MODEL // 4
SB_HEADS = BRANCH_W // HEAD_DIM
HG_HEADS = 4
HG_DK = BRANCH_W // HG_HEADS
HG_DV = BRANCH_W // HG_HEADS
CONV_CH = BRANCH_W
CONV_WIDTH = 31
RW_HEADS = 4
RW_HD = BRANCH_W // RW_HEADS
RW_DECAY_LORA = 64
RW_AAA_LORA = 64
RW_GATE_LORA = 128
SB_COLS = 3 * BRANCH_W
HG_COLS = 4 * BRANCH_W
CV_COLS = 2 * CONV_CH
RW_COLS = 3 * BRANCH_W + RW_DECAY_LORA + RW_AAA_LORA + RW_GATE_LORA
GATE_COLS = N_BRANCH * D_MODEL
IN_COLS = SB_COLS + HG_COLS + CV_COLS + RW_COLS + GATE_COLS
PK_HEADS = 8
PK_NKEYS = 128
PK_EXPERTS = PK_NKEYS * PK_NKEYS
PK_DQ = 256
PK_TOPK = 16
EPS = 1e-6
RW_GN_EPS = 64e-5
SB_TQ = 256
SB_TK = 256
RW_CHUNK = 64
F32 = jnp.float32
BF16 = jnp.bfloat16
SUBLANES = 8
LANES = 128


def _rmsnorm_body(x_ref, g_ref, o_ref):
    x = x_ref[...]
    y = x * lax.rsqrt(jnp.mean(x * x, axis=-1, keepdims=True) + EPS)
    o_ref[...] = y * g_ref[...]


def _rmsnorm_pallas(x, g, rows=512):
    shape = x.shape
    x2 = x.reshape(-1, shape[-1])
    n, d = x2.shape
    rows = min(rows, n)
    out = pl.pallas_call(
        _rmsnorm_body,
        grid=(n // rows,),
        in_specs=[pl.BlockSpec((rows, d), lambda i: (i, 0)),
                  pl.BlockSpec((1, d), lambda i: (0, 0))],
        out_specs=pl.BlockSpec((rows, d), lambda i: (i, 0)),
        out_shape=jax.ShapeDtypeStruct((n, d), x.dtype),
        name="final_rmsnorm",
    )(x2, g.reshape(1, d))
    return out.reshape(shape)


VMEM_LIMIT_V7X = 48 * 1024 * 1024
SB_Q_SCALE = -(HEAD_DIM ** -0.5) * 1.4426950408889634


def _sb_attn_body(q_ref, k_ref, v_ref, o_ref, acc_ref, carry_ref, *, tq, tk, off, n_diag):
    qi = pl.program_id(1)
    q0 = off + qi * tq
    n_full = q0 // tk
    acc_ref[...] = jnp.zeros_like(acc_ref)
    carry_ref[...] = jnp.zeros_like(carry_ref)
    q_all = q_ref[0]
    lane = lax.broadcasted_iota(jnp.int32, (tq, LANES), 1)
    own = [(lane // HEAD_DIM) == (h % 2) for h in range(SB_HEADS)]
    q_heads = [jnp.where(own[h], q_all[:, (h // 2) * LANES:(h // 2 + 1) * LANES], jnp.zeros((), BF16))
               for h in range(SB_HEADS)]
    jj = lax.broadcasted_iota(jnp.int32, (tk, tk), 0)
    ss = lax.broadcasted_iota(jnp.int32, (tk, tk), 1)
    later_mat = (jj > ss).astype(BF16)

    def block(kb, masked):
        start = pl.multiple_of(kb * tk, tk)
        k_blk = k_ref[0, pl.ds(start, tk), :]
        v_blk = v_ref[0, pl.ds(start, tk), :]
        if masked:
            key_pos = start + lax.broadcasted_iota(jnp.int32, (tq, tk), 1)
            q_pos = q0 + lax.broadcasted_iota(jnp.int32, (tq, tk), 0)
            mask = key_pos < q_pos
        dims = (((1,), (1,)), ((), ()))
        nls = [lax.dot_general(q_heads[h], k_blk[:, (h // 2) * LANES:(h // 2 + 1) * LANES], dims,
                               preferred_element_type=F32) for h in range(SB_HEADS)]
        lsms, laters = [], []
        for h in range(SB_HEADS):
            nl = nls[h]
            neg_abs = pltpu.bitcast(pltpu.bitcast(nl, jnp.int32) | jnp.int32(-2 ** 31), F32)
            lsm = jnp.minimum(nl, 0.0) - jnp.log2(1.0 + jnp.exp2(neg_abs))
            if masked:
                lsm = jnp.where(mask, lsm, 0.0)
            lsms.append(lsm)
            laters.append(jnp.dot(lsm.astype(BF16), later_mat, preferred_element_type=F32))
        for h in range(SB_HEADS):
            carry = carry_ref[h]
            expo = (lsms[h] - nls[h]) + laters[h] + jnp.concatenate([carry] * (tk // LANES), axis=1)
            w = jnp.exp2(expo)
            if masked:
                w = jnp.where(mask, w, 0.0)
            acc_ref[h] += jnp.dot(w.astype(BF16), v_blk[:, (h // 2) * LANES:(h // 2 + 1) * LANES],
                                  preferred_element_type=F32)
            row = laters[h][:, 0:1] + lsms[h][:, 0:1]
            carry_ref[h] = carry + jnp.broadcast_to(row, carry.shape)

    for d in range(n_diag - 1, -1, -1):
        block(n_full + d, True)

    def full_step(i, c):
        block(n_full - 1 - i, False)
        return c

    lax.fori_loop(0, n_full, full_step, 0)
    o_ref[0] = jnp.concatenate([jnp.where(own[2 * p], acc_ref[2 * p], acc_ref[2 * p + 1])
                                for p in range(SB_HEADS // 2)], axis=1).astype(o_ref.dtype)


def _sb_attention(q, k, v, *, off, tq, tk, out_dtype=BF16):
    B, Lq, W = q.shape
    Lk = k.shape[1]
    assert W == BRANCH_W and Lq % tq == 0 and Lk % tk == 0 and tk % 128 == 0
    nq = Lq // tq
    assert nq == 1 or (tq % tk == 0 and off % tk == 0)
    n_diag = -(-((off % tk) + tq - 1) // tk)
    assert (off + Lq - 1 + tk - 1) // tk <= Lk // tk
    body = partial(_sb_attn_body, tq=tq, tk=tk, off=off, n_diag=n_diag)
    return pl.pallas_call(
        body,
        grid=(B, nq),
        in_specs=[pl.BlockSpec((1, tq, W), lambda b, i: (b, i, 0)),
                  pl.BlockSpec((1, Lk, W), lambda b, i: (b, 0, 0)),
                  pl.BlockSpec((1, Lk, W), lambda b, i: (b, 0, 0))],
        out_specs=pl.BlockSpec((1, tq, W), lambda b, i: (b, i, 0)),
        out_shape=jax.ShapeDtypeStruct((B, Lq, W), out_dtype),
        scratch_shapes=[pltpu.VMEM((SB_HEADS, tq, LANES), F32), pltpu.VMEM((SB_HEADS, tq, LANES), F32)],
        compiler_params=pltpu.CompilerParams(dimension_semantics=("arbitrary", "arbitrary"),
                                             vmem_limit_bytes=VMEM_LIMIT_V7X),
        name="sb_attention",
    )(q, k, v)


RW_SUB = 16


def _split(x):
    hi = x.astype(BF16)
    lo = (x - hi.astype(F32)).astype(BF16)
    return hi, lo


def _mm3(a, b):
    ah, al = _split(a)
    bh, bl = _split(b)
    d = partial(jnp.dot, preferred_element_type=F32)
    return d(ah, bh) + (d(ah, bl) + d(al, bh))


def _sum01(x, m01):
    xh, xl = _split(x)
    return jnp.dot(xh, m01, preferred_element_type=F32) + jnp.dot(xl, m01, preferred_element_type=F32)


def _sum01_left(m01, x):
    xh, xl = _split(x)
    return jnp.dot(m01, xh, preferred_element_type=F32) + jnp.dot(m01, xl, preferred_element_type=F32)


def _dot_nt(a, b):
    return lax.dot_general(a, b, (((1,), (1,)), ((), ())), preferred_element_type=F32)


def _dot_tn(a, b):
    return lax.dot_general(a, b, (((0,), (0,)), ((), ())), preferred_element_type=F32)


def _rwkv_body(z_ref, shift_ref, s0_ref, mu_ref, w0_ref, w2_ref, a0_ref, a2_ref, g2_ref, kk_ref, ka_ref,
               rk_ref, lng_ref, lnb_ref, o_ref, s_out_ref, shift_out_ref, state_ref, prev_ref, *, T):
    ci = pl.program_id(1)
    nc = pl.num_programs(1)
    W = BRANCH_W
    N = RW_HEADS * T

    @pl.when(ci == 0)
    def _():
        state_ref[...] = s0_ref[0]
        prev_ref[...] = shift_ref[0]

    z = z_ref[0]
    row = lax.broadcasted_iota(jnp.int32, (T, RW_COLS), 0)
    z_prev = jnp.where(row == 0, jnp.broadcast_to(prev_ref[...], (T, RW_COLS)), pltpu.roll(z, 1, axis=0))
    prev_ref[...] = z[T - 1:T, :]
    zs = z + (z_prev - z) * mu_ref[...]
    r = zs[:, 0:W]
    k = zs[:, W:2 * W]
    v = zs[:, 2 * W:3 * W]
    xwa = zs[:, 3 * W:3 * W + 128]
    xg = zs[:, 3 * W + 128:]
    w_log = -jax.nn.softplus(-(w0_ref[...] + _mm3(jnp.tanh(xwa), w2_ref[...]))) - 0.5
    logdec = -jnp.exp(w_log)
    a = jax.nn.sigmoid(a0_ref[...] + _mm3(xwa, a2_ref[...]))
    g = _mm3(jax.nn.sigmoid(xg), g2_ref[...])
    li = lax.broadcasted_iota(jnp.int32, (W, W), 0) // RW_HD
    lj = lax.broadcasted_iota(jnp.int32, (W, W), 1) // RW_HD
    head_sum = (li == lj).astype(BF16)
    kk = k * kk_ref[...]
    kk = kk * lax.rsqrt(_sum01(kk * kk, head_sum) + 1e-12)
    k2 = k * (1.0 + (a - 1.0) * ka_ref[...])
    beta = kk * a
    ti = lax.broadcasted_iota(jnp.int32, (T, T), 0)
    tj = lax.broadcasted_iota(jnp.int32, (T, T), 1)
    b = _sum01_left((ti >= tj).astype(BF16), logdec)
    b_last = b[T - 1:T, :]
    k_in = kk * jnp.exp(b - logdec)
    r_in = r * jnp.exp(b)
    inv_p = jnp.exp(-b)
    k_out = k2 * inv_p
    b_out = beta * inv_p
    to_end = jnp.exp(b_last - b)
    k_end = k2 * to_end
    b_end = beta * to_end

    sh = lax.broadcasted_iota(jnp.int32, (N, W), 0) // T
    sl = lax.broadcasted_iota(jnp.int32, (N, W), 1) // RW_HD
    own = sh == sl

    def stack(x, masked):
        xs = jnp.concatenate([x] * RW_HEADS, axis=0)
        return jnp.where(own, xs, 0.0) if masked else xs

    k_in_s = stack(k_in, True).astype(BF16)
    r_in_s = stack(r_in, True).astype(BF16)
    k_out_s = stack(k_out, False).astype(BF16)
    b_out_s = stack(b_out, False).astype(BF16)
    v_s = stack(v, True).astype(BF16)
    k_end_s = stack(k_end, True).astype(BF16)
    b_end_s = stack(b_end, True).astype(BF16)

    ri = lax.broadcasted_iota(jnp.int32, (N, N), 0)
    rj = lax.broadcasted_iota(jnp.int32, (N, N), 1)
    same_head = (ri // T) == (rj // T)
    strict = same_head & (ri > rj)
    incl = same_head & (ri >= rj)
    a_mat = jnp.where(strict, _dot_nt(k_in_s, b_out_s), 0.0)
    kk_mat = jnp.where(strict, _dot_nt(k_in_s, k_out_s), 0.0)
    rk_mat = jnp.where(incl, _dot_nt(r_in_s, k_out_s), 0.0)
    rb_mat = jnp.where(incl, _dot_nt(r_in_s, b_out_s), 0.0)

    eye = (ri == rj).astype(F32)
    a_bd = jnp.where((ri // RW_SUB) == (rj // RW_SUB), a_mat, 0.0)
    x = eye - a_bd
    p = _mm3(a_bd, a_bd)
    x = x + _mm3(x, p)
    p = _mm3(p, p)
    x = x + _mm3(x, p)
    p = _mm3(p, p)
    x = x + _mm3(x, p)
    size = RW_SUB
    while size < T:
        lower = ((ri // (2 * size)) == (rj // (2 * size))) & ((ri // size) > (rj // size))
        x = x - _mm3(_mm3(x, jnp.where(lower, a_mat, 0.0)), x)
        size *= 2

    state = state_ref[...]
    state_b = state.astype(BF16)
    d = partial(jnp.dot, preferred_element_type=F32)
    rhs = _dot_nt(k_in_s, state_b) + d(kk_mat.astype(BF16), v_s)
    u = _mm3(x, rhs)
    u_b = u.astype(BF16)
    o_s = _dot_nt(r_in_s, state_b) + d(rk_mat.astype(BF16), v_s) - d(rb_mat.astype(BF16), u_b)
    o = o_s[0:T]
    for h in range(1, RW_HEADS):
        o = o + o_s[h * T:(h + 1) * T]
    state_ref[...] = state * jnp.exp(b_last) + _dot_tn(v_s, k_end_s) - _dot_tn(u_b, b_end_s)

    mean = _sum01(o, head_sum) * (1.0 / RW_HD)
    var = _sum01(jnp.square(o - mean), head_sum) * (1.0 / RW_HD)
    o = (o - mean) * lax.rsqrt(var + RW_GN_EPS) * lng_ref[...] + lnb_ref[...]
    bonus = _sum01(r * k2 * rk_ref[...], head_sum) * v
    o_ref[0] = ((o + bonus) * g).astype(o_ref.dtype)

    @pl.when(ci == nc - 1)
    def _():
        s_out_ref[0] = state_ref[...]
        shift_out_ref[0] = z[T - 1:T, :]


def _rwkv7_pallas(z, s0_bd, shift_prev, mu, w0, w2, a0, a2, g2, k_k, k_a, r_k, ln_g, ln_b, *, T, out_dtype=F32):
    B, L, C = z.shape
    assert C == RW_COLS and L % T == 0 and T % RW_SUB == 0
    W = BRANCH_W
    w2p = jnp.concatenate([w2, jnp.zeros_like(w2)], axis=0)
    a2p = jnp.concatenate([jnp.zeros_like(a2), a2], axis=0)
    vec = lambda t: t.reshape(1, -1).astype(F32)
    full = lambda shape: pl.BlockSpec(shape, lambda b, c: (0,) * len(shape))
    return pl.pallas_call(
        partial(_rwkv_body, T=T),
        grid=(B, L // T),
        in_specs=[pl.BlockSpec((1, T, C), lambda b, c: (b, c, 0)),
                  pl.BlockSpec((1, 1, C), lambda b, c: (b, 0, 0)),
                  pl.BlockSpec((1, W, W), lambda b, c: (b, 0, 0)),
                  full((1, C)), full((1, W)), full((128, W)), full((1, W)), full((128, W)), full((128, W)),
                  full((1, W)), full((1, W)), full((1, W)), full((1, W)), full((1, W))],
        out_specs=[pl.BlockSpec((1, T, W), lambda b, c: (b, c, 0)),
                   pl.BlockSpec((1, W, W), lambda b, c: (b, 0, 0)),
                   pl.BlockSpec((1, 1, C), lambda b, c: (b, 0, 0))],
        out_shape=[jax.ShapeDtypeStruct((B, L, W), out_dtype),
                   jax.ShapeDtypeStruct((B, W, W), F32),
                   jax.ShapeDtypeStruct((B, 1, C), F32)],
        scratch_shapes=[pltpu.VMEM((W, W), F32), pltpu.VMEM((1, C), F32)],
        compiler_params=pltpu.CompilerParams(dimension_semantics=("arbitrary", "arbitrary")),
        name="rwkv7_chunked",
    )(z, shift_prev.reshape(B, 1, C), s0_bd, vec(mu), vec(w0), w2p, vec(a0), a2p, g2, vec(k_k), vec(k_a),
      vec(r_k), vec(ln_g), vec(ln_b))


def _state_to_bd(s):
    B = s.shape[0]
    eye = jnp.eye(RW_HEADS, dtype=s.dtype)
    return (s[:, :, :, None, :] * eye[None, :, None, :, None]).reshape(B, BRANCH_W, BRANCH_W)


def _state_from_bd(s_bd):
    B = s_bd.shape[0]
    s5 = s_bd.reshape(B, RW_HEADS, RW_HD, RW_HEADS, RW_HD)
    return jnp.stack([s5[:, h, :, h, :] for h in range(RW_HEADS)], axis=1)


def _hgrn_body(z_ref, s0_ref, loglb_ref, log1mlb_ref, ng_ref, o_ref, s_out_ref, state_ref, *, T):
    ci = pl.program_id(1)
    nc = pl.num_programs(1)
    W = BRANCH_W

    @pl.when(ci == 0)
    def _():
        state_ref[...] = s0_ref[0]

    z = z_ref[0]
    zq, zf, zi, zg = z[:, 0:W], z[:, W:2 * W], z[:, 2 * W:3 * W], z[:, 3 * W:]
    log_sig = jnp.minimum(zf, 0.0) - jnp.log(1.0 + jnp.exp(-jnp.abs(zf)))
    a = loglb_ref[...]
    bv = log1mlb_ref[...] + log_sig
    m = jnp.maximum(a, bv)
    log_f = m + jnp.log(jnp.exp(a - m) + jnp.exp(bv - m))
    k = 1.0 - jnp.exp(log_f)
    q = zq * jax.nn.sigmoid(zq)
    ti = lax.broadcasted_iota(jnp.int32, (T, T), 0)
    tj = lax.broadcasted_iota(jnp.int32, (T, T), 1)
    b = _sum01_left((ti >= tj).astype(BF16), log_f)
    li = lax.broadcasted_iota(jnp.int32, (W, W), 0) // HG_DK
    lj = lax.broadcasted_iota(jnp.int32, (W, W), 1) // HG_DK
    same_head = li == lj
    head_sum = same_head.astype(BF16)

    prods, first = [], []
    for s in range(T):
        r0 = (s // SUBLANES) * SUBLANES
        t_idx = r0 + lax.broadcasted_iota(jnp.int32, (T - r0, W), 0)
        e = jnp.exp(jnp.where(t_idx >= s, b[r0:, :] - b[s:s + 1, :], -jnp.inf))
        prods.append((q[r0:, :] * e * k[s:s + 1, :]).astype(BF16))
        first.append(r0)
    scores = jnp.dot(jnp.concatenate(prods, axis=0), head_sum, preferred_element_type=F32)
    groups = [jnp.zeros((SUBLANES, W), F32) for _ in range(T // SUBLANES)]
    at = 0
    for s in range(T):
        for gi in range(first[s] // SUBLANES, T // SUBLANES):
            lo = at + gi * SUBLANES - first[s]
            groups[gi] = groups[gi] + scores[lo:lo + SUBLANES, :] * zi[s:s + 1, :]
        at += T - first[s]
    o = jnp.concatenate(groups, axis=0)

    state = state_ref[...]
    o = o + _dot_nt((q * jnp.exp(b)).astype(BF16), state.astype(BF16))
    b_last = b[T - 1:T, :]
    k_end = (k * jnp.exp(b_last - b)).astype(BF16)
    upd = _dot_tn(zi.astype(BF16), k_end)
    state_ref[...] = state * jnp.exp(b_last) + jnp.where(same_head, upd, 0.0)

    ms = _sum01(o * o, head_sum) * (1.0 / HG_DK)
    o = o * lax.rsqrt(ms + EPS) * ng_ref[...]
    o_ref[0] = (o * (zg * jax.nn.sigmoid(zg))).astype(o_ref.dtype)

    @pl.when(ci == nc - 1)
    def _():
        s_out_ref[0] = state_ref[...]


def _hgrn2_pallas(z, s0_bd, lb, norm_g, *, T, out_dtype=F32):
    B, L, C = z.shape
    assert C == HG_COLS and L % T == 0
    W = BRANCH_W
    full = lambda shape: pl.BlockSpec(shape, lambda b, c: (0,) * len(shape))
    return pl.pallas_call(
        partial(_hgrn_body, T=T),
        grid=(B, L // T),
        in_specs=[pl.BlockSpec((1, T, C), lambda b, c: (b, c, 0)),
                  pl.BlockSpec((1, W, W), lambda b, c: (b, 0, 0)),
                  full((1, W)), full((1, W)), full((1, W))],
        out_specs=[pl.BlockSpec((1, T, W), lambda b, c: (b, c, 0)),
                   pl.BlockSpec((1, W, W), lambda b, c: (b, 0, 0))],
        out_shape=[jax.ShapeDtypeStruct((B, L, W), out_dtype),
                   jax.ShapeDtypeStruct((B, W, W), F32)],
        scratch_shapes=[pltpu.VMEM((W, W), F32)],
        compiler_params=pltpu.CompilerParams(dimension_semantics=("arbitrary", "arbitrary")),
        name="hgrn2_chunked",
    )(z, s0_bd, jnp.log(lb).reshape(1, W), jnp.log1p(-lb).reshape(1, W), jnp.tile(norm_g, HG_HEADS).reshape(1, W))


PROJ_TM = 256
PROJ_VMEM_LIMIT_V7X = 48 * 1024 * 1024
CONV_CARRY = 32
CONV_TC = 512


def _in_proj_body(x_ref, g_ref, w_ref, q_ref, k_ref, v_ref, kb_ref, vb_ref, hg_ref, cv_ref, rw_ref, gate_ref):
    x = x_ref[...]
    h = (x * lax.rsqrt(jnp.mean(x * x, axis=-1, keepdims=True) + EPS) * g_ref[...]).astype(BF16)
    W = BRANCH_W

    def proj(a, b):
        return jnp.dot(h, w_ref[:, a:b], preferred_element_type=F32)

    q_ref[...] = (proj(0, W) * SB_Q_SCALE).astype(BF16)
    k = proj(W, 2 * W)
    v = proj(2 * W, 3 * W)
    k_ref[...] = k
    v_ref[...] = v
    kb_ref[...] = k.astype(BF16)
    vb_ref[...] = v.astype(BF16)
    c = SB_COLS
    hg_ref[...] = proj(c, c + HG_COLS)
    c += HG_COLS
    cv_ref[...] = proj(c, c + CV_COLS)
    c += CV_COLS
    rw_ref[...] = proj(c, c + RW_COLS)
    c += RW_COLS
    for n in range(N_BRANCH):
        gate_ref[:, n * D_MODEL:(n + 1) * D_MODEL] = jax.nn.sigmoid(
            proj(c + n * D_MODEL, c + (n + 1) * D_MODEL)).astype(BF16)


def _in_proj(x, norm_g, w_in_b):
    B, L, _ = x.shape
    TM = min(PROJ_TM, L)
    assert L % TM == 0
    row = lambda w: pl.BlockSpec((None, TM, w), lambda b, i: (b, i, 0))
    shp = lambda w, dt: jax.ShapeDtypeStruct((B, L, w), dt)
    W = BRANCH_W
    return pl.pallas_call(
        _in_proj_body,
        grid=(B, L // TM),
        in_specs=[row(D_MODEL), pl.BlockSpec((1, D_MODEL), lambda b, i: (0, 0)),
                  pl.BlockSpec((D_MODEL, IN_COLS), lambda b, i: (0, 0), pipeline_mode=pl.Buffered(1))],
        out_specs=[row(W), row(W), row(W), row(W), row(W), row(HG_COLS), row(CV_COLS), row(RW_COLS), row(GATE_COLS)],
        out_shape=[shp(W, BF16), shp(W, F32), shp(W, F32), shp(W, BF16), shp(W, BF16),
                   shp(HG_COLS, F32), shp(CV_COLS, F32), shp(RW_COLS, F32), shp(GATE_COLS, BF16)],
        compiler_params=pltpu.CompilerParams(dimension_semantics=("arbitrary", "arbitrary"),
                                             vmem_limit_bytes=PROJ_VMEM_LIMIT_V7X),
        name="in_proj",
    )(x, norm_g.reshape(1, -1), w_in_b)


def _merge_body(x_ref, sb_ref, hg_ref, cv_ref, rw_ref, gate_ref, wb_ref, wo_ref, o_ref):
    merged = None
    for n, br in enumerate((sb_ref, hg_ref, cv_ref, rw_ref)):
        t = jnp.dot(br[...].astype(BF16), wb_ref[n], preferred_element_type=F32)
        t = t * gate_ref[:, n * D_MODEL:(n + 1) * D_MODEL].astype(F32)
        merged = t if merged is None else merged + t
    o_ref[...] = x_ref[...] + jnp.dot(merged.astype(BF16), wo_ref[...], preferred_element_type=F32)


def _merge(x, o_sb, o_hg, o_cv, o_rw, gates, wb_b, wo_b):
    B, L, _ = x.shape
    TM = min(PROJ_TM, L)
    row = lambda w: pl.BlockSpec((None, TM, w), lambda b, i: (b, i, 0))
    return pl.pallas_call(
        _merge_body,
        grid=(B, L // TM),
        in_specs=[row(D_MODEL), row(BRANCH_W), row(BRANCH_W), row(BRANCH_W), row(BRANCH_W), row(GATE_COLS),
                  pl.BlockSpec((N_BRANCH, BRANCH_W, D_MODEL), lambda b, i: (0, 0, 0)),
                  pl.BlockSpec((D_MODEL, D_MODEL), lambda b, i: (0, 0))],
        out_specs=row(D_MODEL),
        out_shape=jax.ShapeDtypeStruct((B, L, D_MODEL), F32),
        compiler_params=pltpu.CompilerParams(dimension_semantics=("arbitrary", "arbitrary"),
                                             vmem_limit_bytes=PROJ_VMEM_LIMIT_V7X),
        name="branch_merge",
    )(x, o_sb, o_hg, o_cv, o_rw, gates, wb_b, wo_b)


def _conv_body(z_ref, buf_ref, w_ref, b_ref, lg_ref, lb_ref, o_ref, buf_out_ref, hp_ref, *, TC):
    ci = pl.program_id(1)
    nc = pl.num_programs(1)

    @pl.when(ci == 0)
    def _():
        hp_ref[0:CONV_CARRY, :] = buf_ref[0]

    z = z_ref[0]
    hp_ref[CONV_CARRY:, :] = z[:, 0:CONV_CH] * jax.nn.sigmoid(z[:, CONV_CH:])
    first = CONV_CARRY - (CONV_WIDTH - 1)
    y = jnp.zeros((TC, CONV_CH), F32) + b_ref[...]
    for j in range(CONV_WIDTH):
        y = y + hp_ref[first + j:first + j + TC, :] * w_ref[j:j + 1, :]
    mu = jnp.mean(y, axis=-1, keepdims=True)
    d = y - mu
    var = jnp.mean(d * d, axis=-1, keepdims=True)
    y = d * lax.rsqrt(var + EPS) * lg_ref[...] + lb_ref[...]
    o_ref[0] = (y * jax.nn.sigmoid(y)).astype(o_ref.dtype)
    tail = hp_ref[TC:TC + CONV_CARRY, :]

    @pl.when(ci == nc - 1)
    def _():
        buf_out_ref[0] = tail

    hp_ref[0:CONV_CARRY, :] = tail


def _conformer_conv_pallas(z_cv, buf, w, b, ln_g, ln_b, *, TC, out_dtype=F32):
    B, L, _ = z_cv.shape
    assert L % TC == 0 and TC % 8 == 0 and TC >= CONV_CARRY
    C = CONV_CH
    pad = CONV_CARRY - (CONV_WIDTH - 1)
    buf_p = jnp.pad(buf.astype(F32), ((0, 0), (pad, 0), (0, 0)))
    w_p = jnp.pad(w.astype(F32), ((0, 32 - CONV_WIDTH), (0, 0)))
    full = lambda shape: pl.BlockSpec(shape, lambda b, c: (0,) * len(shape))
    y, buf_new = pl.pallas_call(
        partial(_conv_body, TC=TC),
        grid=(B, L // TC),
        in_specs=[pl.BlockSpec((1, TC, 2 * C), lambda b, c: (b, c, 0)),
                  pl.BlockSpec((1, CONV_CARRY, C), lambda b, c: (b, 0, 0)),
                  full((32, C)), full((1, C)), full((1, C)), full((1, C))],
        out_specs=[pl.BlockSpec((1, TC, C), lambda b, c: (b, c, 0)),
                   pl.BlockSpec((1, CONV_CARRY, C), lambda b, c: (b, 0, 0))],
        out_shape=[jax.ShapeDtypeStruct((B, L, C), out_dtype), jax.ShapeDtypeStruct((B, CONV_CARRY, C), F32)],
        scratch_shapes=[pltpu.VMEM((CONV_CARRY + TC, C), F32)],
        compiler_params=pltpu.CompilerParams(dimension_semantics=("arbitrary", "arbitrary")),
        name="conformer_conv",
    )(z_cv, buf_p, w_p, b.reshape(1, C), ln_g.reshape(1, C), ln_b.reshape(1, C))
    return y, buf_new[:, pad:, :]


PK_PAIRS = PK_HEADS * PK_TOPK
PEER_HEAD_GROUP = 4
PEER_TB = 128
ROW_TILES = D_MODEL // LANES
PEER_VMEM_LIMIT_V7X = 52 * 1024 * 1024


def _topk_rows(scores, payloads, k):
    n = scores[0].shape[0]
    rows = lax.broadcasted_iota(jnp.int32, scores[0].shape, 0).astype(F32)
    scores = list(scores)
    vals = [[] for _ in scores]
    ids = [[] for _ in scores]
    for _ in range(k):
        for c, s in enumerate(scores):
            m = jnp.max(s, axis=0, keepdims=True)
            pos = jnp.min(jnp.where(s == m, rows, float(n)), axis=0, keepdims=True)
            sel = rows == pos
            ids[c].append(pos if payloads[c] is None
                          else jnp.max(jnp.where(sel, payloads[c], -1.0), axis=0, keepdims=True))
            vals[c].append(m)
            scores[c] = jnp.where(sel, -jnp.inf, s)
    return [(jnp.concatenate(v, axis=0), jnp.concatenate(i, axis=0)) for v, i in zip(vals, ids)]


def _dot3(ah, al, bh, bl, dims):
    d = partial(lax.dot_general, dimension_numbers=dims, preferred_element_type=F32)
    return d(ah, bh) + (d(ah, bl) + d(al, bh))


def _candidate_tables(tb):
    groups = [[0], [1], [2], [3, 4], [5, 6, 7], list(range(8, 16))]
    rows = []
    for grp in groups:
        part = [(a, b) for a in grp for b in range(PK_TOPK // (a + 1))]
        rows += part + [None] * ((-len(part)) % SUBLANES)
    n = len(rows)
    sel1 = [[1.0 if (r is not None and r[0] == j) else 0.0 for j in range(PK_TOPK)] for r in rows]
    sel2 = [[1.0 if (r is not None and r[1] == j) else 0.0 for j in range(PK_TOPK)] for r in rows]
    cpos = [[-1.0 if r is None else float(r[0] * PK_TOPK + r[1])] * tb for r in rows]
    return jnp.array(sel1, BF16), jnp.array(sel2, BF16), jnp.array(cpos, F32), n


def _pick_rows(sel, v):
    hi = v.astype(BF16)
    r1 = v - hi.astype(F32)
    mid = r1.astype(BF16)
    lo = (r1 - mid.astype(F32)).astype(BF16)
    d = partial(jnp.dot, preferred_element_type=F32)
    return (d(sel, hi) + d(sel, mid)) + d(sel, lo)


def _peer_route_body(x_ref, g_ref, wqh_ref, wql_ref, qg_ref, k1_ref, k2_ref, sel1_ref, sel2_ref, cpos_ref,
                     xn_ref, idx_ref, idxc_ref, gate_ref):
    TB = x_ref.shape[0]
    x = x_ref[...]
    xn = x * lax.rsqrt(jnp.mean(x * x, axis=-1, keepdims=True) + EPS) * g_ref[...]
    xn_ref[...] = xn
    nn = (((1,), (0,)), ((), ()))
    nt = (((1,), (1,)), ((), ()))
    xh, xl = _split(xn)
    q = _dot3(xh, xl, wqh_ref[...], wql_ref[...], nn)
    half = PK_DQ // 2
    experts, gates = [], []
    for h0 in range(0, PK_HEADS, PEER_HEAD_GROUP):
        heads = range(h0, h0 + PEER_HEAD_GROUP)
        subs = []
        for h in heads:
            qh = q[:, h * PK_DQ:(h + 1) * PK_DQ]
            qh = qh * lax.rsqrt(jnp.mean(qh * qh, axis=-1, keepdims=True) + EPS) * qg_ref[...]
            qhh, qhl = _split(qh)
            k1h, k1l = _split(k1_ref[h])
            k2h, k2l = _split(k2_ref[h])
            subs.append(_dot3(k1h, k1l, qhh[:, :half], qhl[:, :half], nt))
            subs.append(_dot3(k2h, k2l, qhh[:, half:], qhl[:, half:], nt))
        tops = _topk_rows(subs, [None] * len(subs), PK_TOPK)
        cands = []
        for j in range(PEER_HEAD_GROUP):
            (v1, _), (v2, _) = tops[2 * j], tops[2 * j + 1]
            cands.append(jnp.where(cpos_ref[...] < 0.0, -jnp.inf,
                                   _pick_rows(sel1_ref[...], v1) + _pick_rows(sel2_ref[...], v2)))
        best = _topk_rows(cands, [cpos_ref[...]] * PEER_HEAD_GROUP, PK_TOPK)
        for j in range(PEER_HEAD_GROUP):
            i1, i2 = tops[2 * j][1], tops[2 * j + 1][1]
            top_s, pos = best[j]
            pa = jnp.floor(pos * (1.0 / PK_TOPK))
            pb = pos - pa * PK_TOPK
            e1 = jnp.zeros_like(pos)
            e2 = jnp.zeros_like(pos)
            for a in range(PK_TOPK):
                e1 = jnp.where(pa == a, i1[a:a + 1, :], e1)
                e2 = jnp.where(pb == a, i2[a:a + 1, :], e2)
            e = jnp.exp(top_s - jnp.max(top_s, axis=0, keepdims=True))
            gates.append(e / jnp.sum(e, axis=0, keepdims=True))
            experts.append(e1 * float(PK_NKEYS) + e2)
    expert = jnp.concatenate(experts, axis=0).astype(jnp.int32)
    row = (expert + 1) * HALF_TILES
    pair = lax.broadcasted_iota(jnp.int32, row.shape, 0)
    row_t = row.T
    rowc_t = (row - (pair & HALF_TILES)).T
    for t in range(TB):
        idx_ref[0, :, t * PK_PAIRS:(t + 1) * PK_PAIRS] = row_t[t:t + 1, :]
        idxc_ref[0, :, t * PK_PAIRS:(t + 1) * PK_PAIRS] = rowc_t[t:t + 1, :]
    gate_ref[...] = jnp.concatenate(gates, axis=0)


def _peer_route(x2, norm_g, wq, qn_g, k1, k2):
    n_tok = x2.shape[0]
    TB = PEER_TB
    assert n_tok % TB == 0
    nb = n_tok // TB
    full = lambda shape: pl.BlockSpec(shape, lambda i: (0,) * len(shape))
    wq_hi = wq.astype(BF16)
    wq_lo = (wq - wq_hi.astype(F32)).astype(BF16)
    sel1, sel2, cpos, n_cand = _candidate_tables(TB)
    return pl.pallas_call(
        _peer_route_body,
        grid=(nb,),
        in_specs=[pl.BlockSpec((TB, D_MODEL), lambda i: (i, 0)), full((1, D_MODEL)),
                  full((D_MODEL, PK_HEADS * PK_DQ)), full((D_MODEL, PK_HEADS * PK_DQ)), full((1, PK_DQ)),
                  full((PK_HEADS, PK_NKEYS, PK_DQ // 2)), full((PK_HEADS, PK_NKEYS, PK_DQ // 2)),
                  full((n_cand, PK_TOPK)), full((n_cand, PK_TOPK)), full((n_cand, TB))],
        out_specs=[pl.BlockSpec((TB, D_MODEL), lambda i: (i, 0)),
                   pl.BlockSpec((1, 1, TB * PK_PAIRS), lambda i: (i, 0, 0)),
                   pl.BlockSpec((1, 1, TB * PK_PAIRS), lambda i: (i, 0, 0)),
                   pl.BlockSpec((PK_PAIRS, TB), lambda i: (0, i))],
        out_shape=[jax.ShapeDtypeStruct((n_tok, D_MODEL), F32),
                   jax.ShapeDtypeStruct((nb, 1, TB * PK_PAIRS), jnp.int32),
                   jax.ShapeDtypeStruct((nb, 1, TB * PK_PAIRS), jnp.int32),
                   jax.ShapeDtypeStruct((PK_PAIRS, n_tok), F32)],
        compiler_params=pltpu.CompilerParams(dimension_semantics=("arbitrary",)),
        name="peer_route",
    )(x2, norm_g.reshape(1, -1), wq_hi, wq_lo, qn_g.reshape(1, -1), k1, k2, sel1, sel2, cpos)


HALF_TILES = ROW_TILES // 2


def _unpack_words(word):
    return pltpu.bitcast(word << 16, F32), pltpu.bitcast(word & jnp.int32(-65536), F32)


def _fold_couples(tiles, sub):
    m2 = (sub & 3) < 2
    lvl2 = []
    for a, b in ((tiles[0], tiles[2]), (tiles[1], tiles[3])):
        lvl2.append(jnp.where(m2, a, b) + jnp.where(m2, pltpu.roll(a, 6, axis=0), pltpu.roll(b, 2, axis=0)))
    m1 = (sub & 1) == 0
    a, b = lvl2
    return jnp.where(m1, a, b) + jnp.where(m1, pltpu.roll(a, 7, axis=0), pltpu.roll(b, 1, axis=0))


def _peer_act_body(idx_ref, x_ref, gate_ref, tab_ref, w_ref, part_ref):
    TB = x_ref.shape[0]
    sub = lax.broadcasted_iota(jnp.int32, (SUBLANES, LANES), 0)
    low = sub < HALF_TILES

    def token(t, c):
        x = x_ref[pl.ds(t, 1), :]
        tiles_x = [x[:, j * LANES:(j + 1) * LANES] for j in range(ROW_TILES)]
        x_lo = jnp.concatenate(tiles_x[:HALF_TILES] * 2, axis=0)
        x_hi = jnp.concatenate(tiles_x[HALF_TILES:] * 2, axis=0)
        start = t * PK_PAIRS
        for g in range(PK_PAIRS // SUBLANES):
            tiles = []
            for j in range(HALF_TILES):
                ra = pl.multiple_of(idx_ref[0, 0, start + g * SUBLANES + j], HALF_TILES)
                rb = pl.multiple_of(idx_ref[0, 0, start + g * SUBLANES + j + HALF_TILES], HALF_TILES)
                word = jnp.where(low, tab_ref[pl.ds(ra, SUBLANES), :], tab_ref[pl.ds(rb, SUBLANES), :])
                lo, hi = _unpack_words(word)
                tiles.append(lo * x_lo + hi * x_hi)
            part_ref[t, g * SUBLANES:(g + 1) * SUBLANES, :] = _fold_couples(tiles, sub)
        return c

    lax.fori_loop(0, TB, token, 0)
    lane = lax.broadcasted_iota(jnp.int32, (PK_PAIRS, TB), 1)
    s = jnp.zeros((PK_PAIRS, TB), F32)
    ones = jnp.ones((LANES, LANES), BF16)
    for t in range(TB):
        tot = jnp.dot(part_ref[t].astype(BF16), ones, preferred_element_type=F32)
        s = jnp.where(lane == t, tot, s)
    w_ref[...] = gate_ref[...] * jax.nn.gelu(s)


PEER_MIX_ACCS = 4


def _peer_mix_body(idx_ref, w_ref, tab_ref, y_ref, wrep_ref):
    TB = y_ref.shape[0]
    w_all = w_ref[...]
    for t in range(TB):
        wrep_ref[t] = jnp.broadcast_to(w_all[:, t:t + 1], (PK_PAIRS, LANES))

    def token(t, c):
        start = t * PK_PAIRS
        acc_lo = [jnp.zeros((HALF_TILES, LANES), F32) for _ in range(PEER_MIX_ACCS)]
        acc_hi = [jnp.zeros((HALF_TILES, LANES), F32) for _ in range(PEER_MIX_ACCS)]
        for p in range(PK_PAIRS):
            row = pl.multiple_of(idx_ref[0, 0, start + p], HALF_TILES)
            lo, hi = _unpack_words(tab_ref[pl.ds(row, HALF_TILES), :])
            w = jnp.broadcast_to(wrep_ref[t, p:p + 1, :], (HALF_TILES, LANES))
            acc_lo[p % PEER_MIX_ACCS] = acc_lo[p % PEER_MIX_ACCS] + w * lo
            acc_hi[p % PEER_MIX_ACCS] = acc_hi[p % PEER_MIX_ACCS] + w * hi
        y_ref[t] = jnp.concatenate([(acc_lo[0] + acc_lo[1]) + (acc_lo[2] + acc_lo[3]),
                                    (acc_hi[0] + acc_hi[1]) + (acc_hi[2] + acc_hi[3])], axis=0)
        return c

    lax.fori_loop(0, TB, token, 0)


def _pack_table(tab):
    bits = lax.bitcast_convert_type(tab.astype(BF16), jnp.uint16).astype(jnp.uint32)
    bits = bits.reshape(PK_EXPERTS, 2, HALF_TILES, LANES)
    word = bits[:, 0] | (bits[:, 1] << 16)
    rows = lax.bitcast_convert_type(word, jnp.int32).reshape(PK_EXPERTS * HALF_TILES, LANES)
    return jnp.pad(rows, ((HALF_TILES, HALF_TILES), (0, 0)))


def _smem_spec(TB):
    return pl.BlockSpec((1, 1, PK_PAIRS * TB), lambda i: (i, 0, 0), memory_space=pltpu.SMEM)


def _table_spec():
    return pl.BlockSpec(((PK_EXPERTS + 2) * HALF_TILES, LANES), lambda i: (0, 0), pipeline_mode=pl.Buffered(1))


def _peer_act(idx, xn, gate_t, tab_packed):
    n_tok = xn.shape[0]
    TB = PEER_TB
    nb = n_tok // TB
    return pl.pallas_call(
        _peer_act_body,
        grid=(nb,),
        in_specs=[_smem_spec(TB),
                  pl.BlockSpec((TB, D_MODEL), lambda i: (i, 0)),
                  pl.BlockSpec((PK_PAIRS, TB), lambda i: (0, i)),
                  _table_spec()],
        out_specs=pl.BlockSpec((PK_PAIRS, TB), lambda i: (0, i)),
        out_shape=jax.ShapeDtypeStruct((PK_PAIRS, n_tok), F32),
        scratch_shapes=[pltpu.VMEM((TB, PK_PAIRS, LANES), F32)],
        compiler_params=pltpu.CompilerParams(dimension_semantics=("arbitrary",),
                                             vmem_limit_bytes=PEER_VMEM_LIMIT_V7X),
        name="peer_act",
    )(idx, xn, gate_t, tab_packed)


def _peer_mix(idx, w_t, tab_packed):
    n_tok = w_t.shape[1]
    TB = PEER_TB
    nb = n_tok // TB
    return pl.pallas_call(
        _peer_mix_body,
        grid=(nb,),
        in_specs=[_smem_spec(TB),
                  pl.BlockSpec((PK_PAIRS, TB), lambda i: (0, i)),
                  _table_spec()],
        out_specs=pl.BlockSpec((TB, ROW_TILES, LANES), lambda i: (i, 0, 0)),
        out_shape=jax.ShapeDtypeStruct((n_tok, ROW_TILES, LANES), F32),
        scratch_shapes=[pltpu.VMEM((TB, PK_PAIRS, LANES), F32)],
        compiler_params=pltpu.CompilerParams(dimension_semantics=("arbitrary",),
                                             vmem_limit_bytes=PEER_VMEM_LIMIT_V7X),
        name="peer_mix",
    )(idx, w_t, tab_packed)


def _peer_ffn_pallas(x, norm_g, wq, qn_g, sub_keys, u_packed, v_packed):
    B, L, D = x.shape
    n_tok = B * L
    pad = (-n_tok) % PEER_TB
    x2 = jnp.pad(x.reshape(n_tok, D), ((0, pad), (0, 0)))
    xn, idx, idx_couple, gate_t = _peer_route(x2, norm_g, wq, qn_g, sub_keys[:, 0], sub_keys[:, 1])
    w_t = _peer_act(idx_couple, xn, gate_t, u_packed)
    y = _peer_mix(idx, w_t, v_packed)
    return y.reshape(-1, D)[:n_tok].reshape(B, L, D)


def _trunk_layer(x, k_past, v_past, s_hg, buf_conv, s_rw, buf_shift, p):
    B_, L, _ = x.shape
    qb, k, v, kb, vb, z_hg, z_cv, z_rw, gates = _in_proj(x, p['norm_mix_g'], p['w_in'])
    if k_past is None:
        o_sb = _sb_attention(qb, kb, vb, off=0, tq=SB_TQ, tk=SB_TK, out_dtype=F32)
    else:
        past = k_past.shape[1]
        pad = jnp.zeros((B_, (-(past + L)) % SB_TK, BRANCH_W), BF16)
        kc = jnp.concatenate([k_past.reshape(B_, past, BRANCH_W).astype(BF16), kb, pad], axis=1)
        vc = jnp.concatenate([v_past.reshape(B_, past, BRANCH_W).astype(BF16), vb, pad], axis=1)
        o_sb = _sb_attention(qb, kc, vc, off=past, tq=L, tk=SB_TK, out_dtype=F32)
    o_hg, s_hg_bd = _hgrn2_pallas(z_hg, _state_to_bd(jnp.swapaxes(s_hg.astype(F32), 2, 3)), p['lb'],
                                  p['hg_norm_g'], T=CHUNK if L % CHUNK == 0 else L)
    s_hg_new = jnp.swapaxes(_state_from_bd(s_hg_bd), 2, 3)
    o_cv, buf_conv_new = _conformer_conv_pallas(z_cv, buf_conv, p['conv_w'], p['conv_b'], p['conv_ln_g'],
                                                p['conv_ln_b'], TC=CONV_TC if L % CONV_TC == 0 else L)
    o_rw, s_rw_bd, shift_new = _rwkv7_pallas(z_rw, _state_to_bd(s_rw.astype(F32)), buf_shift, p['rw_mu'],
                                             p['rw_w0'], p['rw_w2'], p['rw_a0'], p['rw_a2'], p['rw_g2'], p['rw_k_k'],
                                             p['rw_k_a'], p['rw_r_k'], p['rw_ln_g'], p['rw_ln_b'],
                                             T=RW_CHUNK if L % RW_CHUNK == 0 else L)
    s_rw_new = _state_from_bd(s_rw_bd)
    buf_shift_new = shift_new[:, 0]
    x = _merge(x, o_sb, o_hg, o_cv, o_rw, gates, p['w_branch'], p['w_out'])
    x = x + _peer_ffn_pallas(x, p['norm_ffn_g'], p['peer_wq'], p['peer_qn_g'], p['peer_keys'],
                             p['peer_u'], p['peer_v'])
    heads = lambda t: t.reshape(B_, L, SB_HEADS, HEAD_DIM)
    return x, (heads(k), heads(v), s_hg_new, buf_conv_new, s_rw_new, buf_shift_new)


def kernel(x_prompt, x_sample, cache_sb_k, cache_sb_v, state_hgrn, state_conv, state_rwkv, state_shift,
           norm_mix_g, w_in, hg_lb_logits, hg_norm_g, conv_w, conv_b, conv_ln_g, conv_ln_b,
           rw_mu, rw_w0, rw_w2, rw_a0, rw_a2, rw_g2, rw_k_k, rw_k_a, rw_r_k, rw_ln_g, rw_ln_b,
           w_branch, w_out, norm_ffn_g, peer_wq, peer_qn_g, peer_keys, peer_u, peer_v, final_norm_g):
    lb_all = jnp.cumsum(jax.nn.softmax(hg_lb_logits.astype(F32), axis=0), axis=0)
    lb_all = lb_all - lb_all[0:1]
    Bp = x_prompt.shape[0]
    hg0 = jnp.zeros((Bp, HG_HEADS, HG_DK, HG_DV), F32)
    cv0 = jnp.zeros((Bp, CONV_WIDTH - 1, CONV_CH), x_prompt.dtype)
    rw0 = jnp.zeros((Bp, RW_HEADS, RW_HD, RW_HD), F32)
    sh0 = jnp.zeros((Bp, RW_COLS), x_prompt.dtype)
    xp, xs = x_prompt, x_sample
    outs_p, outs_s = [], []
    for l in range(DEPTH):
        lp = dict(norm_mix_g=norm_mix_g[l], w_in=w_in[l].astype(BF16), lb=lb_all[l], hg_norm_g=hg_norm_g[l],
                  conv_w=conv_w[l], conv_b=conv_b[l], conv_ln_g=conv_ln_g[l], conv_ln_b=conv_ln_b[l],
                  rw_mu=rw_mu[l], rw_w0=rw_w0[l], rw_w2=rw_w2[l], rw_a0=rw_a0[l], rw_a2=rw_a2[l],
                  rw_g2=rw_g2[l], rw_k_k=rw_k_k[l], rw_k_a=rw_k_a[l], rw_r_k=rw_r_k[l],
                  rw_ln_g=rw_ln_g[l], rw_ln_b=rw_ln_b[l], w_branch=w_branch[l].astype(BF16),
                  w_out=w_out[l].astype(BF16),
                  norm_ffn_g=norm_ffn_g[l], peer_wq=peer_wq[l], peer_qn_g=peer_qn_g[l],
                  peer_keys=peer_keys[l], peer_u=_pack_table(peer_u[l]), peer_v=_pack_table(peer_v[l]))
        xp, st_p = _trunk_layer(xp, None, None, hg0, cv0, rw0, sh0, lp)
        xs, st_s = _trunk_layer(xs, cache_sb_k[l], cache_sb_v[l], state_hgrn[l], state_conv[l],
                                state_rwkv[l], state_shift[l], lp)
        outs_p.append(st_p)
        outs_s.append(st_s)

    def stk(outs, i):
        return jnp.stack([o[i] for o in outs], axis=0)

    y_prompt = _rmsnorm_pallas(xp, final_norm_g)
    y_sample = _rmsnorm_pallas(xs, final_norm_g)
    return (y_prompt, y_sample,
            stk(outs_p, 0), stk(outs_p, 1), stk(outs_p, 2), stk(outs_p, 3), stk(outs_p, 4), stk(outs_p, 5),
            stk(outs_s, 0), stk(outs_s, 1), stk(outs_s, 2), stk(outs_s, 3), stk(outs_s, 4), stk(outs_s, 5))
```

```python
from functools import partial

import jax
import jax.numpy as jnp
from jax import lax
from jax.experimental import pallas as pl
from jax.experimental.pallas import tpu as pltpu

D_MODEL = 1024
DEPTH = 2
CHUNK = 64
HEAD_DIM = 64
N_BRANCH = 4
BRANCH_W = D_MODEL // 4
SB_HEADS = BRANCH_W // HEAD_DIM
HG_HEADS = 4
HG_DK = BRANCH_W // HG_HEADS
HG_DV = BRANCH_W // HG_HEADS
CONV_CH = BRANCH_W
CONV_WIDTH = 31
RW_HEADS = 4
RW_HD = BRANCH_W // RW_HEADS
RW_DECAY_LORA = 64
RW_AAA_LORA = 64
RW_GATE_LORA = 128
SB_COLS = 3 * BRANCH_W
HG_COLS = 4 * BRANCH_W
CV_COLS = 2 * CONV_CH
RW_COLS = 3 * BRANCH_W + RW_DECAY_LORA + RW_AAA_LORA + RW_GATE_LORA
GATE_COLS = N_BRANCH * D_MODEL
IN_COLS = SB_COLS + HG_COLS + CV_COLS + RW_COLS + GATE_COLS
PK_HEADS = 8
PK_NKEYS = 128
PK_EXPERTS = PK_NKEYS * PK_NKEYS
PK_DQ = 256
PK_TOPK = 16
EPS = 1e-6
RW_GN_EPS = 64e-5
SB_TQ = 256
SB_TK = 256
RW_CHUNK = 64
F32 = jnp.float32
BF16 = jnp.bfloat16
SUBLANES = 8
LANES = 128


def _rmsnorm_body(x_ref, g_ref, o_ref):
    x = x_ref[...]
    y = x * lax.rsqrt(jnp.mean(x * x, axis=-1, keepdims=True) + EPS)
    o_ref[...] = y * g_ref[...]


def _rmsnorm_pallas(x, g, rows=512):
    shape = x.shape
    x2 = x.reshape(-1, shape[-1])
    n, d = x2.shape
    rows = min(rows, n)
    out = pl.pallas_call(
        _rmsnorm_body,
        grid=(n // rows,),
        in_specs=[pl.BlockSpec((rows, d), lambda i: (i, 0)),
                  pl.BlockSpec((1, d), lambda i: (0, 0))],
        out_specs=pl.BlockSpec((rows, d), lambda i: (i, 0)),
        out_shape=jax.ShapeDtypeStruct((n, d), x.dtype),
        name="final_rmsnorm",
    )(x2, g.reshape(1, d))
    return out.reshape(shape)


VMEM_LIMIT_V7X = 48 * 1024 * 1024
SB_Q_SCALE = -(HEAD_DIM ** -0.5) * 1.4426950408889634


def _sb_attn_body(q_ref, k_ref, v_ref, o_ref, acc_ref, carry_ref, *, tq, tk, off, n_diag):
    qi = pl.program_id(1)
    q0 = off + qi * tq
    n_full = q0 // tk
    acc_ref[...] = jnp.zeros_like(acc_ref)
    carry_ref[...] = jnp.zeros_like(carry_ref)
    q_all = q_ref[0]
    lane = lax.broadcasted_iota(jnp.int32, (tq, LANES), 1)
    own = [(lane // HEAD_DIM) == (h % 2) for h in range(SB_HEADS)]
    q_heads = [jnp.where(own[h], q_all[:, (h // 2) * LANES:(h // 2 + 1) * LANES], jnp.zeros((), BF16))
               for h in range(SB_HEADS)]
    jj = lax.broadcasted_iota(jnp.int32, (tk, tk), 0)
    ss = lax.broadcasted_iota(jnp.int32, (tk, tk), 1)
    later_mat = (jj > ss).astype(BF16)

    def block(kb, masked):
        start = pl.multiple_of(kb * tk, tk)
        k_blk = k_ref[0, pl.ds(start, tk), :]
        v_blk = v_ref[0, pl.ds(start, tk), :]
        if masked:
            key_pos = start + lax.broadcasted_iota(jnp.int32, (tq, tk), 1)
            q_pos = q0 + lax.broadcasted_iota(jnp.int32, (tq, tk), 0)
            mask = key_pos < q_pos
        dims = (((1,), (1,)), ((), ()))
        nls = [lax.dot_general(q_heads[h], k_blk[:, (h // 2) * LANES:(h // 2 + 1) * LANES], dims,
                               preferred_element_type=F32) for h in range(SB_HEADS)]
        lsms, laters = [], []
        for h in range(SB_HEADS):
            nl = nls[h]
            neg_abs = pltpu.bitcast(pltpu.bitcast(nl, jnp.int32) | jnp.int32(-2 ** 31), F32)
            lsm = jnp.minimum(nl, 0.0) - jnp.log2(1.0 + jnp.exp2(neg_abs))
            if masked:
                lsm = jnp.where(mask, lsm, 0.0)
            lsms.append(lsm)
            laters.append(jnp.dot(lsm.astype(BF16), later_mat, preferred_element_type=F32))
        for h in range(SB_HEADS):
            carry = carry_ref[h]
            expo = (lsms[h] - nls[h]) + laters[h] + jnp.concatenate([carry] * (tk // LANES), axis=1)
            w = jnp.exp2(expo)
            if masked:
                w = jnp.where(mask, w, 0.0)
            acc_ref[h] += jnp.dot(w.astype(BF16), v_blk[:, (h // 2) * LANES:(h // 2 + 1) * LANES],
                                  preferred_element_type=F32)
            row = laters[h][:, 0:1] + lsms[h][:, 0:1]
            carry_ref[h] = carry + jnp.broadcast_to(row, carry.shape)

    for d in range(n_diag - 1, -1, -1):
        block(n_full + d, True)

    def full_step(i, c):
        block(n_full - 1 - i, False)
        return c

    lax.fori_loop(0, n_full, full_step, 0)
    o_ref[0] = jnp.concatenate([jnp.where(own[2 * p], acc_ref[2 * p], acc_ref[2 * p + 1])
                                for p in range(SB_HEADS // 2)], axis=1).astype(o_ref.dtype)


def _sb_attention(q, k, v, *, off, tq, tk, out_dtype=BF16):
    B, Lq, W = q.shape
    Lk = k.shape[1]
    assert W == BRANCH_W and Lq % tq == 0 and Lk % tk == 0 and tk % 128 == 0
    nq = Lq // tq
    assert nq == 1 or (tq % tk == 0 and off % tk == 0)
    n_diag = -(-((off % tk) + tq - 1) // tk)
    assert (off + Lq - 1 + tk - 1) // tk <= Lk // tk
    body = partial(_sb_attn_body, tq=tq, tk=tk, off=off, n_diag=n_diag)
    return pl.pallas_call(
        body,
        grid=(B, nq),
        in_specs=[pl.BlockSpec((1, tq, W), lambda b, i: (b, i, 0)),
                  pl.BlockSpec((1, Lk, W), lambda b, i: (b, 0, 0)),
                  pl.BlockSpec((1, Lk, W), lambda b, i: (b, 0, 0))],
        out_specs=pl.BlockSpec((1, tq, W), lambda b, i: (b, i, 0)),
        out_shape=jax.ShapeDtypeStruct((B, Lq, W), out_dtype),
        scratch_shapes=[pltpu.VMEM((SB_HEADS, tq, LANES), F32), pltpu.VMEM((SB_HEADS, tq, LANES), F32)],
        compiler_params=pltpu.CompilerParams(dimension_semantics=("arbitrary", "arbitrary"),
                                             vmem_limit_bytes=VMEM_LIMIT_V7X),
        name="sb_attention",
    )(q, k, v)


RW_SUB = 16


def _split(x):
    hi = x.astype(BF16)
    lo = (x - hi.astype(F32)).astype(BF16)
    return hi, lo


def _mm3(a, b):
    ah, al = _split(a)
    bh, bl = _split(b)
    d = partial(jnp.dot, preferred_element_type=F32)
    return d(ah, bh) + (d(ah, bl) + d(al, bh))


def _sum01(x, m01):
    xh, xl = _split(x)
    return jnp.dot(xh, m01, preferred_element_type=F32) + jnp.dot(xl, m01, preferred_element_type=F32)


def _sum01_left(m01, x):
    xh, xl = _split(x)
    return jnp.dot(m01, xh, preferred_element_type=F32) + jnp.dot(m01, xl, preferred_element_type=F32)


def _dot_nt(a, b):
    return lax.dot_general(a, b, (((1,), (1,)), ((), ())), preferred_element_type=F32)


def _dot_tn(a, b):
    return lax.dot_general(a, b, (((0,), (0,)), ((), ())), preferred_element_type=F32)


def _rwkv_body(z_ref, shift_ref, s0_ref, mu_ref, w0_ref, w2_ref, a0_ref, a2_ref, g2_ref, kk_ref, ka_ref,
               rk_ref, lng_ref, lnb_ref, o_ref, s_out_ref, shift_out_ref, state_ref, prev_ref, *, T):
    ci = pl.program_id(1)
    nc = pl.num_programs(1)
    W = BRANCH_W
    N = RW_HEADS * T

    @pl.when(ci == 0)
    def _():
        state_ref[...] = s0_ref[0]
        prev_ref[...] = shift_ref[0]

    z = z_ref[0]
    row = lax.broadcasted_iota(jnp.int32, (T, RW_COLS), 0)
    z_prev = jnp.where(row == 0, jnp.broadcast_to(prev_ref[...], (T, RW_COLS)), pltpu.roll(z, 1, axis=0))
    prev_ref[...] = z[T - 1:T, :]
    zs = z + (z_prev - z) * mu_ref[...]
    r = zs[:, 0:W]
    k = zs[:, W:2 * W]
    v = zs[:, 2 * W:3 * W]
    xwa = zs[:, 3 * W:3 * W + 128]
    xg = zs[:, 3 * W + 128:]
    w_log = -jax.nn.softplus(-(w0_ref[...] + _mm3(jnp.tanh(xwa), w2_ref[...]))) - 0.5
    logdec = -jnp.exp(w_log)
    a = jax.nn.sigmoid(a0_ref[...] + _mm3(xwa, a2_ref[...]))
    g = _mm3(jax.nn.sigmoid(xg), g2_ref[...])
    li = lax.broadcasted_iota(jnp.int32, (W, W), 0) // RW_HD
    lj = lax.broadcasted_iota(jnp.int32, (W, W), 1) // RW_HD
    head_sum = (li == lj).astype(BF16)
    kk = k * kk_ref[...]
    kk = kk * lax.rsqrt(_sum01(kk * kk, head_sum) + 1e-12)
    k2 = k * (1.0 + (a - 1.0) * ka_ref[...])
    beta = kk * a
    ti = lax.broadcasted_iota(jnp.int32, (T, T), 0)
    tj = lax.broadcasted_iota(jnp.int32, (T, T), 1)
    b = _sum01_left((ti >= tj).astype(BF16), logdec)
    b_last = b[T - 1:T, :]
    k_in = kk * jnp.exp(b - logdec)
    r_in = r * jnp.exp(b)
    inv_p = jnp.exp(-b)
    k_out = k2 * inv_p
    b_out = beta * inv_p
    to_end = jnp.exp(b_last - b)
    k_end = k2 * to_end
    b_end = beta * to_end

    sh = lax.broadcasted_iota(jnp.int32, (N, W), 0) // T
    sl = lax.broadcasted_iota(jnp.int32, (N, W), 1) // RW_HD
    own = sh == sl

    def stack(x, masked):
        xs = jnp.concatenate([x] * RW_HEADS, axis=0)
        return jnp.where(own, xs, 0.0) if masked else xs

    k_in_s = stack(k_in, True).astype(BF16)
    r_in_s = stack(r_in, True).astype(BF16)
    k_out_s = stack(k_out, False).astype(BF16)
    b_out_s = stack(b_out, False).astype(BF16)
    v_s = stack(v, True).astype(BF16)
    k_end_s = stack(k_end, True).astype(BF16)
    b_end_s = stack(b_end, True).astype(BF16)

    ri = lax.broadcasted_iota(jnp.int32, (N, N), 0)
    rj = lax.broadcasted_iota(jnp.int32, (N, N), 1)
    same_head = (ri // T) == (rj // T)
    strict = same_head & (ri > rj)
    incl = same_head & (ri >= rj)
    a_mat = jnp.where(strict, _dot_nt(k_in_s, b_out_s), 0.0)
    kk_mat = jnp.where(strict, _dot_nt(k_in_s, k_out_s), 0.0)
    rk_mat = jnp.where(incl, _dot_nt(r_in_s, k_out_s), 0.0)
    rb_mat = jnp.where(incl, _dot_nt(r_in_s, b_out_s), 0.0)

    eye = (ri == rj).astype(F32)
    a_bd = jnp.where((ri // RW_SUB) == (rj // RW_SUB), a_mat, 0.0)
    x = eye - a_bd
    p = _mm3(a_bd, a_bd)
    x = x + _mm3(x, p)
    p = _mm3(p, p)
    x = x + _mm3(x, p)
    p = _mm3(p, p)
    x = x + _mm3(x, p)
    size = RW_SUB
    while size < T:
        lower = ((ri // (2 * size)) == (rj // (2 * size))) & ((ri // size) > (rj // size))
        x = x - _mm3(_mm3(x, jnp.where(lower, a_mat, 0.0)), x)
        size *= 2

    state = state_ref[...]
    state_b = state.astype(BF16)
    d = partial(jnp.dot, preferred_element_type=F32)
    rhs = _dot_nt(k_in_s, state_b) + d(kk_mat.astype(BF16), v_s)
    u = _mm3(x, rhs)
    u_b = u.astype(BF16)
    o_s = _dot_nt(r_in_s, state_b) + d(rk_mat.astype(BF16), v_s) - d(rb_mat.astype(BF16), u_b)
    o = o_s[0:T]
    for h in range(1, RW_HEADS):
        o = o + o_s[h * T:(h + 1) * T]
    state_ref[...] = state * jnp.exp(b_last) + _dot_tn(v_s, k_end_s) - _dot_tn(u_b, b_end_s)

    mean = _sum01(o, head_sum) * (1.0 / RW_HD)
    var = _sum01(jnp.square(o - mean), head_sum) * (1.0 / RW_HD)
    o = (o - mean) * lax.rsqrt(var + RW_GN_EPS) * lng_ref[...] + lnb_ref[...]
    bonus = _sum01(r * k2 * rk_ref[...], head_sum) * v
    o_ref[0] = ((o + bonus) * g).astype(o_ref.dtype)

    @pl.when(ci == nc - 1)
    def _():
        s_out_ref[0] = state_ref[...]
        shift_out_ref[0] = z[T - 1:T, :]


def _rwkv7_pallas(z, s0_bd, shift_prev, mu, w0, w2, a0, a2, g2, k_k, k_a, r_k, ln_g, ln_b, *, T, out_dtype=F32):
    B, L, C = z.shape
    assert C == RW_COLS and L % T == 0 and T % RW_SUB == 0
    W = BRANCH_W
    w2p = jnp.concatenate([w2, jnp.zeros_like(w2)], axis=0)
    a2p = jnp.concatenate([jnp.zeros_like(a2), a2], axis=0)
    vec = lambda t: t.reshape(1, -1).astype(F32)
    full = lambda shape: pl.BlockSpec(shape, lambda b, c: (0,) * len(shape))
    return pl.pallas_call(
        partial(_rwkv_body, T=T),
        grid=(B, L // T),
        in_specs=[pl.BlockSpec((1, T, C), lambda b, c: (b, c, 0)),
                  pl.BlockSpec((1, 1, C), lambda b, c: (b, 0, 0)),
                  pl.BlockSpec((1, W, W), lambda b, c: (b, 0, 0)),
                  full((1, C)), full((1, W)), full((128, W)), full((1, W)), full((128, W)), full((128, W)),
                  full((1, W)), full((1, W)), full((1, W)), full((1, W)), full((1, W))],
        out_specs=[pl.BlockSpec((1, T, W), lambda b, c: (b, c, 0)),
                   pl.BlockSpec((1, W, W), lambda b, c: (b, 0, 0)),
                   pl.BlockSpec((1, 1, C), lambda b, c: (b, 0, 0))],
        out_shape=[jax.ShapeDtypeStruct((B, L, W), out_dtype),
                   jax.ShapeDtypeStruct((B, W, W), F32),
                   jax.ShapeDtypeStruct((B, 1, C), F32)],
        scratch_shapes=[pltpu.VMEM((W, W), F32), pltpu.VMEM((1, C), F32)],
        compiler_params=pltpu.CompilerParams(dimension_semantics=("arbitrary", "arbitrary")),
        name="rwkv7_chunked",
    )(z, shift_prev.reshape(B, 1, C), s0_bd, vec(mu), vec(w0), w2p, vec(a0), a2p, g2, vec(k_k), vec(k_a),
      vec(r_k), vec(ln_g), vec(ln_b))


def _state_to_bd(s):
    B = s.shape[0]
    eye = jnp.eye(RW_HEADS, dtype=s.dtype)
    return (s[:, :, :, None, :] * eye[None, :, None, :, None]).reshape(B, BRANCH_W, BRANCH_W)


def _state_from_bd(s_bd):
    B = s_bd.shape[0]
    s5 = s_bd.reshape(B, RW_HEADS, RW_HD, RW_HEADS, RW_HD)
    return jnp.stack([s5[:, h, :, h, :] for h in range(RW_HEADS)], axis=1)


def _hgrn_body(z_ref, s0_ref, loglb_ref, log1mlb_ref, ng_ref, o_ref, s_out_ref, state_ref, *, T):
    ci = pl.program_id(1)
    nc = pl.num_programs(1)
    W = BRANCH_W

    @pl.when(ci == 0)
    def _():
        state_ref[...] = s0_ref[0]

    z = z_ref[0]
    zq, zf, zi, zg = z[:, 0:W], z[:, W:2 * W], z[:, 2 * W:3 * W], z[:, 3 * W:]
    log_sig = jnp.minimum(zf, 0.0) - jnp.log(1.0 + jnp.exp(-jnp.abs(zf)))
    a = loglb_ref[...]
    bv = log1mlb_ref[...] + log_sig
    m = jnp.maximum(a, bv)
    log_f = m + jnp.log(jnp.exp(a - m) + jnp.exp(bv - m))
    k = 1.0 - jnp.exp(log_f)
    q = zq * jax.nn.sigmoid(zq)
    ti = lax.broadcasted_iota(jnp.int32, (T, T), 0)
    tj = lax.broadcasted_iota(jnp.int32, (T, T), 1)
    b = _sum01_left((ti >= tj).astype(BF16), log_f)
    li = lax.broadcasted_iota(jnp.int32, (W, W), 0) // HG_DK
    lj = lax.broadcasted_iota(jnp.int32, (W, W), 1) // HG_DK
    same_head = li == lj
    head_sum = same_head.astype(BF16)

    prods, first = [], []
    for s in range(T):
        r0 = (s // SUBLANES) * SUBLANES
        t_idx = r0 + lax.broadcasted_iota(jnp.int32, (T - r0, W), 0)
        e = jnp.exp(jnp.where(t_idx >= s, b[r0:, :] - b[s:s + 1, :], -jnp.inf))
        prods.append((q[r0:, :] * e * k[s:s + 1, :]).astype(BF16))
        first.append(r0)
    scores = jnp.dot(jnp.concatenate(prods, axis=0), head_sum, preferred_element_type=F32)
    groups = [jnp.zeros((SUBLANES, W), F32) for _ in range(T // SUBLANES)]
    at = 0
    for s in range(T):
        for gi in range(first[s] // SUBLANES, T // SUBLANES):
            lo = at + gi * SUBLANES - first[s]
            groups[gi] = groups[gi] + scores[lo:lo + SUBLANES, :] * zi[s:s + 1, :]
        at += T - first[s]
    o = jnp.concatenate(groups, axis=0)

    state = state_ref[...]
    o = o + _dot_nt((q * jnp.exp(b)).astype(BF16), state.astype(BF16))
    b_last = b[T - 1:T, :]
    k_end = (k * jnp.exp(b_last - b)).astype(BF16)
    upd = _dot_tn(zi.astype(BF16), k_end)
    state_ref[...] = state * jnp.exp(b_last) + jnp.where(same_head, upd, 0.0)

    ms = _sum01(o * o, head_sum) * (1.0 / HG_DK)
    o = o * lax.rsqrt(ms + EPS) * ng_ref[...]
    o_ref[0] = (o * (zg * jax.nn.sigmoid(zg))).astype(o_ref.dtype)

    @pl.when(ci == nc - 1)
    def _():
        s_out_ref[0] = state_ref[...]


def _hgrn2_pallas(z, s0_bd, lb, norm_g, *, T, out_dtype=F32):
    B, L, C = z.shape
    assert C == HG_COLS and L % T == 0
    W = BRANCH_W
    full = lambda shape: pl.BlockSpec(shape, lambda b, c: (0,) * len(shape))
    return pl.pallas_call(
        partial(_hgrn_body, T=T),
        grid=(B, L // T),
        in_specs=[pl.BlockSpec((1, T, C), lambda b, c: (b, c, 0)),
                  pl.BlockSpec((1, W, W), lambda b, c: (b, 0, 0)),
                  full((1, W)), full((1, W)), full((1, W))],
        out_specs=[pl.BlockSpec((1, T, W), lambda b, c: (b, c, 0)),
                   pl.BlockSpec((1, W, W), lambda b, c: (b, 0, 0))],
        out_shape=[jax.ShapeDtypeStruct((B, L, W), out_dtype),
                   jax.ShapeDtypeStruct((B, W, W), F32)],
        scratch_shapes=[pltpu.VMEM((W, W), F32)],
        compiler_params=pltpu.CompilerParams(dimension_semantics=("arbitrary", "arbitrary")),
        name="hgrn2_chunked",
    )(z, s0_bd, jnp.log(lb).reshape(1, W), jnp.log1p(-lb).reshape(1, W), jnp.tile(norm_g, HG_HEADS).reshape(1, W))


PROJ_TM = 256
PROJ_VMEM_LIMIT_V7X = 48 * 1024 * 1024
CONV_CARRY = 32
CONV_TC = 512


def _in_proj_body(x_ref, g_ref, w_ref, q_ref, k_ref, v_ref, kb_ref, vb_ref, hg_ref, cv_ref, rw_ref, gate_ref):
    x = x_ref[...]
    h = (x * lax.rsqrt(jnp.mean(x * x, axis=-1, keepdims=True) + EPS) * g_ref[...]).astype(BF16)
    W = BRANCH_W

    def proj(a, b):
        return jnp.dot(h, w_ref[:, a:b], preferred_element_type=F32)

    q_ref[...] = (proj(0, W) * SB_Q_SCALE).astype(BF16)
    k = proj(W, 2 * W)
    v = proj(2 * W, 3 * W)
    k_ref[...] = k
    v_ref[...] = v
    kb_ref[...] = k.astype(BF16)
    vb_ref[...] = v.astype(BF16)
    c = SB_COLS
    hg_ref[...] = proj(c, c + HG_COLS)
    c += HG_COLS
    cv_ref[...] = proj(c, c + CV_COLS)
    c += CV_COLS
    rw_ref[...] = proj(c, c + RW_COLS)
    c += RW_COLS
    for n in range(N_BRANCH):
        gate_ref[:, n * D_MODEL:(n + 1) * D_MODEL] = jax.nn.sigmoid(
            proj(c + n * D_MODEL, c + (n + 1) * D_MODEL)).astype(BF16)


def _in_proj(x2, norm_g, w_in_b):
    n = x2.shape[0]
    TM = min(PROJ_TM, n)
    assert n % TM == 0
    row = lambda w: pl.BlockSpec((TM, w), lambda i: (i, 0))
    shp = lambda w, dt: jax.ShapeDtypeStruct((n, w), dt)
    W = BRANCH_W
    return pl.pallas_call(
        _in_proj_body,
        grid=(n // TM,),
        in_specs=[row(D_MODEL), pl.BlockSpec((1, D_MODEL), lambda i: (0, 0)),
                  pl.BlockSpec((D_MODEL, IN_COLS), lambda i: (0, 0), pipeline_mode=pl.Buffered(1))],
        out_specs=[row(W), row(W), row(W), row(W), row(W), row(HG_COLS), row(CV_COLS), row(RW_COLS), row(GATE_COLS)],
        out_shape=[shp(W, BF16), shp(W, F32), shp(W, F32), shp(W, BF16), shp(W, BF16),
                   shp(HG_COLS, F32), shp(CV_COLS, F32), shp(RW_COLS, F32), shp(GATE_COLS, BF16)],
        compiler_params=pltpu.CompilerParams(dimension_semantics=("arbitrary",), vmem_limit_bytes=PROJ_VMEM_LIMIT_V7X),
        name="in_proj",
    )(x2, norm_g.reshape(1, -1), w_in_b)


def _merge_body(x_ref, sb_ref, hg_ref, cv_ref, rw_ref, gate_ref, wb_ref, wo_ref, o_ref):
    merged = None
    for n, br in enumerate((sb_ref, hg_ref, cv_ref, rw_ref)):
        t = jnp.dot(br[...].astype(BF16), wb_ref[n], preferred_element_type=F32)
        t = t * gate_ref[:, n * D_MODEL:(n + 1) * D_MODEL].astype(F32)
        merged = t if merged is None else merged + t
    o_ref[...] = x_ref[...] + jnp.dot(merged.astype(BF16), wo_ref[...], preferred_element_type=F32)


def _merge(x2, o_sb, o_hg, o_cv, o_rw, gates, wb_b, wo_b):
    n = x2.shape[0]
    TM = min(PROJ_TM, n)
    row = lambda w: pl.BlockSpec((TM, w), lambda i: (i, 0))
    return pl.pallas_call(
        _merge_body,
        grid=(n // TM,),
        in_specs=[row(D_MODEL), row(BRANCH_W), row(BRANCH_W), row(BRANCH_W), row(BRANCH_W), row(GATE_COLS),
                  pl.BlockSpec((N_BRANCH, BRANCH_W, D_MODEL), lambda i: (0, 0, 0)),
                  pl.BlockSpec((D_MODEL, D_MODEL), lambda i: (0, 0))],
        out_specs=row(D_MODEL),
        out_shape=jax.ShapeDtypeStruct((n, D_MODEL), F32),
        compiler_params=pltpu.CompilerParams(dimension_semantics=("arbitrary",), vmem_limit_bytes=PROJ_VMEM_LIMIT_V7X),
        name="branch_merge",
    )(x2, o_sb, o_hg, o_cv, o_rw, gates, wb_b, wo_b)


def _conv_body(z_ref, buf_ref, w_ref, b_ref, lg_ref, lb_ref, o_ref, buf_out_ref, hp_ref, *, TC):
    ci = pl.program_id(1)
    nc = pl.num_programs(1)

    @pl.when(ci == 0)
    def _():
        hp_ref[0:CONV_CARRY, :] = buf_ref[0]

    z = z_ref[0]
    hp_ref[CONV_CARRY:, :] = z[:, 0:CONV_CH] * jax.nn.sigmoid(z[:, CONV_CH:])
    first = CONV_CARRY - (CONV_WIDTH - 1)
    y = jnp.zeros((TC, CONV_CH), F32) + b_ref[...]
    for j in range(CONV_WIDTH):
        y = y + hp_ref[first + j:first + j + TC, :] * w_ref[j:j + 1, :]
    mu = jnp.mean(y, axis=-1, keepdims=True)
    d = y - mu
    var = jnp.mean(d * d, axis=-1, keepdims=True)
    y = d * lax.rsqrt(var + EPS) * lg_ref[...] + lb_ref[...]
    o_ref[0] = (y * jax.nn.sigmoid(y)).astype(o_ref.dtype)
    tail = hp_ref[TC:TC + CONV_CARRY, :]

    @pl.when(ci == nc - 1)
    def _():
        buf_out_ref[0] = tail

    hp_ref[0:CONV_CARRY, :] = tail


def _conformer_conv_pallas(z_cv, buf, w, b, ln_g, ln_b, *, TC, out_dtype=F32):
    B, L, _ = z_cv.shape
    assert L % TC == 0 and TC % 8 == 0 and TC >= CONV_CARRY
    C = CONV_CH
    pad = CONV_CARRY - (CONV_WIDTH - 1)
    buf_p = jnp.pad(buf.astype(F32), ((0, 0), (pad, 0), (0, 0)))
    w_p = jnp.pad(w.astype(F32), ((0, 32 - CONV_WIDTH), (0, 0)))
    full = lambda shape: pl.BlockSpec(shape, lambda b, c: (0,) * len(shape))
    y, buf_new = pl.pallas_call(
        partial(_conv_body, TC=TC),
        grid=(B, L // TC),
        in_specs=[pl.BlockSpec((1, TC, 2 * C), lambda b, c: (b, c, 0)),
                  pl.BlockSpec((1, CONV_CARRY, C), lambda b, c: (b, 0, 0)),
                  full((32, C)), full((1, C)), full((1, C)), full((1, C))],
        out_specs=[pl.BlockSpec((1, TC, C), lambda b, c: (b, c, 0)),
                   pl.BlockSpec((1, CONV_CARRY, C), lambda b, c: (b, 0, 0))],
        out_shape=[jax.ShapeDtypeStruct((B, L, C), out_dtype), jax.ShapeDtypeStruct((B, CONV_CARRY, C), F32)],
        scratch_shapes=[pltpu.VMEM((CONV_CARRY + TC, C), F32)],
        compiler_params=pltpu.CompilerParams(dimension_semantics=("arbitrary", "arbitrary")),
        name="conformer_conv",
    )(z_cv, buf_p, w_p, b.reshape(1, C), ln_g.reshape(1, C), ln_b.reshape(1, C))
    return y, buf_new[:, pad:, :]


PK_PAIRS = PK_HEADS * PK_TOPK
PEER_HEAD_GROUP = 4
PEER_TB = 128
ROW_TILES = D_MODEL // LANES
PEER_VMEM_LIMIT_V7X = 52 * 1024 * 1024


def _topk_rows(scores, payloads, k):
    n = scores[0].shape[0]
    rows = lax.broadcasted_iota(jnp.int32, scores[0].shape, 0).astype(F32)
    scores = list(scores)
    vals = [[] for _ in scores]
    ids = [[] for _ in scores]
    for _ in range(k):
        for c, s in enumerate(scores):
            m = jnp.max(s, axis=0, keepdims=True)
            pos = jnp.min(jnp.where(s == m, rows, float(n)), axis=0, keepdims=True)
            sel = rows == pos
            ids[c].append(pos if payloads[c] is None
                          else jnp.max(jnp.where(sel, payloads[c], -1.0), axis=0, keepdims=True))
            vals[c].append(m)
            scores[c] = jnp.where(sel, -jnp.inf, s)
    return [(jnp.concatenate(v, axis=0), jnp.concatenate(i, axis=0)) for v, i in zip(vals, ids)]


def _dot3(ah, al, bh, bl, dims):
    d = partial(lax.dot_general, dimension_numbers=dims, preferred_element_type=F32)
    return d(ah, bh) + (d(ah, bl) + d(al, bh))


def _candidate_tables(tb):
    groups = [[0], [1], [2], [3, 4], [5, 6, 7], list(range(8, 16))]
    rows = []
    for grp in groups:
        part = [(a, b) for a in grp for b in range(PK_TOPK // (a + 1))]
        rows += part + [None] * ((-len(part)) % SUBLANES)
    n = len(rows)
    sel1 = [[1.0 if (r is not None and r[0] == j) else 0.0 for j in range(PK_TOPK)] for r in rows]
    sel2 = [[1.0 if (r is not None and r[1] == j) else 0.0 for j in range(PK_TOPK)] for r in rows]
    cpos = [[-1.0 if r is None else float(r[0] * PK_TOPK + r[1])] * tb for r in rows]
    return jnp.array(sel1, BF16), jnp.array(sel2, BF16), jnp.array(cpos, F32), n


def _pick_rows(sel, v):
    hi = v.astype(BF16)
    r1 = v - hi.astype(F32)
    mid = r1.astype(BF16)
    lo = (r1 - mid.astype(F32)).astype(BF16)
    d = partial(jnp.dot, preferred_element_type=F32)
    return (d(sel, hi) + d(sel, mid)) + d(sel, lo)


def _peer_route_body(x_ref, g_ref, wqh_ref, wql_ref, qg_ref, k1_ref, k2_ref, sel1_ref, sel2_ref, cpos_ref,
                     xn_ref, idx_ref, idxc_ref, gate_ref):
    x = x_ref[...]
    xn = x * lax.rsqrt(jnp.mean(x * x, axis=-1, keepdims=True) + EPS) * g_ref[...]
    xn_ref[...] = xn
    nn = (((1,), (0,)), ((), ()))
    nt = (((1,), (1,)), ((), ()))
    xh, xl = _split(xn)
    q = _dot3(xh, xl, wqh_ref[...], wql_ref[...], nn)
    half = PK_DQ // 2
    experts, gates = [], []
    for h0 in range(0, PK_HEADS, PEER_HEAD_GROUP):
        heads = range(h0, h0 + PEER_HEAD_GROUP)
        subs = []
        for h in heads:
            qh = q[:, h * PK_DQ:(h + 1) * PK_DQ]
            qh = qh * lax.rsqrt(jnp.mean(qh * qh, axis=-1, keepdims=True) + EPS) * qg_ref[...]
            qhh, qhl = _split(qh)
            k1h, k1l = _split(k1_ref[h])
            k2h, k2l = _split(k2_ref[h])
            subs.append(_dot3(k1h, k1l, qhh[:, :half], qhl[:, :half], nt))
            subs.append(_dot3(k2h, k2l, qhh[:, half:], qhl[:, half:], nt))
        tops = _topk_rows(subs, [None] * len(subs), PK_TOPK)
        cands = []
        for j in range(PEER_HEAD_GROUP):
            (v1, _), (v2, _) = tops[2 * j], tops[2 * j + 1]
            cands.append(jnp.where(cpos_ref[...] < 0.0, -jnp.inf,
                                   _pick_rows(sel1_ref[...], v1) + _pick_rows(sel2_ref[...], v2)))
        best = _topk_rows(cands, [cpos_ref[...]] * PEER_HEAD_GROUP, PK_TOPK)
        for j in range(PEER_HEAD_GROUP):
            i1, i2 = tops[2 * j][1], tops[2 * j + 1][1]
            top_s, pos = best[j]
            pa = jnp.floor(pos * (1.0 / PK_TOPK))
            pb = pos - pa * PK_TOPK
            e1 = jnp.zeros_like(pos)
            e2 = jnp.zeros_like(pos)
            for a in range(PK_TOPK):
                e1 = jnp.where(pa == a, i1[a:a + 1, :], e1)
                e2 = jnp.where(pb == a, i2[a:a + 1, :], e2)
            e = jnp.exp(top_s - jnp.max(top_s, axis=0, keepdims=True))
            gates.append(e / jnp.sum(e, axis=0, keepdims=True))
            experts.append(e1 * float(PK_NKEYS) + e2)
    expert = jnp.concatenate(experts, axis=0).astype(jnp.int32)
    row = (expert + 1) * HALF_TILES
    pair = lax.broadcasted_iota(jnp.int32, row.shape, 0)
    idx_ref[...] = row.T
    idxc_ref[...] = (row - (pair & HALF_TILES)).T
    gate_ref[...] = jnp.concatenate(gates, axis=0)


def _peer_route(x2, norm_g, wq, qn_g, k1, k2):
    n_tok = x2.shape[0]
    TB = PEER_TB
    assert n_tok % TB == 0
    nb = n_tok // TB
    full = lambda shape: pl.BlockSpec(shape, lambda i: (0,) * len(shape))
    wq_hi = wq.astype(BF16)
    wq_lo = (wq - wq_hi.astype(F32)).astype(BF16)
    sel1, sel2, cpos, n_cand = _candidate_tables(TB)
    return pl.pallas_call(
        _peer_route_body,
        grid=(nb,),
        in_specs=[pl.BlockSpec((TB, D_MODEL), lambda i: (i, 0)), full((1, D_MODEL)),
                  full((D_MODEL, PK_HEADS * PK_DQ)), full((D_MODEL, PK_HEADS * PK_DQ)), full((1, PK_DQ)),
                  full((PK_HEADS, PK_NKEYS, PK_DQ // 2)), full((PK_HEADS, PK_NKEYS, PK_DQ // 2)),
                  full((n_cand, PK_TOPK)), full((n_cand, PK_TOPK)), full((n_cand, TB))],
        out_specs=[pl.BlockSpec((TB, D_MODEL), lambda i: (i, 0)),
                   pl.BlockSpec((TB, PK_PAIRS), lambda i: (i, 0)),
                   pl.BlockSpec((TB, PK_PAIRS), lambda i: (i, 0)),
                   pl.BlockSpec((PK_PAIRS, TB), lambda i: (0, i))],
        out_shape=[jax.ShapeDtypeStruct((n_tok, D_MODEL), F32),
                   jax.ShapeDtypeStruct((n_tok, PK_PAIRS), jnp.int32),
                   jax.ShapeDtypeStruct((n_tok, PK_PAIRS), jnp.int32),
                   jax.ShapeDtypeStruct((PK_PAIRS, n_tok), F32)],
        compiler_params=pltpu.CompilerParams(dimension_semantics=("arbitrary",)),
        name="peer_route",
    )(x2, norm_g.reshape(1, -1), wq_hi, wq_lo, qn_g.reshape(1, -1), k1, k2, sel1, sel2, cpos)


HALF_TILES = ROW_TILES // 2


def _unpack_words(word):
    return pltpu.bitcast(word << 16, F32), pltpu.bitcast(word & jnp.int32(-65536), F32)


def _fold_couples(tiles, sub):
    m2 = (sub & 3) < 2
    lvl2 = []
    for a, b in ((tiles[0], tiles[2]), (tiles[1], tiles[3])):
        lvl2.append(jnp.where(m2, a, b) + jnp.where(m2, pltpu.roll(a, 6, axis=0), pltpu.roll(b, 2, axis=0)))
    m1 = (sub & 1) == 0
    a, b = lvl2
    return jnp.where(m1, a, b) + jnp.where(m1, pltpu.roll(a, 7, axis=0), pltpu.roll(b, 1, axis=0))


def _peer_act_body(idx_ref, x_ref, gate_ref, tab_ref, w_ref, part_ref):
    TB = x_ref.shape[0]
    sub = lax.broadcasted_iota(jnp.int32, (SUBLANES, LANES), 0)
    low = sub < HALF_TILES

    def token(t, c):
        x = x_ref[pl.ds(t, 1), :]
        tiles_x = [x[:, j * LANES:(j + 1) * LANES] for j in range(ROW_TILES)]
        x_lo = jnp.concatenate(tiles_x[:HALF_TILES] * 2, axis=0)
        x_hi = jnp.concatenate(tiles_x[HALF_TILES:] * 2, axis=0)
        start = t * PK_PAIRS
        for g in range(PK_PAIRS // SUBLANES):
            tiles = []
            for j in range(HALF_TILES):
                ra = pl.multiple_of(idx_ref[0, 0, start + g * SUBLANES + j], HALF_TILES)
                rb = pl.multiple_of(idx_ref[0, 0, start + g * SUBLANES + j + HALF_TILES], HALF_TILES)
                word = jnp.where(low, tab_ref[pl.ds(ra, SUBLANES), :], tab_ref[pl.ds(rb, SUBLANES), :])
                lo, hi = _unpack_words(word)
                tiles.append(lo * x_lo + hi * x_hi)
            part_ref[t, g * SUBLANES:(g + 1) * SUBLANES, :] = _fold_couples(tiles, sub)
        return c

    lax.fori_loop(0, TB, token, 0)
    lane = lax.broadcasted_iota(jnp.int32, (PK_PAIRS, TB), 1)
    s = jnp.zeros((PK_PAIRS, TB), F32)
    ones = jnp.ones((LANES, LANES), BF16)
    for t in range(TB):
        tot = jnp.dot(part_ref[t].astype(BF16), ones, preferred_element_type=F32)
        s = jnp.where(lane == t, tot, s)
    w_ref[...] = gate_ref[...] * jax.nn.gelu(s)


PEER_MIX_ACCS = 4


def _peer_mix_body(idx_ref, w_ref, spread_ref, tab_ref, y_ref, wrep_ref):
    TB = y_ref.shape[0]
    w_b = w_ref[...].astype(BF16)
    step = 16 * LANES
    for c in range(0, TB * LANES, step):
        wrep_ref[:, c:c + step] = jnp.dot(w_b, spread_ref[:, c:c + step], preferred_element_type=F32)

    def token(t, c):
        start = t * PK_PAIRS
        acc_lo = [jnp.zeros((HALF_TILES, LANES), F32) for _ in range(PEER_MIX_ACCS)]
        acc_hi = [jnp.zeros((HALF_TILES, LANES), F32) for _ in range(PEER_MIX_ACCS)]
        for p in range(PK_PAIRS):
            row = pl.multiple_of(idx_ref[0, 0, start + p], HALF_TILES)
            lo, hi = _unpack_words(tab_ref[pl.ds(row, HALF_TILES), :])
            w = jnp.broadcast_to(wrep_ref[p:p + 1, pl.ds(pl.multiple_of(t * LANES, LANES), LANES)],
                                 (HALF_TILES, LANES))
            acc_lo[p % PEER_MIX_ACCS] = acc_lo[p % PEER_MIX_ACCS] + w * lo
            acc_hi[p % PEER_MIX_ACCS] = acc_hi[p % PEER_MIX_ACCS] + w * hi
        y_ref[t] = jnp.concatenate([(acc_lo[0] + acc_lo[1]) + (acc_lo[2] + acc_lo[3]),
                                    (acc_hi[0] + acc_hi[1]) + (acc_hi[2] + acc_hi[3])], axis=0)
        return c

    lax.fori_loop(0, TB, token, 0)


def _pack_table(tab):
    bits = lax.bitcast_convert_type(tab.astype(BF16), jnp.uint16).astype(jnp.uint32)
    bits = bits.reshape(PK_EXPERTS, 2, HALF_TILES, LANES)
    word = bits[:, 0] | (bits[:, 1] << 16)
    rows = lax.bitcast_convert_type(word, jnp.int32).reshape(PK_EXPERTS * HALF_TILES, LANES)
    return jnp.pad(rows, ((HALF_TILES, HALF_TILES), (0, 0)))


def _smem_spec(TB):
    return pl.BlockSpec((1, 1, PK_PAIRS * TB), lambda i: (i, 0, 0), memory_space=pltpu.SMEM)


def _table_spec():
    return pl.BlockSpec(((PK_EXPERTS + 2) * HALF_TILES, LANES), lambda i: (0, 0), pipeline_mode=pl.Buffered(1))


def _peer_act(idx, xn, gate_t, tab_packed):
    n_tok = xn.shape[0]
    TB = PEER_TB
    nb = n_tok // TB
    return pl.pallas_call(
        _peer_act_body,
        grid=(nb,),
        in_specs=[_smem_spec(TB),
                  pl.BlockSpec((TB, D_MODEL), lambda i: (i, 0)),
                  pl.BlockSpec((PK_PAIRS, TB), lambda i: (0, i)),
                  _table_spec()],
        out_specs=pl.BlockSpec((PK_PAIRS, TB), lambda i: (0, i)),
        out_shape=jax.ShapeDtypeStruct((PK_PAIRS, n_tok), F32),
        scratch_shapes=[pltpu.VMEM((TB, PK_PAIRS, LANES), F32)],
        compiler_params=pltpu.CompilerParams(dimension_semantics=("arbitrary",),
                                             vmem_limit_bytes=PEER_VMEM_LIMIT_V7X),
        name="peer_act",
    )(idx.reshape(nb, 1, TB * PK_PAIRS), xn, gate_t, tab_packed)


def _peer_mix(idx, w_t, tab_packed):
    n_tok = w_t.shape[1]
    TB = PEER_TB
    nb = n_tok // TB
    spread = jnp.repeat(jnp.eye(TB, dtype=BF16), LANES, axis=1)
    return pl.pallas_call(
        _peer_mix_body,
        grid=(nb,),
        in_specs=[_smem_spec(TB),
                  pl.BlockSpec((PK_PAIRS, TB), lambda i: (0, i)),
                  pl.BlockSpec((TB, TB * LANES), lambda i: (0, 0), pipeline_mode=pl.Buffered(1)),
                  _table_spec()],
        out_specs=pl.BlockSpec((TB, ROW_TILES, LANES), lambda i: (i, 0, 0)),
        out_shape=jax.ShapeDtypeStruct((n_tok, ROW_TILES, LANES), F32),
        scratch_shapes=[pltpu.VMEM((PK_PAIRS, TB * LANES), F32)],
        compiler_params=pltpu.CompilerParams(dimension_semantics=("arbitrary",),
                                             vmem_limit_bytes=PEER_VMEM_LIMIT_V7X),
        name="peer_mix",
    )(idx.reshape(nb, 1, TB * PK_PAIRS), w_t, spread, tab_packed)


def _peer_ffn_pallas(x, norm_g, wq, qn_g, sub_keys, u_packed, v_packed):
    B, L, D = x.shape
    n_tok = B * L
    pad = (-n_tok) % PEER_TB
    x2 = jnp.pad(x.reshape(n_tok, D), ((0, pad), (0, 0)))
    xn, idx, idx_couple, gate_t = _peer_route(x2, norm_g, wq, qn_g, sub_keys[:, 0], sub_keys[:, 1])
    w_t = _peer_act(idx_couple, xn, gate_t, u_packed)
    y = _peer_mix(idx, w_t, v_packed)
    return y.reshape(-1, D)[:n_tok].reshape(B, L, D)


def _trunk_layer(x, k_past, v_past, s_hg, buf_conv, s_rw, buf_shift, p):
    B_, L, _ = x.shape
    n = B_ * L
    x2 = x.reshape(n, D_MODEL)
    qb, k, v, kb, vb, z_hg, z_cv, z_rw, gates = _in_proj(x2, p['norm_mix_g'], p['w_in'])
    seq = lambda t: t.reshape(B_, L, t.shape[-1])
    if k_past is None:
        o_sb = _sb_attention(seq(qb), seq(kb), seq(vb), off=0, tq=SB_TQ, tk=SB_TK, out_dtype=F32)
    else:
        past = k_past.shape[1]
        pad = jnp.zeros((B_, (-(past + L)) % SB_TK, BRANCH_W), BF16)
        kc = jnp.concatenate([k_past.reshape(B_, past, BRANCH_W).astype(BF16), seq(kb), pad], axis=1)
        vc = jnp.concatenate([v_past.reshape(B_, past, BRANCH_W).astype(BF16), seq(vb), pad], axis=1)
        o_sb = _sb_attention(seq(qb), kc, vc, off=past, tq=L, tk=SB_TK, out_dtype=F32)
    o_hg, s_hg_bd = _hgrn2_pallas(seq(z_hg), _state_to_bd(jnp.swapaxes(s_hg.astype(F32), 2, 3)), p['lb'],
                                  p['hg_norm_g'], T=CHUNK if L % CHUNK == 0 else L)
    s_hg_new = jnp.swapaxes(_state_from_bd(s_hg_bd), 2, 3)
    o_cv, buf_conv_new = _conformer_conv_pallas(seq(z_cv), buf_conv, p['conv_w'], p['conv_b'], p['conv_ln_g'],
                                                p['conv_ln_b'], TC=CONV_TC if L % CONV_TC == 0 else L)
    o_rw, s_rw_bd, shift_new = _rwkv7_pallas(seq(z_rw), _state_to_bd(s_rw.astype(F32)), buf_shift, p['rw_mu'],
                                             p['rw_w0'], p['rw_w2'], p['rw_a0'], p['rw_a2'], p['rw_g2'], p['rw_k_k'],
                                             p['rw_k_a'], p['rw_r_k'], p['rw_ln_g'], p['rw_ln_b'],
                                             T=RW_CHUNK if L % RW_CHUNK == 0 else L)
    s_rw_new = _state_from_bd(s_rw_bd)
    buf_shift_new = shift_new[:, 0]
    flat = lambda t: t.reshape(n, BRANCH_W)
    x = _merge(x2, flat(o_sb), flat(o_hg), flat(o_cv), flat(o_rw), gates, p['w_branch'], p['w_out']).reshape(x.shape)
    x = x + _peer_ffn_pallas(x, p['norm_ffn_g'], p['peer_wq'], p['peer_qn_g'], p['peer_keys'],
                             p['peer_u'], p['peer_v'])
    heads = lambda t: t.reshape(B_, L, SB_HEADS, HEAD_DIM)
    return x, (heads(k), heads(v), s_hg_new, buf_conv_new, s_rw_new, buf_shift_new)


def kernel(x_prompt, x_sample, cache_sb_k, cache_sb_v, state_hgrn, state_conv, state_rwkv, state_shift,
           norm_mix_g, w_in, hg_lb_logits, hg_norm_g, conv_w, conv_b, conv_ln_g, conv_ln_b,
           rw_mu, rw_w0, rw_w2, rw_a0, rw_a2, rw_g2, rw_k_k, rw_k_a, rw_r_k, rw_ln_g, rw_ln_b,
           w_branch, w_out, norm_ffn_g, peer_wq, peer_qn_g, peer_keys, peer_u, peer_v, final_norm_g):
    lb_all = jnp.cumsum(jax.nn.softmax(hg_lb_logits.astype(F32), axis=0), axis=0)
    lb_all = lb_all - lb_all[0:1]
    Bp = x_prompt.shape[0]
    hg0 = jnp.zeros((Bp, HG_HEADS, HG_DK, HG_DV), F32)
    cv0 = jnp.zeros((Bp, CONV_WIDTH - 1, CONV_CH), x_prompt.dtype)
    rw0 = jnp.zeros((Bp, RW_HEADS, RW_HD, RW_HD), F32)
    sh0 = jnp.zeros((Bp, RW_COLS), x_prompt.dtype)
    xp, xs = x_prompt, x_sample
    outs_p, outs_s = [], []
    for l in range(DEPTH):
        lp = dict(norm_mix_g=norm_mix_g[l], w_in=w_in[l].astype(BF16), lb=lb_all[l], hg_norm_g=hg_norm_g[l],
                  conv_w=conv_w[l], conv_b=conv_b[l], conv_ln_g=conv_ln_g[l], conv_ln_b=conv_ln_b[l],
                  rw_mu=rw_mu[l], rw_w0=rw_w0[l], rw_w2=rw_w2[l], rw_a0=rw_a0[l], rw_a2=rw_a2[l],
                  rw_g2=rw_g2[l], rw_k_k=rw_k_k[l], rw_k_a=rw_k_a[l], rw_r_k=rw_r_k[l],
                  rw_ln_g=rw_ln_g[l], rw_ln_b=rw_ln_b[l], w_branch=w_branch[l].astype(BF16),
                  w_out=w_out[l].astype(BF16),
                  norm_ffn_g=norm_ffn_g[l], peer_wq=peer_wq[l], peer_qn_g=peer_qn_g[l],
                  peer_keys=peer_keys[l], peer_u=_pack_table(peer_u[l]), peer_v=_pack_table(peer_v[l]))
        xp, st_p = _trunk_layer(xp, None, None, hg0, cv0, rw0, sh0, lp)
        xs, st_s = _trunk_layer(xs, cache_sb_k[l], cache_sb_v[l], state_hgrn[l], state_conv[l],
                                state_rwkv[l], state_shift[l], lp)
        outs_p.append(st_p)
        outs_s.append(st_s)

    def stk(outs, i):
        return jnp.stack([o[i] for o in outs], axis=0)

    y_prompt = _rmsnorm_pallas(xp, final_norm_g)
    y_sample = _rmsnorm_pallas(xs, final_norm_g)
    return (y_prompt, y_sample,
            stk(outs_p, 0), stk(outs_p, 1), stk(outs_p, 2), stk(outs_p, 3), stk(outs_p, 4), stk(outs_p, 5),
            stk(outs_s, 0), stk(outs_s, 1), stk(outs_s, 2), stk(outs_s, 3), stk(outs_s, 4), stk(outs_s, 5))
```
